```python
import jax, jax.numpy as jnp
from jax import lax
import numpy as np

D_MODEL = 1024
BATCH = 4
SEQ = 4096
DEPTH = 2

HEAD_DIM = 64
BLOCK = 128
A_Q_HEADS = 8
A_KV_HEADS = 2
A_GROUP = A_Q_HEADS // A_KV_HEADS
A_WINDOW = 128
B_HEADS = 8
C_HEADS = 16
C_PATTERNS = ((128, 1), (512, 4), (2048, 16))
MEM_LEN = 256
X_HEADS = 4
X_HEAD_DIM = D_MODEL // X_HEADS
D_FF = 2816
RMS_EPS = 1e-6

A_Q_W = A_Q_HEADS * HEAD_DIM
A_KV_W = A_KV_HEADS * HEAD_DIM
B_W = B_HEADS * HEAD_DIM
EVEN_IN = A_Q_W + 2 * A_KV_W + 3 * B_W
EVEN_MIX = A_Q_W + B_W
ODD_IN = 3 * C_HEADS * HEAD_DIM
ODD_MIX = C_HEADS * HEAD_DIM

kernel_name = 'hybrid_swa_stickbreak_dilated_block'


def rms_norm(x, g):
    xf = x.astype(jnp.float32)
    y = xf * lax.rsqrt(jnp.mean(xf * xf, axis=-1, keepdims=True) + RMS_EPS)
    return (y * g.astype(jnp.float32)).astype(x.dtype)


def alibi_slopes(n_heads):
    return jnp.asarray(2.0 ** (-8.0 * np.arange(1, n_heads + 1) / n_heads), dtype=jnp.float32)


def swiglu_ffn(x, w_gu, w_down):
    gate, up = jnp.split(x @ w_gu, 2, axis=-1)
    return (jax.nn.silu(gate) * up) @ w_down


def banded_attention(q, k, v, slopes, max_dist, step, sinks=None):
    b, l, hkv, g, dh = q.shape
    nb = -(-l // BLOCK)
    lp = nb * BLOCK
    pad = lp - l
    qb = jnp.pad(q, ((0, 0), (0, pad), (0, 0), (0, 0), (0, 0))).reshape(b, nb, BLOCK, hkv, g, dh)
    kv_pad = ((0, 0), (BLOCK, pad), (0, 0), (0, 0))
    k = jnp.pad(k, kv_pad).reshape(b, nb + 1, BLOCK, hkv, dh)
    v = jnp.pad(v, kv_pad).reshape(b, nb + 1, BLOCK, hkv, dh)
    kb = jnp.concatenate([k[:, :-1], k[:, 1:]], axis=2)
    vb = jnp.concatenate([v[:, :-1], v[:, 1:]], axis=2)
    s = jnp.einsum('bnqhgd,bnkhd->bnhgqk', qb, kb).astype(jnp.float32) * (dh ** -0.5)
    dist = jnp.arange(BLOCK)[:, None] + BLOCK - jnp.arange(2 * BLOCK)[None, :]
    kpos = jnp.arange(nb)[:, None] * BLOCK - BLOCK + jnp.arange(2 * BLOCK)[None, :]
    valid = (dist >= 0) & (dist <= max_dist) & (kpos[:, None, :] >= 0)
    bias = -(slopes.astype(jnp.float32) * step)[:, :, None, None] * dist.astype(jnp.float32)
    s = jnp.where(valid[None, :, None, None], s + bias[None, None], -jnp.inf)
    m = jnp.max(s, axis=-1)
    if sinks is not None:
        sk = sinks.astype(jnp.float32)[..., None]
        m = jnp.maximum(m, sk)
    p = jnp.exp(s - m[..., None])
    denom = jnp.sum(p, axis=-1)
    if sinks is not None:
        denom = denom + jnp.exp(sk - m)
    o = jnp.einsum('bnhgqk,bnkhd->bnqhgd', (p / denom[..., None]).astype(v.dtype), vb)
    lse = m + jnp.log(denom)
    o = o.reshape(b, lp, hkv, g, dh)[:, :l]
    lse = jnp.moveaxis(lse, -1, 2).reshape(b, lp, hkv, g)[:, :l]
    return o, lse


def stick_breaking_attention(q, k, v):
    b, s, h, dh = q.shape
    nb = s // BLOCK
    qb = q.reshape(b, nb, BLOCK, h, dh).transpose(1, 0, 2, 3, 4)
    spos = jnp.arange(s)
    scale = dh ** -0.5

    def one_block(args):
        i, qi = args
        z = jnp.einsum('bqhd,bkhd->bhqk', qi, k).astype(jnp.float32) * scale
        tpos = i * BLOCK + jnp.arange(BLOCK)
        strict = spos[None, :] < tpos[:, None]
        log_keep = jnp.where(strict, jax.nn.log_sigmoid(-z), 0.0)
        log_after = lax.cumsum(log_keep, axis=3, reverse=True) - log_keep
        a = jnp.where(strict, jnp.exp(jax.nn.log_sigmoid(z) + log_after), 0.0)
        return jnp.einsum('bhqk,bkhd->bqhd', a.astype(v.dtype), v)

    o = lax.map(one_block, (jnp.arange(nb), qb))
    return o.transpose(1, 0, 2, 3, 4).reshape(b, s, h, dh)


def dilated_attention(q, k, v, slopes):
    b, s, h, dh = q.shape
    outs, lses = [], []
    for window, dil in C_PATTERNS:
        sp = -(-s // dil) * dil
        ls = sp // dil

        def strided(t):
            t = jnp.pad(t, ((0, 0), (0, sp - s), (0, 0), (0, 0)))
            return t.reshape(b, ls, dil, h, dh).transpose(0, 2, 1, 3, 4).reshape(b * dil, ls, h, dh)

        o, lse = banded_attention(strided(q)[:, :, :, None], strided(k), strided(v),
                                  slopes[:, None], window // dil, dil)
        o = o[:, :, :, 0].reshape(b, dil, ls, h, dh).transpose(0, 2, 1, 3, 4).reshape(b, sp, h, dh)[:, :s]
        lse = lse[..., 0].reshape(b, dil, ls, h).transpose(0, 2, 1, 3).reshape(b, sp, h)[:, :s]
        outs.append(o)
        lses.append(lse)
    w = jax.nn.softmax(jnp.stack(lses), axis=0)
    o = jnp.sum(w[..., None] * jnp.stack(outs).astype(jnp.float32), axis=0)
    return o.astype(q.dtype)


def even_mixer(h, w_in, q_gain, k_gain, sinks, w_out):
    b, s, _ = h.shape
    cuts = np.cumsum([A_Q_W, A_KV_W, A_KV_W, B_W, B_W]).tolist()
    qa, ka, va, qb, kb, vb = jnp.split(h @ w_in, cuts, axis=-1)
    qa = rms_norm(qa.reshape(b, s, A_KV_HEADS, A_GROUP, HEAD_DIM), q_gain)
    ka = rms_norm(ka.reshape(b, s, A_KV_HEADS, HEAD_DIM), k_gain)
    va = va.reshape(b, s, A_KV_HEADS, HEAD_DIM)
    slopes_a = alibi_slopes(A_Q_HEADS).reshape(A_KV_HEADS, A_GROUP)
    o_a, _ = banded_attention(qa, ka, va, slopes_a, A_WINDOW - 1, 1,
                              sinks.reshape(A_KV_HEADS, A_GROUP))
    o_b = stick_breaking_attention(qb.reshape(b, s, B_HEADS, HEAD_DIM),
                                   kb.reshape(b, s, B_HEADS, HEAD_DIM),
                                   vb.reshape(b, s, B_HEADS, HEAD_DIM))
    o = jnp.concatenate([o_a.reshape(b, s, A_Q_W), o_b.reshape(b, s, B_W)], axis=-1)
    return o @ w_out


def odd_mixer(h, w_in, q_gain, k_gain, w_out):
    b, s, _ = h.shape
    qkv = (h @ w_in).reshape(b, s, 3, C_HEADS, HEAD_DIM)
    q = rms_norm(qkv[:, :, 0], q_gain)
    k = rms_norm(qkv[:, :, 1], k_gain)
    o = dilated_attention(q, k, qkv[:, :, 2], alibi_slopes(C_HEADS))
    return o.reshape(b, s, ODD_MIX) @ w_out


def memory_cross_attention(h, m, w_q, w_kv, q_gain, k_gain, w_o):
    b, s, _ = h.shape
    q = rms_norm((h @ w_q).reshape(b, s, X_HEADS, X_HEAD_DIM), q_gain)
    kv = (m @ w_kv).reshape(b, m.shape[1], 2, X_HEADS, X_HEAD_DIM)
    k = rms_norm(kv[:, :, 0], k_gain)
    v = kv[:, :, 1]
    sc = jnp.einsum('bqhd,bkhd->bhqk', q, k).astype(jnp.float32) * (X_HEAD_DIM ** -0.5)
    p = jax.nn.softmax(sc, axis=-1).astype(v.dtype)
    o = jnp.einsum('bhqk,bkhd->bqhd', p, v).reshape(b, s, X_HEADS * X_HEAD_DIM)
    return o @ w_o


def setup_inputs(seed: int = 0) -> dict:
    key = jax.random.key(seed)
    k = jax.random.split(key, 25)
    n_even = (DEPTH + 1) // 2
    n_odd = DEPTH // 2
    f32 = jnp.float32

    def dense(kk, shape, fan_in):
        return jax.random.normal(kk, shape, f32) * (fan_in ** -0.5)

    def gain(kk, shape):
        return 1.0 + 0.02 * jax.random.normal(kk, shape, f32)

    return {
        'x': jax.random.normal(k[0], (BATCH, SEQ, D_MODEL), f32),
        'mem': jax.random.normal(k[1], (BATCH, MEM_LEN, D_MODEL), f32),
        'ffn1_norm': gain(k[2], (DEPTH, D_MODEL)),
        'ffn1_w_gu': dense(k[3], (DEPTH, D_MODEL, 2 * D_FF), D_MODEL),
        'ffn1_w_down': dense(k[4], (DEPTH, D_FF, D_MODEL), D_FF),
        'mix_norm': gain(k[5], (DEPTH, D_MODEL)),
        'ev_w_in': dense(k[6], (n_even, D_MODEL, EVEN_IN), D_MODEL),
        'ev_q_gain': gain(k[7], (n_even, HEAD_DIM)),
        'ev_k_gain': gain(k[8], (n_even, HEAD_DIM)),
        'ev_sinks': 0.5 * jax.random.normal(k[9], (n_even, A_Q_HEADS), f32),
        'ev_w_out': dense(k[10], (n_even, EVEN_MIX, D_MODEL), EVEN_MIX),
        'od_w_in': dense(k[11], (n_odd, D_MODEL, ODD_IN), D_MODEL),
        'od_q_gain': gain(k[12], (n_odd, HEAD_DIM)),
        'od_k_gain': gain(k[13], (n_odd, HEAD_DIM)),
        'od_w_out': dense(k[14], (n_odd, ODD_MIX, D_MODEL), ODD_MIX),
        'xa_norm': gain(k[15], (DEPTH, D_MODEL)),
        'xa_mem_norm': gain(k[16], (DEPTH, D_MODEL)),
        'xa_w_q': dense(k[17], (DEPTH, D_MODEL, X_HEADS * X_HEAD_DIM), D_MODEL),
        'xa_w_kv': dense(k[18], (DEPTH, D_MODEL, 2 * X_HEADS * X_HEAD_DIM), D_MODEL),
        'xa_q_gain': gain(k[19], (DEPTH, X_HEAD_DIM)),
        'xa_k_gain': gain(k[20], (DEPTH, X_HEAD_DIM)),
        'xa_w_o': dense(k[21], (DEPTH, X_HEADS * X_HEAD_DIM, D_MODEL), X_HEADS * X_HEAD_DIM),
        'ffn2_norm': gain(k[22], (DEPTH, D_MODEL)),
        'ffn2_w_gu': dense(k[23], (DEPTH, D_MODEL, 2 * D_FF), D_MODEL),
        'ffn2_w_down': dense(k[24], (DEPTH, D_FF, D_MODEL), D_FF),
    }


def reference(x, mem, ffn1_norm, ffn1_w_gu, ffn1_w_down, mix_norm,
              ev_w_in, ev_q_gain, ev_k_gain, ev_sinks, ev_w_out,
              od_w_in, od_q_gain, od_k_gain, od_w_out,
              xa_norm, xa_mem_norm, xa_w_q, xa_w_kv, xa_q_gain, xa_k_gain, xa_w_o,
              ffn2_norm, ffn2_w_gu, ffn2_w_down):
    for layer in range(DEPTH):
        x = x + 0.5 * swiglu_ffn(rms_norm(x, ffn1_norm[layer]), ffn1_w_gu[layer], ffn1_w_down[layer])
        h = rms_norm(x, mix_norm[layer])
        if layer % 2 == 0:
            j = layer // 2
            x = x + even_mixer(h, ev_w_in[j], ev_q_gain[j], ev_k_gain[j], ev_sinks[j], ev_w_out[j])
        else:
            j = layer // 2
            x = x + odd_mixer(h, od_w_in[j], od_q_gain[j], od_k_gain[j], od_w_out[j])
        x = x + memory_cross_attention(rms_norm(x, xa_norm[layer]), rms_norm(mem, xa_mem_norm[layer]),
                                       xa_w_q[layer], xa_w_kv[layer], xa_q_gain[layer],
                                       xa_k_gain[layer], xa_w_o[layer])
        x = x + 0.5 * swiglu_ffn(rms_norm(x, ffn2_norm[layer]), ffn2_w_gu[layer], ffn2_w_down[layer])
    return x
```

```python
import functools

import numpy as np
import jax
import jax.numpy as jnp
from jax import lax
from jax.experimental import pallas as pl
from jax.experimental.pallas import tpu as pltpu

D_MODEL = 1024
BATCH = 4
SEQ = 4096
N_TOK = BATCH * SEQ
DEPTH = 2
HEAD_DIM = 64
BLOCK = 128
A_Q_HEADS = 8
A_KV_HEADS = 2
A_WINDOW = 128
B_HEADS = 8
C_HEADS = 16
C_PATTERNS = ((128, 1), (512, 4), (2048, 16))
MEM_LEN = 256
X_HEADS = 4
X_HEAD_DIM = D_MODEL // X_HEADS
D_FF = 2816
RMS_EPS = 1e-6

LANES = 128
MXU_N = 256
VMEM_LIMIT = 56 * 1024 * 1024

BF = jnp.bfloat16
F32 = jnp.float32
NT_DIMS = (((1,), (1,)), ((), ()))


def _params(sem, vmem=None):
    return pltpu.CompilerParams(dimension_semantics=sem, vmem_limit_bytes=vmem)


def _resident(shape):
    nd = len(shape)
    return pl.BlockSpec(shape, lambda *_: (0,) * nd, pipeline_mode=pl.Buffered(1))


def _rms(xv, g):
    ms = jnp.mean(xv * xv, axis=-1, keepdims=True)
    return xv * lax.rsqrt(ms + RMS_EPS) * g


def _ffn_kernel(x_ref, g_ref, wgu_ref, wd_ref, o_ref, *, tf):
    xv = x_ref[...]
    h = _rms(xv, g_ref[...]).astype(BF)
    acc = jnp.zeros_like(xv)
    for c in range(D_FF // tf):
        gate = jnp.dot(h, wgu_ref[:, c * tf:(c + 1) * tf], preferred_element_type=F32)
        up = jnp.dot(h, wgu_ref[:, D_FF + c * tf:D_FF + (c + 1) * tf], preferred_element_type=F32)
        act = (gate * jax.nn.sigmoid(gate) * up).astype(BF)
        acc = acc + jnp.dot(act, wd_ref[c * tf:(c + 1) * tf, :], preferred_element_type=F32)
    o_ref[...] = xv + 0.5 * acc


def _ffn(x, g, w_gu, w_down, *, tm=512, tf=1408):
    return pl.pallas_call(
        functools.partial(_ffn_kernel, tf=tf),
        grid=(N_TOK // tm,),
        in_specs=[pl.BlockSpec((tm, D_MODEL), lambda i: (i, 0)),
                  _resident((1, D_MODEL)),
                  _resident((D_MODEL, 2 * D_FF)),
                  _resident((D_FF, D_MODEL))],
        out_specs=pl.BlockSpec((tm, D_MODEL), lambda i: (i, 0)),
        out_shape=jax.ShapeDtypeStruct((N_TOK, D_MODEL), F32),
        compiler_params=_params(("parallel",), VMEM_LIMIT),
        name="ffn",
    )(x, g.reshape(1, D_MODEL), w_gu.astype(BF), w_down.astype(BF))


def _group_ones(gs):
    idx = np.arange(MXU_N) // gs
    return jnp.asarray(idx[:, None] == idx[None, :], dtype=BF)


def _proj_kernel(x_ref, g_ref, w_ref, cs_ref, bd_ref, o_ref, *, normed, gs):
    h = _rms(x_ref[...], g_ref[...]).astype(BF)
    for j, is_normed in enumerate(normed):
        cols = slice(MXU_N * j, MXU_N * (j + 1))
        acc = jnp.dot(h, w_ref[:, cols], preferred_element_type=F32)
        y = acc * cs_ref[:, cols]
        if is_normed:
            ss = jnp.dot((acc * acc).astype(BF), bd_ref[...], preferred_element_type=F32)
            y = y * lax.rsqrt(ss * (1.0 / gs) + RMS_EPS)
        o_ref[2 * j] = y[:, :LANES].astype(BF)
        o_ref[2 * j + 1] = y[:, LANES:].astype(BF)


def _proj(x, g, w, colscale, normed, gs, *, tm=512):
    rows = x.shape[0]
    wout = w.shape[1]
    assert wout == MXU_N * len(normed)
    c = wout // LANES
    return pl.pallas_call(
        functools.partial(_proj_kernel, normed=tuple(normed), gs=gs),
        grid=(rows // tm,),
        in_specs=[pl.BlockSpec((tm, D_MODEL), lambda i: (i, 0)),
                  _resident((1, D_MODEL)),
                  _resident((D_MODEL, wout)),
                  _resident((1, wout)),
                  _resident((MXU_N, MXU_N))],
        out_specs=pl.BlockSpec((c, tm, LANES), lambda i: (0, i, 0)),
        out_shape=jax.ShapeDtypeStruct((c, rows, LANES), BF),
        compiler_params=_params(("parallel",), VMEM_LIMIT),
        name="proj",
    )(x, g.reshape(1, D_MODEL), w.astype(BF), colscale.reshape(1, wout).astype(F32), _group_ones(gs))


def _banded_kernel(*refs, n_pairs, kv_div, step, max_dist, has_sinks, want_lse):
    refs = list(refs)
    slopes_ref = refs.pop(0)
    sinks_ref = refs.pop(0) if has_sinks else None
    q_ref, kp_ref, kc_ref, vp_ref, vc_ref, o_ref = refs[:6]
    lse_ref = refs[6] if want_lse else None

    n = pl.program_id(2)
    row = lax.broadcasted_iota(jnp.int32, (BLOCK, 2 * BLOCK), 0)
    col = lax.broadcasted_iota(jnp.int32, (BLOCK, 2 * BLOCK), 1)
    dist = row + BLOCK - col
    valid = (dist >= 0) & (dist <= max_dist) & ((col >= BLOCK) | (n > 0))
    negmask = jnp.where(valid, 0.0, -jnp.inf)
    distf = dist.astype(F32) * float(step)
    lo = lax.broadcasted_iota(jnp.int32, (BLOCK, LANES), 1) < HEAD_DIM

    def pair_body(p, carry):
        q2 = q_ref[p, 0].astype(F32)
        g = p // kv_div
        kp, kc, vp, vc = kp_ref[g, 0], kc_ref[g, 0], vp_ref[g, 0], vc_ref[g, 0]
        res = []
        for hh in range(2):
            h = 2 * p + hh
            qm = jnp.where(lo if hh == 0 else jnp.logical_not(lo), q2, 0.0).astype(BF)
            s = jnp.concatenate(
                [lax.dot_general(qm, kp, NT_DIMS, preferred_element_type=F32),
                 lax.dot_general(qm, kc, NT_DIMS, preferred_element_type=F32)], axis=1)
            s = s - slopes_ref[h] * distf + negmask
            m = jnp.max(s, axis=1, keepdims=True)
            if has_sinks:
                m = jnp.maximum(m, sinks_ref[h])
            pe = jnp.exp(s - m)
            l = jnp.sum(pe, axis=1, keepdims=True)
            if has_sinks:
                l = l + jnp.exp(sinks_ref[h] - m)
            pb = pe.astype(BF)
            pv = (jnp.dot(pb[:, :BLOCK], vp, preferred_element_type=F32)
                  + jnp.dot(pb[:, BLOCK:], vc, preferred_element_type=F32))
            res.append((pv / l, m + jnp.log(l)))
        o_ref[p, 0] = jnp.where(lo, res[0][0], res[1][0]).astype(BF)
        if want_lse:
            lse_ref[p, 0] = jnp.where(lo, jnp.broadcast_to(res[0][1], (BLOCK, LANES)),
                                      jnp.broadcast_to(res[1][1], (BLOCK, LANES)))
        return carry

    lax.fori_loop(0, n_pairs, pair_body, 0)


def _banded(qkv, slopes, sinks, *, d, q_blk, k_blk, v_blk, kv_div, max_dist, want_lse):
    c = qkv.shape[0]
    l_sub = SEQ // d
    nb = l_sub // BLOCK
    view = qkv.reshape(c, BATCH, l_sub, d * LANES)
    n_pairs = q_blk[0]

    def spec(blk, prev):
        size, idx = blk
        if prev:
            return pl.BlockSpec((size, 1, BLOCK, LANES),
                                lambda b, r, n: (idx, b, jnp.maximum(n - 1, 0), r))
        return pl.BlockSpec((size, 1, BLOCK, LANES), lambda b, r, n: (idx, b, n, r))

    smem = pl.BlockSpec(memory_space=pltpu.SMEM)
    has_sinks = sinks is not None
    in_specs = [smem] + ([smem] if has_sinks else []) + [
        spec(q_blk, False), spec(k_blk, True), spec(k_blk, False), spec(v_blk, True), spec(v_blk, False)]
    out_spec = pl.BlockSpec((n_pairs, 1, BLOCK, LANES), lambda b, r, n: (0, b, n, r))
    o_shape = jax.ShapeDtypeStruct((n_pairs, BATCH, l_sub, d * LANES), BF)
    out_specs, out_shape = out_spec, o_shape
    if want_lse:
        out_specs = [out_spec, out_spec]
        out_shape = [o_shape, jax.ShapeDtypeStruct((n_pairs, BATCH, l_sub, d * LANES), F32)]
    args = [slopes] + ([sinks] if has_sinks else []) + [view] * 5
    out = pl.pallas_call(
        functools.partial(_banded_kernel, n_pairs=n_pairs, kv_div=kv_div, step=d, max_dist=max_dist,
                          has_sinks=has_sinks, want_lse=want_lse),
        grid=(BATCH, d, nb),
        in_specs=in_specs, out_specs=out_specs, out_shape=out_shape,
        compiler_params=_params(("parallel", "parallel", "parallel")),
        name=f"banded_d{d}",
    )(*args)
    if want_lse:
        return out[0].reshape(n_pairs, N_TOK, LANES), out[1].reshape(n_pairs, N_TOK, LANES)
    return out.reshape(n_pairs, N_TOK, LANES)


def _combine_kernel(o1, o2, o3, l1, l2, l3, out):
    a, b, c = l1[...], l2[...], l3[...]
    m = jnp.maximum(jnp.maximum(a, b), c)
    ea, eb, ec = jnp.exp(a - m), jnp.exp(b - m), jnp.exp(c - m)
    num = ea * o1[...].astype(F32) + eb * o2[...].astype(F32) + ec * o3[...].astype(F32)
    out[...] = (num / (ea + eb + ec)).astype(BF)


def _combine(os, ls, *, tm=256):
    c = os[0].shape[0]
    spec = pl.BlockSpec((c, tm, LANES), lambda i: (0, i, 0))
    return pl.pallas_call(
        _combine_kernel, grid=(N_TOK // tm,),
        in_specs=[spec] * 6, out_specs=spec,
        out_shape=jax.ShapeDtypeStruct((c, N_TOK, LANES), BF),
        compiler_params=_params(("parallel",), VMEM_LIMIT),
        name="combine",
    )(*os, *ls)


def _sb_kernel(q_ref, k_ref, v_ref, uo_ref, o_ref):
    i = pl.program_id(2)
    q2 = q_ref[0].astype(F32)
    uo = uo_ref[...]
    row = lax.broadcasted_iota(jnp.int32, (BLOCK, BLOCK), 0)
    col = lax.broadcasted_iota(jnp.int32, (BLOCK, BLOCK), 1)
    strict = col < row
    lo = col < HEAD_DIM

    res = []
    for hh in range(2):
        qm = jnp.where(lo if hh == 0 else jnp.logical_not(lo), q2, 0.0).astype(BF)

        def tile(jj, c, o, diag, qm=qm):
            off = pl.multiple_of(jj * BLOCK, BLOCK)
            kt = k_ref[0, pl.ds(off, BLOCK), :]
            vt = v_ref[0, pl.ds(off, BLOCK), :]
            z = lax.dot_general(qm, kt, NT_DIMS, preferred_element_type=F32)
            sp = jnp.maximum(z, 0.0) + jnp.log(1.0 + jnp.exp(-jnp.abs(z)))
            lk = -sp
            if diag:
                lk = jnp.where(strict, lk, 0.0)
            w = jnp.dot(lk.astype(BF), uo, preferred_element_type=F32)
            a = jnp.exp((z - sp) + c + w[:, :BLOCK])
            if diag:
                a = jnp.where(strict, a, 0.0)
            o = o + jnp.dot(a.astype(BF), vt, preferred_element_type=F32)
            return c + w[:, BLOCK:], o

        zero = jnp.zeros((BLOCK, BLOCK), F32)
        c, o = tile(i, zero, zero, True)
        c, o = lax.fori_loop(0, i, lambda j, co: tile(i - 1 - j, co[0], co[1], False), (c, o))
        res.append(o)
    o_ref[0] = jnp.where(lo, res[0], res[1]).astype(BF)


def _stick_breaking(qkv, q_c0, k_c0, v_c0, n_pairs):
    idx = np.arange(BLOCK)
    uo = jnp.asarray(np.concatenate([idx[:, None] > idx[None, :], np.ones((BLOCK, BLOCK), bool)], axis=1),
                     dtype=BF)
    nb = SEQ // BLOCK
    return pl.pallas_call(
        _sb_kernel,
        grid=(BATCH, n_pairs, nb),
        in_specs=[pl.BlockSpec((1, BLOCK, LANES), lambda b, p, i: (q_c0 + p, b * nb + i, 0)),
                  pl.BlockSpec((1, SEQ, LANES), lambda b, p, i: (k_c0 + p, b, 0)),
                  pl.BlockSpec((1, SEQ, LANES), lambda b, p, i: (v_c0 + p, b, 0)),
                  _resident((BLOCK, 2 * BLOCK))],
        out_specs=pl.BlockSpec((1, BLOCK, LANES), lambda b, p, i: (p, b * nb + i, 0)),
        out_shape=jax.ShapeDtypeStruct((n_pairs, N_TOK, LANES), BF),
        compiler_params=_params(("parallel", "parallel", "parallel")),
        name="stick_breaking",
    )(qkv, qkv, qkv, uo)


def _oproj_kernel(*refs):
    x_ref, w_ref, out_ref = refs[0], refs[-2], refs[-1]
    parts = [r[c] for r in refs[1:-2] for c in range(r.shape[0])]
    o = jnp.concatenate(parts, axis=1)
    out_ref[...] = x_ref[...] + jnp.dot(o, w_ref[...], preferred_element_type=F32)


def _oproj(x, heads, w, *, tm=512):
    in_specs = [pl.BlockSpec((tm, D_MODEL), lambda i: (i, 0))]
    in_specs += [pl.BlockSpec((h.shape[0], tm, LANES), lambda i: (0, i, 0)) for h in heads]
    in_specs += [_resident(w.shape)]
    return pl.pallas_call(
        _oproj_kernel, grid=(N_TOK // tm,),
        in_specs=in_specs,
        out_specs=pl.BlockSpec((tm, D_MODEL), lambda i: (i, 0)),
        out_shape=jax.ShapeDtypeStruct((N_TOK, D_MODEL), F32),
        compiler_params=_params(("parallel",), VMEM_LIMIT),
        name="oproj",
    )(x, *heads, w.astype(BF))


def _xattn_kernel(x_ref, g_ref, wq_ref, cs_ref, bd_ref, kv_ref, wo_ref, o_ref):
    xv = x_ref[...]
    h = _rms(xv, g_ref[...]).astype(BF)
    outs = []
    for hd in range(X_HEADS):
        cols = slice(X_HEAD_DIM * hd, X_HEAD_DIM * (hd + 1))
        acc = jnp.dot(h, wq_ref[:, cols], preferred_element_type=F32)
        ss = jnp.dot((acc * acc).astype(BF), bd_ref[...], preferred_element_type=F32)
        q = (acc * cs_ref[:, cols] * lax.rsqrt(ss * (1.0 / X_HEAD_DIM) + RMS_EPS)).astype(BF)
        kh = jnp.concatenate([kv_ref[2 * hd], kv_ref[2 * hd + 1]], axis=1)
        vh = jnp.concatenate([kv_ref[2 * X_HEADS + 2 * hd], kv_ref[2 * X_HEADS + 2 * hd + 1]], axis=1)
        s = lax.dot_general(q, kh, NT_DIMS, preferred_element_type=F32)
        m = jnp.max(s, axis=1, keepdims=True)
        pe = jnp.exp(s - m)
        l = jnp.sum(pe, axis=1, keepdims=True)
        outs.append((jnp.dot(pe.astype(BF), vh, preferred_element_type=F32) / l).astype(BF))
    o = jnp.concatenate(outs, axis=1)
    o_ref[...] = xv + jnp.dot(o, wo_ref[...], preferred_element_type=F32)


def _xattn(x, g, w_q, q_colscale, kv, w_o, *, tm=512):
    assert X_HEAD_DIM == MXU_N
    tiles_per_batch = SEQ // tm
    return pl.pallas_call(
        _xattn_kernel, grid=(N_TOK // tm,),
        in_specs=[pl.BlockSpec((tm, D_MODEL), lambda i: (i, 0)),
                  _resident((1, D_MODEL)),
                  _resident((D_MODEL, D_MODEL)),
                  _resident((1, D_MODEL)),
                  _resident((MXU_N, MXU_N)),
                  pl.BlockSpec((4 * X_HEADS, MEM_LEN, LANES), lambda i: (0, i // tiles_per_batch, 0)),
                  _resident((D_MODEL, D_MODEL))],
        out_specs=pl.BlockSpec((tm, D_MODEL), lambda i: (i, 0)),
        out_shape=jax.ShapeDtypeStruct((N_TOK, D_MODEL), F32),
        compiler_params=_params(("parallel",), VMEM_LIMIT),
        name="xattn",
    )(x, g.reshape(1, D_MODEL), w_q.astype(BF), q_colscale.reshape(1, D_MODEL).astype(F32),
      _group_ones(X_HEAD_DIM), kv, w_o.astype(BF))


def _alibi(n_heads):
    return jnp.asarray(2.0 ** (-8.0 * np.arange(1, n_heads + 1) / n_heads), dtype=F32)


def _even_mixer(x, norm_g, w_in, q_gain, k_gain, sinks, w_out):
    hd = HEAD_DIM
    qa, ka, va, rest = w_in[:, :512], w_in[:, 512:640], w_in[:, 640:768], w_in[:, 768:]
    dup = lambda w: jnp.concatenate([w[:, :hd], w[:, :hd], w[:, hd:], w[:, hd:]], axis=1)
    w_aug = jnp.concatenate([qa, dup(ka), dup(va), rest], axis=1)
    scale = hd ** -0.5
    ones = lambda n: jnp.ones((n,), F32)
    cs = jnp.concatenate([jnp.tile(q_gain, A_Q_HEADS) * scale, jnp.tile(k_gain, 4), ones(256),
                          ones(512) * scale, ones(1024)])
    normed = [True, True, True] + [False] * 7
    p = _proj(x, norm_g, w_aug, cs, normed, hd)
    o_a = _banded(p, _alibi(A_Q_HEADS), sinks.astype(F32), d=1, q_blk=(4, 0), k_blk=(2, 2), v_blk=(2, 3),
                  kv_div=2, max_dist=A_WINDOW - 1, want_lse=False)
    o_b = _stick_breaking(p, 8, 12, 16, B_HEADS // 2)
    return _oproj(x, [o_a, o_b], w_out)


def _odd_mixer(x, norm_g, w_in, q_gain, k_gain, w_out):
    hd = HEAD_DIM
    cs = jnp.concatenate([jnp.tile(q_gain, C_HEADS) * hd ** -0.5, jnp.tile(k_gain, C_HEADS),
                          jnp.ones((C_HEADS * hd,), F32)])
    normed = [True] * 8 + [False] * 4
    p = _proj(x, norm_g, w_in, cs, normed, hd)
    slopes = _alibi(C_HEADS)
    os, ls = [], []
    for window, dil in C_PATTERNS:
        o, lse = _banded(p, slopes, None, d=dil, q_blk=(8, 0), k_blk=(8, 1), v_blk=(8, 2),
                         kv_div=1, max_dist=window // dil, want_lse=True)
        os.append(o)
        ls.append(lse)
    return _oproj(x, [_combine(os, ls)], w_out)


def _cross_attention(x, mem2d, norm_g, mem_g, w_q, w_kv, q_gain, k_gain, w_o):
    cs_kv = jnp.concatenate([jnp.tile(k_gain, X_HEADS), jnp.ones((D_MODEL,), F32)])
    kv = _proj(mem2d, mem_g, w_kv, cs_kv, [True] * 4 + [False] * 4, X_HEAD_DIM, tm=256)
    cs_q = jnp.tile(q_gain, X_HEADS) * X_HEAD_DIM ** -0.5
    return _xattn(x, norm_g, w_q, cs_q, kv, w_o)


def kernel(x, mem, ffn1_norm, ffn1_w_gu, ffn1_w_down, mix_norm, ev_w_in, ev_q_gain, ev_k_gain, ev_sinks, ev_w_out, od_w_in, od_q_gain, od_k_gain, od_w_out, xa_norm, xa_mem_norm, xa_w_q, xa_w_kv, xa_q_gain, xa_k_gain, xa_w_o, ffn2_norm, ffn2_w_gu, ffn2_w_down):
    x = x.reshape(N_TOK, D_MODEL)
    mem2d = mem.reshape(BATCH * MEM_LEN, D_MODEL)
    for layer in range(DEPTH):
        j = layer // 2
        x = _ffn(x, ffn1_norm[layer], ffn1_w_gu[layer], ffn1_w_down[layer])
        if layer % 2 == 0:
            x = _even_mixer(x, mix_norm[layer], ev_w_in[j], ev_q_gain[j], ev_k_gain[j], ev_sinks[j], ev_w_out[j])
        else:
            x = _odd_mixer(x, mix_norm[layer], od_w_in[j], od_q_gain[j], od_k_gain[j], od_w_out[j])
        x = _cross_attention(x, mem2d, xa_norm[layer], xa_mem_norm[layer], xa_w_q[layer], xa_w_kv[layer],
                             xa_q_gain[layer], xa_k_gain[layer], xa_w_o[layer])
        x = _ffn(x, ffn2_norm[layer], ffn2_w_gu[layer], ffn2_w_down[layer])
    return x.reshape(BATCH, SEQ, D_MODEL)
```

```python
import functools

import numpy as np
import jax
import jax.numpy as jnp
from jax import lax
from jax.experimental import pallas as pl
from jax.experimental.pallas import tpu as pltpu

D_MODEL = 1024
BATCH = 4
SEQ = 4096
N_TOK = BATCH * SEQ
DEPTH = 2
HEAD_DIM = 64
BLOCK = 128
A_Q_HEADS = 8
A_KV_HEADS = 2
A_WINDOW = 128
B_HEADS = 8
C_HEADS = 16
C_PATTERNS = ((128, 1), (512, 4), (2048, 16))
MEM_LEN = 256
X_HEADS = 4
X_HEAD_DIM = D_MODEL // X_HEADS
D_FF = 2816
RMS_EPS = 1e-6

LANES = 128
MXU_N = 256
VMEM_LIMIT = 56 * 1024 * 1024

BF = jnp.bfloat16
F32 = jnp.float32
NT_DIMS = (((1,), (1,)), ((), ()))


def _params(sem, vmem=None):
    return pltpu.CompilerParams(dimension_semantics=sem, vmem_limit_bytes=vmem)


def _resident(shape):
    nd = len(shape)
    return pl.BlockSpec(shape, lambda *_: (0,) * nd, pipeline_mode=pl.Buffered(1))


def _rms(xv, g):
    ms = jnp.mean(xv * xv, axis=-1, keepdims=True)
    return xv * lax.rsqrt(ms + RMS_EPS) * g


def _ffn_kernel(x_ref, g_ref, wgu_ref, wd_ref, o_ref, *, tf):
    xv = x_ref[...]
    h = _rms(xv, g_ref[...]).astype(BF)
    acc = jnp.zeros_like(xv)
    for c in range(D_FF // tf):
        gate = jnp.dot(h, wgu_ref[:, c * tf:(c + 1) * tf], preferred_element_type=F32)
        up = jnp.dot(h, wgu_ref[:, D_FF + c * tf:D_FF + (c + 1) * tf], preferred_element_type=F32)
        act = (gate * jax.nn.sigmoid(gate) * up).astype(BF)
        acc = acc + jnp.dot(act, wd_ref[c * tf:(c + 1) * tf, :], preferred_element_type=F32)
    o_ref[...] = xv + 0.5 * acc


def _ffn(x, g, w_gu, w_down, *, tm=512, tf=1408):
    return pl.pallas_call(
        functools.partial(_ffn_kernel, tf=tf),
        grid=(N_TOK // tm,),
        in_specs=[pl.BlockSpec((tm, D_MODEL), lambda i: (i, 0)),
                  _resident((1, D_MODEL)),
                  _resident((D_MODEL, 2 * D_FF)),
                  _resident((D_FF, D_MODEL))],
        out_specs=pl.BlockSpec((tm, D_MODEL), lambda i: (i, 0)),
        out_shape=jax.ShapeDtypeStruct((N_TOK, D_MODEL), F32),
        compiler_params=_params(("parallel",), VMEM_LIMIT),
        name="ffn",
    )(x, g.reshape(1, D_MODEL), w_gu.astype(BF), w_down.astype(BF))


def _group_ones(gs):
    idx = np.arange(MXU_N) // gs
    return jnp.asarray(idx[:, None] == idx[None, :], dtype=BF)


def _proj_kernel(x_ref, g_ref, w_ref, cs_ref, bd_ref, o_ref, *, normed, gs):
    h = _rms(x_ref[...], g_ref[...]).astype(BF)
    for j, is_normed in enumerate(normed):
        cols = slice(MXU_N * j, MXU_N * (j + 1))
        acc = jnp.dot(h, w_ref[:, cols], preferred_element_type=F32)
        y = acc * cs_ref[:, cols]
        if is_normed:
            ss = jnp.dot((acc * acc).astype(BF), bd_ref[...], preferred_element_type=F32)
            y = y * lax.rsqrt(ss * (1.0 / gs) + RMS_EPS)
        o_ref[2 * j] = y[:, :LANES].astype(BF)
        o_ref[2 * j + 1] = y[:, LANES:].astype(BF)


def _proj(x, g, w, colscale, normed, gs, *, tm=512):
    rows = x.shape[0]
    wout = w.shape[1]
    assert wout == MXU_N * len(normed)
    c = wout // LANES
    return pl.pallas_call(
        functools.partial(_proj_kernel, normed=tuple(normed), gs=gs),
        grid=(rows // tm,),
        in_specs=[pl.BlockSpec((tm, D_MODEL), lambda i: (i, 0)),
                  _resident((1, D_MODEL)),
                  _resident((D_MODEL, wout)),
                  _resident((1, wout)),
                  _resident((MXU_N, MXU_N))],
        out_specs=pl.BlockSpec((c, tm, LANES), lambda i: (0, i, 0)),
        out_shape=jax.ShapeDtypeStruct((c, rows, LANES), BF),
        compiler_params=_params(("parallel",), VMEM_LIMIT),
        name="proj",
    )(x, g.reshape(1, D_MODEL), w.astype(BF), colscale.reshape(1, wout).astype(F32), _group_ones(gs))


BANDED_AHEAD = 2


def _banded_kernel(*refs, n_pairs, kv_div, step, max_dist, has_sinks, want_lse):
    refs = list(refs)
    slopes_ref = refs.pop(0)
    sinks_ref = refs.pop(0) if has_sinks else None
    q_ref, kp_ref, kc_ref, vp_ref, vc_ref, o_ref = refs[:6]
    lse_ref = refs[6] if want_lse else None

    n = pl.program_id(2)
    row = lax.broadcasted_iota(jnp.int32, (BLOCK, 2 * BLOCK), 0)
    col = lax.broadcasted_iota(jnp.int32, (BLOCK, 2 * BLOCK), 1)
    dist = row + BLOCK - col
    valid = (dist >= 0) & (dist <= max_dist) & ((col >= BLOCK) | (n > 0))
    negmask = jnp.where(valid, 0.0, -jnp.inf)
    distf = dist.astype(F32) * float(step)
    lo = lax.broadcasted_iota(jnp.int32, (BLOCK, LANES), 1) < HEAD_DIM

    def scores(h):
        p, g = h // 2, (h // 2) // kv_div
        q2 = q_ref[p, 0].astype(F32)
        qm = jnp.where(lo, q2, 0.0) if h % 2 == 0 else jnp.where(lo, 0.0, q2)
        qm = qm.astype(BF)
        return jnp.concatenate(
            [lax.dot_general(qm, kp_ref[g, 0], NT_DIMS, preferred_element_type=F32),
             lax.dot_general(qm, kc_ref[g, 0], NT_DIMS, preferred_element_type=F32)], axis=1)

    def finish(h, s):
        g = (h // 2) // kv_div
        s = s - slopes_ref[h] * distf + negmask
        m = jnp.max(s, axis=1, keepdims=True)
        if has_sinks:
            m = jnp.maximum(m, sinks_ref[h])
        pe = jnp.exp(s - m)
        l = jnp.sum(pe, axis=1, keepdims=True)
        if has_sinks:
            l = l + jnp.exp(sinks_ref[h] - m)
        pb = pe.astype(BF)
        pv = (jnp.dot(pb[:, :BLOCK], vp_ref[g, 0], preferred_element_type=F32)
              + jnp.dot(pb[:, BLOCK:], vc_ref[g, 0], preferred_element_type=F32))
        return pv / l, m + jnp.log(l)

    n_heads = 2 * n_pairs
    pending = {h: scores(h) for h in range(min(BANDED_AHEAD, n_heads))}
    res = {}
    for h in range(n_heads):
        if h + BANDED_AHEAD < n_heads:
            pending[h + BANDED_AHEAD] = scores(h + BANDED_AHEAD)
        res[h] = finish(h, pending.pop(h))
        if h % 2 == 1:
            p = h // 2
            (o0, l0), (o1, l1) = res.pop(h - 1), res.pop(h)
            o_ref[p, 0] = jnp.where(lo, o0, o1).astype(BF)
            if want_lse:
                lse_ref[p, 0] = jnp.where(lo, jnp.broadcast_to(l0, (BLOCK, LANES)),
                                          jnp.broadcast_to(l1, (BLOCK, LANES)))


def _banded(qkv, slopes, sinks, *, d, q_blk, k_blk, v_blk, kv_div, max_dist, want_lse):
    c = qkv.shape[0]
    l_sub = SEQ // d
    nb = l_sub // BLOCK
    view = qkv.reshape(c, BATCH, l_sub, d * LANES)
    n_pairs = q_blk[0]

    def spec(blk, prev):
        size, idx = blk
        if prev:
            return pl.BlockSpec((size, 1, BLOCK, LANES),
                                lambda b, r, n: (idx, b, jnp.maximum(n - 1, 0), r))
        return pl.BlockSpec((size, 1, BLOCK, LANES), lambda b, r, n: (idx, b, n, r))

    smem = pl.BlockSpec(memory_space=pltpu.SMEM)
    has_sinks = sinks is not None
    in_specs = [smem] + ([smem] if has_sinks else []) + [
        spec(q_blk, False), spec(k_blk, True), spec(k_blk, False), spec(v_blk, True), spec(v_blk, False)]
    out_spec = pl.BlockSpec((n_pairs, 1, BLOCK, LANES), lambda b, r, n: (0, b, n, r))
    o_shape = jax.ShapeDtypeStruct((n_pairs, BATCH, l_sub, d * LANES), BF)
    out_specs, out_shape = out_spec, o_shape
    if want_lse:
        out_specs = [out_spec, out_spec]
        out_shape = [o_shape, jax.ShapeDtypeStruct((n_pairs, BATCH, l_sub, d * LANES), F32)]
    args = [slopes] + ([sinks] if has_sinks else []) + [view] * 5
    out = pl.pallas_call(
        functools.partial(_banded_kernel, n_pairs=n_pairs, kv_div=kv_div, step=d, max_dist=max_dist,
                          has_sinks=has_sinks, want_lse=want_lse),
        grid=(BATCH, d, nb),
        in_specs=in_specs, out_specs=out_specs, out_shape=out_shape,
        compiler_params=_params(("parallel", "parallel", "parallel")),
        name=f"banded_d{d}",
    )(*args)
    if want_lse:
        return out[0].reshape(n_pairs, N_TOK, LANES), out[1].reshape(n_pairs, N_TOK, LANES)
    return out.reshape(n_pairs, N_TOK, LANES)


def _combine_kernel(o1, o2, o3, l1, l2, l3, out):
    a, b, c = l1[...], l2[...], l3[...]
    m = jnp.maximum(jnp.maximum(a, b), c)
    ea, eb, ec = jnp.exp(a - m), jnp.exp(b - m), jnp.exp(c - m)
    num = ea * o1[...].astype(F32) + eb * o2[...].astype(F32) + ec * o3[...].astype(F32)
    out[...] = (num / (ea + eb + ec)).astype(BF)


def _combine(os, ls, *, tm=256):
    c = os[0].shape[0]
    spec = pl.BlockSpec((c, tm, LANES), lambda i: (0, i, 0))
    return pl.pallas_call(
        _combine_kernel, grid=(N_TOK // tm,),
        in_specs=[spec] * 6, out_specs=spec,
        out_shape=jax.ShapeDtypeStruct((c, N_TOK, LANES), BF),
        compiler_params=_params(("parallel",), VMEM_LIMIT),
        name="combine",
    )(*os, *ls)


SB_GROUP = 512
SB_SUB = MXU_N


def _sb_kernel(q_ref, k_ref, v_ref, uo_ref, o_ref):
    i = pl.program_id(2)
    q2 = q_ref[0].astype(F32)
    uo = uo_ref[...]
    lo = lax.broadcasted_iota(jnp.int32, (BLOCK, LANES), 1) < HEAD_DIM
    qms = [jnp.where(lo, q2, 0.0).astype(BF), jnp.where(lo, 0.0, q2).astype(BF)]
    rel = (lax.broadcasted_iota(jnp.int32, (BLOCK, SB_SUB), 1)
           - lax.broadcasted_iota(jnp.int32, (BLOCK, SB_SUB), 0))

    n_sub = SB_GROUP // SB_SUB
    subs = list(reversed(range(n_sub)))

    def scores(g):
        off = pl.multiple_of(g * SB_GROUP, SB_GROUP)
        kt = k_ref[0, pl.ds(off, SB_GROUP), :]
        return tuple(lax.dot_general(qm, kt, NT_DIMS, preferred_element_type=F32) for qm in qms)

    def group(g, carry, masked):
        zs, cs, o = carry[0], list(carry[1]), carry[2]
        z_next = scores(jnp.maximum(g - 1, 0))
        off = pl.multiple_of(g * SB_GROUP, SB_GROUP)
        lkb, e, strict = {}, {}, {}
        for sub in subs:
            if masked:
                strict[sub] = rel < (i * BLOCK - off - sub * SB_SUB)
            for hh in range(2):
                z = zs[hh][:, sub * SB_SUB:(sub + 1) * SB_SUB]
                sp = jnp.maximum(z, 0.0) + jnp.log(1.0 + jnp.exp(-jnp.abs(z)))
                lk = -sp
                if masked:
                    lk = jnp.where(strict[sub], lk, 0.0)
                lkb[hh, sub] = lk.astype(BF)
                e[hh, sub] = z - sp
        w = {key: jnp.dot(val, uo, preferred_element_type=F32) for key, val in lkb.items()}
        parts = {}
        for sub in subs:
            for hh in range(2):
                wk = w[hh, sub]
                a = jnp.exp(e[hh, sub] + jnp.concatenate([cs[hh]] * (SB_SUB // LANES), axis=1) + wk[:, :SB_SUB])
                if masked:
                    a = jnp.where(strict[sub], a, 0.0)
                parts[hh, sub] = a.astype(BF)
                cs[hh] = cs[hh] + wk[:, SB_SUB:]
        vt = v_ref[0, pl.ds(off, SB_GROUP), :]
        pvs = [jnp.dot(jnp.concatenate([parts[hh, sub] for sub in range(n_sub)], axis=1), vt,
                       preferred_element_type=F32) for hh in range(2)]
        return z_next, tuple(cs), o + jnp.where(lo, pvs[0], pvs[1])

    zero = jnp.zeros((BLOCK, LANES), F32)
    top = i // (SB_GROUP // BLOCK)
    carry = group(top, (scores(top), (zero, zero), zero), True)
    carry = lax.fori_loop(0, top, lambda j, cr: group(top - 1 - j, cr, False), carry)
    o_ref[0] = carry[2].astype(BF)


def _stick_breaking(qkv, q_c0, k_c0, v_c0, n_pairs):
    idx = np.arange(SB_SUB)
    uo = jnp.asarray(np.concatenate([idx[:, None] > idx[None, :], np.ones((SB_SUB, LANES), bool)], axis=1),
                     dtype=BF)
    nb = SEQ // BLOCK
    return pl.pallas_call(
        _sb_kernel,
        grid=(BATCH, n_pairs, nb),
        in_specs=[pl.BlockSpec((1, BLOCK, LANES), lambda b, p, i: (q_c0 + p, b * nb + i, 0)),
                  pl.BlockSpec((1, SEQ, LANES), lambda b, p, i: (k_c0 + p, b, 0)),
                  pl.BlockSpec((1, SEQ, LANES), lambda b, p, i: (v_c0 + p, b, 0)),
                  _resident((SB_SUB, SB_SUB + LANES))],
        out_specs=pl.BlockSpec((1, BLOCK, LANES), lambda b, p, i: (p, b * nb + i, 0)),
        out_shape=jax.ShapeDtypeStruct((n_pairs, N_TOK, LANES), BF),
        compiler_params=_params(("parallel", "parallel", "parallel")),
        name="stick_breaking",
    )(qkv, qkv, qkv, uo)


def _oproj_kernel(*refs):
    x_ref, w_ref, out_ref = refs[0], refs[-2], refs[-1]
    parts = [r[c] for r in refs[1:-2] for c in range(r.shape[0])]
    o = jnp.concatenate(parts, axis=1)
    out_ref[...] = x_ref[...] + jnp.dot(o, w_ref[...], preferred_element_type=F32)


def _oproj(x, heads, w, *, tm=512):
    in_specs = [pl.BlockSpec((tm, D_MODEL), lambda i: (i, 0))]
    in_specs += [pl.BlockSpec((h.shape[0], tm, LANES), lambda i: (0, i, 0)) for h in heads]
    in_specs += [_resident(w.shape)]
    return pl.pallas_call(
        _oproj_kernel, grid=(N_TOK // tm,),
        in_specs=in_specs,
        out_specs=pl.BlockSpec((tm, D_MODEL), lambda i: (i, 0)),
        out_shape=jax.ShapeDtypeStruct((N_TOK, D_MODEL), F32),
        compiler_params=_params(("parallel",), VMEM_LIMIT),
        name="oproj",
    )(x, *heads, w.astype(BF))


def _xattn_kernel(x_ref, g_ref, wq_ref, cs_ref, bd_ref, kv_ref, wo_ref, o_ref):
    xv = x_ref[...]
    h = _rms(xv, g_ref[...]).astype(BF)
    outs = []
    for hd in range(X_HEADS):
        cols = slice(X_HEAD_DIM * hd, X_HEAD_DIM * (hd + 1))
        acc = jnp.dot(h, wq_ref[:, cols], preferred_element_type=F32)
        ss = jnp.dot((acc * acc).astype(BF), bd_ref[...], preferred_element_type=F32)
        q = (acc * cs_ref[:, cols] * lax.rsqrt(ss * (1.0 / X_HEAD_DIM) + RMS_EPS)).astype(BF)
        kh = jnp.concatenate([kv_ref[2 * hd], kv_ref[2 * hd + 1]], axis=1)
        vh = jnp.concatenate([kv_ref[2 * X_HEADS + 2 * hd], kv_ref[2 * X_HEADS + 2 * hd + 1]], axis=1)
        s = lax.dot_general(q, kh, NT_DIMS, preferred_element_type=F32)
        m = jnp.max(s, axis=1, keepdims=True)
        pe = jnp.exp(s - m)
        l = jnp.sum(pe, axis=1, keepdims=True)
        outs.append((jnp.dot(pe.astype(BF), vh, preferred_element_type=F32) / l).astype(BF))
    o = jnp.concatenate(outs, axis=1)
    o_ref[...] = xv + jnp.dot(o, wo_ref[...], preferred_element_type=F32)


def _xattn(x, g, w_q, q_colscale, kv, w_o, *, tm=512):
    assert X_HEAD_DIM == MXU_N
    tiles_per_batch = SEQ // tm
    return pl.pallas_call(
        _xattn_kernel, grid=(N_TOK // tm,),
        in_specs=[pl.BlockSpec((tm, D_MODEL), lambda i: (i, 0)),
                  _resident((1, D_MODEL)),
                  _resident((D_MODEL, D_MODEL)),
                  _resident((1, D_MODEL)),
                  _resident((MXU_N, MXU_N)),
                  pl.BlockSpec((4 * X_HEADS, MEM_LEN, LANES), lambda i: (0, i // tiles_per_batch, 0)),
                  _resident((D_MODEL, D_MODEL))],
        out_specs=pl.BlockSpec((tm, D_MODEL), lambda i: (i, 0)),
        out_shape=jax.ShapeDtypeStruct((N_TOK, D_MODEL), F32),
        compiler_params=_params(("parallel",), VMEM_LIMIT),
        name="xattn",
    )(x, g.reshape(1, D_MODEL), w_q.astype(BF), q_colscale.reshape(1, D_MODEL).astype(F32),
      _group_ones(X_HEAD_DIM), kv, w_o.astype(BF))


def _alibi(n_heads):
    return jnp.asarray(2.0 ** (-8.0 * np.arange(1, n_heads + 1) / n_heads), dtype=F32)


def _even_mixer(x, norm_g, w_in, q_gain, k_gain, sinks, w_out):
    hd = HEAD_DIM
    qa, ka, va, rest = w_in[:, :512], w_in[:, 512:640], w_in[:, 640:768], w_in[:, 768:]
    dup = lambda w: jnp.concatenate([w[:, :hd], w[:, :hd], w[:, hd:], w[:, hd:]], axis=1)
    w_aug = jnp.concatenate([qa, dup(ka), dup(va), rest], axis=1)
    scale = hd ** -0.5
    ones = lambda n: jnp.ones((n,), F32)
    cs = jnp.concatenate([jnp.tile(q_gain, A_Q_HEADS) * scale, jnp.tile(k_gain, 4), ones(256),
                          ones(512) * scale, ones(1024)])
    normed = [True, True, True] + [False] * 7
    p = _proj(x, norm_g, w_aug, cs, normed, hd)
    o_a = _banded(p, _alibi(A_Q_HEADS), sinks.astype(F32), d=1, q_blk=(4, 0), k_blk=(2, 2), v_blk=(2, 3),
                  kv_div=2, max_dist=A_WINDOW - 1, want_lse=False)
    o_b = _stick_breaking(p, 8, 12, 16, B_HEADS // 2)
    return _oproj(x, [o_a, o_b], w_out)


def _odd_mixer(x, norm_g, w_in, q_gain, k_gain, w_out):
    hd = HEAD_DIM
    cs = jnp.concatenate([jnp.tile(q_gain, C_HEADS) * hd ** -0.5, jnp.tile(k_gain, C_HEADS),
                          jnp.ones((C_HEADS * hd,), F32)])
    normed = [True] * 8 + [False] * 4
    p = _proj(x, norm_g, w_in, cs, normed, hd)
    slopes = _alibi(C_HEADS)
    os, ls = [], []
    for window, dil in C_PATTERNS:
        o, lse = _banded(p, slopes, None, d=dil, q_blk=(8, 0), k_blk=(8, 1), v_blk=(8, 2),
                         kv_div=1, max_dist=window // dil, want_lse=True)
        os.append(o)
        ls.append(lse)
    return _oproj(x, [_combine(os, ls)], w_out)


def _cross_attention(x, mem2d, norm_g, mem_g, w_q, w_kv, q_gain, k_gain, w_o):
    cs_kv = jnp.concatenate([jnp.tile(k_gain, X_HEADS), jnp.ones((D_MODEL,), F32)])
    kv = _proj(mem2d, mem_g, w_kv, cs_kv, [True] * 4 + [False] * 4, X_HEAD_DIM, tm=256)
    cs_q = jnp.tile(q_gain, X_HEADS) * X_HEAD_DIM ** -0.5
    return _xattn(x, norm_g, w_q, cs_q, kv, w_o)


def kernel(x, mem, ffn1_norm, ffn1_w_gu, ffn1_w_down, mix_norm, ev_w_in, ev_q_gain, ev_k_gain, ev_sinks, ev_w_out, od_w_in, od_q_gain, od_k_gain, od_w_out, xa_norm, xa_mem_norm, xa_w_q, xa_w_kv, xa_q_gain, xa_k_gain, xa_w_o, ffn2_norm, ffn2_w_gu, ffn2_w_down):
    x = x.reshape(N_TOK, D_MODEL)
    mem2d = mem.reshape(BATCH * MEM_LEN, D_MODEL)
    for layer in range(DEPTH):
        j = layer // 2
        x = _ffn(x, ffn1_norm[layer], ffn1_w_gu[layer], ffn1_w_down[layer])
        if layer % 2 == 0:
            x = _even_mixer(x, mix_norm[layer], ev_w_in[j], ev_q_gain[j], ev_k_gain[j], ev_sinks[j], ev_w_out[j])
        else:
            x = _odd_mixer(x, mix_norm[layer], od_w_in[j], od_q_gain[j], od_k_gain[j], od_w_out[j])
        x = _cross_attention(x, mem2d, xa_norm[layer], xa_mem_norm[layer], xa_w_q[layer], xa_w_kv[layer],
                             xa_q_gain[layer], xa_k_gain[layer], xa_w_o[layer])
        x = _ffn(x, ffn2_norm[layer], ffn2_w_gu[layer], ffn2_w_down[layer])
    return x.reshape(BATCH, SEQ, D_MODEL)
```

```python
import functools

import numpy as np
import jax
import jax.numpy as jnp
from jax import lax
from jax.experimental import pallas as pl
from jax.experimental.pallas import tpu as pltpu

D_MODEL = 1024
BATCH = 4
SEQ = 4096
N_TOK = BATCH * SEQ
DEPTH = 2
HEAD_DIM = 64
BLOCK = 128
A_Q_HEADS = 8
A_KV_HEADS = 2
A_WINDOW = 128
B_HEADS = 8
C_HEADS = 16
C_PATTERNS = ((128, 1), (512, 4), (2048, 16))
MEM_LEN = 256
X_HEADS = 4
X_HEAD_DIM = D_MODEL // X_HEADS
D_FF = 2816
RMS_EPS = 1e-6

LANES = 128
MXU_N = 256
VMEM_LIMIT = 56 * 1024 * 1024

BF = jnp.bfloat16
F32 = jnp.float32
NT_DIMS = (((1,), (1,)), ((), ()))


def _params(sem, vmem=None):
    return pltpu.CompilerParams(dimension_semantics=sem, vmem_limit_bytes=vmem)


def _resident(shape):
    nd = len(shape)
    return pl.BlockSpec(shape, lambda *_: (0,) * nd, pipeline_mode=pl.Buffered(1))


def _rms(xv, g):
    ms = jnp.mean(xv * xv, axis=-1, keepdims=True)
    return xv * lax.rsqrt(ms + RMS_EPS) * g


def _ffn_kernel(x_ref, g_ref, wgu_ref, wd_ref, o_ref, *, tf):
    xv = x_ref[...]
    h = _rms(xv, g_ref[...]).astype(BF)
    acc = jnp.zeros_like(xv)
    for c in range(D_FF // tf):
        gate = jnp.dot(h, wgu_ref[:, c * tf:(c + 1) * tf], preferred_element_type=F32)
        up = jnp.dot(h, wgu_ref[:, D_FF + c * tf:D_FF + (c + 1) * tf], preferred_element_type=F32)
        act = (gate * jax.nn.sigmoid(gate) * up).astype(BF)
        acc = acc + jnp.dot(act, wd_ref[c * tf:(c + 1) * tf, :], preferred_element_type=F32)
    o_ref[...] = xv + 0.5 * acc


def _ffn(x, g, w_gu, w_down, *, tm=512, tf=1408):
    return pl.pallas_call(
        functools.partial(_ffn_kernel, tf=tf),
        grid=(N_TOK // tm,),
        in_specs=[pl.BlockSpec((tm, D_MODEL), lambda i: (i, 0)),
                  _resident((1, D_MODEL)),
                  _resident((D_MODEL, 2 * D_FF)),
                  _resident((D_FF, D_MODEL))],
        out_specs=pl.BlockSpec((tm, D_MODEL), lambda i: (i, 0)),
        out_shape=jax.ShapeDtypeStruct((N_TOK, D_MODEL), F32),
        compiler_params=_params(("parallel",), VMEM_LIMIT),
        name="ffn",
    )(x, g.reshape(1, D_MODEL), w_gu.astype(BF), w_down.astype(BF))


def _group_ones(gs):
    idx = np.arange(MXU_N) // gs
    return jnp.asarray(idx[:, None] == idx[None, :], dtype=BF)


def _proj_kernel(x_ref, g_ref, w_ref, cs_ref, bd_ref, o_ref, *, normed, gs):
    h = _rms(x_ref[...], g_ref[...]).astype(BF)
    for j, is_normed in enumerate(normed):
        cols = slice(MXU_N * j, MXU_N * (j + 1))
        acc = jnp.dot(h, w_ref[:, cols], preferred_element_type=F32)
        y = acc * cs_ref[:, cols]
        if is_normed:
            ss = jnp.dot((acc * acc).astype(BF), bd_ref[...], preferred_element_type=F32)
            y = y * lax.rsqrt(ss * (1.0 / gs) + RMS_EPS)
        o_ref[2 * j] = y[:, :LANES].astype(BF)
        o_ref[2 * j + 1] = y[:, LANES:].astype(BF)


def _proj(x, g, w, colscale, normed, gs, *, tm=512):
    rows = x.shape[0]
    wout = w.shape[1]
    assert wout == MXU_N * len(normed)
    c = wout // LANES
    return pl.pallas_call(
        functools.partial(_proj_kernel, normed=tuple(normed), gs=gs),
        grid=(rows // tm,),
        in_specs=[pl.BlockSpec((tm, D_MODEL), lambda i: (i, 0)),
                  _resident((1, D_MODEL)),
                  _resident((D_MODEL, wout)),
                  _resident((1, wout)),
                  _resident((MXU_N, MXU_N))],
        out_specs=pl.BlockSpec((c, tm, LANES), lambda i: (0, i, 0)),
        out_shape=jax.ShapeDtypeStruct((c, rows, LANES), BF),
        compiler_params=_params(("parallel",), VMEM_LIMIT),
        name="proj",
    )(x, g.reshape(1, D_MODEL), w.astype(BF), colscale.reshape(1, wout).astype(F32), _group_ones(gs))


def _banded_kernel(*refs, n_pairs, kv_div, step, max_dist, has_sinks, want_lse):
    refs = list(refs)
    slopes_ref = refs.pop(0)
    sinks_ref = refs.pop(0) if has_sinks else None
    q_ref, kp_ref, kc_ref, vp_ref, vc_ref, o_ref = refs[:6]
    lse_ref = refs[6] if want_lse else None

    n = pl.program_id(2)
    row = lax.broadcasted_iota(jnp.int32, (BLOCK, 2 * BLOCK), 0)
    col = lax.broadcasted_iota(jnp.int32, (BLOCK, 2 * BLOCK), 1)
    dist = row + BLOCK - col
    valid = (dist >= 0) & (dist <= max_dist) & ((col >= BLOCK) | (n > 0))
    negmask = jnp.where(valid, 0.0, -jnp.inf)
    distf = dist.astype(F32) * float(step)
    lo = lax.broadcasted_iota(jnp.int32, (BLOCK, LANES), 1) < HEAD_DIM

    n_groups = n_pairs // kv_div
    heads_per_group = 2 * kv_div
    scores = []
    for g in range(n_groups):
        parts = []
        for p in range(g * kv_div, (g + 1) * kv_div):
            q2 = q_ref[p, 0].astype(F32)
            parts += [jnp.where(lo, q2, 0.0), jnp.where(lo, 0.0, q2)]
        q_stack = jnp.concatenate(parts, axis=0).astype(BF)
        scores.append(jnp.concatenate(
            [lax.dot_general(q_stack, kp_ref[g, 0], NT_DIMS, preferred_element_type=F32),
             lax.dot_general(q_stack, kc_ref[g, 0], NT_DIMS, preferred_element_type=F32)], axis=1))
    soft = []
    for g in range(n_groups):
        res = []
        for j in range(heads_per_group):
            h = g * heads_per_group + j
            s = scores[g][j * BLOCK:(j + 1) * BLOCK] - slopes_ref[h] * distf + negmask
            m = jnp.max(s, axis=1, keepdims=True)
            if has_sinks:
                m = jnp.maximum(m, sinks_ref[h])
            pe = jnp.exp(s - m)
            l = jnp.sum(pe, axis=1, keepdims=True)
            if has_sinks:
                l = l + jnp.exp(sinks_ref[h] - m)
            res.append((pe.astype(BF), l, m + jnp.log(l)))
        soft.append(res)
    for g in range(n_groups):
        pb = jnp.concatenate([r[0] for r in soft[g]], axis=0)
        pv = (jnp.dot(pb[:, :BLOCK], vp_ref[g, 0], preferred_element_type=F32)
              + jnp.dot(pb[:, BLOCK:], vc_ref[g, 0], preferred_element_type=F32))
        for jp in range(kv_div):
            p = g * kv_div + jp
            (_, l0, lse0), (_, l1, lse1) = soft[g][2 * jp], soft[g][2 * jp + 1]
            o0 = pv[(2 * jp) * BLOCK:(2 * jp + 1) * BLOCK] / l0
            o1 = pv[(2 * jp + 1) * BLOCK:(2 * jp + 2) * BLOCK] / l1
            o_ref[p, 0] = jnp.where(lo, o0, o1).astype(BF)
            if want_lse:
                lse_ref[p, 0] = jnp.where(lo, jnp.broadcast_to(lse0, (BLOCK, LANES)),
                                          jnp.broadcast_to(lse1, (BLOCK, LANES)))


def _banded(qkv, slopes, sinks, *, d, q_blk, k_blk, v_blk, kv_div, max_dist, want_lse):
    c = qkv.shape[0]
    l_sub = SEQ // d
    nb = l_sub // BLOCK
    view = qkv.reshape(c, BATCH, l_sub, d * LANES)
    n_pairs = q_blk[0]

    def spec(blk, prev):
        size, idx = blk
        if prev:
            return pl.BlockSpec((size, 1, BLOCK, LANES),
                                lambda b, r, n: (idx, b, jnp.maximum(n - 1, 0), r))
        return pl.BlockSpec((size, 1, BLOCK, LANES), lambda b, r, n: (idx, b, n, r))

    smem = pl.BlockSpec(memory_space=pltpu.SMEM)
    has_sinks = sinks is not None
    in_specs = [smem] + ([smem] if has_sinks else []) + [
        spec(q_blk, False), spec(k_blk, True), spec(k_blk, False), spec(v_blk, True), spec(v_blk, False)]
    out_spec = pl.BlockSpec((n_pairs, 1, BLOCK, LANES), lambda b, r, n: (0, b, n, r))
    o_shape = jax.ShapeDtypeStruct((n_pairs, BATCH, l_sub, d * LANES), BF)
    out_specs, out_shape = out_spec, o_shape
    if want_lse:
        out_specs = [out_spec, out_spec]
        out_shape = [o_shape, jax.ShapeDtypeStruct((n_pairs, BATCH, l_sub, d * LANES), F32)]
    args = [slopes] + ([sinks] if has_sinks else []) + [view] * 5
    out = pl.pallas_call(
        functools.partial(_banded_kernel, n_pairs=n_pairs, kv_div=kv_div, step=d, max_dist=max_dist,
                          has_sinks=has_sinks, want_lse=want_lse),
        grid=(BATCH, d, nb),
        in_specs=in_specs, out_specs=out_specs, out_shape=out_shape,
        compiler_params=_params(("parallel", "parallel", "parallel")),
        name=f"banded_d{d}",
    )(*args)
    if want_lse:
        return out[0].reshape(n_pairs, N_TOK, LANES), out[1].reshape(n_pairs, N_TOK, LANES)
    return out.reshape(n_pairs, N_TOK, LANES)


DIL_ORDER = tuple(sorted(C_PATTERNS, key=lambda wd: -wd[1]))
DIL_UNROLL = 4
DIL_CONVERT_ROWS = 512


def _dilated_kernel(slopes_ref, q_ref, k_ref, v_ref, o_ref, q0_s, q1_s, k_s, v_s, acc_s, m_s, l_s):
    p = pl.program_id(1)
    lo = lax.broadcasted_iota(jnp.int32, (BLOCK, LANES), 1) < HEAD_DIM

    def convert(c, carry):
        rows = pl.ds(pl.multiple_of(c * DIL_CONVERT_ROWS, DIL_CONVERT_ROWS), DIL_CONVERT_ROWS)
        lo_c = lax.broadcasted_iota(jnp.int32, (DIL_CONVERT_ROWS, LANES), 1) < HEAD_DIM
        q = q_ref[0, rows, :].astype(F32)
        q0_s[rows, :] = jnp.where(lo_c, q, 0.0)
        q1_s[rows, :] = jnp.where(lo_c, 0.0, q)
        k_s[rows, :] = k_ref[0, rows, :].astype(F32)
        v_s[rows, :] = v_ref[0, rows, :].astype(F32)
        return carry

    lax.fori_loop(0, SEQ // DIL_CONVERT_ROWS, convert, 0)

    row = lax.broadcasted_iota(jnp.int32, (BLOCK, 2 * BLOCK), 0)
    col = lax.broadcasted_iota(jnp.int32, (BLOCK, 2 * BLOCK), 1)
    dist = row + BLOCK - col
    distf = dist.astype(F32)
    no_prev = jnp.where(col < BLOCK, -jnp.inf, 0.0)

    def bcast2(a0, a1):
        return jnp.where(lo, jnp.broadcast_to(a0, (BLOCK, LANES)), jnp.broadcast_to(a1, (BLOCK, LANES)))

    for pi, (window, d) in enumerate(DIL_ORDER):
        first, last = pi == 0, pi == len(DIL_ORDER) - 1
        nb = SEQ // d // BLOCK
        band = (dist >= 0) & (dist <= window // d)
        bias = [jnp.where(band, (-float(d) * slopes_ref[2 * p + hh]) * distf, -jnp.inf) for hh in range(2)]

        def rows_of(start, d=d):
            return pl.ds(start, BLOCK, stride=d) if d > 1 else pl.ds(pl.multiple_of(start, BLOCK), BLOCK)

        def step(it, carry, d=d, nb=nb, bias=bias, first=first, last=last, rows_of=rows_of):
            blocks = []
            for u in range(DIL_UNROLL):
                t = it * DIL_UNROLL + u
                r, n = t // nb, t % nb
                rows = rows_of(r + d * BLOCK * n)
                prows = rows_of(r + d * BLOCK * jnp.maximum(n - 1, 0))
                kp, kc = k_s[prows, :].astype(BF), k_s[rows, :].astype(BF)
                qh = jnp.concatenate([q0_s[rows, :], q1_s[rows, :]], axis=0).astype(BF)
                s = jnp.concatenate(
                    [lax.dot_general(qh, kp, NT_DIMS, preferred_element_type=F32),
                     lax.dot_general(qh, kc, NT_DIMS, preferred_element_type=F32)], axis=1)
                blocks.append((n, rows, prows, (s[:BLOCK], s[BLOCK:])))
            soft = []
            for n, rows, prows, s in blocks:
                extra = jnp.where(n == 0, no_prev, 0.0)
                res = []
                for hh in range(2):
                    sh = s[hh] + bias[hh] + extra
                    m = jnp.max(sh, axis=1, keepdims=True)
                    pe = jnp.exp(sh - m)
                    res.append((m, jnp.sum(pe, axis=1, keepdims=True), pe.astype(BF)))
                soft.append(res)
            pvs = []
            for (n, rows, prows, s), res in zip(blocks, soft):
                vp, vc = v_s[prows, :].astype(BF), v_s[rows, :].astype(BF)
                pb = jnp.concatenate([res[0][2], res[1][2]], axis=0)
                pv = (jnp.dot(pb[:, :BLOCK], vp, preferred_element_type=F32)
                      + jnp.dot(pb[:, BLOCK:], vc, preferred_element_type=F32))
                pvs.append((pv[:BLOCK], pv[BLOCK:]))
            for (n, rows, prows, s), res, pv in zip(blocks, soft, pvs):
                m2 = bcast2(res[0][0], res[1][0])
                l2 = bcast2(res[0][1], res[1][1])
                acc2 = jnp.where(lo, pv[0], pv[1])
                if not first:
                    m_old = m_s[rows, :]
                    m_new = jnp.maximum(m_old, m2)
                    a_old, a_new = jnp.exp(m_old - m_new), jnp.exp(m2 - m_new)
                    l2 = a_old * l_s[rows, :] + a_new * l2
                    acc2 = a_old * acc_s[rows, :] + a_new * acc2
                    m2 = m_new
                if last:
                    o_ref[0, rows, :] = (acc2 / l2).astype(BF)
                else:
                    m_s[rows, :] = m2
                    l_s[rows, :] = l2
                    acc_s[rows, :] = acc2
            return carry

        lax.fori_loop(0, SEQ // BLOCK // DIL_UNROLL, step, 0)


def _dilated(qkv, slopes):
    n_pairs = C_HEADS // 2
    seq_f32 = pltpu.VMEM((SEQ, LANES), F32)
    return pl.pallas_call(
        _dilated_kernel,
        grid=(BATCH, n_pairs),
        in_specs=[pl.BlockSpec(memory_space=pltpu.SMEM),
                  pl.BlockSpec((1, SEQ, LANES), lambda b, p: (p, b, 0)),
                  pl.BlockSpec((1, SEQ, LANES), lambda b, p: (n_pairs + p, b, 0)),
                  pl.BlockSpec((1, SEQ, LANES), lambda b, p: (2 * n_pairs + p, b, 0))],
        out_specs=pl.BlockSpec((1, SEQ, LANES), lambda b, p: (p, b, 0)),
        out_shape=jax.ShapeDtypeStruct((n_pairs, N_TOK, LANES), BF),
        scratch_shapes=[seq_f32] * 7,
        compiler_params=_params(("parallel", "parallel"), VMEM_LIMIT),
        name="dilated",
    )(slopes, qkv, qkv, qkv)


SB_GROUP = 512
SB_SUB = MXU_N
SB_DEAD_LOG2 = -150.0
LOG2_E = 1.4426950408889634


def _neg_abs(z):
    bits = lax.bitcast_convert_type(z, jnp.uint32) | jnp.uint32(0x80000000)
    return lax.bitcast_convert_type(bits, F32)


def _sb_kernel(q_ref, k_ref, v_ref, uo_ref, o_ref):
    i = pl.program_id(2)
    q2 = q_ref[0].astype(F32)
    uo = uo_ref[...]
    lo = lax.broadcasted_iota(jnp.int32, (BLOCK, LANES), 1) < HEAD_DIM
    q_stack = jnp.concatenate([jnp.where(lo, q2, 0.0), jnp.where(lo, 0.0, q2)], axis=0).astype(BF)
    rel = (lax.broadcasted_iota(jnp.int32, (BLOCK, SB_SUB), 1)
           - lax.broadcasted_iota(jnp.int32, (BLOCK, SB_SUB), 0))

    n_sub = SB_GROUP // SB_SUB
    subs = list(reversed(range(n_sub)))

    def scores(g):
        off = pl.multiple_of(g * SB_GROUP, SB_GROUP)
        kt = k_ref[0, pl.ds(off, SB_GROUP), :]
        z = lax.dot_general(q_stack, kt, NT_DIMS, preferred_element_type=F32)
        return z[:BLOCK], z[BLOCK:]

    def group(g, carry, masked):
        zs, cs, o = carry[0], list(carry[1]), carry[2]
        z_next = scores(jnp.maximum(g - 1, 0))
        off = pl.multiple_of(g * SB_GROUP, SB_GROUP)
        spb, e, strict = {}, {}, {}
        for sub in subs:
            if masked:
                strict[sub] = rel < (i * BLOCK - off - sub * SB_SUB)
            for hh in range(2):
                z = zs[hh][:, sub * SB_SUB:(sub + 1) * SB_SUB]
                sp = jnp.maximum(z, 0.0) + jnp.log2(1.0 + jnp.exp2(_neg_abs(z)))
                e[hh, sub] = z - sp
                if masked:
                    sp = jnp.where(strict[sub], sp, 0.0)
                spb[hh, sub] = sp.astype(BF)
        keys = [(hh, sub) for sub in subs for hh in range(2)]
        w_all = jnp.dot(jnp.concatenate([spb[key] for key in keys], axis=0), uo, preferred_element_type=F32)
        w = {key: w_all[j * BLOCK:(j + 1) * BLOCK] for j, key in enumerate(keys)}
        parts = {}
        for sub in subs:
            for hh in range(2):
                wk = w[hh, sub]
                a = jnp.exp2(e[hh, sub] + jnp.concatenate([cs[hh]] * (SB_SUB // LANES), axis=1) + wk[:, :SB_SUB])
                if masked:
                    a = jnp.where(strict[sub], a, 0.0)
                parts[hh, sub] = a.astype(BF)
                cs[hh] = cs[hh] + wk[:, SB_SUB:]
        vt = v_ref[0, pl.ds(off, SB_GROUP), :]
        a_stack = jnp.concatenate(
            [jnp.concatenate([parts[hh, sub] for sub in range(n_sub)], axis=1) for hh in range(2)], axis=0)
        pv = jnp.dot(a_stack, vt, preferred_element_type=F32)
        return z_next, tuple(cs), o + jnp.where(lo, pv[:BLOCK], pv[BLOCK:])

    zero = jnp.zeros((BLOCK, LANES), F32)
    top = i // (SB_GROUP // BLOCK)
    def alive(cs):
        return jnp.max(jnp.maximum(cs[0], cs[1])) > SB_DEAD_LOG2

    def body(state):
        g = state[0]
        zs, cs, o = group(g, state[2:], False)
        return g - 1, alive(cs), zs, cs, o

    zs, cs, o = group(top, (scores(top), (zero, zero), zero), True)
    state = lax.while_loop(lambda st: (st[0] >= 0) & st[1], body, (top - 1, alive(cs), zs, cs, o))
    o_ref[0] = state[4].astype(BF)


def _stick_breaking(qkv, q_c0, k_c0, v_c0, n_pairs):
    idx = np.arange(SB_SUB)
    tri_ones = np.concatenate([idx[:, None] > idx[None, :], np.ones((SB_SUB, LANES), bool)], axis=1)
    uo = jnp.asarray(-tri_ones.astype(np.float32), dtype=BF)
    nb = SEQ // BLOCK
    return pl.pallas_call(
        _sb_kernel,
        grid=(BATCH, n_pairs, nb),
        in_specs=[pl.BlockSpec((1, BLOCK, LANES), lambda b, p, i: (q_c0 + p, b * nb + i, 0)),
                  pl.BlockSpec((1, SEQ, LANES), lambda b, p, i: (k_c0 + p, b, 0)),
                  pl.BlockSpec((1, SEQ, LANES), lambda b, p, i: (v_c0 + p, b, 0)),
                  _resident((SB_SUB, SB_SUB + LANES))],
        out_specs=pl.BlockSpec((1, BLOCK, LANES), lambda b, p, i: (p, b * nb + i, 0)),
        out_shape=jax.ShapeDtypeStruct((n_pairs, N_TOK, LANES), BF),
        compiler_params=_params(("parallel", "parallel", "parallel")),
        name="stick_breaking",
    )(qkv, qkv, qkv, uo)


def _oproj_kernel(*refs):
    x_ref, w_ref, out_ref = refs[0], refs[-2], refs[-1]
    parts = [r[c] for r in refs[1:-2] for c in range(r.shape[0])]
    o = jnp.concatenate(parts, axis=1)
    out_ref[...] = x_ref[...] + jnp.dot(o, w_ref[...], preferred_element_type=F32)


def _oproj(x, heads, w, *, tm=512):
    in_specs = [pl.BlockSpec((tm, D_MODEL), lambda i: (i, 0))]
    in_specs += [pl.BlockSpec((h.shape[0], tm, LANES), lambda i: (0, i, 0)) for h in heads]
    in_specs += [_resident(w.shape)]
    return pl.pallas_call(
        _oproj_kernel, grid=(N_TOK // tm,),
        in_specs=in_specs,
        out_specs=pl.BlockSpec((tm, D_MODEL), lambda i: (i, 0)),
        out_shape=jax.ShapeDtypeStruct((N_TOK, D_MODEL), F32),
        compiler_params=_params(("parallel",), VMEM_LIMIT),
        name="oproj",
    )(x, *heads, w.astype(BF))


def _xattn_kernel(x_ref, g_ref, wq_ref, cs_ref, bd_ref, kv_ref, wo_ref, o_ref):
    xv = x_ref[...]
    h = _rms(xv, g_ref[...]).astype(BF)
    outs = []
    for hd in range(X_HEADS):
        cols = slice(X_HEAD_DIM * hd, X_HEAD_DIM * (hd + 1))
        acc = jnp.dot(h, wq_ref[:, cols], preferred_element_type=F32)
        ss = jnp.dot((acc * acc).astype(BF), bd_ref[...], preferred_element_type=F32)
        q = (acc * cs_ref[:, cols] * lax.rsqrt(ss * (1.0 / X_HEAD_DIM) + RMS_EPS)).astype(BF)
        kh = jnp.concatenate([kv_ref[2 * hd], kv_ref[2 * hd + 1]], axis=1)
        vh = jnp.concatenate([kv_ref[2 * X_HEADS + 2 * hd], kv_ref[2 * X_HEADS + 2 * hd + 1]], axis=1)
        s = lax.dot_general(q, kh, NT_DIMS, preferred_element_type=F32)
        m = jnp.max(s, axis=1, keepdims=True)
        pe = jnp.exp(s - m)
        l = jnp.sum(pe, axis=1, keepdims=True)
        outs.append((jnp.dot(pe.astype(BF), vh, preferred_element_type=F32) / l).astype(BF))
    o = jnp.concatenate(outs, axis=1)
    o_ref[...] = xv + jnp.dot(o, wo_ref[...], preferred_element_type=F32)


def _xattn(x, g, w_q, q_colscale, kv, w_o, *, tm=512):
    assert X_HEAD_DIM == MXU_N
    tiles_per_batch = SEQ // tm
    return pl.pallas_call(
        _xattn_kernel, grid=(N_TOK // tm,),
        in_specs=[pl.BlockSpec((tm, D_MODEL), lambda i: (i, 0)),
                  _resident((1, D_MODEL)),
                  _resident((D_MODEL, D_MODEL)),
                  _resident((1, D_MODEL)),
                  _resident((MXU_N, MXU_N)),
                  pl.BlockSpec((4 * X_HEADS, MEM_LEN, LANES), lambda i: (0, i // tiles_per_batch, 0)),
                  _resident((D_MODEL, D_MODEL))],
        out_specs=pl.BlockSpec((tm, D_MODEL), lambda i: (i, 0)),
        out_shape=jax.ShapeDtypeStruct((N_TOK, D_MODEL), F32),
        compiler_params=_params(("parallel",), VMEM_LIMIT),
        name="xattn",
    )(x, g.reshape(1, D_MODEL), w_q.astype(BF), q_colscale.reshape(1, D_MODEL).astype(F32),
      _group_ones(X_HEAD_DIM), kv, w_o.astype(BF))


def _alibi(n_heads):
    return jnp.asarray(2.0 ** (-8.0 * np.arange(1, n_heads + 1) / n_heads), dtype=F32)


def _even_mixer(x, norm_g, w_in, q_gain, k_gain, sinks, w_out):
    hd = HEAD_DIM
    qa, ka, va, rest = w_in[:, :512], w_in[:, 512:640], w_in[:, 640:768], w_in[:, 768:]
    dup = lambda w: jnp.concatenate([w[:, :hd], w[:, :hd], w[:, hd:], w[:, hd:]], axis=1)
    w_aug = jnp.concatenate([qa, dup(ka), dup(va), rest], axis=1)
    scale = hd ** -0.5
    ones = lambda n: jnp.ones((n,), F32)
    cs = jnp.concatenate([jnp.tile(q_gain, A_Q_HEADS) * scale, jnp.tile(k_gain, 4), ones(256),
                          ones(512) * (scale * LOG2_E), ones(1024)])
    normed = [True, True, True] + [False] * 7
    p = _proj(x, norm_g, w_aug, cs, normed, hd)
    o_a = _banded(p, _alibi(A_Q_HEADS), sinks.astype(F32), d=1, q_blk=(4, 0), k_blk=(2, 2), v_blk=(2, 3),
                  kv_div=2, max_dist=A_WINDOW - 1, want_lse=False)
    o_b = _stick_breaking(p, 8, 12, 16, B_HEADS // 2)
    return _oproj(x, [o_a, o_b], w_out)


def _odd_mixer(x, norm_g, w_in, q_gain, k_gain, w_out):
    hd = HEAD_DIM
    cs = jnp.concatenate([jnp.tile(q_gain, C_HEADS) * hd ** -0.5, jnp.tile(k_gain, C_HEADS),
                          jnp.ones((C_HEADS * hd,), F32)])
    normed = [True] * 8 + [False] * 4
    p = _proj(x, norm_g, w_in, cs, normed, hd)
    return _oproj(x, [_dilated(p, _alibi(C_HEADS))], w_out)


def _cross_attention(x, mem2d, norm_g, mem_g, w_q, w_kv, q_gain, k_gain, w_o):
    cs_kv = jnp.concatenate([jnp.tile(k_gain, X_HEADS), jnp.ones((D_MODEL,), F32)])
    kv = _proj(mem2d, mem_g, w_kv, cs_kv, [True] * 4 + [False] * 4, X_HEAD_DIM, tm=256)
    cs_q = jnp.tile(q_gain, X_HEADS) * X_HEAD_DIM ** -0.5
    return _xattn(x, norm_g, w_q, cs_q, kv, w_o)


def kernel(x, mem, ffn1_norm, ffn1_w_gu, ffn1_w_down, mix_norm, ev_w_in, ev_q_gain, ev_k_gain, ev_sinks, ev_w_out, od_w_in, od_q_gain, od_k_gain, od_w_out, xa_norm, xa_mem_norm, xa_w_q, xa_w_kv, xa_q_gain, xa_k_gain, xa_w_o, ffn2_norm, ffn2_w_gu, ffn2_w_down):
    x = x.reshape(N_TOK, D_MODEL)
    mem2d = mem.reshape(BATCH * MEM_LEN, D_MODEL)
    for layer in range(DEPTH):
        j = layer // 2
        x = _ffn(x, ffn1_norm[layer], ffn1_w_gu[layer], ffn1_w_down[layer])
        if layer % 2 == 0:
            x = _even_mixer(x, mix_norm[layer], ev_w_in[j], ev_q_gain[j], ev_k_gain[j], ev_sinks[j], ev_w_out[j])
        else:
            x = _odd_mixer(x, mix_norm[layer], od_w_in[j], od_q_gain[j], od_k_gain[j], od_w_out[j])
        x = _cross_attention(x, mem2d, xa_norm[layer], xa_mem_norm[layer], xa_w_q[layer], xa_w_kv[layer],
                             xa_q_gain[layer], xa_k_gain[layer], xa_w_o[layer])
        x = _ffn(x, ffn2_norm[layer], ffn2_w_gu[layer], ffn2_w_down[layer])
    return x.reshape(BATCH, SEQ, D_MODEL)
```

```python
import functools

import numpy as np
import jax
import jax.numpy as jnp
from jax import lax
from jax.experimental import pallas as pl
from jax.experimental.pallas import tpu as pltpu

D_MODEL = 1024
BATCH = 4
SEQ = 4096
N_TOK = BATCH * SEQ
DEPTH = 2
HEAD_DIM = 64
BLOCK = 128
A_Q_HEADS = 8
A_KV_HEADS = 2
A_WINDOW = 128
B_HEADS = 8
C_HEADS = 16
C_PATTERNS = ((128, 1), (512, 4), (2048, 16))
MEM_LEN = 256
X_HEADS = 4
X_HEAD_DIM = D_MODEL // X_HEADS
D_FF = 2816
RMS_EPS = 1e-6

LANES = 128
MXU_N = 256
VMEM_LIMIT = 56 * 1024 * 1024

BF = jnp.bfloat16
F32 = jnp.float32
NT_DIMS = (((1,), (1,)), ((), ()))


def _params(sem, vmem=None):
    return pltpu.CompilerParams(dimension_semantics=sem, vmem_limit_bytes=vmem)


def _resident(shape):
    nd = len(shape)
    return pl.BlockSpec(shape, lambda *_: (0,) * nd, pipeline_mode=pl.Buffered(1))


def _rms(xv, g):
    ms = jnp.mean(xv * xv, axis=-1, keepdims=True)
    return xv * lax.rsqrt(ms + RMS_EPS) * g


FFN_SPLIT = (D_FF // MXU_N + 1) // 2 * MXU_N
FFN_CHUNKS = ((0, FFN_SPLIT), (FFN_SPLIT, D_FF))


def _ffn_kernel(x_ref, g_ref, wgu_ref, wd_ref, o_ref):
    xv = x_ref[...]
    h = _rms(xv, g_ref[...]).astype(BF)
    acc = jnp.zeros_like(xv)
    for c0, c1 in FFN_CHUNKS:
        gate = jnp.dot(h, wgu_ref[:, c0:c1], preferred_element_type=F32)
        up = jnp.dot(h, wgu_ref[:, D_FF + c0:D_FF + c1], preferred_element_type=F32)
        act = (gate * jax.nn.sigmoid(gate) * up).astype(BF)
        acc = acc + jnp.dot(act, wd_ref[c0:c1, :], preferred_element_type=F32)
    o_ref[...] = xv + 0.5 * acc


def _ffn(x, g, w_gu, w_down, *, tm=512):
    return pl.pallas_call(
        _ffn_kernel,
        grid=(N_TOK // tm,),
        in_specs=[pl.BlockSpec((tm, D_MODEL), lambda i: (i, 0)),
                  _resident((1, D_MODEL)),
                  _resident((D_MODEL, 2 * D_FF)),
                  _resident((D_FF, D_MODEL))],
        out_specs=pl.BlockSpec((tm, D_MODEL), lambda i: (i, 0)),
        out_shape=jax.ShapeDtypeStruct((N_TOK, D_MODEL), F32),
        compiler_params=_params(("parallel",), VMEM_LIMIT),
        name="ffn",
    )(x, g.reshape(1, D_MODEL), w_gu.astype(BF), w_down.astype(BF))


def _group_ones(gs):
    idx = np.arange(MXU_N) // gs
    return jnp.asarray(idx[:, None] == idx[None, :], dtype=BF)


def _proj_kernel(x_ref, g_ref, w_ref, cs_ref, bd_ref, o_ref, *, normed, gs):
    h = _rms(x_ref[...], g_ref[...]).astype(BF)

    def main(j):
        return jnp.dot(h, w_ref[:, MXU_N * j:MXU_N * (j + 1)], preferred_element_type=F32)

    acc_next = main(0)
    for j, is_normed in enumerate(normed):
        cols = slice(MXU_N * j, MXU_N * (j + 1))
        acc = acc_next
        if j + 1 < len(normed):
            acc_next = main(j + 1)
        y = acc * cs_ref[:, cols]
        if is_normed:
            ss = jnp.dot((acc * acc).astype(BF), bd_ref[...], preferred_element_type=F32)
            y = y * lax.rsqrt(ss * (1.0 / gs) + RMS_EPS)
        o_ref[2 * j] = y[:, :LANES].astype(BF)
        o_ref[2 * j + 1] = y[:, LANES:].astype(BF)


def _proj(x, g, w, colscale, normed, gs, *, tm=512):
    rows = x.shape[0]
    wout = w.shape[1]
    assert wout == MXU_N * len(normed)
    c = wout // LANES
    return pl.pallas_call(
        functools.partial(_proj_kernel, normed=tuple(normed), gs=gs),
        grid=(rows // tm,),
        in_specs=[pl.BlockSpec((tm, D_MODEL), lambda i: (i, 0)),
                  _resident((1, D_MODEL)),
                  _resident((D_MODEL, wout)),
                  _resident((1, wout)),
                  _resident((MXU_N, MXU_N))],
        out_specs=pl.BlockSpec((c, tm, LANES), lambda i: (0, i, 0)),
        out_shape=jax.ShapeDtypeStruct((c, rows, LANES), BF),
        compiler_params=_params(("parallel",), VMEM_LIMIT),
        name="proj",
    )(x, g.reshape(1, D_MODEL), w.astype(BF), colscale.reshape(1, wout).astype(F32), _group_ones(gs))


def _swa_kernel(slopes_ref, sinks_ref, q_ref, kp_ref, kc_ref, vp_ref, vc_ref, o_ref, *, kv_div, max_dist):
    n = pl.program_id(1)
    row = lax.broadcasted_iota(jnp.int32, (BLOCK, 2 * BLOCK), 0)
    col = lax.broadcasted_iota(jnp.int32, (BLOCK, 2 * BLOCK), 1)
    dist = row + BLOCK - col
    valid = (dist >= 0) & (dist <= max_dist) & ((col >= BLOCK) | (n > 0))
    negmask = jnp.where(valid, 0.0, -jnp.inf)
    distf = dist.astype(F32)
    lo = lax.broadcasted_iota(jnp.int32, (BLOCK, LANES), 1) < HEAD_DIM

    n_groups = q_ref.shape[0] // kv_div
    heads_per_group = 2 * kv_div
    scores = []
    for g in range(n_groups):
        parts = []
        for p in range(g * kv_div, (g + 1) * kv_div):
            q2 = q_ref[p, 0].astype(F32)
            parts += [jnp.where(lo, q2, 0.0), jnp.where(lo, 0.0, q2)]
        q_stack = jnp.concatenate(parts, axis=0).astype(BF)
        scores.append(jnp.concatenate(
            [lax.dot_general(q_stack, kp_ref[g, 0], NT_DIMS, preferred_element_type=F32),
             lax.dot_general(q_stack, kc_ref[g, 0], NT_DIMS, preferred_element_type=F32)], axis=1))
    soft = []
    for g in range(n_groups):
        res = []
        for j in range(heads_per_group):
            h = g * heads_per_group + j
            s = scores[g][j * BLOCK:(j + 1) * BLOCK] - slopes_ref[h] * distf + negmask
            m = jnp.maximum(jnp.max(s, axis=1, keepdims=True), sinks_ref[h])
            pe = jnp.exp(s - m)
            res.append((pe.astype(BF), jnp.sum(pe, axis=1, keepdims=True) + jnp.exp(sinks_ref[h] - m)))
        soft.append(res)
    for g in range(n_groups):
        pb = jnp.concatenate([r[0] for r in soft[g]], axis=0)
        pv = (jnp.dot(pb[:, :BLOCK], vp_ref[g, 0], preferred_element_type=F32)
              + jnp.dot(pb[:, BLOCK:], vc_ref[g, 0], preferred_element_type=F32))
        for jp in range(kv_div):
            o0 = pv[(2 * jp) * BLOCK:(2 * jp + 1) * BLOCK] / soft[g][2 * jp][1]
            o1 = pv[(2 * jp + 1) * BLOCK:(2 * jp + 2) * BLOCK] / soft[g][2 * jp + 1][1]
            o_ref[g * kv_div + jp, 0] = jnp.where(lo, o0, o1).astype(BF)


def _swa(qkv, slopes, sinks, *, n_pairs, n_kv, max_dist):
    c = qkv.shape[0]
    nb = SEQ // BLOCK
    view = qkv.reshape(c, BATCH, SEQ, LANES)
    k_idx, v_idx = n_pairs // n_kv, n_pairs // n_kv + 1

    def cur(size, idx):
        return pl.BlockSpec((size, 1, BLOCK, LANES), lambda b, n: (idx, b, n, 0))

    def prev(size, idx):
        return pl.BlockSpec((size, 1, BLOCK, LANES), lambda b, n: (idx, b, jnp.maximum(n - 1, 0), 0))

    smem = pl.BlockSpec(memory_space=pltpu.SMEM)
    out = pl.pallas_call(
        functools.partial(_swa_kernel, kv_div=n_pairs // n_kv, max_dist=max_dist),
        grid=(BATCH, nb),
        in_specs=[smem, smem, cur(n_pairs, 0), prev(n_kv, k_idx), cur(n_kv, k_idx),
                  prev(n_kv, v_idx), cur(n_kv, v_idx)],
        out_specs=pl.BlockSpec((n_pairs, 1, BLOCK, LANES), lambda b, n: (0, b, n, 0)),
        out_shape=jax.ShapeDtypeStruct((n_pairs, BATCH, SEQ, LANES), BF),
        compiler_params=_params(("parallel", "parallel")),
        name="swa",
    )(slopes, sinks, view, view, view, view, view)
    return out.reshape(n_pairs, N_TOK, LANES)


DIL_ORDER = tuple(sorted(C_PATTERNS, key=lambda wd: -wd[1]))
DIL_UNROLL = 4
DIL_BASE = 4
DIL_Q = SEQ // DIL_BASE
DIL_CONVERT_ROWS = DIL_BASE * BLOCK


def _dilated_kernel(slopes_ref, q_ref, k_ref, v_ref, o_ref, tq_s, tk_s, tv_s, q0_s, q1_s, k_s, v_s,
                    acc_r, m_r, l_r, acc_n, m_n, l_n):
    assert all(d == 1 or d % DIL_BASE == 0 for _, d in DIL_ORDER) and DIL_ORDER[-1][1] == 1
    p = pl.program_id(1)
    lo = lax.broadcasted_iota(jnp.int32, (BLOCK, LANES), 1) < HEAD_DIM

    def convert(c, carry):
        rows = pl.ds(pl.multiple_of(c * DIL_CONVERT_ROWS, DIL_CONVERT_ROWS), DIL_CONVERT_ROWS)
        tq_s[...] = q_ref[0, rows, :].astype(F32)
        tk_s[...] = k_ref[0, rows, :].astype(F32)
        tv_s[...] = v_ref[0, rows, :].astype(F32)
        for rho in range(DIL_BASE):
            src = pl.ds(rho, BLOCK, stride=DIL_BASE)
            dst = pl.ds(pl.multiple_of(rho * DIL_Q + c * BLOCK, BLOCK), BLOCK)
            q = tq_s[src, :]
            q0_s[dst, :] = jnp.where(lo, q, 0.0)
            q1_s[dst, :] = jnp.where(lo, 0.0, q)
            k_s[dst, :] = tk_s[src, :]
            v_s[dst, :] = tv_s[src, :]
        return carry

    lax.fori_loop(0, SEQ // DIL_CONVERT_ROWS, convert, 0)

    row = lax.broadcasted_iota(jnp.int32, (BLOCK, 2 * BLOCK), 0)
    col = lax.broadcasted_iota(jnp.int32, (BLOCK, 2 * BLOCK), 1)
    dist = row + BLOCK - col
    distf = dist.astype(F32)
    no_prev = jnp.where(col < BLOCK, -jnp.inf, 0.0)

    def bcast2(a0, a1):
        return jnp.where(lo, jnp.broadcast_to(a0, (BLOCK, LANES)), jnp.broadcast_to(a1, (BLOCK, LANES)))

    for pi, (window, d) in enumerate(DIL_ORDER):
        first, last = pi == 0, pi == len(DIL_ORDER) - 1
        natural = d == 1
        nb = SEQ // d // BLOCK
        band = (dist >= 0) & (dist <= window // d)
        bias = [jnp.where(band, (-float(d) * slopes_ref[2 * p + hh]) * distf, -jnp.inf) for hh in range(2)]
        acc_s, m_s, l_s = (acc_n, m_n, l_n) if natural else (acc_r, m_r, l_r)

        if natural and not first:
            for rho in range(DIL_BASE):
                src, dst = pl.ds(rho * DIL_Q, DIL_Q), pl.ds(rho, DIL_Q, stride=DIL_BASE)
                acc_n[dst, :] = acc_r[src, :]
                m_n[dst, :] = m_r[src, :]
                l_n[dst, :] = l_r[src, :]

        def rows_of(r, n, d=d, natural=natural):
            if natural:
                return pl.ds(pl.multiple_of(BLOCK * n, BLOCK), BLOCK)
            inner = d // DIL_BASE
            start = (r % DIL_BASE) * DIL_Q + r // DIL_BASE + inner * BLOCK * n
            return pl.ds(start, BLOCK, stride=inner) if inner > 1 else pl.ds(pl.multiple_of(start, BLOCK), BLOCK)

        def step(it, carry, nb=nb, bias=bias, first=first, last=last, natural=natural, rows_of=rows_of,
                 acc_s=acc_s, m_s=m_s, l_s=l_s):
            blocks = []
            for u in range(DIL_UNROLL):
                t = it * DIL_UNROLL + u
                r, n = t // nb, t % nb
                rows, prows = rows_of(r, n), rows_of(r, jnp.maximum(n - 1, 0))
                if natural:
                    kp, kc = k_ref[0, prows, :], k_ref[0, rows, :]
                    q2 = q_ref[0, rows, :].astype(F32)
                    qh = jnp.concatenate([jnp.where(lo, q2, 0.0), jnp.where(lo, 0.0, q2)], axis=0).astype(BF)
                else:
                    kp, kc = k_s[prows, :].astype(BF), k_s[rows, :].astype(BF)
                    qh = jnp.concatenate([q0_s[rows, :], q1_s[rows, :]], axis=0).astype(BF)
                s = jnp.concatenate(
                    [lax.dot_general(qh, kp, NT_DIMS, preferred_element_type=F32),
                     lax.dot_general(qh, kc, NT_DIMS, preferred_element_type=F32)], axis=1)
                blocks.append((n, rows, prows, (s[:BLOCK], s[BLOCK:])))
            soft = []
            for n, rows, prows, s in blocks:
                extra = jnp.where(n == 0, no_prev, 0.0)
                res = []
                for hh in range(2):
                    sh = s[hh] + bias[hh] + extra
                    m = jnp.max(sh, axis=1, keepdims=True)
                    pe = jnp.exp(sh - m)
                    res.append((m, jnp.sum(pe, axis=1, keepdims=True), pe.astype(BF)))
                soft.append(res)
            pvs = []
            for (n, rows, prows, s), res in zip(blocks, soft):
                if natural:
                    vp, vc = v_ref[0, prows, :], v_ref[0, rows, :]
                else:
                    vp, vc = v_s[prows, :].astype(BF), v_s[rows, :].astype(BF)
                pb = jnp.concatenate([res[0][2], res[1][2]], axis=0)
                pv = (jnp.dot(pb[:, :BLOCK], vp, preferred_element_type=F32)
                      + jnp.dot(pb[:, BLOCK:], vc, preferred_element_type=F32))
                pvs.append((pv[:BLOCK], pv[BLOCK:]))
            for (n, rows, prows, s), res, pv in zip(blocks, soft, pvs):
                m2 = bcast2(res[0][0], res[1][0])
                l2 = bcast2(res[0][1], res[1][1])
                acc2 = jnp.where(lo, pv[0], pv[1])
                if not first:
                    m_old = m_s[rows, :]
                    m_new = jnp.maximum(m_old, m2)
                    a_old, a_new = jnp.exp(m_old - m_new), jnp.exp(m2 - m_new)
                    l2 = a_old * l_s[rows, :] + a_new * l2
                    acc2 = a_old * acc_s[rows, :] + a_new * acc2
                    m2 = m_new
                if last:
                    o_ref[0, rows, :] = (acc2 / l2).astype(BF)
                else:
                    m_s[rows, :] = m2
                    l_s[rows, :] = l2
                    acc_s[rows, :] = acc2
            return carry

        lax.fori_loop(0, SEQ // BLOCK // DIL_UNROLL, step, 0)


def _dilated(qkv, slopes):
    n_pairs = C_HEADS // 2
    seq_f32 = pltpu.VMEM((SEQ, LANES), F32)
    chunk_f32 = pltpu.VMEM((DIL_CONVERT_ROWS, LANES), F32)
    return pl.pallas_call(
        _dilated_kernel,
        grid=(BATCH, n_pairs),
        in_specs=[pl.BlockSpec(memory_space=pltpu.SMEM),
                  pl.BlockSpec((1, SEQ, LANES), lambda b, p: (p, b, 0)),
                  pl.BlockSpec((1, SEQ, LANES), lambda b, p: (n_pairs + p, b, 0)),
                  pl.BlockSpec((1, SEQ, LANES), lambda b, p: (2 * n_pairs + p, b, 0))],
        out_specs=pl.BlockSpec((1, SEQ, LANES), lambda b, p: (p, b, 0)),
        out_shape=jax.ShapeDtypeStruct((n_pairs, N_TOK, LANES), BF),
        scratch_shapes=[chunk_f32] * 3 + [seq_f32] * 10,
        compiler_params=_params(("parallel", "parallel"), VMEM_LIMIT),
        name="dilated",
    )(slopes, qkv, qkv, qkv)


SB_GROUP = 512
SB_SUB = MXU_N
SB_DEAD_LOG2 = -150.0
LOG2_E = 1.4426950408889634


def _sb_kernel(q_ref, k_ref, v_ref, uo_ref, o_ref):
    i = pl.program_id(2)
    q2 = q_ref[0].astype(F32)
    uo = uo_ref[...]
    lo = lax.broadcasted_iota(jnp.int32, (BLOCK, LANES), 1) < HEAD_DIM
    q_stack = jnp.concatenate([jnp.where(lo, q2, 0.0), jnp.where(lo, 0.0, q2)], axis=0).astype(BF)
    rel = (lax.broadcasted_iota(jnp.int32, (BLOCK, SB_SUB), 1)
           - lax.broadcasted_iota(jnp.int32, (BLOCK, SB_SUB), 0))

    n_sub = SB_GROUP // SB_SUB
    subs = list(reversed(range(n_sub)))

    def scores(g):
        off = pl.multiple_of(g * SB_GROUP, SB_GROUP)
        kt = k_ref[0, pl.ds(off, SB_GROUP), :]
        z = lax.dot_general(q_stack, kt, NT_DIMS, preferred_element_type=F32)
        return z[:BLOCK], z[BLOCK:]

    def group(g, carry, masked):
        zs, cs, o = carry[0], list(carry[1]), carry[2]
        z_next = scores(jnp.maximum(g - 1, 0))
        off = pl.multiple_of(g * SB_GROUP, SB_GROUP)
        spb, e, strict = {}, {}, {}
        for sub in subs:
            if masked:
                strict[sub] = rel < (i * BLOCK - off - sub * SB_SUB)
            for hh in range(2):
                z = zs[hh][:, sub * SB_SUB:(sub + 1) * SB_SUB]
                sp = jnp.maximum(z, 0.0) + jnp.log2(1.0 + jnp.exp2(-jnp.abs(z)))
                e[hh, sub] = z - sp
                if masked:
                    sp = jnp.where(strict[sub], sp, 0.0)
                spb[hh, sub] = sp.astype(BF)
        keys = [(hh, sub) for sub in subs for hh in range(2)]
        w_all = jnp.dot(jnp.concatenate([spb[key] for key in keys], axis=0), uo, preferred_element_type=F32)
        w = {key: w_all[j * BLOCK:(j + 1) * BLOCK] for j, key in enumerate(keys)}
        parts = {}
        for sub in subs:
            for hh in range(2):
                wk = w[hh, sub]
                a = jnp.exp2(e[hh, sub] + jnp.concatenate([cs[hh]] * (SB_SUB // LANES), axis=1) + wk[:, :SB_SUB])
                if masked:
                    a = jnp.where(strict[sub], a, 0.0)
                parts[hh, sub] = a.astype(BF)
                cs[hh] = cs[hh] + wk[:, SB_SUB:]
        vt = v_ref[0, pl.ds(off, SB_GROUP), :]
        a_stack = jnp.concatenate(
            [jnp.concatenate([parts[hh, sub] for sub in range(n_sub)], axis=1) for hh in range(2)], axis=0)
        pv = jnp.dot(a_stack, vt, preferred_element_type=F32)
        return z_next, tuple(cs), o + jnp.where(lo, pv[:BLOCK], pv[BLOCK:])

    def alive(cs):
        return jnp.max(jnp.maximum(cs[0], cs[1])) > SB_DEAD_LOG2

    def body(state):
        g = state[0]
        zs, cs, o = group(g, state[2:], False)
        return g - 1, alive(cs), zs, cs, o

    zero = jnp.zeros((BLOCK, LANES), F32)
    top = i // (SB_GROUP // BLOCK)
    zs, cs, o = group(top, (scores(top), (zero, zero), zero), True)
    state = lax.while_loop(lambda st: (st[0] >= 0) & st[1], body, (top - 1, alive(cs), zs, cs, o))
    o_ref[0] = state[4].astype(BF)


def _stick_breaking(qkv, q_c0, k_c0, v_c0, n_pairs):
    idx = np.arange(SB_SUB)
    tri_ones = np.concatenate([idx[:, None] > idx[None, :], np.ones((SB_SUB, LANES), bool)], axis=1)
    uo = jnp.asarray(-tri_ones.astype(np.float32), dtype=BF)
    nb = SEQ // BLOCK
    return pl.pallas_call(
        _sb_kernel,
        grid=(BATCH, n_pairs, nb),
        in_specs=[pl.BlockSpec((1, BLOCK, LANES), lambda b, p, i: (q_c0 + p, b * nb + i, 0)),
                  pl.BlockSpec((1, SEQ, LANES), lambda b, p, i: (k_c0 + p, b, 0)),
                  pl.BlockSpec((1, SEQ, LANES), lambda b, p, i: (v_c0 + p, b, 0)),
                  _resident((SB_SUB, SB_SUB + LANES))],
        out_specs=pl.BlockSpec((1, BLOCK, LANES), lambda b, p, i: (p, b * nb + i, 0)),
        out_shape=jax.ShapeDtypeStruct((n_pairs, N_TOK, LANES), BF),
        compiler_params=_params(("parallel", "parallel", "parallel")),
        name="stick_breaking",
    )(qkv, qkv, qkv, uo)


def _mix_xattn_kernel(*refs):
    x_ref = refs[0]
    wm_ref, g_ref, wq_ref, cs_ref, bd_ref, kv_ref, wo_ref, o_ref = refs[-8:]
    mixed = jnp.concatenate([r[c] for r in refs[1:-8] for c in range(r.shape[0])], axis=1)
    xv = x_ref[...] + jnp.dot(mixed, wm_ref[...], preferred_element_type=F32)
    h = _rms(xv, g_ref[...]).astype(BF)
    heads = range(X_HEADS)
    cols = [slice(X_HEAD_DIM * hd, X_HEAD_DIM * (hd + 1)) for hd in heads]
    acc = [jnp.dot(h, wq_ref[:, cols[hd]], preferred_element_type=F32) for hd in heads]
    ss = [jnp.dot((acc[hd] * acc[hd]).astype(BF), bd_ref[...], preferred_element_type=F32) for hd in heads]
    q = [(acc[hd] * cs_ref[:, cols[hd]] * lax.rsqrt(ss[hd] * (1.0 / X_HEAD_DIM) + RMS_EPS)).astype(BF)
         for hd in heads]
    s = [lax.dot_general(q[hd], jnp.concatenate([kv_ref[2 * hd], kv_ref[2 * hd + 1]], axis=1), NT_DIMS,
                         preferred_element_type=F32) for hd in heads]
    pe, l = [], []
    for hd in heads:
        e = jnp.exp(s[hd] - jnp.max(s[hd], axis=1, keepdims=True))
        l.append(jnp.sum(e, axis=1, keepdims=True))
        pe.append(e.astype(BF))
    v0 = 2 * X_HEADS
    pv = [jnp.dot(pe[hd], jnp.concatenate([kv_ref[v0 + 2 * hd], kv_ref[v0 + 2 * hd + 1]], axis=1),
                  preferred_element_type=F32) for hd in heads]
    o = jnp.concatenate([(pv[hd] / l[hd]).astype(BF) for hd in heads], axis=1)
    o_ref[...] = xv + jnp.dot(o, wo_ref[...], preferred_element_type=F32)


def _mix_xattn(x, mixer_heads, w_mix, g, w_q, q_colscale, kv, w_o, *, tm=512):
    assert X_HEAD_DIM == MXU_N
    tiles_per_batch = SEQ // tm
    in_specs = [pl.BlockSpec((tm, D_MODEL), lambda i: (i, 0))]
    in_specs += [pl.BlockSpec((mh.shape[0], tm, LANES), lambda i: (0, i, 0)) for mh in mixer_heads]
    in_specs += [_resident((D_MODEL, D_MODEL)),
                 _resident((1, D_MODEL)),
                 _resident((D_MODEL, D_MODEL)),
                 _resident((1, D_MODEL)),
                 _resident((MXU_N, MXU_N)),
                 pl.BlockSpec((4 * X_HEADS, MEM_LEN, LANES), lambda i: (0, i // tiles_per_batch, 0)),
                 _resident((D_MODEL, D_MODEL))]
    return pl.pallas_call(
        _mix_xattn_kernel, grid=(N_TOK // tm,),
        in_specs=in_specs,
        out_specs=pl.BlockSpec((tm, D_MODEL), lambda i: (i, 0)),
        out_shape=jax.ShapeDtypeStruct((N_TOK, D_MODEL), F32),
        compiler_params=_params(("parallel",), VMEM_LIMIT),
        name="mix_xattn",
    )(x, *mixer_heads, w_mix.astype(BF), g.reshape(1, D_MODEL), w_q.astype(BF),
      q_colscale.reshape(1, D_MODEL).astype(F32), _group_ones(X_HEAD_DIM), kv, w_o.astype(BF))


def _alibi(n_heads):
    return jnp.asarray(2.0 ** (-8.0 * np.arange(1, n_heads + 1) / n_heads), dtype=F32)


def _even_mixer_heads(x, norm_g, w_in, q_gain, k_gain, sinks):
    hd = HEAD_DIM
    qa, ka, va, rest = w_in[:, :512], w_in[:, 512:640], w_in[:, 640:768], w_in[:, 768:]
    dup = lambda w: jnp.concatenate([w[:, :hd], w[:, :hd], w[:, hd:], w[:, hd:]], axis=1)
    w_aug = jnp.concatenate([qa, dup(ka), dup(va), rest], axis=1)
    scale = hd ** -0.5
    ones = lambda n: jnp.ones((n,), F32)
    cs = jnp.concatenate([jnp.tile(q_gain, A_Q_HEADS) * scale, jnp.tile(k_gain, 4), ones(256),
                          ones(512) * (scale * LOG2_E), ones(1024)])
    normed = [True, True, True] + [False] * 7
    p = _proj(x, norm_g, w_aug, cs, normed, hd)
    o_a = _swa(p, _alibi(A_Q_HEADS), sinks.astype(F32), n_pairs=A_Q_HEADS // 2, n_kv=A_KV_HEADS,
               max_dist=A_WINDOW - 1)
    o_b = _stick_breaking(p, 8, 12, 16, B_HEADS // 2)
    return [o_a, o_b]


def _odd_mixer_heads(x, norm_g, w_in, q_gain, k_gain):
    hd = HEAD_DIM
    cs = jnp.concatenate([jnp.tile(q_gain, C_HEADS) * hd ** -0.5, jnp.tile(k_gain, C_HEADS),
                          jnp.ones((C_HEADS * hd,), F32)])
    normed = [True] * 8 + [False] * 4
    p = _proj(x, norm_g, w_in, cs, normed, hd)
    return [_dilated(p, _alibi(C_HEADS))]


def _mix_and_cross_attention(x, mixer_heads, w_mix, mem2d, norm_g, mem_g, w_q, w_kv, q_gain, k_gain, w_o):
    cs_kv = jnp.concatenate([jnp.tile(k_gain, X_HEADS), jnp.ones((D_MODEL,), F32)])
    kv = _proj(mem2d, mem_g, w_kv, cs_kv, [True] * 4 + [False] * 4, X_HEAD_DIM, tm=256)
    cs_q = jnp.tile(q_gain, X_HEADS) * X_HEAD_DIM ** -0.5
    return _mix_xattn(x, mixer_heads, w_mix, norm_g, w_q, cs_q, kv, w_o)


def kernel(x, mem, ffn1_norm, ffn1_w_gu, ffn1_w_down, mix_norm, ev_w_in, ev_q_gain, ev_k_gain, ev_sinks, ev_w_out, od_w_in, od_q_gain, od_k_gain, od_w_out, xa_norm, xa_mem_norm, xa_w_q, xa_w_kv, xa_q_gain, xa_k_gain, xa_w_o, ffn2_norm, ffn2_w_gu, ffn2_w_down):
    x = x.reshape(N_TOK, D_MODEL)
    mem2d = mem.reshape(BATCH * MEM_LEN, D_MODEL)
    for layer in range(DEPTH):
        j = layer // 2
        x = _ffn(x, ffn1_norm[layer], ffn1_w_gu[layer], ffn1_w_down[layer])
        if layer % 2 == 0:
            heads = _even_mixer_heads(x, mix_norm[layer], ev_w_in[j], ev_q_gain[j], ev_k_gain[j], ev_sinks[j])
            w_mix = ev_w_out[j]
        else:
            heads = _odd_mixer_heads(x, mix_norm[layer], od_w_in[j], od_q_gain[j], od_k_gain[j])
            w_mix = od_w_out[j]
        x = _mix_and_cross_attention(x, heads, w_mix, mem2d, xa_norm[layer], xa_mem_norm[layer], xa_w_q[layer],
                                     xa_w_kv[layer], xa_q_gain[layer], xa_k_gain[layer], xa_w_o[layer])
        x = _ffn(x, ffn2_norm[layer], ffn2_w_gu[layer], ffn2_w_down[layer])
    return x.reshape(BATCH, SEQ, D_MODEL)
```

```python
import functools

import numpy as np
import jax
import jax.numpy as jnp
from jax import lax
from jax.experimental import pallas as pl
from jax.experimental.pallas import tpu as pltpu

D_MODEL = 1024
BATCH = 4
SEQ = 4096
N_TOK = BATCH * SEQ
DEPTH = 2
HEAD_DIM = 64
BLOCK = 128
A_Q_HEADS = 8
A_KV_HEADS = 2
A_WINDOW = 128
B_HEADS = 8
C_HEADS = 16
C_PATTERNS = ((128, 1), (512, 4), (2048, 16))
MEM_LEN = 256
X_HEADS = 4
X_HEAD_DIM = D_MODEL // X_HEADS
D_FF = 2816
RMS_EPS = 1e-6

LANES = 128
MXU_N = 256
VMEM_LIMIT = 56 * 1024 * 1024

BF = jnp.bfloat16
F32 = jnp.float32
NT_DIMS = (((1,), (1,)), ((), ()))


def _params(sem, vmem=None):
    return pltpu.CompilerParams(dimension_semantics=sem, vmem_limit_bytes=vmem)


def _resident(shape):
    nd = len(shape)
    return pl.BlockSpec(shape, lambda *_: (0,) * nd, pipeline_mode=pl.Buffered(1))


def _layer_weight(w3, layer):
    _, r, c = w3.shape
    return pl.BlockSpec((None, r, c), lambda *_: (layer, 0, 0), pipeline_mode=pl.Buffered(1))


CAST_ROWS = 128


def _cast_kernel(w_ref, o_ref):
    o_ref[...] = w_ref[...].astype(BF)


def _to_bf16(w3):
    l, r, c = w3.shape
    spec = pl.BlockSpec((1, CAST_ROWS, c), lambda i, j: (i, j, 0))
    return pl.pallas_call(
        _cast_kernel, grid=(l, r // CAST_ROWS), in_specs=[spec], out_specs=spec,
        out_shape=jax.ShapeDtypeStruct(w3.shape, BF),
        compiler_params=_params(("parallel", "parallel")),
        name="cast",
    )(w3)


def _rms(xv, g):
    ms = jnp.mean(xv * xv, axis=-1, keepdims=True)
    return xv * lax.rsqrt(ms + RMS_EPS) * g


FFN_SPLIT = (D_FF // MXU_N + 1) // 2 * MXU_N
FFN_CHUNKS = ((0, FFN_SPLIT), (FFN_SPLIT, D_FF))


def _ffn_kernel(x_ref, g_ref, wgu_ref, wd_ref, o_ref):
    xv = x_ref[...]
    h = _rms(xv, g_ref[...]).astype(BF)
    acc = jnp.zeros_like(xv)
    for c0, c1 in FFN_CHUNKS:
        gate = jnp.dot(h, wgu_ref[:, c0:c1], preferred_element_type=F32)
        up = jnp.dot(h, wgu_ref[:, D_FF + c0:D_FF + c1], preferred_element_type=F32)
        act = (gate * jax.nn.sigmoid(gate) * up).astype(BF)
        acc = acc + jnp.dot(act, wd_ref[c0:c1, :], preferred_element_type=F32)
    o_ref[...] = xv + 0.5 * acc


def _ffn(x, g, w_gu, w_down, layer, *, tm=512):
    return pl.pallas_call(
        _ffn_kernel,
        grid=(N_TOK // tm,),
        in_specs=[pl.BlockSpec((tm, D_MODEL), lambda i: (i, 0)),
                  _resident((1, D_MODEL)),
                  _layer_weight(w_gu, layer),
                  _layer_weight(w_down, layer)],
        out_specs=pl.BlockSpec((tm, D_MODEL), lambda i: (i, 0)),
        out_shape=jax.ShapeDtypeStruct((N_TOK, D_MODEL), F32),
        compiler_params=_params(("parallel",), VMEM_LIMIT),
        name="ffn",
    )(x, g.reshape(1, D_MODEL), w_gu, w_down)


def _group_ones(gs):
    idx = np.arange(MXU_N) // gs
    return jnp.asarray(idx[:, None] == idx[None, :], dtype=BF)


def _proj_kernel(x_ref, g_ref, w_ref, cs_ref, bd_ref, o_ref, *, normed, gs):
    h = _rms(x_ref[...], g_ref[...]).astype(BF)

    def main(j):
        return jnp.dot(h, w_ref[:, MXU_N * j:MXU_N * (j + 1)], preferred_element_type=F32)

    acc_next = main(0)
    for j, is_normed in enumerate(normed):
        cols = slice(MXU_N * j, MXU_N * (j + 1))
        acc = acc_next
        if j + 1 < len(normed):
            acc_next = main(j + 1)
        y = acc * cs_ref[:, cols]
        if is_normed:
            ss = jnp.dot((acc * acc).astype(BF), bd_ref[...], preferred_element_type=F32)
            y = y * lax.rsqrt(ss * (1.0 / gs) + RMS_EPS)
        o_ref[2 * j] = y[:, :LANES].astype(BF)
        o_ref[2 * j + 1] = y[:, LANES:].astype(BF)


def _proj(x, g, w, layer, colscale, normed, gs, *, tm=512):
    rows = x.shape[0]
    wout = w.shape[2]
    assert wout == MXU_N * len(normed)
    c = wout // LANES
    return pl.pallas_call(
        functools.partial(_proj_kernel, normed=tuple(normed), gs=gs),
        grid=(rows // tm,),
        in_specs=[pl.BlockSpec((tm, D_MODEL), lambda i: (i, 0)),
                  _resident((1, D_MODEL)),
                  _layer_weight(w, layer),
                  _resident((1, wout)),
                  _resident((MXU_N, MXU_N))],
        out_specs=pl.BlockSpec((c, tm, LANES), lambda i: (0, i, 0)),
        out_shape=jax.ShapeDtypeStruct((c, rows, LANES), BF),
        compiler_params=_params(("parallel",), VMEM_LIMIT),
        name="proj",
    )(x, g.reshape(1, D_MODEL), w, colscale.reshape(1, wout).astype(F32), _group_ones(gs))


def _swa_kernel(slopes_ref, sinks_ref, q_ref, kp_ref, kc_ref, vp_ref, vc_ref, o_ref, *, kv_div, max_dist):
    n = pl.program_id(1)
    row = lax.broadcasted_iota(jnp.int32, (BLOCK, 2 * BLOCK), 0)
    col = lax.broadcasted_iota(jnp.int32, (BLOCK, 2 * BLOCK), 1)
    dist = row + BLOCK - col
    valid = (dist >= 0) & (dist <= max_dist) & ((col >= BLOCK) | (n > 0))
    negmask = jnp.where(valid, 0.0, -jnp.inf)
    distf = dist.astype(F32)
    lo = lax.broadcasted_iota(jnp.int32, (BLOCK, LANES), 1) < HEAD_DIM

    n_groups = q_ref.shape[0] // kv_div
    heads_per_group = 2 * kv_div
    scores = []
    for g in range(n_groups):
        parts = []
        for p in range(g * kv_div, (g + 1) * kv_div):
            q2 = q_ref[p, 0].astype(F32)
            parts += [jnp.where(lo, q2, 0.0), jnp.where(lo, 0.0, q2)]
        q_stack = jnp.concatenate(parts, axis=0).astype(BF)
        scores.append(jnp.concatenate(
            [lax.dot_general(q_stack, kp_ref[g, 0], NT_DIMS, preferred_element_type=F32),
             lax.dot_general(q_stack, kc_ref[g, 0], NT_DIMS, preferred_element_type=F32)], axis=1))
    soft = []
    for g in range(n_groups):
        res = []
        for j in range(heads_per_group):
            h = g * heads_per_group + j
            s = scores[g][j * BLOCK:(j + 1) * BLOCK] - slopes_ref[h] * distf + negmask
            m = jnp.maximum(jnp.max(s, axis=1, keepdims=True), sinks_ref[h])
            pe = jnp.exp(s - m)
            res.append((pe.astype(BF), jnp.sum(pe, axis=1, keepdims=True) + jnp.exp(sinks_ref[h] - m)))
        soft.append(res)
    for g in range(n_groups):
        pb = jnp.concatenate([r[0] for r in soft[g]], axis=0)
        pv = (jnp.dot(pb[:, :BLOCK], vp_ref[g, 0], preferred_element_type=F32)
              + jnp.dot(pb[:, BLOCK:], vc_ref[g, 0], preferred_element_type=F32))
        for jp in range(kv_div):
            o0 = pv[(2 * jp) * BLOCK:(2 * jp + 1) * BLOCK] / soft[g][2 * jp][1]
            o1 = pv[(2 * jp + 1) * BLOCK:(2 * jp + 2) * BLOCK] / soft[g][2 * jp + 1][1]
            o_ref[g * kv_div + jp, 0] = jnp.where(lo, o0, o1).astype(BF)


def _swa(qkv, slopes, sinks, *, n_pairs, n_kv, max_dist):
    c = qkv.shape[0]
    nb = SEQ // BLOCK
    view = qkv.reshape(c, BATCH, SEQ, LANES)
    k_idx, v_idx = n_pairs // n_kv, n_pairs // n_kv + 1

    def cur(size, idx):
        return pl.BlockSpec((size, 1, BLOCK, LANES), lambda b, n: (idx, b, n, 0))

    def prev(size, idx):
        return pl.BlockSpec((size, 1, BLOCK, LANES), lambda b, n: (idx, b, jnp.maximum(n - 1, 0), 0))

    smem = pl.BlockSpec(memory_space=pltpu.SMEM)
    out = pl.pallas_call(
        functools.partial(_swa_kernel, kv_div=n_pairs // n_kv, max_dist=max_dist),
        grid=(BATCH, nb),
        in_specs=[smem, smem, cur(n_pairs, 0), prev(n_kv, k_idx), cur(n_kv, k_idx),
                  prev(n_kv, v_idx), cur(n_kv, v_idx)],
        out_specs=pl.BlockSpec((n_pairs, 1, BLOCK, LANES), lambda b, n: (0, b, n, 0)),
        out_shape=jax.ShapeDtypeStruct((n_pairs, BATCH, SEQ, LANES), BF),
        compiler_params=_params(("parallel", "parallel")),
        name="swa",
    )(slopes, sinks, view, view, view, view, view)
    return out.reshape(n_pairs, N_TOK, LANES)


DIL_ORDER = tuple(sorted(C_PATTERNS, key=lambda wd: -wd[1]))
DIL_UNROLL = 8
DIL_AHEAD = 2
DIL_BASE = 4
DIL_Q = SEQ // DIL_BASE
DIL_CONVERT_ROWS = DIL_BASE * BLOCK


def _dilated_kernel(slopes_ref, q_ref, k_ref, v_ref, o_ref, tq_s, tk_s, tv_s, q0_s, q1_s, k_s, v_s,
                    acc_r, m_r, l_r, acc_n, m_n, l_n):
    assert all(d == 1 or d % DIL_BASE == 0 for _, d in DIL_ORDER) and DIL_ORDER[-1][1] == 1
    p = pl.program_id(1)
    lo = lax.broadcasted_iota(jnp.int32, (BLOCK, LANES), 1) < HEAD_DIM

    def convert(c, carry):
        rows = pl.ds(pl.multiple_of(c * DIL_CONVERT_ROWS, DIL_CONVERT_ROWS), DIL_CONVERT_ROWS)
        tq_s[...] = q_ref[0, rows, :].astype(F32)
        tk_s[...] = k_ref[0, rows, :].astype(F32)
        tv_s[...] = v_ref[0, rows, :].astype(F32)
        for rho in range(DIL_BASE):
            src = pl.ds(rho, BLOCK, stride=DIL_BASE)
            dst = pl.ds(pl.multiple_of(rho * DIL_Q + c * BLOCK, BLOCK), BLOCK)
            q = tq_s[src, :]
            q0_s[dst, :] = jnp.where(lo, q, 0.0)
            q1_s[dst, :] = jnp.where(lo, 0.0, q)
            k_s[dst, :] = tk_s[src, :]
            v_s[dst, :] = tv_s[src, :]
        return carry

    lax.fori_loop(0, SEQ // DIL_CONVERT_ROWS, convert, 0)

    row = lax.broadcasted_iota(jnp.int32, (BLOCK, 2 * BLOCK), 0)
    col = lax.broadcasted_iota(jnp.int32, (BLOCK, 2 * BLOCK), 1)
    dist = row + BLOCK - col
    distf = dist.astype(F32)
    no_prev = jnp.where(col < BLOCK, -jnp.inf, 0.0)

    def bcast2(a0, a1):
        return jnp.where(lo, jnp.broadcast_to(a0, (BLOCK, LANES)), jnp.broadcast_to(a1, (BLOCK, LANES)))

    for pi, (window, d) in enumerate(DIL_ORDER):
        first, last = pi == 0, pi == len(DIL_ORDER) - 1
        natural = d == 1
        nb = SEQ // d // BLOCK
        band = (dist >= 0) & (dist <= window // d)
        bias = [jnp.where(band, (-float(d) * slopes_ref[2 * p + hh]) * distf, -jnp.inf) for hh in range(2)]
        acc_s, m_s, l_s = (acc_n, m_n, l_n) if natural else (acc_r, m_r, l_r)

        if natural and not first:
            for rho in range(DIL_BASE):
                src, dst = pl.ds(rho * DIL_Q, DIL_Q), pl.ds(rho, DIL_Q, stride=DIL_BASE)
                acc_n[dst, :] = acc_r[src, :]
                m_n[dst, :] = m_r[src, :]
                l_n[dst, :] = l_r[src, :]

        def rows_of(r, n, d=d, natural=natural):
            if natural:
                return pl.ds(pl.multiple_of(BLOCK * n, BLOCK), BLOCK)
            inner = d // DIL_BASE
            start = (r % DIL_BASE) * DIL_Q + r // DIL_BASE + inner * BLOCK * n
            return pl.ds(start, BLOCK, stride=inner) if inner > 1 else pl.ds(pl.multiple_of(start, BLOCK), BLOCK)

        def step(it, carry, nb=nb, bias=bias, first=first, last=last, natural=natural, rows_of=rows_of,
                 acc_s=acc_s, m_s=m_s, l_s=l_s):
            assert DIL_UNROLL % nb == 0 or nb % DIL_UNROLL == 0
            load_k = (lambda rr: k_ref[0, rr, :]) if natural else (lambda rr: k_s[rr, :].astype(BF))
            load_v = (lambda rr: v_ref[0, rr, :]) if natural else (lambda rr: v_s[rr, :].astype(BF))

            def scores(u):
                t = it * DIL_UNROLL + u
                r, n = t // nb, t % nb
                prev = (u % nb != 0) if nb <= DIL_UNROLL else (True if u else None)
                rows = rows_of(r, n)
                prows = rows_of(r, jnp.maximum(n - 1, 0)) if prev is not False else None
                if natural:
                    q2 = q_ref[0, rows, :].astype(F32)
                    qh = jnp.concatenate([jnp.where(lo, q2, 0.0), jnp.where(lo, 0.0, q2)], axis=0).astype(BF)
                else:
                    qh = jnp.concatenate([q0_s[rows, :], q1_s[rows, :]], axis=0).astype(BF)
                s = lax.dot_general(qh, load_k(rows), NT_DIMS, preferred_element_type=F32)
                if prev is not False:
                    s = jnp.concatenate(
                        [lax.dot_general(qh, load_k(prows), NT_DIMS, preferred_element_type=F32), s], axis=1)
                return n, prev, rows, prows, (s[:BLOCK], s[BLOCK:])

            def softmax_pv(blk):
                n, prev, rows, prows, s = blk
                res = []
                for hh in range(2):
                    sh = s[hh] + (bias[hh][:, BLOCK:] if prev is False else bias[hh])
                    if prev is None:
                        sh = sh + jnp.where(n == 0, no_prev, 0.0)
                    m = jnp.max(sh, axis=1, keepdims=True)
                    pe = jnp.exp(sh - m)
                    res.append((m, jnp.sum(pe, axis=1, keepdims=True), pe.astype(BF)))
                pb = jnp.concatenate([res[0][2], res[1][2]], axis=0)
                if prev is False:
                    pv = jnp.dot(pb, load_v(rows), preferred_element_type=F32)
                else:
                    pv = (jnp.dot(pb[:, :BLOCK], load_v(prows), preferred_element_type=F32)
                          + jnp.dot(pb[:, BLOCK:], load_v(rows), preferred_element_type=F32))
                return rows, res, (pv[:BLOCK], pv[BLOCK:])

            def merge(rows, res, pv):
                m2 = bcast2(res[0][0], res[1][0])
                l2 = bcast2(res[0][1], res[1][1])
                acc2 = jnp.where(lo, pv[0], pv[1])
                if not first:
                    m_old = m_s[rows, :]
                    m_new = jnp.maximum(m_old, m2)
                    a_old, a_new = jnp.exp(m_old - m_new), jnp.exp(m2 - m_new)
                    l2 = a_old * l_s[rows, :] + a_new * l2
                    acc2 = a_old * acc_s[rows, :] + a_new * acc2
                    m2 = m_new
                if last:
                    o_ref[0, rows, :] = (acc2 / l2).astype(BF)
                else:
                    m_s[rows, :] = m2
                    l_s[rows, :] = l2
                    acc_s[rows, :] = acc2

            pending = {u: scores(u) for u in range(DIL_AHEAD)}
            done = None
            for u in range(DIL_UNROLL):
                if u + DIL_AHEAD < DIL_UNROLL:
                    pending[u + DIL_AHEAD] = scores(u + DIL_AHEAD)
                cur = softmax_pv(pending.pop(u))
                if done is not None:
                    merge(*done)
                done = cur
            merge(*done)
            return carry

        lax.fori_loop(0, SEQ // BLOCK // DIL_UNROLL, step, 0)


def _dilated(qkv, slopes):
    n_pairs = C_HEADS // 2
    seq_f32 = pltpu.VMEM((SEQ, LANES), F32)
    chunk_f32 = pltpu.VMEM((DIL_CONVERT_ROWS, LANES), F32)
    return pl.pallas_call(
        _dilated_kernel,
        grid=(BATCH, n_pairs),
        in_specs=[pl.BlockSpec(memory_space=pltpu.SMEM),
                  pl.BlockSpec((1, SEQ, LANES), lambda b, p: (p, b, 0)),
                  pl.BlockSpec((1, SEQ, LANES), lambda b, p: (n_pairs + p, b, 0)),
                  pl.BlockSpec((1, SEQ, LANES), lambda b, p: (2 * n_pairs + p, b, 0))],
        out_specs=pl.BlockSpec((1, SEQ, LANES), lambda b, p: (p, b, 0)),
        out_shape=jax.ShapeDtypeStruct((n_pairs, N_TOK, LANES), BF),
        scratch_shapes=[chunk_f32] * 3 + [seq_f32] * 10,
        compiler_params=_params(("parallel", "parallel"), VMEM_LIMIT),
        name="dilated",
    )(slopes, qkv, qkv, qkv)


SB_QB = MXU_N
SB_DEAD_LOG2 = -150.0
LOG2_E = 1.4426950408889634


def _sb_kernel(q_ref, k_ref, v_ref, uo_ref, o_ref):
    i = pl.program_id(2)
    q2 = q_ref[0].astype(F32)
    uo = uo_ref[...]
    lo = lax.broadcasted_iota(jnp.int32, (SB_QB, LANES), 1) < HEAD_DIM
    q_stack = jnp.concatenate([jnp.where(lo, q2, 0.0), jnp.where(lo, 0.0, q2)], axis=0).astype(BF)
    strict1 = (lax.broadcasted_iota(jnp.int32, (SB_QB, SB_QB), 1)
               < lax.broadcasted_iota(jnp.int32, (SB_QB, SB_QB), 0))
    strict = jnp.concatenate([strict1, strict1], axis=0)

    def scores(g):
        kt = k_ref[0, pl.ds(pl.multiple_of(g * SB_QB, SB_QB), SB_QB), :]
        return lax.dot_general(q_stack, kt, NT_DIMS, preferred_element_type=F32)

    def tile(g, carry, diagonal):
        z, c, o = carry
        z_next = scores(jnp.maximum(g - 1, 0))
        sp = jnp.maximum(z, 0.0) + jnp.log2(1.0 + jnp.exp2(-jnp.abs(z)))
        e = z - sp
        if diagonal:
            sp = jnp.where(strict, sp, 0.0)
        w = jnp.dot(sp.astype(BF), uo, preferred_element_type=F32)
        a = jnp.exp2(e + jnp.concatenate([c] * (SB_QB // LANES), axis=1) + w[:, :SB_QB])
        if diagonal:
            a = jnp.where(strict, a, 0.0)
        vt = v_ref[0, pl.ds(pl.multiple_of(g * SB_QB, SB_QB), SB_QB), :]
        pv = jnp.dot(a.astype(BF), vt, preferred_element_type=F32)
        return z_next, c + w[:, SB_QB:], o + jnp.where(lo, pv[:SB_QB], pv[SB_QB:])

    def alive(c):
        return jnp.max(c) > SB_DEAD_LOG2

    def body(state):
        g = state[0]
        z, c, o = tile(g, state[2:], False)
        return g - 1, alive(c), z, c, o

    z, c, o = tile(i, (scores(i), jnp.zeros((2 * SB_QB, LANES), F32), jnp.zeros((SB_QB, LANES), F32)), True)
    state = lax.while_loop(lambda st: (st[0] >= 0) & st[1], body, (i - 1, alive(c), z, c, o))
    o_ref[0] = state[4].astype(BF)


def _stick_breaking(qkv, q_c0, k_c0, v_c0, n_pairs):
    idx = np.arange(SB_QB)
    tri_ones = np.concatenate([idx[:, None] > idx[None, :], np.ones((SB_QB, LANES), bool)], axis=1)
    uo = jnp.asarray(-tri_ones.astype(np.float32), dtype=BF)
    nb = SEQ // SB_QB
    return pl.pallas_call(
        _sb_kernel,
        grid=(BATCH, n_pairs, nb),
        in_specs=[pl.BlockSpec((1, SB_QB, LANES), lambda b, p, i: (q_c0 + p, b * nb + i, 0)),
                  pl.BlockSpec((1, SEQ, LANES), lambda b, p, i: (k_c0 + p, b, 0)),
                  pl.BlockSpec((1, SEQ, LANES), lambda b, p, i: (v_c0 + p, b, 0)),
                  _resident((SB_QB, SB_QB + LANES))],
        out_specs=pl.BlockSpec((1, SB_QB, LANES), lambda b, p, i: (p, b * nb + i, 0)),
        out_shape=jax.ShapeDtypeStruct((n_pairs, N_TOK, LANES), BF),
        compiler_params=_params(("parallel", "parallel", "parallel")),
        name="stick_breaking",
    )(qkv, qkv, qkv, uo)


def _mix_xattn_kernel(*refs):
    x_ref = refs[0]
    wm_ref, g_ref, wq_ref, cs_ref, bd_ref, kv_ref, wo_ref, o_ref = refs[-8:]
    mixed = jnp.concatenate([r[c] for r in refs[1:-8] for c in range(r.shape[0])], axis=1)
    xv = x_ref[...] + jnp.dot(mixed, wm_ref[...], preferred_element_type=F32)
    h = _rms(xv, g_ref[...]).astype(BF)
    heads = range(X_HEADS)
    cols = [slice(X_HEAD_DIM * hd, X_HEAD_DIM * (hd + 1)) for hd in heads]
    acc = [jnp.dot(h, wq_ref[:, cols[hd]], preferred_element_type=F32) for hd in heads]
    ss = [jnp.dot((acc[hd] * acc[hd]).astype(BF), bd_ref[...], preferred_element_type=F32) for hd in heads]
    q = [(acc[hd] * cs_ref[:, cols[hd]] * lax.rsqrt(ss[hd] * (1.0 / X_HEAD_DIM) + RMS_EPS)).astype(BF)
         for hd in heads]
    s = [lax.dot_general(q[hd], jnp.concatenate([kv_ref[2 * hd], kv_ref[2 * hd + 1]], axis=1), NT_DIMS,
                         preferred_element_type=F32) for hd in heads]
    pe, l = [], []
    for hd in heads:
        e = jnp.exp(s[hd] - jnp.max(s[hd], axis=1, keepdims=True))
        l.append(jnp.sum(e, axis=1, keepdims=True))
        pe.append(e.astype(BF))
    v0 = 2 * X_HEADS
    pv = [jnp.dot(pe[hd], jnp.concatenate([kv_ref[v0 + 2 * hd], kv_ref[v0 + 2 * hd + 1]], axis=1),
                  preferred_element_type=F32) for hd in heads]
    o = jnp.concatenate([(pv[hd] / l[hd]).astype(BF) for hd in heads], axis=1)
    o_ref[...] = xv + jnp.dot(o, wo_ref[...], preferred_element_type=F32)


def _mix_xattn(x, mixer_heads, w_mix, mix_layer, g, w_q, q_colscale, kv, w_o, layer, *, tm=512):
    assert X_HEAD_DIM == MXU_N
    tiles_per_batch = SEQ // tm
    in_specs = [pl.BlockSpec((tm, D_MODEL), lambda i: (i, 0))]
    in_specs += [pl.BlockSpec((mh.shape[0], tm, LANES), lambda i: (0, i, 0)) for mh in mixer_heads]
    in_specs += [_layer_weight(w_mix, mix_layer),
                 _resident((1, D_MODEL)),
                 _layer_weight(w_q, layer),
                 _resident((1, D_MODEL)),
                 _resident((MXU_N, MXU_N)),
                 pl.BlockSpec((4 * X_HEADS, MEM_LEN, LANES), lambda i: (0, i // tiles_per_batch, 0)),
                 _layer_weight(w_o, layer)]
    return pl.pallas_call(
        _mix_xattn_kernel, grid=(N_TOK // tm,),
        in_specs=in_specs,
        out_specs=pl.BlockSpec((tm, D_MODEL), lambda i: (i, 0)),
        out_shape=jax.ShapeDtypeStruct((N_TOK, D_MODEL), F32),
        compiler_params=_params(("parallel",), VMEM_LIMIT),
        name="mix_xattn",
    )(x, *mixer_heads, w_mix, g.reshape(1, D_MODEL), w_q,
      q_colscale.reshape(1, D_MODEL).astype(F32), _group_ones(X_HEAD_DIM), kv, w_o)


def _alibi(n_heads):
    return jnp.asarray(2.0 ** (-8.0 * np.arange(1, n_heads + 1) / n_heads), dtype=F32)


def _widen_even_w_in(w_in):
    hd = HEAD_DIM
    qa, ka, va, rest = w_in[..., :512], w_in[..., 512:640], w_in[..., 640:768], w_in[..., 768:]
    dup = lambda w: jnp.concatenate([w[..., :hd], w[..., :hd], w[..., hd:], w[..., hd:]], axis=-1)
    return jnp.concatenate([qa, dup(ka), dup(va), rest], axis=-1)


def _even_mixer_heads(x, norm_g, w_aug, j, q_gain, k_gain, sinks):
    hd = HEAD_DIM
    scale = hd ** -0.5
    ones = lambda n: jnp.ones((n,), F32)
    cs = jnp.concatenate([jnp.tile(q_gain, A_Q_HEADS) * scale, jnp.tile(k_gain, 4), ones(256),
                          ones(512) * (scale * LOG2_E), ones(1024)])
    normed = [True, True, True] + [False] * 7
    p = _proj(x, norm_g, w_aug, j, cs, normed, hd)
    o_a = _swa(p, _alibi(A_Q_HEADS), sinks.astype(F32), n_pairs=A_Q_HEADS // 2, n_kv=A_KV_HEADS,
               max_dist=A_WINDOW - 1)
    o_b = _stick_breaking(p, 8, 12, 16, B_HEADS // 2)
    return [o_a, o_b]


def _odd_mixer_heads(x, norm_g, w_in, j, q_gain, k_gain):
    hd = HEAD_DIM
    cs = jnp.concatenate([jnp.tile(q_gain, C_HEADS) * hd ** -0.5, jnp.tile(k_gain, C_HEADS),
                          jnp.ones((C_HEADS * hd,), F32)])
    normed = [True] * 8 + [False] * 4
    p = _proj(x, norm_g, w_in, j, cs, normed, hd)
    return [_dilated(p, _alibi(C_HEADS))]


def _mix_and_cross_attention(x, mixer_heads, w_mix, j, mem2d, norm_g, mem_g, w_q, w_kv, q_gain, k_gain, w_o, layer):
    cs_kv = jnp.concatenate([jnp.tile(k_gain, X_HEADS), jnp.ones((D_MODEL,), F32)])
    kv = _proj(mem2d, mem_g, w_kv, layer, cs_kv, [True] * 4 + [False] * 4, X_HEAD_DIM, tm=256)
    cs_q = jnp.tile(q_gain, X_HEADS) * X_HEAD_DIM ** -0.5
    return _mix_xattn(x, mixer_heads, w_mix, j, norm_g, w_q, cs_q, kv, w_o, layer)


def kernel(x, mem, ffn1_norm, ffn1_w_gu, ffn1_w_down, mix_norm, ev_w_in, ev_q_gain, ev_k_gain, ev_sinks, ev_w_out, od_w_in, od_q_gain, od_k_gain, od_w_out, xa_norm, xa_mem_norm, xa_w_q, xa_w_kv, xa_q_gain, xa_k_gain, xa_w_o, ffn2_norm, ffn2_w_gu, ffn2_w_down):
    x = x.reshape(N_TOK, D_MODEL)
    mem2d = mem.reshape(BATCH * MEM_LEN, D_MODEL)
    ffn1_w_gu, ffn1_w_down, ffn2_w_gu, ffn2_w_down = map(_to_bf16, (ffn1_w_gu, ffn1_w_down, ffn2_w_gu, ffn2_w_down))
    ev_w_aug, ev_w_out, od_w_in, od_w_out = map(_to_bf16, (_widen_even_w_in(ev_w_in), ev_w_out, od_w_in, od_w_out))
    xa_w_q, xa_w_kv, xa_w_o = map(_to_bf16, (xa_w_q, xa_w_kv, xa_w_o))
    for layer in range(DEPTH):
        j = layer // 2
        x = _ffn(x, ffn1_norm[layer], ffn1_w_gu, ffn1_w_down, layer)
        if layer % 2 == 0:
            heads = _even_mixer_heads(x, mix_norm[layer], ev_w_aug, j, ev_q_gain[j], ev_k_gain[j], ev_sinks[j])
            w_mix = ev_w_out
        else:
            heads = _odd_mixer_heads(x, mix_norm[layer], od_w_in, j, od_q_gain[j], od_k_gain[j])
            w_mix = od_w_out
        x = _mix_and_cross_attention(x, heads, w_mix, j, mem2d, xa_norm[layer], xa_mem_norm[layer], xa_w_q,
                                     xa_w_kv, xa_q_gain[layer], xa_k_gain[layer], xa_w_o, layer)
        x = _ffn(x, ffn2_norm[layer], ffn2_w_gu, ffn2_w_down, layer)
    return x.reshape(BATCH, SEQ, D_MODEL)
```

```python
import functools

import numpy as np
import jax
import jax.numpy as jnp
from jax import lax
from jax.experimental import pallas as pl
from jax.experimental.pallas import tpu as pltpu

D_MODEL = 1024
BATCH = 4
SEQ = 4096
N_TOK = BATCH * SEQ
DEPTH = 2
HEAD_DIM = 64
BLOCK = 128
A_Q_HEADS = 8
A_KV_HEADS = 2
A_WINDOW = 128
B_HEADS = 8
C_HEADS = 16
C_PATTERNS = ((128, 1), (512, 4), (2048, 16))
MEM_LEN = 256
X_HEADS = 4
X_HEAD_DIM = D_MODEL // X_HEADS
D_FF = 2816
RMS_EPS = 1e-6

LANES = 128
MXU_N = 256
VMEM_LIMIT = 56 * 1024 * 1024

BF = jnp.bfloat16
F32 = jnp.float32
NT_DIMS = (((1,), (1,)), ((), ()))


def _params(sem, vmem=None):
    return pltpu.CompilerParams(dimension_semantics=sem, vmem_limit_bytes=vmem)


def _resident(shape):
    nd = len(shape)
    return pl.BlockSpec(shape, lambda *_: (0,) * nd, pipeline_mode=pl.Buffered(1))


BF16_SUBLANES = 16


def _cast_specs(job, steps):
    w3, layer = job
    _, r, c = w3.shape
    rb = next(rb for rb in range(BF16_SUBLANES, r + 1, BF16_SUBLANES) if r % rb == 0 and r // rb <= steps)
    last = r // rb - 1
    return (pl.BlockSpec((None, rb, c), lambda i: (layer, jnp.minimum(i, last), 0)),
            pl.BlockSpec((rb, c), lambda i: (jnp.minimum(i, last), 0)),
            jax.ShapeDtypeStruct((r, c), BF))


def _run_cast_jobs(in_refs, out_refs):
    for src, dst in zip(in_refs, out_refs):
        dst[...] = src[...].astype(BF)


def _cast_kernel(w_ref, o_ref):
    _run_cast_jobs([w_ref], [o_ref])


def _cast_now(job, *, rows=128):
    steps = job[0].shape[1] // rows
    in_spec, out_spec, out_shape = _cast_specs(job, steps)
    return pl.pallas_call(
        _cast_kernel, grid=(steps,), in_specs=[in_spec], out_specs=out_spec, out_shape=out_shape,
        compiler_params=_params(("arbitrary",)),
        name="cast",
    )(job[0])


def _rms(xv, g):
    ms = jnp.mean(xv * xv, axis=-1, keepdims=True)
    return xv * lax.rsqrt(ms + RMS_EPS) * g


FFN_SPLIT = (D_FF // MXU_N + 1) // 2 * MXU_N
FFN_CHUNKS = ((0, FFN_SPLIT), (FFN_SPLIT, D_FF))


def _ffn_kernel(*refs, n_jobs):
    x_ref, g_ref, wgu_ref, wd_ref = refs[:4]
    o_ref = refs[4 + n_jobs]
    xv = x_ref[...]
    h = _rms(xv, g_ref[...]).astype(BF)
    acc = jnp.zeros_like(xv)
    for c0, c1 in FFN_CHUNKS:
        gate = jnp.dot(h, wgu_ref[:, c0:c1], preferred_element_type=F32)
        up = jnp.dot(h, wgu_ref[:, D_FF + c0:D_FF + c1], preferred_element_type=F32)
        act = (gate * jax.nn.sigmoid(gate) * up).astype(BF)
        acc = acc + jnp.dot(act, wd_ref[c0:c1, :], preferred_element_type=F32)
    o_ref[...] = xv + 0.5 * acc
    _run_cast_jobs(refs[4:4 + n_jobs], refs[5 + n_jobs:])


def _ffn(x, g, w_gu, w_down, cast_jobs=(), *, tm=512):
    steps = N_TOK // tm
    specs = [_cast_specs(job, steps) for job in cast_jobs]
    out = pl.pallas_call(
        functools.partial(_ffn_kernel, n_jobs=len(cast_jobs)),
        grid=(steps,),
        in_specs=[pl.BlockSpec((tm, D_MODEL), lambda i: (i, 0)),
                  _resident((1, D_MODEL)),
                  _resident(w_gu.shape),
                  _resident(w_down.shape)] + [s[0] for s in specs],
        out_specs=[pl.BlockSpec((tm, D_MODEL), lambda i: (i, 0))] + [s[1] for s in specs],
        out_shape=[jax.ShapeDtypeStruct((N_TOK, D_MODEL), F32)] + [s[2] for s in specs],
        compiler_params=_params(("arbitrary",), VMEM_LIMIT),
        name="ffn",
    )(x, g.reshape(1, D_MODEL), w_gu, w_down, *[job[0] for job in cast_jobs])
    return out[0], out[1:]


def _group_ones(gs):
    idx = np.arange(MXU_N) // gs
    return jnp.asarray(idx[:, None] == idx[None, :], dtype=BF)


def _proj_kernel(x_ref, g_ref, w_ref, cs_ref, bd_ref, o_ref, *, normed, gs):
    h = _rms(x_ref[...], g_ref[...]).astype(BF)

    def main(j):
        return jnp.dot(h, w_ref[:, MXU_N * j:MXU_N * (j + 1)], preferred_element_type=F32)

    acc_next = main(0)
    for j, is_normed in enumerate(normed):
        cols = slice(MXU_N * j, MXU_N * (j + 1))
        acc = acc_next
        if j + 1 < len(normed):
            acc_next = main(j + 1)
        y = acc * cs_ref[:, cols]
        if is_normed:
            ss = jnp.dot((acc * acc).astype(BF), bd_ref[...], preferred_element_type=F32)
            y = y * lax.rsqrt(ss * (1.0 / gs) + RMS_EPS)
        o_ref[2 * j] = y[:, :LANES].astype(BF)
        o_ref[2 * j + 1] = y[:, LANES:].astype(BF)


def _proj(x, g, w, colscale, normed, gs, *, tm=512):
    rows = x.shape[0]
    wout = w.shape[1]
    assert wout == MXU_N * len(normed)
    c = wout // LANES
    return pl.pallas_call(
        functools.partial(_proj_kernel, normed=tuple(normed), gs=gs),
        grid=(rows // tm,),
        in_specs=[pl.BlockSpec((tm, D_MODEL), lambda i: (i, 0)),
                  _resident((1, D_MODEL)),
                  _resident(w.shape),
                  _resident((1, wout)),
                  _resident((MXU_N, MXU_N))],
        out_specs=pl.BlockSpec((c, tm, LANES), lambda i: (0, i, 0)),
        out_shape=jax.ShapeDtypeStruct((c, rows, LANES), BF),
        compiler_params=_params(("parallel",), VMEM_LIMIT),
        name="proj",
    )(x, g.reshape(1, D_MODEL), w, colscale.reshape(1, wout).astype(F32), _group_ones(gs))


def _swa_kernel(slopes_ref, sinks_ref, q_ref, kp_ref, kc_ref, vp_ref, vc_ref, o_ref, *, kv_div, max_dist):
    n = pl.program_id(1)
    row = lax.broadcasted_iota(jnp.int32, (BLOCK, 2 * BLOCK), 0)
    col = lax.broadcasted_iota(jnp.int32, (BLOCK, 2 * BLOCK), 1)
    dist = row + BLOCK - col
    valid = (dist >= 0) & (dist <= max_dist) & ((col >= BLOCK) | (n > 0))
    negmask = jnp.where(valid, 0.0, -jnp.inf)
    distf = dist.astype(F32)
    lo = lax.broadcasted_iota(jnp.int32, (BLOCK, LANES), 1) < HEAD_DIM

    n_groups = q_ref.shape[0] // kv_div
    heads_per_group = 2 * kv_div
    scores = []
    for g in range(n_groups):
        parts = []
        for p in range(g * kv_div, (g + 1) * kv_div):
            q2 = q_ref[p, 0].astype(F32)
            parts += [jnp.where(lo, q2, 0.0), jnp.where(lo, 0.0, q2)]
        q_stack = jnp.concatenate(parts, axis=0).astype(BF)
        scores.append(jnp.concatenate(
            [lax.dot_general(q_stack, kp_ref[g, 0], NT_DIMS, preferred_element_type=F32),
             lax.dot_general(q_stack, kc_ref[g, 0], NT_DIMS, preferred_element_type=F32)], axis=1))
    soft = []
    for g in range(n_groups):
        res = []
        for j in range(heads_per_group):
            h = g * heads_per_group + j
            s = scores[g][j * BLOCK:(j + 1) * BLOCK] - slopes_ref[h] * distf + negmask
            m = jnp.maximum(jnp.max(s, axis=1, keepdims=True), sinks_ref[h])
            pe = jnp.exp(s - m)
            res.append((pe.astype(BF), jnp.sum(pe, axis=1, keepdims=True) + jnp.exp(sinks_ref[h] - m)))
        soft.append(res)
    for g in range(n_groups):
        pb = jnp.concatenate([r[0] for r in soft[g]], axis=0)
        pv = (jnp.dot(pb[:, :BLOCK], vp_ref[g, 0], preferred_element_type=F32)
              + jnp.dot(pb[:, BLOCK:], vc_ref[g, 0], preferred_element_type=F32))
        for jp in range(kv_div):
            o0 = pv[(2 * jp) * BLOCK:(2 * jp + 1) * BLOCK] / soft[g][2 * jp][1]
            o1 = pv[(2 * jp + 1) * BLOCK:(2 * jp + 2) * BLOCK] / soft[g][2 * jp + 1][1]
            o_ref[g * kv_div + jp, 0] = jnp.where(lo, o0, o1).astype(BF)


def _swa(qkv, slopes, sinks, *, n_pairs, n_kv, max_dist):
    c = qkv.shape[0]
    nb = SEQ // BLOCK
    view = qkv.reshape(c, BATCH, SEQ, LANES)
    k_idx, v_idx = n_pairs // n_kv, n_pairs // n_kv + 1

    def cur(size, idx):
        return pl.BlockSpec((size, 1, BLOCK, LANES), lambda b, n: (idx, b, n, 0))

    def prev(size, idx):
        return pl.BlockSpec((size, 1, BLOCK, LANES), lambda b, n: (idx, b, jnp.maximum(n - 1, 0), 0))

    smem = pl.BlockSpec(memory_space=pltpu.SMEM)
    out = pl.pallas_call(
        functools.partial(_swa_kernel, kv_div=n_pairs // n_kv, max_dist=max_dist),
        grid=(BATCH, nb),
        in_specs=[smem, smem, cur(n_pairs, 0), prev(n_kv, k_idx), cur(n_kv, k_idx),
                  prev(n_kv, v_idx), cur(n_kv, v_idx)],
        out_specs=pl.BlockSpec((n_pairs, 1, BLOCK, LANES), lambda b, n: (0, b, n, 0)),
        out_shape=jax.ShapeDtypeStruct((n_pairs, BATCH, SEQ, LANES), BF),
        compiler_params=_params(("parallel", "parallel")),
        name="swa",
    )(slopes, sinks, view, view, view, view, view)
    return out.reshape(n_pairs, N_TOK, LANES)


DIL_ORDER = tuple(sorted(C_PATTERNS, key=lambda wd: -wd[1]))
DIL_UNROLL = 8
DIL_AHEAD = 2
DIL_BASE = 4
DIL_Q = SEQ // DIL_BASE
DIL_CONVERT_ROWS = DIL_BASE * BLOCK


def _dilated_kernel(slopes_ref, q_ref, k_ref, v_ref, o_ref, tq_s, tk_s, tv_s, q0_s, q1_s, k_s, v_s,
                    acc_r, m_r, l_r, acc_n, m_n, l_n):
    assert all(d == 1 or d % DIL_BASE == 0 for _, d in DIL_ORDER) and DIL_ORDER[-1][1] == 1
    p = pl.program_id(1)
    lo = lax.broadcasted_iota(jnp.int32, (BLOCK, LANES), 1) < HEAD_DIM

    def convert(c, carry):
        rows = pl.ds(pl.multiple_of(c * DIL_CONVERT_ROWS, DIL_CONVERT_ROWS), DIL_CONVERT_ROWS)
        tq_s[...] = q_ref[0, rows, :].astype(F32)
        tk_s[...] = k_ref[0, rows, :].astype(F32)
        tv_s[...] = v_ref[0, rows, :].astype(F32)
        for rho in range(DIL_BASE):
            src = pl.ds(rho, BLOCK, stride=DIL_BASE)
            dst = pl.ds(pl.multiple_of(rho * DIL_Q + c * BLOCK, BLOCK), BLOCK)
            q = tq_s[src, :]
            q0_s[dst, :] = jnp.where(lo, q, 0.0)
            q1_s[dst, :] = jnp.where(lo, 0.0, q)
            k_s[dst, :] = tk_s[src, :]
            v_s[dst, :] = tv_s[src, :]
        return carry

    lax.fori_loop(0, SEQ // DIL_CONVERT_ROWS, convert, 0)

    row = lax.broadcasted_iota(jnp.int32, (BLOCK, 2 * BLOCK), 0)
    col = lax.broadcasted_iota(jnp.int32, (BLOCK, 2 * BLOCK), 1)
    dist = row + BLOCK - col
    distf = dist.astype(F32)
    no_prev = jnp.where(col < BLOCK, -jnp.inf, 0.0)

    def bcast2(a0, a1):
        return jnp.where(lo, jnp.broadcast_to(a0, (BLOCK, LANES)), jnp.broadcast_to(a1, (BLOCK, LANES)))

    for pi, (window, d) in enumerate(DIL_ORDER):
        first, last = pi == 0, pi == len(DIL_ORDER) - 1
        natural = d == 1
        nb = SEQ // d // BLOCK
        band = (dist >= 0) & (dist <= window // d)
        bias = [jnp.where(band, (-float(d) * slopes_ref[2 * p + hh]) * distf, -jnp.inf) for hh in range(2)]
        acc_s, m_s, l_s = (acc_n, m_n, l_n) if natural else (acc_r, m_r, l_r)

        if natural and not first:
            for rho in range(DIL_BASE):
                src, dst = pl.ds(rho * DIL_Q, DIL_Q), pl.ds(rho, DIL_Q, stride=DIL_BASE)
                acc_n[dst, :] = acc_r[src, :]
                m_n[dst, :] = m_r[src, :]
                l_n[dst, :] = l_r[src, :]

        def rows_of(r, n, d=d, natural=natural):
            if natural:
                return pl.ds(pl.multiple_of(BLOCK * n, BLOCK), BLOCK)
            inner = d // DIL_BASE
            start = (r % DIL_BASE) * DIL_Q + r // DIL_BASE + inner * BLOCK * n
            return pl.ds(start, BLOCK, stride=inner) if inner > 1 else pl.ds(pl.multiple_of(start, BLOCK), BLOCK)

        def step(it, carry, nb=nb, bias=bias, first=first, last=last, natural=natural, rows_of=rows_of,
                 acc_s=acc_s, m_s=m_s, l_s=l_s):
            assert DIL_UNROLL % nb == 0 or nb % DIL_UNROLL == 0
            load_k = (lambda rr: k_ref[0, rr, :]) if natural else (lambda rr: k_s[rr, :].astype(BF))
            load_v = (lambda rr: v_ref[0, rr, :]) if natural else (lambda rr: v_s[rr, :].astype(BF))

            def scores(u):
                t = it * DIL_UNROLL + u
                r, n = t // nb, t % nb
                prev = (u % nb != 0) if nb <= DIL_UNROLL else (True if u else None)
                rows = rows_of(r, n)
                prows = rows_of(r, jnp.maximum(n - 1, 0)) if prev is not False else None
                if natural:
                    q2 = q_ref[0, rows, :].astype(F32)
                    qh = jnp.concatenate([jnp.where(lo, q2, 0.0), jnp.where(lo, 0.0, q2)], axis=0).astype(BF)
                else:
                    qh = jnp.concatenate([q0_s[rows, :], q1_s[rows, :]], axis=0).astype(BF)
                s = lax.dot_general(qh, load_k(rows), NT_DIMS, preferred_element_type=F32)
                if prev is not False:
                    s = jnp.concatenate(
                        [lax.dot_general(qh, load_k(prows), NT_DIMS, preferred_element_type=F32), s], axis=1)
                return n, prev, rows, prows, (s[:BLOCK], s[BLOCK:])

            def softmax_pv(blk):
                n, prev, rows, prows, s = blk
                res = []
                for hh in range(2):
                    sh = s[hh] + (bias[hh][:, BLOCK:] if prev is False else bias[hh])
                    if prev is None:
                        sh = sh + jnp.where(n == 0, no_prev, 0.0)
                    m = jnp.max(sh, axis=1, keepdims=True)
                    pe = jnp.exp(sh - m)
                    res.append((m, jnp.sum(pe, axis=1, keepdims=True), pe.astype(BF)))
                pb = jnp.concatenate([res[0][2], res[1][2]], axis=0)
                if prev is False:
                    pv = jnp.dot(pb, load_v(rows), preferred_element_type=F32)
                else:
                    pv = (jnp.dot(pb[:, :BLOCK], load_v(prows), preferred_element_type=F32)
                          + jnp.dot(pb[:, BLOCK:], load_v(rows), preferred_element_type=F32))
                return rows, res, (pv[:BLOCK], pv[BLOCK:])

            def merge(rows, res, pv):
                m2 = bcast2(res[0][0], res[1][0])
                l2 = bcast2(res[0][1], res[1][1])
                acc2 = jnp.where(lo, pv[0], pv[1])
                if not first:
                    m_old = m_s[rows, :]
                    m_new = jnp.maximum(m_old, m2)
                    a_old, a_new = jnp.exp(m_old - m_new), jnp.exp(m2 - m_new)
                    l2 = a_old * l_s[rows, :] + a_new * l2
                    acc2 = a_old * acc_s[rows, :] + a_new * acc2
                    m2 = m_new
                if last:
                    o_ref[0, rows, :] = (acc2 / l2).astype(BF)
                else:
                    m_s[rows, :] = m2
                    l_s[rows, :] = l2
                    acc_s[rows, :] = acc2

            pending = {u: scores(u) for u in range(DIL_AHEAD)}
            done = None
            for u in range(DIL_UNROLL):
                if u + DIL_AHEAD < DIL_UNROLL:
                    pending[u + DIL_AHEAD] = scores(u + DIL_AHEAD)
                cur = softmax_pv(pending.pop(u))
                if done is not None:
                    merge(*done)
                done = cur
            merge(*done)
            return carry

        lax.fori_loop(0, SEQ // BLOCK // DIL_UNROLL, step, 0)


def _dilated(qkv, slopes):
    n_pairs = C_HEADS // 2
    seq_f32 = pltpu.VMEM((SEQ, LANES), F32)
    chunk_f32 = pltpu.VMEM((DIL_CONVERT_ROWS, LANES), F32)
    return pl.pallas_call(
        _dilated_kernel,
        grid=(BATCH, n_pairs),
        in_specs=[pl.BlockSpec(memory_space=pltpu.SMEM),
                  pl.BlockSpec((1, SEQ, LANES), lambda b, p: (p, b, 0)),
                  pl.BlockSpec((1, SEQ, LANES), lambda b, p: (n_pairs + p, b, 0)),
                  pl.BlockSpec((1, SEQ, LANES), lambda b, p: (2 * n_pairs + p, b, 0))],
        out_specs=pl.BlockSpec((1, SEQ, LANES), lambda b, p: (p, b, 0)),
        out_shape=jax.ShapeDtypeStruct((n_pairs, N_TOK, LANES), BF),
        scratch_shapes=[chunk_f32] * 3 + [seq_f32] * 10,
        compiler_params=_params(("parallel", "parallel"), VMEM_LIMIT),
        name="dilated",
    )(slopes, qkv, qkv, qkv)


SB_QB = MXU_N
SB_DEAD_LOG2 = -150.0
LOG2_E = 1.4426950408889634


def _sb_kernel(q_ref, k_ref, v_ref, uo_ref, o_ref):
    i = pl.program_id(2)
    q2 = q_ref[0].astype(F32)
    uo = uo_ref[...]
    lo = lax.broadcasted_iota(jnp.int32, (SB_QB, LANES), 1) < HEAD_DIM
    q_stack = jnp.concatenate([jnp.where(lo, q2, 0.0), jnp.where(lo, 0.0, q2)], axis=0).astype(BF)
    strict1 = (lax.broadcasted_iota(jnp.int32, (SB_QB, SB_QB), 1)
               < lax.broadcasted_iota(jnp.int32, (SB_QB, SB_QB), 0))
    strict = jnp.concatenate([strict1, strict1], axis=0)

    def scores(g):
        kt = k_ref[0, pl.ds(pl.multiple_of(g * SB_QB, SB_QB), SB_QB), :]
        return lax.dot_general(q_stack, kt, NT_DIMS, preferred_element_type=F32)

    def tile(g, carry, diagonal):
        z, c, o = carry
        z_next = scores(jnp.maximum(g - 1, 0))
        sp = jnp.maximum(z, 0.0) + jnp.log2(1.0 + jnp.exp2(-jnp.abs(z)))
        e = z - sp
        if diagonal:
            sp = jnp.where(strict, sp, 0.0)
        w = jnp.dot(sp.astype(BF), uo, preferred_element_type=F32)
        a = jnp.exp2(e + jnp.concatenate([c] * (SB_QB // LANES), axis=1) + w[:, :SB_QB])
        if diagonal:
            a = jnp.where(strict, a, 0.0)
        vt = v_ref[0, pl.ds(pl.multiple_of(g * SB_QB, SB_QB), SB_QB), :]
        pv = jnp.dot(a.astype(BF), vt, preferred_element_type=F32)
        return z_next, c + w[:, SB_QB:], o + jnp.where(lo, pv[:SB_QB], pv[SB_QB:])

    def alive(c):
        return jnp.max(c) > SB_DEAD_LOG2

    def body(state):
        g = state[0]
        z, c, o = tile(g, state[2:], False)
        return g - 1, alive(c), z, c, o

    z, c, o = tile(i, (scores(i), jnp.zeros((2 * SB_QB, LANES), F32), jnp.zeros((SB_QB, LANES), F32)), True)
    state = lax.while_loop(lambda st: (st[0] >= 0) & st[1], body, (i - 1, alive(c), z, c, o))
    o_ref[0] = state[4].astype(BF)


def _stick_breaking(qkv, q_c0, k_c0, v_c0, n_pairs):
    idx = np.arange(SB_QB)
    tri_ones = np.concatenate([idx[:, None] > idx[None, :], np.ones((SB_QB, LANES), bool)], axis=1)
    uo = jnp.asarray(-tri_ones.astype(np.float32), dtype=BF)
    nb = SEQ // SB_QB
    return pl.pallas_call(
        _sb_kernel,
        grid=(BATCH, n_pairs, nb),
        in_specs=[pl.BlockSpec((1, SB_QB, LANES), lambda b, p, i: (q_c0 + p, b * nb + i, 0)),
                  pl.BlockSpec((1, SEQ, LANES), lambda b, p, i: (k_c0 + p, b, 0)),
                  pl.BlockSpec((1, SEQ, LANES), lambda b, p, i: (v_c0 + p, b, 0)),
                  _resident((SB_QB, SB_QB + LANES))],
        out_specs=pl.BlockSpec((1, SB_QB, LANES), lambda b, p, i: (p, b * nb + i, 0)),
        out_shape=jax.ShapeDtypeStruct((n_pairs, N_TOK, LANES), BF),
        compiler_params=_params(("parallel", "parallel", "parallel")),
        name="stick_breaking",
    )(qkv, qkv, qkv, uo)


def _mix_xattn_kernel(*refs):
    x_ref = refs[0]
    wm_ref, g_ref, wq_ref, cs_ref, bd_ref, kv_ref, wo_ref, o_ref = refs[-8:]
    mixed = jnp.concatenate([r[c] for r in refs[1:-8] for c in range(r.shape[0])], axis=1)
    xv = x_ref[...] + jnp.dot(mixed, wm_ref[...], preferred_element_type=F32)
    h = _rms(xv, g_ref[...]).astype(BF)
    heads = range(X_HEADS)
    cols = [slice(X_HEAD_DIM * hd, X_HEAD_DIM * (hd + 1)) for hd in heads]
    acc = [jnp.dot(h, wq_ref[:, cols[hd]], preferred_element_type=F32) for hd in heads]
    ss = [jnp.dot((acc[hd] * acc[hd]).astype(BF), bd_ref[...], preferred_element_type=F32) for hd in heads]
    q = [(acc[hd] * cs_ref[:, cols[hd]] * lax.rsqrt(ss[hd] * (1.0 / X_HEAD_DIM) + RMS_EPS)).astype(BF)
         for hd in heads]
    s = [lax.dot_general(q[hd], jnp.concatenate([kv_ref[2 * hd], kv_ref[2 * hd + 1]], axis=1), NT_DIMS,
                         preferred_element_type=F32) for hd in heads]
    pe, l = [], []
    for hd in heads:
        e = jnp.exp(s[hd] - jnp.max(s[hd], axis=1, keepdims=True))
        l.append(jnp.sum(e, axis=1, keepdims=True))
        pe.append(e.astype(BF))
    v0 = 2 * X_HEADS
    pv = [jnp.dot(pe[hd], jnp.concatenate([kv_ref[v0 + 2 * hd], kv_ref[v0 + 2 * hd + 1]], axis=1),
                  preferred_element_type=F32) for hd in heads]
    o = jnp.concatenate([(pv[hd] / l[hd]).astype(BF) for hd in heads], axis=1)
    o_ref[...] = xv + jnp.dot(o, wo_ref[...], preferred_element_type=F32)


def _mix_xattn(x, mixer_heads, w_mix, g, w_q, q_colscale, kv, w_o, *, tm=512):
    assert X_HEAD_DIM == MXU_N
    tiles_per_batch = SEQ // tm
    in_specs = [pl.BlockSpec((tm, D_MODEL), lambda i: (i, 0))]
    in_specs += [pl.BlockSpec((mh.shape[0], tm, LANES), lambda i: (0, i, 0)) for mh in mixer_heads]
    in_specs += [_resident(w_mix.shape),
                 _resident((1, D_MODEL)),
                 _resident(w_q.shape),
                 _resident((1, D_MODEL)),
                 _resident((MXU_N, MXU_N)),
                 pl.BlockSpec((4 * X_HEADS, MEM_LEN, LANES), lambda i: (0, i // tiles_per_batch, 0)),
                 _resident(w_o.shape)]
    return pl.pallas_call(
        _mix_xattn_kernel, grid=(N_TOK // tm,),
        in_specs=in_specs,
        out_specs=pl.BlockSpec((tm, D_MODEL), lambda i: (i, 0)),
        out_shape=jax.ShapeDtypeStruct((N_TOK, D_MODEL), F32),
        compiler_params=_params(("parallel",), VMEM_LIMIT),
        name="mix_xattn",
    )(x, *mixer_heads, w_mix, g.reshape(1, D_MODEL), w_q,
      q_colscale.reshape(1, D_MODEL).astype(F32), _group_ones(X_HEAD_DIM), kv, w_o)


def _alibi(n_heads):
    return jnp.asarray(2.0 ** (-8.0 * np.arange(1, n_heads + 1) / n_heads), dtype=F32)


def _widen_even_w_in(w_in):
    hd = HEAD_DIM
    qa, ka, va, rest = w_in[..., :512], w_in[..., 512:640], w_in[..., 640:768], w_in[..., 768:]
    dup = lambda w: jnp.concatenate([w[..., :hd], w[..., :hd], w[..., hd:], w[..., hd:]], axis=-1)
    return jnp.concatenate([qa, dup(ka), dup(va), rest], axis=-1)


def _even_mixer_heads(x, norm_g, w_aug, q_gain, k_gain, sinks):
    hd = HEAD_DIM
    scale = hd ** -0.5
    ones = lambda n: jnp.ones((n,), F32)
    cs = jnp.concatenate([jnp.tile(q_gain, A_Q_HEADS) * scale, jnp.tile(k_gain, 4), ones(256),
                          ones(512) * (scale * LOG2_E), ones(1024)])
    normed = [True, True, True] + [False] * 7
    p = _proj(x, norm_g, w_aug, cs, normed, hd)
    o_a = _swa(p, _alibi(A_Q_HEADS), sinks.astype(F32), n_pairs=A_Q_HEADS // 2, n_kv=A_KV_HEADS,
               max_dist=A_WINDOW - 1)
    o_b = _stick_breaking(p, 8, 12, 16, B_HEADS // 2)
    return [o_a, o_b]


def _odd_mixer_heads(x, norm_g, w_in, q_gain, k_gain):
    hd = HEAD_DIM
    cs = jnp.concatenate([jnp.tile(q_gain, C_HEADS) * hd ** -0.5, jnp.tile(k_gain, C_HEADS),
                          jnp.ones((C_HEADS * hd,), F32)])
    normed = [True] * 8 + [False] * 4
    p = _proj(x, norm_g, w_in, cs, normed, hd)
    return [_dilated(p, _alibi(C_HEADS))]


def _mix_and_cross_attention(x, mixer_heads, w_mix, mem2d, norm_g, mem_g, w_q, w_kv, q_gain, k_gain, w_o):
    cs_kv = jnp.concatenate([jnp.tile(k_gain, X_HEADS), jnp.ones((D_MODEL,), F32)])
    kv = _proj(mem2d, mem_g, w_kv, cs_kv, [True] * 4 + [False] * 4, X_HEAD_DIM, tm=256)
    cs_q = jnp.tile(q_gain, X_HEADS) * X_HEAD_DIM ** -0.5
    return _mix_xattn(x, mixer_heads, w_mix, norm_g, w_q, cs_q, kv, w_o)


def kernel(x, mem, ffn1_norm, ffn1_w_gu, ffn1_w_down, mix_norm, ev_w_in, ev_q_gain, ev_k_gain, ev_sinks, ev_w_out, od_w_in, od_q_gain, od_k_gain, od_w_out, xa_norm, xa_mem_norm, xa_w_q, xa_w_kv, xa_q_gain, xa_k_gain, xa_w_o, ffn2_norm, ffn2_w_gu, ffn2_w_down):
    x = x.reshape(N_TOK, D_MODEL)
    mem2d = mem.reshape(BATCH * MEM_LEN, D_MODEL)
    ev_w_aug = _widen_even_w_in(ev_w_in)
    w_gu, w_down = _cast_now((ffn1_w_gu, 0)), _cast_now((ffn1_w_down, 0))
    for layer in range(DEPTH):
        j = layer // 2
        even = layer % 2 == 0
        w_in3, w_mix3 = (ev_w_aug, ev_w_out) if even else (od_w_in, od_w_out)
        jobs = [(w_in3, j), (w_mix3, j), (xa_w_q, layer), (xa_w_kv, layer), (xa_w_o, layer),
                (ffn2_w_gu, layer), (ffn2_w_down, layer)]
        x, (w_in, w_mix, w_q, w_kv, w_o, w_gu, w_down) = _ffn(x, ffn1_norm[layer], w_gu, w_down, jobs)
        if even:
            heads = _even_mixer_heads(x, mix_norm[layer], w_in, ev_q_gain[j], ev_k_gain[j], ev_sinks[j])
        else:
            heads = _odd_mixer_heads(x, mix_norm[layer], w_in, od_q_gain[j], od_k_gain[j])
        x = _mix_and_cross_attention(x, heads, w_mix, mem2d, xa_norm[layer], xa_mem_norm[layer], w_q, w_kv,
                                     xa_q_gain[layer], xa_k_gain[layer], w_o)
        jobs = [(ffn1_w_gu, layer + 1), (ffn1_w_down, layer + 1)] if layer + 1 < DEPTH else []
        x, next_ffn1 = _ffn(x, ffn2_norm[layer], w_gu, w_down, jobs)
        if next_ffn1:
            w_gu, w_down = next_ffn1
    return x.reshape(BATCH, SEQ, D_MODEL)
```

```python
import functools

import numpy as np
import jax
import jax.numpy as jnp
from jax import lax
from jax.experimental import pallas as pl
from jax.experimental.pallas import tpu as pltpu

D_MODEL = 1024
BATCH = 4
SEQ = 4096
N_TOK = BATCH * SEQ
DEPTH = 2
HEAD_DIM = 64
BLOCK = 128
A_Q_HEADS = 8
A_KV_HEADS = 2
A_WINDOW = 128
B_HEADS = 8
C_HEADS = 16
C_PATTERNS = ((128, 1), (512, 4), (2048, 16))
MEM_LEN = 256
X_HEADS = 4
X_HEAD_DIM = D_MODEL // X_HEADS
D_FF = 2816
RMS_EPS = 1e-6

LANES = 128
MXU_N = 256
VMEM_LIMIT = 56 * 1024 * 1024

BF = jnp.bfloat16
F32 = jnp.float32
NT_DIMS = (((1,), (1,)), ((), ()))
LOG2_E = 1.4426950408889634


def _params(sem, vmem=None):
    return pltpu.CompilerParams(dimension_semantics=sem, vmem_limit_bytes=vmem)


def _resident(shape):
    nd = len(shape)
    return pl.BlockSpec(shape, lambda *_: (0,) * nd, pipeline_mode=pl.Buffered(1))


BF16_SUBLANES = 16


def _cast_specs(job, steps):
    w3, layer = job
    _, r, c = w3.shape
    rb = next(rb for rb in range(BF16_SUBLANES, r + 1, BF16_SUBLANES) if r % rb == 0 and r // rb <= steps)
    last = r // rb - 1
    return (pl.BlockSpec((None, rb, c), lambda i: (layer, jnp.minimum(i, last), 0)),
            pl.BlockSpec((rb, c), lambda i: (jnp.minimum(i, last), 0)),
            jax.ShapeDtypeStruct((r, c), BF))


def _run_cast_jobs(in_refs, out_refs):
    for src, dst in zip(in_refs, out_refs):
        dst[...] = src[...].astype(BF)


def _cast_kernel(w_ref, o_ref):
    _run_cast_jobs([w_ref], [o_ref])


def _cast_now(job, *, rows=128):
    steps = job[0].shape[1] // rows
    in_spec, out_spec, out_shape = _cast_specs(job, steps)
    return pl.pallas_call(
        _cast_kernel, grid=(steps,), in_specs=[in_spec], out_specs=out_spec, out_shape=out_shape,
        compiler_params=_params(("arbitrary",)),
        name="cast",
    )(job[0])


def _rms(xv, g):
    ms = jnp.mean(xv * xv, axis=-1, keepdims=True)
    return xv * lax.rsqrt(ms + RMS_EPS) * g


FFN_SPLIT = (D_FF // MXU_N + 1) // 2 * MXU_N
FFN_CHUNKS = ((0, FFN_SPLIT), (FFN_SPLIT, D_FF))


def _ffn_kernel(*refs, n_jobs):
    x_ref, g_ref, wgu_ref, wd_ref = refs[:4]
    o_ref = refs[4 + n_jobs]
    xv = x_ref[...]
    h = _rms(xv, g_ref[...]).astype(BF)
    acc = jnp.zeros_like(xv)
    for c0, c1 in FFN_CHUNKS:
        gate = jnp.dot(h, wgu_ref[:, c0:c1], preferred_element_type=F32)
        up = jnp.dot(h, wgu_ref[:, D_FF + c0:D_FF + c1], preferred_element_type=F32)
        act = (gate * jax.nn.sigmoid(gate) * up).astype(BF)
        acc = acc + jnp.dot(act, wd_ref[c0:c1, :], preferred_element_type=F32)
    o_ref[...] = xv + 0.5 * acc
    _run_cast_jobs(refs[4:4 + n_jobs], refs[5 + n_jobs:])


def _ffn(x, g, w_gu, w_down, cast_jobs=(), *, tm=512):
    steps = N_TOK // tm
    specs = [_cast_specs(job, steps) for job in cast_jobs]
    out = pl.pallas_call(
        functools.partial(_ffn_kernel, n_jobs=len(cast_jobs)),
        grid=(steps,),
        in_specs=[pl.BlockSpec((tm, D_MODEL), lambda i: (i, 0)),
                  _resident((1, D_MODEL)),
                  _resident(w_gu.shape),
                  _resident(w_down.shape)] + [s[0] for s in specs],
        out_specs=[pl.BlockSpec((tm, D_MODEL), lambda i: (i, 0))] + [s[1] for s in specs],
        out_shape=[jax.ShapeDtypeStruct((N_TOK, D_MODEL), F32)] + [s[2] for s in specs],
        compiler_params=_params(("arbitrary",), VMEM_LIMIT),
        name="ffn",
    )(x, g.reshape(1, D_MODEL), w_gu, w_down, *[job[0] for job in cast_jobs])
    return out[0], out[1:]


def _group_ones(gs):
    idx = np.arange(MXU_N) // gs
    return jnp.asarray(idx[:, None] == idx[None, :], dtype=BF)


def _proj_kernel(x_ref, g_ref, w_ref, cs_ref, bd_ref, o_ref, *, normed, gs):
    h = _rms(x_ref[...], g_ref[...]).astype(BF)

    def main(j):
        return jnp.dot(h, w_ref[:, MXU_N * j:MXU_N * (j + 1)], preferred_element_type=F32)

    acc_next = main(0)
    for j, is_normed in enumerate(normed):
        cols = slice(MXU_N * j, MXU_N * (j + 1))
        acc = acc_next
        if j + 1 < len(normed):
            acc_next = main(j + 1)
        y = acc * cs_ref[:, cols]
        if is_normed:
            ss = jnp.dot((acc * acc).astype(BF), bd_ref[...], preferred_element_type=F32)
            y = y * lax.rsqrt(ss * (1.0 / gs) + RMS_EPS)
        o_ref[2 * j] = y[:, :LANES].astype(BF)
        o_ref[2 * j + 1] = y[:, LANES:].astype(BF)


def _proj(x, g, w, colscale, normed, gs, *, tm=512):
    rows = x.shape[0]
    wout = w.shape[1]
    assert wout == MXU_N * len(normed)
    c = wout // LANES
    return pl.pallas_call(
        functools.partial(_proj_kernel, normed=tuple(normed), gs=gs),
        grid=(rows // tm,),
        in_specs=[pl.BlockSpec((tm, D_MODEL), lambda i: (i, 0)),
                  _resident((1, D_MODEL)),
                  _resident(w.shape),
                  _resident((1, wout)),
                  _resident((MXU_N, MXU_N))],
        out_specs=pl.BlockSpec((c, tm, LANES), lambda i: (0, i, 0)),
        out_shape=jax.ShapeDtypeStruct((c, rows, LANES), BF),
        compiler_params=_params(("parallel",), VMEM_LIMIT),
        name="proj",
    )(x, g.reshape(1, D_MODEL), w, colscale.reshape(1, wout).astype(F32), _group_ones(gs))


def _swa_kernel(slopes_ref, sinks_ref, q_ref, kp_ref, kc_ref, vp_ref, vc_ref, o_ref, *, kv_div, max_dist):
    n = pl.program_id(1)
    row = lax.broadcasted_iota(jnp.int32, (BLOCK, 2 * BLOCK), 0)
    col = lax.broadcasted_iota(jnp.int32, (BLOCK, 2 * BLOCK), 1)
    dist = row + BLOCK - col
    valid = (dist >= 0) & (dist <= max_dist) & ((col >= BLOCK) | (n > 0))
    negmask = jnp.where(valid, 0.0, -jnp.inf)
    distf = dist.astype(F32)
    lo = lax.broadcasted_iota(jnp.int32, (BLOCK, LANES), 1) < HEAD_DIM

    n_groups = q_ref.shape[0] // kv_div
    heads_per_group = 2 * kv_div
    scores = []
    for g in range(n_groups):
        parts = []
        for p in range(g * kv_div, (g + 1) * kv_div):
            q2 = q_ref[p, 0].astype(F32)
            parts += [jnp.where(lo, q2, 0.0), jnp.where(lo, 0.0, q2)]
        q_stack = jnp.concatenate(parts, axis=0).astype(BF)
        scores.append(jnp.concatenate(
            [lax.dot_general(q_stack, kp_ref[g, 0], NT_DIMS, preferred_element_type=F32),
             lax.dot_general(q_stack, kc_ref[g, 0], NT_DIMS, preferred_element_type=F32)], axis=1))
    soft = []
    for g in range(n_groups):
        res = []
        for j in range(heads_per_group):
            h = g * heads_per_group + j
            s = scores[g][j * BLOCK:(j + 1) * BLOCK] - slopes_ref[h] * distf + negmask
            m = jnp.maximum(jnp.max(s, axis=1, keepdims=True), sinks_ref[h])
            pe = jnp.exp2(s - m)
            res.append((pe.astype(BF), jnp.sum(pe, axis=1, keepdims=True) + jnp.exp2(sinks_ref[h] - m)))
        soft.append(res)
    for g in range(n_groups):
        pb = jnp.concatenate([r[0] for r in soft[g]], axis=0)
        pv = (jnp.dot(pb[:, :BLOCK], vp_ref[g, 0], preferred_element_type=F32)
              + jnp.dot(pb[:, BLOCK:], vc_ref[g, 0], preferred_element_type=F32))
        for jp in range(kv_div):
            o0 = pv[(2 * jp) * BLOCK:(2 * jp + 1) * BLOCK] / soft[g][2 * jp][1]
            o1 = pv[(2 * jp + 1) * BLOCK:(2 * jp + 2) * BLOCK] / soft[g][2 * jp + 1][1]
            o_ref[g * kv_div + jp, 0] = jnp.where(lo, o0, o1).astype(BF)


def _swa(qkv, slopes, sinks, *, n_pairs, n_kv, max_dist):
    c = qkv.shape[0]
    nb = SEQ // BLOCK
    view = qkv.reshape(c, BATCH, SEQ, LANES)
    k_idx, v_idx = n_pairs // n_kv, n_pairs // n_kv + 1

    def cur(size, idx):
        return pl.BlockSpec((size, 1, BLOCK, LANES), lambda b, n: (idx, b, n, 0))

    def prev(size, idx):
        return pl.BlockSpec((size, 1, BLOCK, LANES), lambda b, n: (idx, b, jnp.maximum(n - 1, 0), 0))

    smem = pl.BlockSpec(memory_space=pltpu.SMEM)
    out = pl.pallas_call(
        functools.partial(_swa_kernel, kv_div=n_pairs // n_kv, max_dist=max_dist),
        grid=(BATCH, nb),
        in_specs=[smem, smem, cur(n_pairs, 0), prev(n_kv, k_idx), cur(n_kv, k_idx),
                  prev(n_kv, v_idx), cur(n_kv, v_idx)],
        out_specs=pl.BlockSpec((n_pairs, 1, BLOCK, LANES), lambda b, n: (0, b, n, 0)),
        out_shape=jax.ShapeDtypeStruct((n_pairs, BATCH, SEQ, LANES), BF),
        compiler_params=_params(("parallel", "parallel")),
        name="swa",
    )(slopes, sinks, view, view, view, view, view)
    return out.reshape(n_pairs, N_TOK, LANES)


DIL_ORDER = tuple(sorted(C_PATTERNS, key=lambda wd: -wd[1]))
DIL_UNROLL = 8
DIL_AHEAD = 2
DIL_BASE = 4
DIL_Q = SEQ // DIL_BASE
DIL_CONVERT_ROWS = DIL_BASE * BLOCK


def _dilated_kernel(slopes_ref, q_ref, k_ref, v_ref, o_ref, qn_s, tq_s, tk_s, tv_s, q0_s, q1_s, k_s, v_s,
                    acc_r, m_r, l_r, acc_n, m_n, l_n):
    assert all(d == 1 or d % DIL_BASE == 0 for _, d in DIL_ORDER) and DIL_ORDER[-1][1] == 1
    p = pl.program_id(1)
    lo = lax.broadcasted_iota(jnp.int32, (BLOCK, LANES), 1) < HEAD_DIM

    def convert(c, carry):
        rows = pl.ds(pl.multiple_of(c * DIL_CONVERT_ROWS, DIL_CONVERT_ROWS), DIL_CONVERT_ROWS)
        q_nat = q_ref[0, rows, :].astype(F32)
        lo_c = lax.broadcasted_iota(jnp.int32, (DIL_CONVERT_ROWS, LANES), 1) < HEAD_DIM
        qn_s[0, rows, :] = jnp.where(lo_c, q_nat, 0.0).astype(BF)
        qn_s[1, rows, :] = jnp.where(lo_c, 0.0, q_nat).astype(BF)
        tq_s[...] = q_nat
        tk_s[...] = k_ref[0, rows, :].astype(F32)
        tv_s[...] = v_ref[0, rows, :].astype(F32)
        for rho in range(DIL_BASE):
            src = pl.ds(rho, BLOCK, stride=DIL_BASE)
            dst = pl.ds(pl.multiple_of(rho * DIL_Q + c * BLOCK, BLOCK), BLOCK)
            q = tq_s[src, :]
            q0_s[dst, :] = jnp.where(lo, q, 0.0)
            q1_s[dst, :] = jnp.where(lo, 0.0, q)
            k_s[dst, :] = tk_s[src, :]
            v_s[dst, :] = tv_s[src, :]
        return carry

    lax.fori_loop(0, SEQ // DIL_CONVERT_ROWS, convert, 0)

    row = lax.broadcasted_iota(jnp.int32, (BLOCK, 2 * BLOCK), 0)
    col = lax.broadcasted_iota(jnp.int32, (BLOCK, 2 * BLOCK), 1)
    dist = row + BLOCK - col
    distf = dist.astype(F32)
    no_prev = jnp.where(col < BLOCK, -jnp.inf, 0.0)

    def bcast2(a0, a1):
        return jnp.where(lo, jnp.broadcast_to(a0, (BLOCK, LANES)), jnp.broadcast_to(a1, (BLOCK, LANES)))

    for pi, (window, d) in enumerate(DIL_ORDER):
        first, last = pi == 0, pi == len(DIL_ORDER) - 1
        natural = d == 1
        nb = SEQ // d // BLOCK
        band = (dist >= 0) & (dist <= window // d)
        bias = [jnp.where(band, (-float(d) * slopes_ref[2 * p + hh]) * distf, -jnp.inf) for hh in range(2)]
        acc_s, m_s, l_s = (acc_n, m_n, l_n) if natural else (acc_r, m_r, l_r)

        if natural and not first:
            for rho in range(DIL_BASE):
                src, dst = pl.ds(rho * DIL_Q, DIL_Q), pl.ds(rho, DIL_Q, stride=DIL_BASE)
                acc_n[dst, :] = acc_r[src, :]
                m_n[dst, :] = m_r[src, :]
                l_n[dst, :] = l_r[src, :]

        def rows_of(r, n, n_blocks=1, d=d, natural=natural):
            size = n_blocks * BLOCK
            if natural:
                return pl.ds(pl.multiple_of(BLOCK * n, BLOCK), size)
            inner = d // DIL_BASE
            start = (r % DIL_BASE) * DIL_Q + r // DIL_BASE + inner * BLOCK * n
            return pl.ds(start, size, stride=inner) if inner > 1 else pl.ds(pl.multiple_of(start, BLOCK), size)

        def step(it, carry, nb=nb, bias=bias, first=first, last=last, natural=natural, rows_of=rows_of,
                 acc_s=acc_s, m_s=m_s, l_s=l_s):
            assert DIL_UNROLL % nb == 0 or nb % DIL_UNROLL == 0
            load_k = (lambda rr: k_ref[0, rr, :]) if natural else (lambda rr: k_s[rr, :].astype(BF))
            load_v = (lambda rr: v_ref[0, rr, :]) if natural else (lambda rr: v_s[rr, :].astype(BF))

            def scores(u):
                t = it * DIL_UNROLL + u
                r, n = t // nb, t % nb
                prev = (u % nb != 0) if nb <= DIL_UNROLL else (True if u else None)
                rows = rows_of(r, n)
                if prev is True:
                    kv_rows = [rows_of(r, n - 1, 2)]
                elif prev is None:
                    kv_rows = [rows_of(r, jnp.maximum(n - 1, 0)), rows]
                else:
                    kv_rows = [rows]
                if natural:
                    qh = jnp.concatenate([qn_s[0, rows, :], qn_s[1, rows, :]], axis=0)
                else:
                    qh = jnp.concatenate([q0_s[rows, :], q1_s[rows, :]], axis=0).astype(BF)
                s = jnp.concatenate([lax.dot_general(qh, load_k(rr), NT_DIMS, preferred_element_type=F32)
                                     for rr in kv_rows], axis=1)
                return n, prev, rows, kv_rows, (s[:BLOCK], s[BLOCK:])

            def softmax_pv(blk):
                n, prev, rows, kv_rows, s = blk
                ms, ls, pes = [], [], []
                for hh in range(2):
                    sh = s[hh] + (bias[hh][:, BLOCK:] if prev is False else bias[hh])
                    if prev is None:
                        sh = sh + jnp.where(n == 0, no_prev, 0.0)
                    m = jnp.max(sh, axis=1, keepdims=True)
                    pe = jnp.exp2(sh - m)
                    ms.append(m)
                    ls.append(jnp.sum(pe, axis=1, keepdims=True))
                    pes.append(pe.astype(BF))
                pb = jnp.concatenate(pes, axis=0)
                v = jnp.concatenate([load_v(rr) for rr in kv_rows], axis=0) if len(kv_rows) > 1 else load_v(kv_rows[0])
                pv = jnp.dot(pb, v, preferred_element_type=F32)
                return rows, ms, ls, (pv[:BLOCK], pv[BLOCK:])

            def merge(rows, ms, ls, pv):
                m2 = bcast2(ms[0], ms[1])
                l2 = bcast2(ls[0], ls[1])
                acc2 = jnp.where(lo, pv[0], pv[1])
                if not first:
                    m_old = m_s[rows, :]
                    m_new = jnp.maximum(m_old, m2)
                    a_old, a_new = jnp.exp2(m_old - m_new), jnp.exp2(m2 - m_new)
                    l2 = a_old * l_s[rows, :] + a_new * l2
                    acc2 = a_old * acc_s[rows, :] + a_new * acc2
                    m2 = m_new
                if last:
                    o_ref[0, rows, :] = (acc2 / l2).astype(BF)
                else:
                    m_s[rows, :] = m2
                    l_s[rows, :] = l2
                    acc_s[rows, :] = acc2

            pending = {u: scores(u) for u in range(DIL_AHEAD)}
            done = None
            for u in range(DIL_UNROLL):
                if u + DIL_AHEAD < DIL_UNROLL:
                    pending[u + DIL_AHEAD] = scores(u + DIL_AHEAD)
                cur = softmax_pv(pending.pop(u))
                if done is not None:
                    merge(*done)
                done = cur
            merge(*done)
            return carry

        lax.fori_loop(0, SEQ // BLOCK // DIL_UNROLL, step, 0)


def _dilated(qkv, slopes):
    n_pairs = C_HEADS // 2
    seq_f32 = pltpu.VMEM((SEQ, LANES), F32)
    chunk_f32 = pltpu.VMEM((DIL_CONVERT_ROWS, LANES), F32)
    return pl.pallas_call(
        _dilated_kernel,
        grid=(BATCH, n_pairs),
        in_specs=[pl.BlockSpec(memory_space=pltpu.SMEM),
                  pl.BlockSpec((1, SEQ, LANES), lambda b, p: (p, b, 0)),
                  pl.BlockSpec((1, SEQ, LANES), lambda b, p: (n_pairs + p, b, 0)),
                  pl.BlockSpec((1, SEQ, LANES), lambda b, p: (2 * n_pairs + p, b, 0))],
        out_specs=pl.BlockSpec((1, SEQ, LANES), lambda b, p: (p, b, 0)),
        out_shape=jax.ShapeDtypeStruct((n_pairs, N_TOK, LANES), BF),
        scratch_shapes=[pltpu.VMEM((2, SEQ, LANES), BF)] + [chunk_f32] * 3 + [seq_f32] * 10,
        compiler_params=_params(("parallel", "parallel"), VMEM_LIMIT),
        name="dilated",
    )(slopes, qkv, qkv, qkv)


SB_QB = MXU_N
SB_FIRST_TILES = 2
SB_DEAD_LOG2 = -150.0


def _sb_kernel(q_ref, k_ref, v_ref, uo_ref, o_ref):
    i = pl.program_id(2)
    q2 = q_ref[0].astype(F32)
    uo = uo_ref[...]
    lo = lax.broadcasted_iota(jnp.int32, (SB_QB, LANES), 1) < HEAD_DIM
    q_stack = jnp.concatenate([jnp.where(lo, q2, 0.0), jnp.where(lo, 0.0, q2)], axis=0).astype(BF)
    rel1 = (lax.broadcasted_iota(jnp.int32, (SB_QB, SB_QB), 1)
            - lax.broadcasted_iota(jnp.int32, (SB_QB, SB_QB), 0))
    rel = jnp.concatenate([rel1, rel1], axis=0)

    def keys(ref, first, n_tiles):
        return ref[0, pl.ds(pl.multiple_of(first * SB_QB, SB_QB), n_tiles * SB_QB), :]

    def scores(first, n_tiles):
        return lax.dot_general(q_stack, keys(k_ref, first, n_tiles), NT_DIMS,
                               preferred_element_type=F32)

    def walk(first, n_tiles, carry, masked):
        z, c, o = carry
        z_next = scores(jnp.maximum(first - 1, 0), 1)
        order = list(reversed(range(n_tiles)))
        sps, es, stricts, totals = [], {}, {}, {}
        for t in order:
            zt = z[:, t * SB_QB:(t + 1) * SB_QB]
            sp = jnp.maximum(zt, 0.0) + jnp.log2(1.0 + jnp.exp2(-jnp.abs(zt)))
            es[t] = zt - sp
            if masked:
                stricts[t] = rel < (i - first - t) * SB_QB
                sp = jnp.where(stricts[t], sp, 0.0)
            sps.append(sp.astype(BF))
            totals[t] = jnp.sum(sp, axis=1, keepdims=True)
        w_all = jnp.dot(jnp.concatenate(sps, axis=0), uo, preferred_element_type=F32)
        parts = {}
        for idx, t in enumerate(order):
            w = w_all[idx * 2 * SB_QB:(idx + 1) * 2 * SB_QB]
            a = jnp.exp2(es[t] + jnp.concatenate([c] * (SB_QB // LANES), axis=1) + w)
            if masked:
                a = jnp.where(stricts[t], a, 0.0)
            parts[t] = a.astype(BF)
            c = c - totals[t]
        pv = jnp.dot(jnp.concatenate([parts[t] for t in range(n_tiles)], axis=1), keys(v_ref, first, n_tiles),
                     preferred_element_type=F32)
        return z_next, c, o + jnp.where(lo, pv[:SB_QB], pv[SB_QB:])

    def alive(c):
        return jnp.max(c) > SB_DEAD_LOG2

    def body(state):
        g = state[0]
        z, c, o = walk(g, 1, state[2:], False)
        return g - 1, alive(c), z, c, o

    first = jnp.maximum(i - 1, 0)
    zeros = (jnp.zeros((2 * SB_QB, LANES), F32), jnp.zeros((SB_QB, LANES), F32))
    z, c, o = walk(first, SB_FIRST_TILES, (scores(first, SB_FIRST_TILES),) + zeros, True)
    state = lax.while_loop(lambda st: (st[0] >= 0) & st[1], body, (first - 1, alive(c), z, c, o))
    o_ref[0] = state[4].astype(BF)


def _stick_breaking(qkv, q_c0, k_c0, v_c0, n_pairs):
    idx = np.arange(SB_QB)
    uo = jnp.asarray(-(idx[:, None] > idx[None, :]).astype(np.float32), dtype=BF)
    nb = SEQ // SB_QB
    return pl.pallas_call(
        _sb_kernel,
        grid=(BATCH, n_pairs, nb),
        in_specs=[pl.BlockSpec((1, SB_QB, LANES), lambda b, p, i: (q_c0 + p, b * nb + i, 0)),
                  pl.BlockSpec((1, SEQ, LANES), lambda b, p, i: (k_c0 + p, b, 0)),
                  pl.BlockSpec((1, SEQ, LANES), lambda b, p, i: (v_c0 + p, b, 0)),
                  _resident((SB_QB, SB_QB))],
        out_specs=pl.BlockSpec((1, SB_QB, LANES), lambda b, p, i: (p, b * nb + i, 0)),
        out_shape=jax.ShapeDtypeStruct((n_pairs, N_TOK, LANES), BF),
        compiler_params=_params(("parallel", "parallel", "parallel")),
        name="stick_breaking",
    )(qkv, qkv, qkv, uo)


def _mix_xattn_kernel(*refs):
    x_ref = refs[0]
    wm_ref, g_ref, wq_ref, cs_ref, kv_ref, wo_ref, o_ref = refs[-7:]
    mixed = jnp.concatenate([r[c] for r in refs[1:-7] for c in range(r.shape[0])], axis=1)
    xv = x_ref[...] + jnp.dot(mixed, wm_ref[...], preferred_element_type=F32)
    h = _rms(xv, g_ref[...]).astype(BF)
    heads = range(X_HEADS)
    cols = [slice(X_HEAD_DIM * hd, X_HEAD_DIM * (hd + 1)) for hd in heads]
    acc = [jnp.dot(h, wq_ref[:, cols[hd]], preferred_element_type=F32) for hd in heads]
    ms = [jnp.mean(acc[hd] * acc[hd], axis=1, keepdims=True) for hd in heads]
    q = [(acc[hd] * cs_ref[:, cols[hd]] * lax.rsqrt(ms[hd] + RMS_EPS)).astype(BF) for hd in heads]
    s = [lax.dot_general(q[hd], jnp.concatenate([kv_ref[2 * hd], kv_ref[2 * hd + 1]], axis=1), NT_DIMS,
                         preferred_element_type=F32) for hd in heads]
    pe, l = [], []
    for hd in heads:
        e = jnp.exp2(s[hd] - jnp.max(s[hd], axis=1, keepdims=True))
        l.append(jnp.sum(e, axis=1, keepdims=True))
        pe.append(e.astype(BF))
    v0 = 2 * X_HEADS
    pv = [jnp.dot(pe[hd], jnp.concatenate([kv_ref[v0 + 2 * hd], kv_ref[v0 + 2 * hd + 1]], axis=1),
                  preferred_element_type=F32) for hd in heads]
    o = jnp.concatenate([(pv[hd] / l[hd]).astype(BF) for hd in heads], axis=1)
    o_ref[...] = xv + jnp.dot(o, wo_ref[...], preferred_element_type=F32)


def _mix_xattn(x, mixer_heads, w_mix, g, w_q, q_colscale, kv, w_o, *, tm=512):
    tiles_per_batch = SEQ // tm
    in_specs = [pl.BlockSpec((tm, D_MODEL), lambda i: (i, 0))]
    in_specs += [pl.BlockSpec((mh.shape[0], tm, LANES), lambda i: (0, i, 0)) for mh in mixer_heads]
    in_specs += [_resident(w_mix.shape),
                 _resident((1, D_MODEL)),
                 _resident(w_q.shape),
                 _resident((1, D_MODEL)),
                 pl.BlockSpec((4 * X_HEADS, MEM_LEN, LANES), lambda i: (0, i // tiles_per_batch, 0)),
                 _resident(w_o.shape)]
    return pl.pallas_call(
        _mix_xattn_kernel, grid=(N_TOK // tm,),
        in_specs=in_specs,
        out_specs=pl.BlockSpec((tm, D_MODEL), lambda i: (i, 0)),
        out_shape=jax.ShapeDtypeStruct((N_TOK, D_MODEL), F32),
        compiler_params=_params(("parallel",), VMEM_LIMIT),
        name="mix_xattn",
    )(x, *mixer_heads, w_mix, g.reshape(1, D_MODEL), w_q,
      q_colscale.reshape(1, D_MODEL).astype(F32), kv, w_o)


def _alibi_log2(n_heads):
    return jnp.asarray(LOG2_E * 2.0 ** (-8.0 * np.arange(1, n_heads + 1) / n_heads), dtype=F32)


def _widen_even_w_in(w_in):
    hd = HEAD_DIM
    qa, ka, va, rest = w_in[..., :512], w_in[..., 512:640], w_in[..., 640:768], w_in[..., 768:]
    dup = lambda w: jnp.concatenate([w[..., :hd], w[..., :hd], w[..., hd:], w[..., hd:]], axis=-1)
    return jnp.concatenate([qa, dup(ka), dup(va), rest], axis=-1)


def _even_mixer_heads(x, norm_g, w_aug, q_gain, k_gain, sinks):
    hd = HEAD_DIM
    scale = hd ** -0.5 * LOG2_E
    ones = lambda n: jnp.ones((n,), F32)
    cs = jnp.concatenate([jnp.tile(q_gain, A_Q_HEADS) * scale, jnp.tile(k_gain, 4), ones(256),
                          ones(512) * scale, ones(1024)])
    normed = [True, True, True] + [False] * 7
    p = _proj(x, norm_g, w_aug, cs, normed, hd)
    o_a = _swa(p, _alibi_log2(A_Q_HEADS), sinks.astype(F32) * LOG2_E, n_pairs=A_Q_HEADS // 2, n_kv=A_KV_HEADS,
               max_dist=A_WINDOW - 1)
    o_b = _stick_breaking(p, 8, 12, 16, B_HEADS // 2)
    return [o_a, o_b]


def _odd_mixer_heads(x, norm_g, w_in, q_gain, k_gain):
    hd = HEAD_DIM
    cs = jnp.concatenate([jnp.tile(q_gain, C_HEADS) * (hd ** -0.5 * LOG2_E), jnp.tile(k_gain, C_HEADS),
                          jnp.ones((C_HEADS * hd,), F32)])
    normed = [True] * 8 + [False] * 4
    p = _proj(x, norm_g, w_in, cs, normed, hd)
    return [_dilated(p, _alibi_log2(C_HEADS))]


def _mix_and_cross_attention(x, mixer_heads, w_mix, mem2d, norm_g, mem_g, w_q, w_kv, q_gain, k_gain, w_o):
    cs_kv = jnp.concatenate([jnp.tile(k_gain, X_HEADS), jnp.ones((D_MODEL,), F32)])
    kv = _proj(mem2d, mem_g, w_kv, cs_kv, [True] * 4 + [False] * 4, X_HEAD_DIM, tm=256)
    cs_q = jnp.tile(q_gain, X_HEADS) * (X_HEAD_DIM ** -0.5 * LOG2_E)
    return _mix_xattn(x, mixer_heads, w_mix, norm_g, w_q, cs_q, kv, w_o)


def kernel(x, mem, ffn1_norm, ffn1_w_gu, ffn1_w_down, mix_norm, ev_w_in, ev_q_gain, ev_k_gain, ev_sinks, ev_w_out, od_w_in, od_q_gain, od_k_gain, od_w_out, xa_norm, xa_mem_norm, xa_w_q, xa_w_kv, xa_q_gain, xa_k_gain, xa_w_o, ffn2_norm, ffn2_w_gu, ffn2_w_down):
    x = x.reshape(N_TOK, D_MODEL)
    mem2d = mem.reshape(BATCH * MEM_LEN, D_MODEL)
    ev_w_aug = _widen_even_w_in(ev_w_in)
    w_gu, w_down = _cast_now((ffn1_w_gu, 0)), _cast_now((ffn1_w_down, 0))
    for layer in range(DEPTH):
        j = layer // 2
        even = layer % 2 == 0
        w_in3, w_mix3 = (ev_w_aug, ev_w_out) if even else (od_w_in, od_w_out)
        jobs = [(w_in3, j), (w_mix3, j), (xa_w_q, layer), (xa_w_kv, layer), (xa_w_o, layer),
                (ffn2_w_gu, layer), (ffn2_w_down, layer)]
        x, (w_in, w_mix, w_q, w_kv, w_o, w_gu, w_down) = _ffn(x, ffn1_norm[layer], w_gu, w_down, jobs)
        if even:
            heads = _even_mixer_heads(x, mix_norm[layer], w_in, ev_q_gain[j], ev_k_gain[j], ev_sinks[j])
        else:
            heads = _odd_mixer_heads(x, mix_norm[layer], w_in, od_q_gain[j], od_k_gain[j])
        x = _mix_and_cross_attention(x, heads, w_mix, mem2d, xa_norm[layer], xa_mem_norm[layer], w_q, w_kv,
                                     xa_q_gain[layer], xa_k_gain[layer], w_o)
        jobs = [(ffn1_w_gu, layer + 1), (ffn1_w_down, layer + 1)] if layer + 1 < DEPTH else []
        x, next_ffn1 = _ffn(x, ffn2_norm[layer], w_gu, w_down, jobs)
        if next_ffn1:
            w_gu, w_down = next_ffn1
    return x.reshape(BATCH, SEQ, D_MODEL)
```

```python
import functools

import numpy as np
import jax
import jax.numpy as jnp
from jax import lax
from jax.experimental import pallas as pl
from jax.experimental.pallas import tpu as pltpu

D_MODEL = 1024
BATCH = 4
SEQ = 4096
N_TOK = BATCH * SEQ
DEPTH = 2
HEAD_DIM = 64
BLOCK = 128
A_Q_HEADS = 8
A_KV_HEADS = 2
A_WINDOW = 128
B_HEADS = 8
C_HEADS = 16
C_PATTERNS = ((128, 1), (512, 4), (2048, 16))
MEM_LEN = 256
X_HEADS = 4
X_HEAD_DIM = D_MODEL // X_HEADS
D_FF = 2816
RMS_EPS = 1e-6

LANES = 128
MXU_N = 256
VMEM_LIMIT = 56 * 1024 * 1024

BF = jnp.bfloat16
F32 = jnp.float32
NT_DIMS = (((1,), (1,)), ((), ()))
LOG2_E = 1.4426950408889634


def _params(sem, vmem=None):
    return pltpu.CompilerParams(dimension_semantics=sem, vmem_limit_bytes=vmem)


def _resident(shape):
    nd = len(shape)
    return pl.BlockSpec(shape, lambda *_: (0,) * nd, pipeline_mode=pl.Buffered(1))


BF16_SUBLANES = 16


def _cast_specs(job, steps):
    w3, layer = job
    _, r, c = w3.shape
    rb = next(rb for rb in range(BF16_SUBLANES, r + 1, BF16_SUBLANES) if r % rb == 0 and r // rb <= steps)
    last = r // rb - 1
    return (pl.BlockSpec((None, rb, c), lambda i: (layer, jnp.minimum(i, last), 0)),
            pl.BlockSpec((rb, c), lambda i: (jnp.minimum(i, last), 0)),
            jax.ShapeDtypeStruct((r, c), BF))


def _run_cast_jobs(in_refs, out_refs):
    for src, dst in zip(in_refs, out_refs):
        dst[...] = src[...].astype(BF)


def _cast_kernel(w_ref, o_ref):
    _run_cast_jobs([w_ref], [o_ref])


def _cast_now(job, *, rows=128):
    steps = job[0].shape[1] // rows
    in_spec, out_spec, out_shape = _cast_specs(job, steps)
    return pl.pallas_call(
        _cast_kernel, grid=(steps,), in_specs=[in_spec], out_specs=out_spec, out_shape=out_shape,
        compiler_params=_params(("arbitrary",)),
        name="cast",
    )(job[0])


def _rms(xv, g):
    ms = jnp.mean(xv * xv, axis=-1, keepdims=True)
    return xv * lax.rsqrt(ms + RMS_EPS) * g


FFN_SPLIT = (D_FF // MXU_N + 1) // 2 * MXU_N
FFN_CHUNKS = ((0, FFN_SPLIT), (FFN_SPLIT, D_FF))


def _ffn_kernel(*refs, n_jobs):
    x_ref, g_ref, wgu_ref, wd_ref = refs[:4]
    o_ref = refs[4 + n_jobs]
    xv = x_ref[...]
    h = _rms(xv, g_ref[...]).astype(BF)
    acc = jnp.zeros_like(xv)
    for c0, c1 in FFN_CHUNKS:
        gate = jnp.dot(h, wgu_ref[:, c0:c1], preferred_element_type=F32)
        up = jnp.dot(h, wgu_ref[:, D_FF + c0:D_FF + c1], preferred_element_type=F32)
        act = (gate * jax.nn.sigmoid(gate) * up).astype(BF)
        acc = acc + jnp.dot(act, wd_ref[c0:c1, :], preferred_element_type=F32)
    o_ref[...] = xv + 0.5 * acc
    _run_cast_jobs(refs[4:4 + n_jobs], refs[5 + n_jobs:])


def _ffn(x, g, w_gu, w_down, cast_jobs=(), *, tm=512):
    steps = N_TOK // tm
    specs = [_cast_specs(job, steps) for job in cast_jobs]
    out = pl.pallas_call(
        functools.partial(_ffn_kernel, n_jobs=len(cast_jobs)),
        grid=(steps,),
        in_specs=[pl.BlockSpec((tm, D_MODEL), lambda i: (i, 0)),
                  _resident((1, D_MODEL)),
                  _resident(w_gu.shape),
                  _resident(w_down.shape)] + [s[0] for s in specs],
        out_specs=[pl.BlockSpec((tm, D_MODEL), lambda i: (i, 0))] + [s[1] for s in specs],
        out_shape=[jax.ShapeDtypeStruct((N_TOK, D_MODEL), F32)] + [s[2] for s in specs],
        compiler_params=_params(("arbitrary",), VMEM_LIMIT),
        name="ffn",
    )(x, g.reshape(1, D_MODEL), w_gu, w_down, *[job[0] for job in cast_jobs])
    return out[0], out[1:]


def _group_ones(gs):
    idx = np.arange(MXU_N) // gs
    return jnp.asarray(idx[:, None] == idx[None, :], dtype=BF)


def _proj_kernel(x_ref, g_ref, w_ref, cs_ref, bd_ref, o_ref, *, plan, gs):
    h = _rms(x_ref[...], g_ref[...]).astype(BF)
    n_chunks = len(plan) // 2
    lo = lax.broadcasted_iota(jnp.int32, (x_ref.shape[0], LANES), 1) < HEAD_DIM

    def main(j):
        return jnp.dot(h, w_ref[:, MXU_N * j:MXU_N * (j + 1)], preferred_element_type=F32)

    acc_next = main(0)
    out = 0
    for j in range(n_chunks):
        cols = slice(MXU_N * j, MXU_N * (j + 1))
        acc = acc_next
        if j + 1 < n_chunks:
            acc_next = main(j + 1)
        y = acc * cs_ref[:, cols]
        halves = plan[2 * j:2 * j + 2]
        if any(normed for normed, _ in halves):
            ss = jnp.dot((acc * acc).astype(BF), bd_ref[...], preferred_element_type=F32)
            inv = lax.rsqrt(ss * (1.0 / gs) + RMS_EPS)
        for half, (normed, dup) in enumerate(halves):
            lanes = slice(LANES * half, LANES * (half + 1))
            yh = y[:, lanes] * inv[:, lanes] if normed else y[:, lanes]
            if dup:
                swapped = pltpu.roll(yh, HEAD_DIM, axis=1)
                o_ref[out] = jnp.where(lo, yh, swapped).astype(BF)
                o_ref[out + 1] = jnp.where(lo, swapped, yh).astype(BF)
                out += 2
            else:
                o_ref[out] = yh.astype(BF)
                out += 1


def _proj(x, g, w, colscale, plan, gs, *, tm=512):
    rows = x.shape[0]
    wout = w.shape[1]
    assert wout == LANES * len(plan) and len(plan) % 2 == 0
    c = sum(2 if dup else 1 for _, dup in plan)
    return pl.pallas_call(
        functools.partial(_proj_kernel, plan=tuple(plan), gs=gs),
        grid=(rows // tm,),
        in_specs=[pl.BlockSpec((tm, D_MODEL), lambda i: (i, 0)),
                  _resident((1, D_MODEL)),
                  _resident(w.shape),
                  _resident((1, wout)),
                  _resident((MXU_N, MXU_N))],
        out_specs=pl.BlockSpec((c, tm, LANES), lambda i: (0, i, 0)),
        out_shape=jax.ShapeDtypeStruct((c, rows, LANES), BF),
        compiler_params=_params(("parallel",), VMEM_LIMIT),
        name="proj",
    )(x, g.reshape(1, D_MODEL), w, colscale.reshape(1, wout).astype(F32), _group_ones(gs))


def _swa_kernel(slopes_ref, sinks_ref, q_ref, kp_ref, kc_ref, vp_ref, vc_ref, o_ref, *, kv_div, max_dist):
    n = pl.program_id(1)
    row = lax.broadcasted_iota(jnp.int32, (BLOCK, 2 * BLOCK), 0)
    col = lax.broadcasted_iota(jnp.int32, (BLOCK, 2 * BLOCK), 1)
    dist = row + BLOCK - col
    valid = (dist >= 0) & (dist <= max_dist) & ((col >= BLOCK) | (n > 0))
    negmask = jnp.where(valid, 0.0, -jnp.inf)
    distf = dist.astype(F32)
    lo = lax.broadcasted_iota(jnp.int32, (BLOCK, LANES), 1) < HEAD_DIM

    n_groups = q_ref.shape[0] // kv_div
    heads_per_group = 2 * kv_div
    scores = []
    for g in range(n_groups):
        parts = []
        for p in range(g * kv_div, (g + 1) * kv_div):
            q2 = q_ref[p, 0].astype(F32)
            parts += [jnp.where(lo, q2, 0.0), jnp.where(lo, 0.0, q2)]
        q_stack = jnp.concatenate(parts, axis=0).astype(BF)
        scores.append(jnp.concatenate(
            [lax.dot_general(q_stack, kp_ref[g, 0], NT_DIMS, preferred_element_type=F32),
             lax.dot_general(q_stack, kc_ref[g, 0], NT_DIMS, preferred_element_type=F32)], axis=1))
    soft = []
    for g in range(n_groups):
        res = []
        for j in range(heads_per_group):
            h = g * heads_per_group + j
            s = scores[g][j * BLOCK:(j + 1) * BLOCK] - slopes_ref[h] * distf + negmask
            m = jnp.maximum(jnp.max(s, axis=1, keepdims=True), sinks_ref[h])
            pe = jnp.exp2(s - m)
            res.append((pe.astype(BF), jnp.sum(pe, axis=1, keepdims=True) + jnp.exp2(sinks_ref[h] - m)))
        soft.append(res)
    for g in range(n_groups):
        pb = jnp.concatenate([r[0] for r in soft[g]], axis=0)
        pv = (jnp.dot(pb[:, :BLOCK], vp_ref[g, 0], preferred_element_type=F32)
              + jnp.dot(pb[:, BLOCK:], vc_ref[g, 0], preferred_element_type=F32))
        for jp in range(kv_div):
            o0 = pv[(2 * jp) * BLOCK:(2 * jp + 1) * BLOCK] / soft[g][2 * jp][1]
            o1 = pv[(2 * jp + 1) * BLOCK:(2 * jp + 2) * BLOCK] / soft[g][2 * jp + 1][1]
            o_ref[g * kv_div + jp, 0] = jnp.where(lo, o0, o1).astype(BF)


def _swa(qkv, slopes, sinks, *, n_pairs, n_kv, max_dist):
    c = qkv.shape[0]
    nb = SEQ // BLOCK
    view = qkv.reshape(c, BATCH, SEQ, LANES)
    k_idx, v_idx = n_pairs // n_kv, n_pairs // n_kv + 1

    def cur(size, idx):
        return pl.BlockSpec((size, 1, BLOCK, LANES), lambda b, n: (idx, b, n, 0))

    def prev(size, idx):
        return pl.BlockSpec((size, 1, BLOCK, LANES), lambda b, n: (idx, b, jnp.maximum(n - 1, 0), 0))

    smem = pl.BlockSpec(memory_space=pltpu.SMEM)
    out = pl.pallas_call(
        functools.partial(_swa_kernel, kv_div=n_pairs // n_kv, max_dist=max_dist),
        grid=(BATCH, nb),
        in_specs=[smem, smem, cur(n_pairs, 0), prev(n_kv, k_idx), cur(n_kv, k_idx),
                  prev(n_kv, v_idx), cur(n_kv, v_idx)],
        out_specs=pl.BlockSpec((n_pairs, 1, BLOCK, LANES), lambda b, n: (0, b, n, 0)),
        out_shape=jax.ShapeDtypeStruct((n_pairs, BATCH, SEQ, LANES), BF),
        compiler_params=_params(("parallel", "parallel")),
        name="swa",
    )(slopes, sinks, view, view, view, view, view)
    return out.reshape(n_pairs, N_TOK, LANES)


DIL_ORDER = tuple(sorted(C_PATTERNS, key=lambda wd: -wd[1]))
DIL_UNROLL = 8
DIL_AHEAD = 2
DIL_BASE = 4
DIL_Q = SEQ // DIL_BASE
DIL_CONVERT_ROWS = DIL_BASE * BLOCK


def _dilated_kernel(slopes_ref, q_ref, k_ref, v_ref, o_ref, qn_s, tq_s, tk_s, tv_s, q0_s, q1_s, k_s, v_s,
                    acc_r, m_r, l_r, acc_n, m_n, l_n):
    assert all(d == 1 or d % DIL_BASE == 0 for _, d in DIL_ORDER) and DIL_ORDER[-1][1] == 1
    p = pl.program_id(1)
    lo = lax.broadcasted_iota(jnp.int32, (BLOCK, LANES), 1) < HEAD_DIM

    def convert(c, carry):
        rows = pl.ds(pl.multiple_of(c * DIL_CONVERT_ROWS, DIL_CONVERT_ROWS), DIL_CONVERT_ROWS)
        q_nat = q_ref[0, rows, :].astype(F32)
        lo_c = lax.broadcasted_iota(jnp.int32, (DIL_CONVERT_ROWS, LANES), 1) < HEAD_DIM
        qn_s[0, rows, :] = jnp.where(lo_c, q_nat, 0.0).astype(BF)
        qn_s[1, rows, :] = jnp.where(lo_c, 0.0, q_nat).astype(BF)
        tq_s[...] = q_nat
        tk_s[...] = k_ref[0, rows, :].astype(F32)
        tv_s[...] = v_ref[0, rows, :].astype(F32)
        for rho in range(DIL_BASE):
            src = pl.ds(rho, BLOCK, stride=DIL_BASE)
            dst = pl.ds(pl.multiple_of(rho * DIL_Q + c * BLOCK, BLOCK), BLOCK)
            q = tq_s[src, :]
            q0_s[dst, :] = jnp.where(lo, q, 0.0)
            q1_s[dst, :] = jnp.where(lo, 0.0, q)
            k_s[dst, :] = tk_s[src, :]
            v_s[dst, :] = tv_s[src, :]
        return carry

    lax.fori_loop(0, SEQ // DIL_CONVERT_ROWS, convert, 0)

    row = lax.broadcasted_iota(jnp.int32, (BLOCK, 2 * BLOCK), 0)
    col = lax.broadcasted_iota(jnp.int32, (BLOCK, 2 * BLOCK), 1)
    dist = row + BLOCK - col
    distf = dist.astype(F32)
    no_prev = jnp.where(col < BLOCK, -jnp.inf, 0.0)

    def bcast2(a0, a1):
        return jnp.where(lo, jnp.broadcast_to(a0, (BLOCK, LANES)), jnp.broadcast_to(a1, (BLOCK, LANES)))

    for pi, (window, d) in enumerate(DIL_ORDER):
        first, last = pi == 0, pi == len(DIL_ORDER) - 1
        natural = d == 1
        nb = SEQ // d // BLOCK
        band = (dist >= 0) & (dist <= window // d)
        bias = [jnp.where(band, (-float(d) * slopes_ref[2 * p + hh]) * distf, -jnp.inf) for hh in range(2)]
        acc_s, m_s, l_s = (acc_n, m_n, l_n) if natural else (acc_r, m_r, l_r)

        if natural and not first:
            for rho in range(DIL_BASE):
                src, dst = pl.ds(rho * DIL_Q, DIL_Q), pl.ds(rho, DIL_Q, stride=DIL_BASE)
                acc_n[dst, :] = acc_r[src, :]
                m_n[dst, :] = m_r[src, :]
                l_n[dst, :] = l_r[src, :]

        def rows_of(r, n, n_blocks=1, d=d, natural=natural):
            size = n_blocks * BLOCK
            if natural:
                return pl.ds(pl.multiple_of(BLOCK * n, BLOCK), size)
            inner = d // DIL_BASE
            start = (r % DIL_BASE) * DIL_Q + r // DIL_BASE + inner * BLOCK * n
            return pl.ds(start, size, stride=inner) if inner > 1 else pl.ds(pl.multiple_of(start, BLOCK), size)

        def step(it, carry, nb=nb, bias=bias, first=first, last=last, natural=natural, rows_of=rows_of,
                 acc_s=acc_s, m_s=m_s, l_s=l_s):
            assert DIL_UNROLL % nb == 0 or nb % DIL_UNROLL == 0
            load_k = (lambda rr: k_ref[0, rr, :]) if natural else (lambda rr: k_s[rr, :].astype(BF))
            load_v = (lambda rr: v_ref[0, rr, :]) if natural else (lambda rr: v_s[rr, :].astype(BF))

            def scores(u):
                t = it * DIL_UNROLL + u
                r, n = t // nb, t % nb
                prev = (u % nb != 0) if nb <= DIL_UNROLL else (True if u else None)
                rows = rows_of(r, n)
                if prev is True:
                    kv_rows = [rows_of(r, n - 1, 2)]
                elif prev is None:
                    kv_rows = [rows_of(r, jnp.maximum(n - 1, 0)), rows]
                else:
                    kv_rows = [rows]
                if natural:
                    qh = jnp.concatenate([qn_s[0, rows, :], qn_s[1, rows, :]], axis=0)
                else:
                    qh = jnp.concatenate([q0_s[rows, :], q1_s[rows, :]], axis=0).astype(BF)
                s = jnp.concatenate([lax.dot_general(qh, load_k(rr), NT_DIMS, preferred_element_type=F32)
                                     for rr in kv_rows], axis=1)
                return n, prev, rows, kv_rows, (s[:BLOCK], s[BLOCK:])

            def softmax_pv(blk):
                n, prev, rows, kv_rows, s = blk
                ms, ls, pes = [], [], []
                for hh in range(2):
                    sh = s[hh] + (bias[hh][:, BLOCK:] if prev is False else bias[hh])
                    if prev is None:
                        sh = sh + jnp.where(n == 0, no_prev, 0.0)
                    m = jnp.max(sh, axis=1, keepdims=True)
                    pe = jnp.exp2(sh - m)
                    ms.append(m)
                    ls.append(jnp.sum(pe, axis=1, keepdims=True))
                    pes.append(pe.astype(BF))
                pb = jnp.concatenate(pes, axis=0)
                v = jnp.concatenate([load_v(rr) for rr in kv_rows], axis=0) if len(kv_rows) > 1 else load_v(kv_rows[0])
                pv = jnp.dot(pb, v, preferred_element_type=F32)
                return rows, ms, ls, (pv[:BLOCK], pv[BLOCK:])

            def merge(rows, ms, ls, pv):
                m2 = bcast2(ms[0], ms[1])
                l2 = bcast2(ls[0], ls[1])
                acc2 = jnp.where(lo, pv[0], pv[1])
                if not first:
                    m_old = m_s[rows, :]
                    m_new = jnp.maximum(m_old, m2)
                    a_old, a_new = jnp.exp2(m_old - m_new), jnp.exp2(m2 - m_new)
                    l2 = a_old * l_s[rows, :] + a_new * l2
                    acc2 = a_old * acc_s[rows, :] + a_new * acc2
                    m2 = m_new
                if last:
                    o_ref[0, rows, :] = (acc2 / l2).astype(BF)
                else:
                    m_s[rows, :] = m2
                    l_s[rows, :] = l2
                    acc_s[rows, :] = acc2

            pending = {u: scores(u) for u in range(DIL_AHEAD)}
            done = None
            for u in range(DIL_UNROLL):
                if u + DIL_AHEAD < DIL_UNROLL:
                    pending[u + DIL_AHEAD] = scores(u + DIL_AHEAD)
                cur = softmax_pv(pending.pop(u))
                if done is not None:
                    merge(*done)
                done = cur
            merge(*done)
            return carry

        lax.fori_loop(0, SEQ // BLOCK // DIL_UNROLL, step, 0)


def _dilated(qkv, slopes):
    n_pairs = C_HEADS // 2
    seq_f32 = pltpu.VMEM((SEQ, LANES), F32)
    chunk_f32 = pltpu.VMEM((DIL_CONVERT_ROWS, LANES), F32)
    return pl.pallas_call(
        _dilated_kernel,
        grid=(BATCH, n_pairs),
        in_specs=[pl.BlockSpec(memory_space=pltpu.SMEM),
                  pl.BlockSpec((1, SEQ, LANES), lambda b, p: (p, b, 0)),
                  pl.BlockSpec((1, SEQ, LANES), lambda b, p: (n_pairs + p, b, 0)),
                  pl.BlockSpec((1, SEQ, LANES), lambda b, p: (2 * n_pairs + p, b, 0))],
        out_specs=pl.BlockSpec((1, SEQ, LANES), lambda b, p: (p, b, 0)),
        out_shape=jax.ShapeDtypeStruct((n_pairs, N_TOK, LANES), BF),
        scratch_shapes=[pltpu.VMEM((2, SEQ, LANES), BF)] + [chunk_f32] * 3 + [seq_f32] * 10,
        compiler_params=_params(("parallel", "parallel"), VMEM_LIMIT),
        name="dilated",
    )(slopes, qkv, qkv, qkv)


SB_QB = MXU_N
SB_FIRST_TILES = 2
SB_DEAD_LOG2 = -150.0


def _sb_kernel(q_ref, k_ref, v_ref, uo_ref, o_ref):
    i = pl.program_id(2)
    q2 = q_ref[0].astype(F32)
    uo = uo_ref[...]
    lo = lax.broadcasted_iota(jnp.int32, (SB_QB, LANES), 1) < HEAD_DIM
    q_stack = jnp.concatenate([jnp.where(lo, q2, 0.0), jnp.where(lo, 0.0, q2)], axis=0).astype(BF)
    rel1 = (lax.broadcasted_iota(jnp.int32, (SB_QB, SB_QB), 1)
            - lax.broadcasted_iota(jnp.int32, (SB_QB, SB_QB), 0))
    rel = jnp.concatenate([rel1, rel1], axis=0)

    def keys(ref, first, n_tiles):
        return ref[0, pl.ds(pl.multiple_of(first * SB_QB, SB_QB), n_tiles * SB_QB), :]

    def scores(first, n_tiles):
        return lax.dot_general(q_stack, keys(k_ref, first, n_tiles), NT_DIMS,
                               preferred_element_type=F32)

    def walk(first, n_tiles, carry, masked):
        z, c, o = carry
        z_next = scores(jnp.maximum(first - 1, 0), 1)
        order = list(reversed(range(n_tiles)))
        sps, es, stricts, totals = [], {}, {}, {}
        for t in order:
            zt = z[:, t * SB_QB:(t + 1) * SB_QB]
            sp = jnp.maximum(zt, 0.0) + jnp.log2(1.0 + jnp.exp2(-jnp.abs(zt)))
            es[t] = zt - sp
            if masked:
                stricts[t] = rel < (i - first - t) * SB_QB
                sp = jnp.where(stricts[t], sp, 0.0)
            sps.append(sp.astype(BF))
            totals[t] = jnp.sum(sp, axis=1, keepdims=True)
        w_all = jnp.dot(jnp.concatenate(sps, axis=0), uo, preferred_element_type=F32)
        parts = {}
        for idx, t in enumerate(order):
            w = w_all[idx * 2 * SB_QB:(idx + 1) * 2 * SB_QB]
            a = jnp.exp2(es[t] + jnp.concatenate([c] * (SB_QB // LANES), axis=1) + w)
            if masked:
                a = jnp.where(stricts[t], a, 0.0)
            parts[t] = a.astype(BF)
            c = c - totals[t]
        pv = jnp.dot(jnp.concatenate([parts[t] for t in range(n_tiles)], axis=1), keys(v_ref, first, n_tiles),
                     preferred_element_type=F32)
        return z_next, c, o + jnp.where(lo, pv[:SB_QB], pv[SB_QB:])

    def alive(c):
        return jnp.max(c) > SB_DEAD_LOG2

    def body(state):
        g = state[0]
        z, c, o = walk(g, 1, state[2:], False)
        return g - 1, alive(c), z, c, o

    first = jnp.maximum(i - 1, 0)
    zeros = (jnp.zeros((2 * SB_QB, LANES), F32), jnp.zeros((SB_QB, LANES), F32))
    z, c, o = walk(first, SB_FIRST_TILES, (scores(first, SB_FIRST_TILES),) + zeros, True)
    state = lax.while_loop(lambda st: (st[0] >= 0) & st[1], body, (first - 1, alive(c), z, c, o))
    o_ref[0] = state[4].astype(BF)


def _stick_breaking(qkv, q_c0, k_c0, v_c0, n_pairs):
    idx = np.arange(SB_QB)
    uo = jnp.asarray(-(idx[:, None] > idx[None, :]).astype(np.float32), dtype=BF)
    nb = SEQ // SB_QB
    return pl.pallas_call(
        _sb_kernel,
        grid=(BATCH, n_pairs, nb),
        in_specs=[pl.BlockSpec((1, SB_QB, LANES), lambda b, p, i: (q_c0 + p, b * nb + i, 0)),
                  pl.BlockSpec((1, SEQ, LANES), lambda b, p, i: (k_c0 + p, b, 0)),
                  pl.BlockSpec((1, SEQ, LANES), lambda b, p, i: (v_c0 + p, b, 0)),
                  _resident((SB_QB, SB_QB))],
        out_specs=pl.BlockSpec((1, SB_QB, LANES), lambda b, p, i: (p, b * nb + i, 0)),
        out_shape=jax.ShapeDtypeStruct((n_pairs, N_TOK, LANES), BF),
        compiler_params=_params(("parallel", "parallel", "parallel")),
        name="stick_breaking",
    )(qkv, qkv, qkv, uo)


def _mix_xattn_kernel(*refs):
    x_ref = refs[0]
    wm_ref, g_ref, wq_ref, cs_ref, kv_ref, wo_ref, o_ref = refs[-7:]
    mixed = jnp.concatenate([r[c] for r in refs[1:-7] for c in range(r.shape[0])], axis=1)
    xv = x_ref[...] + jnp.dot(mixed, wm_ref[...], preferred_element_type=F32)
    h = _rms(xv, g_ref[...]).astype(BF)
    heads = range(X_HEADS)
    cols = [slice(X_HEAD_DIM * hd, X_HEAD_DIM * (hd + 1)) for hd in heads]
    acc = [jnp.dot(h, wq_ref[:, cols[hd]], preferred_element_type=F32) for hd in heads]
    ms = [jnp.mean(acc[hd] * acc[hd], axis=1, keepdims=True) for hd in heads]
    q = [(acc[hd] * cs_ref[:, cols[hd]] * lax.rsqrt(ms[hd] + RMS_EPS)).astype(BF) for hd in heads]
    s = [lax.dot_general(q[hd], jnp.concatenate([kv_ref[2 * hd], kv_ref[2 * hd + 1]], axis=1), NT_DIMS,
                         preferred_element_type=F32) for hd in heads]
    pe, l = [], []
    for hd in heads:
        e = jnp.exp2(s[hd] - jnp.max(s[hd], axis=1, keepdims=True))
        l.append(jnp.sum(e, axis=1, keepdims=True))
        pe.append(e.astype(BF))
    v0 = 2 * X_HEADS
    pv = [jnp.dot(pe[hd], jnp.concatenate([kv_ref[v0 + 2 * hd], kv_ref[v0 + 2 * hd + 1]], axis=1),
                  preferred_element_type=F32) for hd in heads]
    o = jnp.concatenate([(pv[hd] / l[hd]).astype(BF) for hd in heads], axis=1)
    o_ref[...] = xv + jnp.dot(o, wo_ref[...], preferred_element_type=F32)


def _mix_xattn(x, mixer_heads, w_mix, g, w_q, q_colscale, kv, w_o, *, tm=512):
    tiles_per_batch = SEQ // tm
    in_specs = [pl.BlockSpec((tm, D_MODEL), lambda i: (i, 0))]
    in_specs += [pl.BlockSpec((mh.shape[0], tm, LANES), lambda i: (0, i, 0)) for mh in mixer_heads]
    in_specs += [_resident(w_mix.shape),
                 _resident((1, D_MODEL)),
                 _resident(w_q.shape),
                 _resident((1, D_MODEL)),
                 pl.BlockSpec((4 * X_HEADS, MEM_LEN, LANES), lambda i: (0, i // tiles_per_batch, 0)),
                 _resident(w_o.shape)]
    return pl.pallas_call(
        _mix_xattn_kernel, grid=(N_TOK // tm,),
        in_specs=in_specs,
        out_specs=pl.BlockSpec((tm, D_MODEL), lambda i: (i, 0)),
        out_shape=jax.ShapeDtypeStruct((N_TOK, D_MODEL), F32),
        compiler_params=_params(("parallel",), VMEM_LIMIT),
        name="mix_xattn",
    )(x, *mixer_heads, w_mix, g.reshape(1, D_MODEL), w_q,
      q_colscale.reshape(1, D_MODEL).astype(F32), kv, w_o)


def _alibi_log2(n_heads):
    return jnp.asarray(LOG2_E * 2.0 ** (-8.0 * np.arange(1, n_heads + 1) / n_heads), dtype=F32)


def _even_mixer_heads(x, norm_g, w_in, q_gain, k_gain, sinks):
    hd = HEAD_DIM
    scale = hd ** -0.5 * LOG2_E
    ones = lambda n: jnp.ones((n,), F32)
    cs = jnp.concatenate([jnp.tile(q_gain, A_Q_HEADS) * scale, jnp.tile(k_gain, A_KV_HEADS), ones(128),
                          ones(512) * scale, ones(1024)])
    plan = [(True, False)] * 4 + [(True, True), (False, True)] + [(False, False)] * 12
    p = _proj(x, norm_g, w_in, cs, plan, hd)
    o_a = _swa(p, _alibi_log2(A_Q_HEADS), sinks.astype(F32) * LOG2_E, n_pairs=A_Q_HEADS // 2, n_kv=A_KV_HEADS,
               max_dist=A_WINDOW - 1)
    o_b = _stick_breaking(p, 8, 12, 16, B_HEADS // 2)
    return [o_a, o_b]


def _odd_mixer_heads(x, norm_g, w_in, q_gain, k_gain):
    hd = HEAD_DIM
    cs = jnp.concatenate([jnp.tile(q_gain, C_HEADS) * (hd ** -0.5 * LOG2_E), jnp.tile(k_gain, C_HEADS),
                          jnp.ones((C_HEADS * hd,), F32)])
    plan = [(True, False)] * 16 + [(False, False)] * 8
    p = _proj(x, norm_g, w_in, cs, plan, hd)
    return [_dilated(p, _alibi_log2(C_HEADS))]


def _mix_and_cross_attention(x, mixer_heads, w_mix, mem2d, norm_g, mem_g, w_q, w_kv, q_gain, k_gain, w_o):
    cs_kv = jnp.concatenate([jnp.tile(k_gain, X_HEADS), jnp.ones((D_MODEL,), F32)])
    plan = [(True, False)] * 8 + [(False, False)] * 8
    kv = _proj(mem2d, mem_g, w_kv, cs_kv, plan, X_HEAD_DIM, tm=256)
    cs_q = jnp.tile(q_gain, X_HEADS) * (X_HEAD_DIM ** -0.5 * LOG2_E)
    return _mix_xattn(x, mixer_heads, w_mix, norm_g, w_q, cs_q, kv, w_o)


def kernel(x, mem, ffn1_norm, ffn1_w_gu, ffn1_w_down, mix_norm, ev_w_in, ev_q_gain, ev_k_gain, ev_sinks, ev_w_out, od_w_in, od_q_gain, od_k_gain, od_w_out, xa_norm, xa_mem_norm, xa_w_q, xa_w_kv, xa_q_gain, xa_k_gain, xa_w_o, ffn2_norm, ffn2_w_gu, ffn2_w_down):
    x = x.reshape(N_TOK, D_MODEL)
    mem2d = mem.reshape(BATCH * MEM_LEN, D_MODEL)
    w_gu, w_down = _cast_now((ffn1_w_gu, 0)), _cast_now((ffn1_w_down, 0))
    for layer in range(DEPTH):
        j = layer // 2
        even = layer % 2 == 0
        w_in3, w_mix3 = (ev_w_in, ev_w_out) if even else (od_w_in, od_w_out)
        jobs = [(w_in3, j), (w_mix3, j), (xa_w_q, layer), (xa_w_kv, layer), (xa_w_o, layer),
                (ffn2_w_gu, layer), (ffn2_w_down, layer)]
        x, (w_in, w_mix, w_q, w_kv, w_o, w_gu, w_down) = _ffn(x, ffn1_norm[layer], w_gu, w_down, jobs)
        if even:
            heads = _even_mixer_heads(x, mix_norm[layer], w_in, ev_q_gain[j], ev_k_gain[j], ev_sinks[j])
        else:
            heads = _odd_mixer_heads(x, mix_norm[layer], w_in, od_q_gain[j], od_k_gain[j])
        x = _mix_and_cross_attention(x, heads, w_mix, mem2d, xa_norm[layer], xa_mem_norm[layer], w_q, w_kv,
                                     xa_q_gain[layer], xa_k_gain[layer], w_o)
        jobs = [(ffn1_w_gu, layer + 1), (ffn1_w_down, layer + 1)] if layer + 1 < DEPTH else []
        x, next_ffn1 = _ffn(x, ffn2_norm[layer], w_gu, w_down, jobs)
        if next_ffn1:
            w_gu, w_down = next_ffn1
    return x.reshape(BATCH, SEQ, D_MODEL)
```

```python
import functools

import numpy as np
import jax
import jax.numpy as jnp
from jax import lax
from jax.experimental import pallas as pl
from jax.experimental.pallas import tpu as pltpu

D_MODEL = 1024
BATCH = 4
SEQ = 4096
N_TOK = BATCH * SEQ
DEPTH = 2
HEAD_DIM = 64
BLOCK = 128
A_Q_HEADS = 8
A_KV_HEADS = 2
A_WINDOW = 128
B_HEADS = 8
C_HEADS = 16
C_PATTERNS = ((128, 1), (512, 4), (2048, 16))
MEM_LEN = 256
X_HEADS = 4
X_HEAD_DIM = D_MODEL // X_HEADS
D_FF = 2816
RMS_EPS = 1e-6

LANES = 128
MXU_N = 256
VMEM_LIMIT = 56 * 1024 * 1024

BF = jnp.bfloat16
F32 = jnp.float32
NT_DIMS = (((1,), (1,)), ((), ()))
LOG2_E = 1.4426950408889634


def _params(sem, vmem=None):
    return pltpu.CompilerParams(dimension_semantics=sem, vmem_limit_bytes=vmem)


def _resident(shape):
    nd = len(shape)
    return pl.BlockSpec(shape, lambda *_: (0,) * nd, pipeline_mode=pl.Buffered(1))


BF16_SUBLANES = 16


def _cast_specs(job, steps):
    w3, layer = job
    _, r, c = w3.shape
    rb = next(rb for rb in range(BF16_SUBLANES, r + 1, BF16_SUBLANES) if r % rb == 0 and r // rb <= steps)
    last = r // rb - 1
    return (pl.BlockSpec((None, rb, c), lambda i: (layer, jnp.minimum(i, last), 0)),
            pl.BlockSpec((rb, c), lambda i: (jnp.minimum(i, last), 0)),
            jax.ShapeDtypeStruct((r, c), BF))


def _run_cast_jobs(in_refs, out_refs):
    for src, dst in zip(in_refs, out_refs):
        dst[...] = src[...].astype(BF)


def _cast_kernel(w_ref, o_ref):
    _run_cast_jobs([w_ref], [o_ref])


def _cast_now(job, *, rows=128):
    steps = job[0].shape[1] // rows
    in_spec, out_spec, out_shape = _cast_specs(job, steps)
    return pl.pallas_call(
        _cast_kernel, grid=(steps,), in_specs=[in_spec], out_specs=out_spec, out_shape=out_shape,
        compiler_params=_params(("arbitrary",)),
        name="cast",
    )(job[0])


def _rms(xv, g):
    ms = jnp.mean(xv * xv, axis=-1, keepdims=True)
    return xv * lax.rsqrt(ms + RMS_EPS) * g


FFN_SPLIT = (D_FF // MXU_N + 1) // 2 * MXU_N
FFN_CHUNKS = ((0, FFN_SPLIT), (FFN_SPLIT, D_FF))


def _ffn_kernel(*refs, n_jobs):
    x_ref, g_ref, wgu_ref, wd_ref = refs[:4]
    o_ref = refs[4 + n_jobs]
    xv = x_ref[...]
    h = _rms(xv, g_ref[...]).astype(BF)
    acc = jnp.zeros_like(xv)
    for c0, c1 in FFN_CHUNKS:
        gate = jnp.dot(h, wgu_ref[:, c0:c1], preferred_element_type=F32)
        up = jnp.dot(h, wgu_ref[:, D_FF + c0:D_FF + c1], preferred_element_type=F32)
        act = (gate * jax.nn.sigmoid(gate) * up).astype(BF)
        acc = acc + jnp.dot(act, wd_ref[c0:c1, :], preferred_element_type=F32)
    o_ref[...] = xv + 0.5 * acc
    _run_cast_jobs(refs[4:4 + n_jobs], refs[5 + n_jobs:])


def _ffn(x, g, w_gu, w_down, cast_jobs=(), *, tm=512):
    steps = N_TOK // tm
    specs = [_cast_specs(job, steps) for job in cast_jobs]
    out = pl.pallas_call(
        functools.partial(_ffn_kernel, n_jobs=len(cast_jobs)),
        grid=(steps,),
        in_specs=[pl.BlockSpec((tm, D_MODEL), lambda i: (i, 0)),
                  _resident((1, D_MODEL)),
                  _resident(w_gu.shape),
                  _resident(w_down.shape)] + [s[0] for s in specs],
        out_specs=[pl.BlockSpec((tm, D_MODEL), lambda i: (i, 0))] + [s[1] for s in specs],
        out_shape=[jax.ShapeDtypeStruct((N_TOK, D_MODEL), F32)] + [s[2] for s in specs],
        compiler_params=_params(("arbitrary",), VMEM_LIMIT),
        name="ffn",
    )(x, g.reshape(1, D_MODEL), w_gu, w_down, *[job[0] for job in cast_jobs])
    return out[0], out[1:]


def _group_ones(gs):
    idx = np.arange(MXU_N) // gs
    return jnp.asarray(idx[:, None] == idx[None, :], dtype=BF)


def _proj_kernel(x_ref, g_ref, w_ref, cs_ref, bd_ref, o_ref, *, plan, gs):
    h = _rms(x_ref[...], g_ref[...]).astype(BF)
    n_chunks = len(plan) // 2
    lo = lax.broadcasted_iota(jnp.int32, (x_ref.shape[0], LANES), 1) < HEAD_DIM

    def main(j):
        return jnp.dot(h, w_ref[:, MXU_N * j:MXU_N * (j + 1)], preferred_element_type=F32)

    acc_next = main(0)
    out = 0
    for j in range(n_chunks):
        cols = slice(MXU_N * j, MXU_N * (j + 1))
        acc = acc_next
        if j + 1 < n_chunks:
            acc_next = main(j + 1)
        y = acc * cs_ref[:, cols]
        halves = plan[2 * j:2 * j + 2]
        if any(normed for normed, _ in halves):
            ss = jnp.dot((acc * acc).astype(BF), bd_ref[...], preferred_element_type=F32)
            inv = lax.rsqrt(ss * (1.0 / gs) + RMS_EPS)
        for half, (normed, dup) in enumerate(halves):
            lanes = slice(LANES * half, LANES * (half + 1))
            yh = y[:, lanes] * inv[:, lanes] if normed else y[:, lanes]
            if dup:
                swapped = pltpu.roll(yh, HEAD_DIM, axis=1)
                o_ref[out] = jnp.where(lo, yh, swapped).astype(BF)
                o_ref[out + 1] = jnp.where(lo, swapped, yh).astype(BF)
                out += 2
            else:
                o_ref[out] = yh.astype(BF)
                out += 1


def _proj(x, g, w, colscale, plan, gs, *, tm=1024):
    rows = x.shape[0]
    wout = w.shape[1]
    assert wout == LANES * len(plan) and len(plan) % 2 == 0
    c = sum(2 if dup else 1 for _, dup in plan)
    return pl.pallas_call(
        functools.partial(_proj_kernel, plan=tuple(plan), gs=gs),
        grid=(rows // tm,),
        in_specs=[pl.BlockSpec((tm, D_MODEL), lambda i: (i, 0)),
                  _resident((1, D_MODEL)),
                  _resident(w.shape),
                  _resident((1, wout)),
                  _resident((MXU_N, MXU_N))],
        out_specs=pl.BlockSpec((c, tm, LANES), lambda i: (0, i, 0)),
        out_shape=jax.ShapeDtypeStruct((c, rows, LANES), BF),
        compiler_params=_params(("parallel",), VMEM_LIMIT),
        name="proj",
    )(x, g.reshape(1, D_MODEL), w, colscale.reshape(1, wout).astype(F32), _group_ones(gs))


def _swa_block(q_blocks, kp, kc, vp, vc, has_prev, slopes_ref, sinks_ref, *, kv_div, max_dist):
    row = lax.broadcasted_iota(jnp.int32, (BLOCK, 2 * BLOCK), 0)
    col = lax.broadcasted_iota(jnp.int32, (BLOCK, 2 * BLOCK), 1)
    dist = row + BLOCK - col
    valid = (dist >= 0) & (dist <= max_dist)
    if has_prev is not True:
        valid = valid & ((col >= BLOCK) | has_prev)
    negmask = jnp.where(valid, 0.0, -jnp.inf)
    distf = dist.astype(F32)
    lo = lax.broadcasted_iota(jnp.int32, (BLOCK, LANES), 1) < HEAD_DIM

    n_groups = len(q_blocks) // kv_div
    heads_per_group = 2 * kv_div
    scores = []
    for g in range(n_groups):
        parts = []
        for p in range(g * kv_div, (g + 1) * kv_div):
            q2 = q_blocks[p].astype(F32)
            parts += [jnp.where(lo, q2, 0.0), jnp.where(lo, 0.0, q2)]
        q_stack = jnp.concatenate(parts, axis=0).astype(BF)
        scores.append(jnp.concatenate(
            [lax.dot_general(q_stack, kp[g], NT_DIMS, preferred_element_type=F32),
             lax.dot_general(q_stack, kc[g], NT_DIMS, preferred_element_type=F32)], axis=1))
    soft = []
    for g in range(n_groups):
        res = []
        for j in range(heads_per_group):
            h = g * heads_per_group + j
            s = scores[g][j * BLOCK:(j + 1) * BLOCK] - slopes_ref[h] * distf + negmask
            m = jnp.maximum(jnp.max(s, axis=1, keepdims=True), sinks_ref[h])
            pe = jnp.exp2(s - m)
            res.append((pe.astype(BF), jnp.sum(pe, axis=1, keepdims=True) + jnp.exp2(sinks_ref[h] - m)))
        soft.append(res)
    outs = []
    for g in range(n_groups):
        pb = jnp.concatenate([r[0] for r in soft[g]], axis=0)
        pv = (jnp.dot(pb[:, :BLOCK], vp[g], preferred_element_type=F32)
              + jnp.dot(pb[:, BLOCK:], vc[g], preferred_element_type=F32))
        for jp in range(kv_div):
            o0 = pv[(2 * jp) * BLOCK:(2 * jp + 1) * BLOCK] / soft[g][2 * jp][1]
            o1 = pv[(2 * jp + 1) * BLOCK:(2 * jp + 2) * BLOCK] / soft[g][2 * jp + 1][1]
            outs.append(jnp.where(lo, o0, o1))
    return outs


DIL_ORDER = tuple(sorted(C_PATTERNS, key=lambda wd: -wd[1]))
DIL_UNROLL = 8
DIL_AHEAD = 2
DIL_BASE = 4
DIL_Q = SEQ // DIL_BASE
DIL_CONVERT_ROWS = DIL_BASE * BLOCK


def _dilated_kernel(slopes_ref, q_ref, k_ref, v_ref, o_ref, qn_s, tq_s, tk_s, tv_s, q0_s, q1_s, k_s, v_s,
                    acc_r, m_r, l_r, acc_n, m_n, l_n):
    assert all(d == 1 or d % DIL_BASE == 0 for _, d in DIL_ORDER) and DIL_ORDER[-1][1] == 1
    p = pl.program_id(1)
    lo = lax.broadcasted_iota(jnp.int32, (BLOCK, LANES), 1) < HEAD_DIM

    def convert(c, carry):
        rows = pl.ds(pl.multiple_of(c * DIL_CONVERT_ROWS, DIL_CONVERT_ROWS), DIL_CONVERT_ROWS)
        q_nat = q_ref[0, rows, :].astype(F32)
        lo_c = lax.broadcasted_iota(jnp.int32, (DIL_CONVERT_ROWS, LANES), 1) < HEAD_DIM
        qn_s[0, rows, :] = jnp.where(lo_c, q_nat, 0.0).astype(BF)
        qn_s[1, rows, :] = jnp.where(lo_c, 0.0, q_nat).astype(BF)
        tq_s[...] = q_nat
        tk_s[...] = k_ref[0, rows, :].astype(F32)
        tv_s[...] = v_ref[0, rows, :].astype(F32)
        for rho in range(DIL_BASE):
            src = pl.ds(rho, BLOCK, stride=DIL_BASE)
            dst = pl.ds(pl.multiple_of(rho * DIL_Q + c * BLOCK, BLOCK), BLOCK)
            q = tq_s[src, :]
            q0_s[dst, :] = jnp.where(lo, q, 0.0)
            q1_s[dst, :] = jnp.where(lo, 0.0, q)
            k_s[dst, :] = tk_s[src, :]
            v_s[dst, :] = tv_s[src, :]
        return carry

    lax.fori_loop(0, SEQ // DIL_CONVERT_ROWS, convert, 0)

    row = lax.broadcasted_iota(jnp.int32, (BLOCK, 2 * BLOCK), 0)
    col = lax.broadcasted_iota(jnp.int32, (BLOCK, 2 * BLOCK), 1)
    dist = row + BLOCK - col
    distf = dist.astype(F32)
    no_prev = jnp.where(col < BLOCK, -jnp.inf, 0.0)

    def bcast2(a0, a1):
        return jnp.where(lo, jnp.broadcast_to(a0, (BLOCK, LANES)), jnp.broadcast_to(a1, (BLOCK, LANES)))

    for pi, (window, d) in enumerate(DIL_ORDER):
        first, last = pi == 0, pi == len(DIL_ORDER) - 1
        natural = d == 1
        nb = SEQ // d // BLOCK
        band = (dist >= 0) & (dist <= window // d)
        bias = [jnp.where(band, (-float(d) * slopes_ref[2 * p + hh]) * distf, -jnp.inf) for hh in range(2)]
        acc_s, m_s, l_s = (acc_n, m_n, l_n) if natural else (acc_r, m_r, l_r)

        if natural and not first:
            for rho in range(DIL_BASE):
                src, dst = pl.ds(rho * DIL_Q, DIL_Q), pl.ds(rho, DIL_Q, stride=DIL_BASE)
                acc_n[dst, :] = acc_r[src, :]
                m_n[dst, :] = m_r[src, :]
                l_n[dst, :] = l_r[src, :]

        def rows_of(r, n, n_blocks=1, d=d, natural=natural):
            size = n_blocks * BLOCK
            if natural:
                return pl.ds(pl.multiple_of(BLOCK * n, BLOCK), size)
            inner = d // DIL_BASE
            start = (r % DIL_BASE) * DIL_Q + r // DIL_BASE + inner * BLOCK * n
            return pl.ds(start, size, stride=inner) if inner > 1 else pl.ds(pl.multiple_of(start, BLOCK), size)

        def step(it, carry, nb=nb, bias=bias, first=first, last=last, natural=natural, rows_of=rows_of,
                 acc_s=acc_s, m_s=m_s, l_s=l_s):
            assert DIL_UNROLL % nb == 0 or nb % DIL_UNROLL == 0
            load_k = (lambda rr: k_ref[0, rr, :]) if natural else (lambda rr: k_s[rr, :].astype(BF))
            load_v = (lambda rr: v_ref[0, rr, :]) if natural else (lambda rr: v_s[rr, :].astype(BF))

            def scores(u):
                t = it * DIL_UNROLL + u
                r, n = t // nb, t % nb
                prev = (u % nb != 0) if nb <= DIL_UNROLL else (True if u else None)
                rows = rows_of(r, n)
                if prev is True:
                    kv_rows = [rows_of(r, n - 1, 2)]
                elif prev is None:
                    kv_rows = [rows_of(r, jnp.maximum(n - 1, 0)), rows]
                else:
                    kv_rows = [rows]
                if natural:
                    qh = jnp.concatenate([qn_s[0, rows, :], qn_s[1, rows, :]], axis=0)
                else:
                    qh = jnp.concatenate([q0_s[rows, :], q1_s[rows, :]], axis=0).astype(BF)
                s = jnp.concatenate([lax.dot_general(qh, load_k(rr), NT_DIMS, preferred_element_type=F32)
                                     for rr in kv_rows], axis=1)
                return n, prev, rows, kv_rows, (s[:BLOCK], s[BLOCK:])

            def softmax_pv(blk):
                n, prev, rows, kv_rows, s = blk
                ms, ls, pes = [], [], []
                for hh in range(2):
                    sh = s[hh] + (bias[hh][:, BLOCK:] if prev is False else bias[hh])
                    if prev is None:
                        sh = sh + jnp.where(n == 0, no_prev, 0.0)
                    m = jnp.max(sh, axis=1, keepdims=True)
                    pe = jnp.exp2(sh - m)
                    ms.append(m)
                    ls.append(jnp.sum(pe, axis=1, keepdims=True))
                    pes.append(pe.astype(BF))
                pb = jnp.concatenate(pes, axis=0)
                v = jnp.concatenate([load_v(rr) for rr in kv_rows], axis=0) if len(kv_rows) > 1 else load_v(kv_rows[0])
                pv = jnp.dot(pb, v, preferred_element_type=F32)
                return rows, ms, ls, (pv[:BLOCK], pv[BLOCK:])

            def merge(rows, ms, ls, pv):
                m2 = bcast2(ms[0], ms[1])
                l2 = bcast2(ls[0], ls[1])
                acc2 = jnp.where(lo, pv[0], pv[1])
                if not first:
                    m_old = m_s[rows, :]
                    m_new = jnp.maximum(m_old, m2)
                    a_old, a_new = jnp.exp2(m_old - m_new), jnp.exp2(m2 - m_new)
                    l2 = a_old * l_s[rows, :] + a_new * l2
                    acc2 = a_old * acc_s[rows, :] + a_new * acc2
                    m2 = m_new
                if last:
                    o_ref[0, rows, :] = (acc2 / l2).astype(BF)
                else:
                    m_s[rows, :] = m2
                    l_s[rows, :] = l2
                    acc_s[rows, :] = acc2

            pending = {u: scores(u) for u in range(DIL_AHEAD)}
            done = None
            for u in range(DIL_UNROLL):
                if u + DIL_AHEAD < DIL_UNROLL:
                    pending[u + DIL_AHEAD] = scores(u + DIL_AHEAD)
                cur = softmax_pv(pending.pop(u))
                if done is not None:
                    merge(*done)
                done = cur
            merge(*done)
            return carry

        lax.fori_loop(0, SEQ // BLOCK // DIL_UNROLL, step, 0)


def _dilated(qkv, slopes):
    n_pairs = C_HEADS // 2
    seq_f32 = pltpu.VMEM((SEQ, LANES), F32)
    chunk_f32 = pltpu.VMEM((DIL_CONVERT_ROWS, LANES), F32)
    return pl.pallas_call(
        _dilated_kernel,
        grid=(BATCH, n_pairs),
        in_specs=[pl.BlockSpec(memory_space=pltpu.SMEM),
                  pl.BlockSpec((1, SEQ, LANES), lambda b, p: (p, b, 0)),
                  pl.BlockSpec((1, SEQ, LANES), lambda b, p: (n_pairs + p, b, 0)),
                  pl.BlockSpec((1, SEQ, LANES), lambda b, p: (2 * n_pairs + p, b, 0))],
        out_specs=pl.BlockSpec((1, SEQ, LANES), lambda b, p: (p, b, 0)),
        out_shape=jax.ShapeDtypeStruct((n_pairs, N_TOK, LANES), BF),
        scratch_shapes=[pltpu.VMEM((2, SEQ, LANES), BF)] + [chunk_f32] * 3 + [seq_f32] * 10,
        compiler_params=_params(("parallel", "parallel"), VMEM_LIMIT),
        name="dilated",
    )(slopes, qkv, qkv, qkv)


SB_QB = MXU_N
SB_FIRST_TILES = 2
SB_DEAD_LOG2 = -150.0


def _sb_matrix():
    idx = np.arange(SB_QB)
    return jnp.asarray(-(idx[:, None] > idx[None, :]).astype(np.float32), dtype=BF)


def _sb_unit(q2, load_k, load_v, iq, uo, fill=lambda: None):
    lo = lax.broadcasted_iota(jnp.int32, (SB_QB, LANES), 1) < HEAD_DIM
    q_stack = jnp.concatenate([jnp.where(lo, q2, 0.0), jnp.where(lo, 0.0, q2)], axis=0).astype(BF)
    rel1 = (lax.broadcasted_iota(jnp.int32, (SB_QB, SB_QB), 1)
            - lax.broadcasted_iota(jnp.int32, (SB_QB, SB_QB), 0))
    rel = jnp.concatenate([rel1, rel1], axis=0)

    def scores(first, n_tiles):
        return lax.dot_general(q_stack, load_k(first, n_tiles), NT_DIMS,
                               preferred_element_type=F32)

    def walk(first, n_tiles, carry, masked):
        z, c, o = carry
        z_next = scores(jnp.maximum(first - 1, 0), 1)
        if masked:
            fill()
        order = list(reversed(range(n_tiles)))
        sps, es, stricts, totals = [], {}, {}, {}
        for t in order:
            zt = z[:, t * SB_QB:(t + 1) * SB_QB]
            sp = jnp.maximum(zt, 0.0) + jnp.log2(1.0 + jnp.exp2(-jnp.abs(zt)))
            es[t] = zt - sp
            if masked:
                stricts[t] = rel < (iq - first - t) * SB_QB
                sp = jnp.where(stricts[t], sp, 0.0)
            sps.append(sp.astype(BF))
            totals[t] = jnp.sum(sp, axis=1, keepdims=True)
        w_all = jnp.dot(jnp.concatenate(sps, axis=0), uo, preferred_element_type=F32)
        if masked:
            fill()
        parts = {}
        for idx, t in enumerate(order):
            w = w_all[idx * 2 * SB_QB:(idx + 1) * 2 * SB_QB]
            a = jnp.exp2(es[t] + jnp.concatenate([c] * (SB_QB // LANES), axis=1) + w)
            if masked:
                a = jnp.where(stricts[t], a, 0.0)
            parts[t] = a.astype(BF)
            c = c - totals[t]
        pv = jnp.dot(jnp.concatenate([parts[t] for t in range(n_tiles)], axis=1), load_v(first, n_tiles),
                     preferred_element_type=F32)
        if masked:
            fill()
        return z_next, c, o + jnp.where(lo, pv[:SB_QB], pv[SB_QB:])

    def alive(c):
        return jnp.max(c) > SB_DEAD_LOG2

    def body(state):
        g = state[0]
        z, c, o = walk(g, 1, state[2:], False)
        return g - 1, alive(c), z, c, o

    first = jnp.maximum(iq - 1, 0)
    zeros = (jnp.zeros((2 * SB_QB, LANES), F32), jnp.zeros((SB_QB, LANES), F32))
    z, c, o = walk(first, SB_FIRST_TILES, (scores(first, SB_FIRST_TILES),) + zeros, True)
    state = lax.while_loop(lambda st: (st[0] >= 0) & st[1], body, (first - 1, alive(c), z, c, o))
    return state[4]


def _mix_xattn_kernel(*refs):
    x_ref = refs[0]
    wm_ref, g_ref, wq_ref, cs_ref, kv_ref, wo_ref, o_ref = refs[-7:]
    mixed = jnp.concatenate([r[c] for r in refs[1:-7] for c in range(r.shape[0])], axis=1)
    xv = x_ref[...] + jnp.dot(mixed, wm_ref[...], preferred_element_type=F32)
    h = _rms(xv, g_ref[...]).astype(BF)
    heads = range(X_HEADS)
    cols = [slice(X_HEAD_DIM * hd, X_HEAD_DIM * (hd + 1)) for hd in heads]
    acc = [jnp.dot(h, wq_ref[:, cols[hd]], preferred_element_type=F32) for hd in heads]
    ms = [jnp.mean(acc[hd] * acc[hd], axis=1, keepdims=True) for hd in heads]
    q = [(acc[hd] * cs_ref[:, cols[hd]] * lax.rsqrt(ms[hd] + RMS_EPS)).astype(BF) for hd in heads]
    s = [lax.dot_general(q[hd], jnp.concatenate([kv_ref[2 * hd], kv_ref[2 * hd + 1]], axis=1), NT_DIMS,
                         preferred_element_type=F32) for hd in heads]
    pe, l = [], []
    for hd in heads:
        e = jnp.exp2(s[hd] - jnp.max(s[hd], axis=1, keepdims=True))
        l.append(jnp.sum(e, axis=1, keepdims=True))
        pe.append(e.astype(BF))
    v0 = 2 * X_HEADS
    pv = [jnp.dot(pe[hd], jnp.concatenate([kv_ref[v0 + 2 * hd], kv_ref[v0 + 2 * hd + 1]], axis=1),
                  preferred_element_type=F32) for hd in heads]
    o = jnp.concatenate([(pv[hd] / l[hd]).astype(BF) for hd in heads], axis=1)
    o_ref[...] = xv + jnp.dot(o, wo_ref[...], preferred_element_type=F32)


def _mix_xattn(x, mixer_heads, w_mix, g, w_q, q_colscale, kv, w_o, *, tm=1024):
    tiles_per_batch = SEQ // tm
    in_specs = [pl.BlockSpec((tm, D_MODEL), lambda i: (i, 0))]
    in_specs += [pl.BlockSpec((mh.shape[0], tm, LANES), lambda i: (0, i, 0)) for mh in mixer_heads]
    in_specs += [_resident(w_mix.shape),
                 _resident((1, D_MODEL)),
                 _resident(w_q.shape),
                 _resident((1, D_MODEL)),
                 pl.BlockSpec((4 * X_HEADS, MEM_LEN, LANES), lambda i: (0, i // tiles_per_batch, 0)),
                 _resident(w_o.shape)]
    return pl.pallas_call(
        _mix_xattn_kernel, grid=(N_TOK // tm,),
        in_specs=in_specs,
        out_specs=pl.BlockSpec((tm, D_MODEL), lambda i: (i, 0)),
        out_shape=jax.ShapeDtypeStruct((N_TOK, D_MODEL), F32),
        compiler_params=_params(("parallel",), VMEM_LIMIT),
        name="mix_xattn",
    )(x, *mixer_heads, w_mix, g.reshape(1, D_MODEL), w_q,
      q_colscale.reshape(1, D_MODEL).astype(F32), kv, w_o)


EVEN_TILE = 512


def _even_tail_kernel(slopes_ref, sinks_ref, x_ref, swq_ref, swkp_ref, swkc_ref, swvp_ref, swvc_ref,
                      sbq_ref, sbk_ref, sbv_ref, uo_ref, wm_ref, g_ref, wq_ref, cs_ref, kv_ref, wo_ref,
                      o_ref, heads_s, *, kv_div, max_dist):
    i = pl.program_id(0)
    tiles_per_seq = SEQ // EVEN_TILE
    t_in_seq = jnp.minimum(i, N_TOK // EVEN_TILE - 1) % tiles_per_seq
    wr, rd = i % 2, (i + 1) % 2
    n_pairs_a, n_pairs_b = swq_ref.shape[0], sbq_ref.shape[0]

    @pl.when(i == 0)
    def _():
        heads_s[1] = jnp.zeros(heads_s.shape[1:], BF)

    heads = range(X_HEADS)
    cols = [slice(X_HEAD_DIM * hd, X_HEAD_DIM * (hd + 1)) for hd in heads]
    st = {"xv": [None] * X_HEADS, "q": [None] * X_HEADS, "pe": [None] * X_HEADS, "l": [None] * X_HEADS,
          "o": [None] * X_HEADS}

    def project(c):
        if c == 0:
            st["mixed"] = jnp.concatenate([heads_s[rd, k] for k in range(n_pairs_a + n_pairs_b)], axis=1)
        st["xv"][c] = x_ref[:, cols[c]] + jnp.dot(st["mixed"], wm_ref[:, cols[c]], preferred_element_type=F32)

    def q_proj(hd):
        if hd == 0:
            xv = jnp.concatenate(st["xv"], axis=1)
            st["h"] = _rms(xv, g_ref[...]).astype(BF)
        acc = jnp.dot(st["h"], wq_ref[:, cols[hd]], preferred_element_type=F32)
        ms = jnp.mean(acc * acc, axis=1, keepdims=True)
        st["q"][hd] = (acc * cs_ref[:, cols[hd]] * lax.rsqrt(ms + RMS_EPS)).astype(BF)

    def scores(hd):
        kh = jnp.concatenate([kv_ref[2 * hd], kv_ref[2 * hd + 1]], axis=1)
        s = lax.dot_general(st["q"][hd], kh, NT_DIMS, preferred_element_type=F32)
        e = jnp.exp2(s - jnp.max(s, axis=1, keepdims=True))
        st["l"][hd] = jnp.sum(e, axis=1, keepdims=True)
        st["pe"][hd] = e.astype(BF)

    def values(hd):
        v0 = 2 * X_HEADS
        vh = jnp.concatenate([kv_ref[v0 + 2 * hd], kv_ref[v0 + 2 * hd + 1]], axis=1)
        st["o"][hd] = (jnp.dot(st["pe"][hd], vh, preferred_element_type=F32) / st["l"][hd]).astype(BF)

    def out_proj(c):
        if c == 0:
            st["oc"] = jnp.concatenate(st["o"], axis=1)
        o_ref[:, cols[c]] = st["xv"][c] + jnp.dot(st["oc"], wo_ref[:, cols[c]], preferred_element_type=F32)

    chain = [functools.partial(f, k) for f in (project, q_proj, scores, values, out_proj) for k in heads]

    def fill():
        if chain:
            chain.pop(0)()

    uo = uo_ref[...]
    blocks_per_tile = EVEN_TILE // SB_QB

    def sb_unit(p, j):
        rows = slice(SB_QB * j, SB_QB * (j + 1))
        load = lambda ref: (lambda first, n: ref[p, pl.ds(pl.multiple_of(first * SB_QB, SB_QB), n * SB_QB), :])
        o = _sb_unit(sbq_ref[p, rows, :].astype(F32), load(sbk_ref), load(sbv_ref),
                     blocks_per_tile * t_in_seq + j, uo, fill)
        heads_s[wr, n_pairs_a + p, rows, :] = o.astype(BF)

    for p in range(n_pairs_b):
        for j in range(blocks_per_tile):
            sb_unit(p, j)
    while chain:
        fill()

    n_kv = swkc_ref.shape[0]
    for m in range(EVEN_TILE // BLOCK):
        rows = slice(BLOCK * m, BLOCK * (m + 1))
        if m == 0:
            kp, vp, has_prev = [swkp_ref[g] for g in range(n_kv)], [swvp_ref[g] for g in range(n_kv)], t_in_seq > 0
        else:
            prows = slice(BLOCK * (m - 1), BLOCK * m)
            kp, vp, has_prev = [swkc_ref[g, prows, :] for g in range(n_kv)], [swvc_ref[g, prows, :] for g in range(n_kv)], True
        outs = _swa_block([swq_ref[p, rows, :] for p in range(n_pairs_a)], kp, [swkc_ref[g, rows, :] for g in range(n_kv)],
                          vp, [swvc_ref[g, rows, :] for g in range(n_kv)], has_prev, slopes_ref, sinks_ref,
                          kv_div=kv_div, max_dist=max_dist)
        for p in range(n_pairs_a):
            heads_s[wr, p, rows, :] = outs[p].astype(BF)


def _even_tail(x, p, slopes, sinks, w_mix, g, w_q, q_colscale, kv, w_o):
    n_pairs_a, n_kv, n_pairs_b = A_Q_HEADS // 2, A_KV_HEADS, B_HEADS // 2
    assert p.shape[0] == n_pairs_a + 2 * n_kv + 3 * n_pairs_b and n_pairs_a == n_pairs_b == 2 * n_kv
    n_tiles = N_TOK // EVEN_TILE
    tiles_per_seq = SEQ // EVEN_TILE
    blocks_per_tile = EVEN_TILE // BLOCK
    att = lambda i: jnp.minimum(i, n_tiles - 1)
    mix = lambda i: jnp.maximum(i - 1, 0)
    tile_rows = lambda size, idx: pl.BlockSpec((size, EVEN_TILE, LANES), lambda i: (idx, att(i), 0))
    prev_block = lambda idx: pl.BlockSpec(
        (n_kv, BLOCK, LANES), lambda i: (idx, jnp.maximum(blocks_per_tile * att(i) - 1, 0), 0))
    whole_seq = lambda idx: pl.BlockSpec((n_pairs_b, SEQ, LANES), lambda i: (idx, att(i) // tiles_per_seq, 0),
                                         pipeline_mode=pl.Buffered(1))
    smem = pl.BlockSpec(memory_space=pltpu.SMEM)
    return pl.pallas_call(
        functools.partial(_even_tail_kernel, kv_div=n_pairs_a // n_kv, max_dist=A_WINDOW - 1),
        grid=(n_tiles + 1,),
        in_specs=[smem, smem,
                  pl.BlockSpec((EVEN_TILE, D_MODEL), lambda i: (mix(i), 0)),
                  tile_rows(n_pairs_a, 0), prev_block(2), tile_rows(n_kv, 2), prev_block(3), tile_rows(n_kv, 3),
                  tile_rows(n_pairs_b, 2), whole_seq(3), whole_seq(4),
                  _resident((SB_QB, SB_QB)),
                  _resident(w_mix.shape), _resident((1, D_MODEL)), _resident(w_q.shape), _resident((1, D_MODEL)),
                  pl.BlockSpec((4 * X_HEADS, MEM_LEN, LANES), lambda i: (0, mix(i) // tiles_per_seq, 0)),
                  _resident(w_o.shape)],
        out_specs=pl.BlockSpec((EVEN_TILE, D_MODEL), lambda i: (mix(i), 0)),
        out_shape=jax.ShapeDtypeStruct((N_TOK, D_MODEL), F32),
        scratch_shapes=[pltpu.VMEM((2, n_pairs_a + n_pairs_b, EVEN_TILE, LANES), BF)],
        compiler_params=_params(("arbitrary",), VMEM_LIMIT),
        name="even_tail",
    )(slopes, sinks, x, p, p, p, p, p, p, p, p, _sb_matrix(), w_mix, g.reshape(1, D_MODEL), w_q,
      q_colscale.reshape(1, D_MODEL).astype(F32), kv, w_o)


def _alibi_log2(n_heads):
    return jnp.asarray(LOG2_E * 2.0 ** (-8.0 * np.arange(1, n_heads + 1) / n_heads), dtype=F32)


def _even_projection(x, norm_g, w_in, q_gain, k_gain):
    hd = HEAD_DIM
    scale = hd ** -0.5 * LOG2_E
    ones = lambda n: jnp.ones((n,), F32)
    cs = jnp.concatenate([jnp.tile(q_gain, A_Q_HEADS) * scale, jnp.tile(k_gain, A_KV_HEADS), ones(128),
                          ones(512) * scale, ones(1024)])
    plan = [(True, False)] * 4 + [(True, True), (False, True)] + [(False, False)] * 12
    return _proj(x, norm_g, w_in, cs, plan, hd)


def _odd_mixer_heads(x, norm_g, w_in, q_gain, k_gain):
    hd = HEAD_DIM
    cs = jnp.concatenate([jnp.tile(q_gain, C_HEADS) * (hd ** -0.5 * LOG2_E), jnp.tile(k_gain, C_HEADS),
                          jnp.ones((C_HEADS * hd,), F32)])
    plan = [(True, False)] * 16 + [(False, False)] * 8
    p = _proj(x, norm_g, w_in, cs, plan, hd)
    return [_dilated(p, _alibi_log2(C_HEADS))]


def _memory_kv(mem2d, mem_g, w_kv, k_gain):
    cs_kv = jnp.concatenate([jnp.tile(k_gain, X_HEADS), jnp.ones((D_MODEL,), F32)])
    plan = [(True, False)] * 8 + [(False, False)] * 8
    return _proj(mem2d, mem_g, w_kv, cs_kv, plan, X_HEAD_DIM, tm=256)


def kernel(x, mem, ffn1_norm, ffn1_w_gu, ffn1_w_down, mix_norm, ev_w_in, ev_q_gain, ev_k_gain, ev_sinks, ev_w_out, od_w_in, od_q_gain, od_k_gain, od_w_out, xa_norm, xa_mem_norm, xa_w_q, xa_w_kv, xa_q_gain, xa_k_gain, xa_w_o, ffn2_norm, ffn2_w_gu, ffn2_w_down):
    x = x.reshape(N_TOK, D_MODEL)
    mem2d = mem.reshape(BATCH * MEM_LEN, D_MODEL)
    w_gu, w_down = _cast_now((ffn1_w_gu, 0)), _cast_now((ffn1_w_down, 0))
    for layer in range(DEPTH):
        j = layer // 2
        even = layer % 2 == 0
        w_in3, w_mix3 = (ev_w_in, ev_w_out) if even else (od_w_in, od_w_out)
        jobs = [(w_in3, j), (w_mix3, j), (xa_w_q, layer), (xa_w_kv, layer), (xa_w_o, layer),
                (ffn2_w_gu, layer), (ffn2_w_down, layer)]
        x, (w_in, w_mix, w_q, w_kv, w_o, w_gu, w_down) = _ffn(x, ffn1_norm[layer], w_gu, w_down, jobs)
        kv = _memory_kv(mem2d, xa_mem_norm[layer], w_kv, xa_k_gain[layer])
        cs_q = jnp.tile(xa_q_gain[layer], X_HEADS) * (X_HEAD_DIM ** -0.5 * LOG2_E)
        if even:
            p = _even_projection(x, mix_norm[layer], w_in, ev_q_gain[j], ev_k_gain[j])
            x = _even_tail(x, p, _alibi_log2(A_Q_HEADS), ev_sinks[j].astype(F32) * LOG2_E, w_mix,
                           xa_norm[layer], w_q, cs_q, kv, w_o)
        else:
            heads = _odd_mixer_heads(x, mix_norm[layer], w_in, od_q_gain[j], od_k_gain[j])
            x = _mix_xattn(x, heads, w_mix, xa_norm[layer], w_q, cs_q, kv, w_o)
        jobs = [(ffn1_w_gu, layer + 1), (ffn1_w_down, layer + 1)] if layer + 1 < DEPTH else []
        x, next_ffn1 = _ffn(x, ffn2_norm[layer], w_gu, w_down, jobs)
        if next_ffn1:
            w_gu, w_down = next_ffn1
    return x.reshape(BATCH, SEQ, D_MODEL)
```

```python
import functools

import numpy as np
import jax
import jax.numpy as jnp
from jax import lax
from jax.experimental import pallas as pl
from jax.experimental.pallas import tpu as pltpu

D_MODEL = 1024
BATCH = 4
SEQ = 4096
N_TOK = BATCH * SEQ
DEPTH = 2
HEAD_DIM = 64
BLOCK = 128
A_Q_HEADS = 8
A_KV_HEADS = 2
A_WINDOW = 128
B_HEADS = 8
C_HEADS = 16
C_PATTERNS = ((128, 1), (512, 4), (2048, 16))
MEM_LEN = 256
X_HEADS = 4
X_HEAD_DIM = D_MODEL // X_HEADS
D_FF = 2816
RMS_EPS = 1e-6

LANES = 128
MXU_N = 256
VMEM_LIMIT = 56 * 1024 * 1024

BF = jnp.bfloat16
F32 = jnp.float32
NT_DIMS = (((1,), (1,)), ((), ()))
LOG2_E = 1.4426950408889634


def _params(sem, vmem=None):
    return pltpu.CompilerParams(dimension_semantics=sem, vmem_limit_bytes=vmem)


def _resident(shape):
    nd = len(shape)
    return pl.BlockSpec(shape, lambda *_: (0,) * nd, pipeline_mode=pl.Buffered(1))


BF16_SUBLANES = 16


def _cast_specs(job, steps):
    w3, layer = job
    _, r, c = w3.shape
    rb = next(rb for rb in range(BF16_SUBLANES, r + 1, BF16_SUBLANES) if r % rb == 0 and r // rb <= steps)
    last = r // rb - 1
    return (pl.BlockSpec((None, rb, c), lambda i: (layer, jnp.minimum(i, last), 0)),
            pl.BlockSpec((rb, c), lambda i: (jnp.minimum(i, last), 0)),
            jax.ShapeDtypeStruct((r, c), BF))


def _run_cast_jobs(in_refs, out_refs):
    for src, dst in zip(in_refs, out_refs):
        dst[...] = src[...].astype(BF)


def _cast_kernel(w_ref, o_ref):
    _run_cast_jobs([w_ref], [o_ref])


def _cast_now(job, *, rows=128):
    steps = job[0].shape[1] // rows
    in_spec, out_spec, out_shape = _cast_specs(job, steps)
    return pl.pallas_call(
        _cast_kernel, grid=(steps,), in_specs=[in_spec], out_specs=out_spec, out_shape=out_shape,
        compiler_params=_params(("arbitrary",)),
        name="cast",
    )(job[0])


def _rms(xv, g):
    ms = jnp.mean(xv * xv, axis=-1, keepdims=True)
    return xv * lax.rsqrt(ms + RMS_EPS) * g


FFN_SPLIT = (D_FF // MXU_N + 1) // 2 * MXU_N
FFN_CHUNKS = ((0, FFN_SPLIT), (FFN_SPLIT, D_FF))


def _ffn_kernel(*refs, n_jobs):
    x_ref, g_ref, wgu_ref, wd_ref = refs[:4]
    o_ref = refs[4 + n_jobs]
    xv = x_ref[...]
    h = _rms(xv, g_ref[...]).astype(BF)
    acc = jnp.zeros_like(xv)
    for c0, c1 in FFN_CHUNKS:
        gate = jnp.dot(h, wgu_ref[:, c0:c1], preferred_element_type=F32)
        up = jnp.dot(h, wgu_ref[:, D_FF + c0:D_FF + c1], preferred_element_type=F32)
        act = (gate * jax.nn.sigmoid(gate) * up).astype(BF)
        acc = acc + jnp.dot(act, wd_ref[c0:c1, :], preferred_element_type=F32)
    o_ref[...] = xv + 0.5 * acc
    _run_cast_jobs(refs[4:4 + n_jobs], refs[5 + n_jobs:])


def _ffn(x, g, w_gu, w_down, cast_jobs=(), *, tm=512):
    steps = N_TOK // tm
    specs = [_cast_specs(job, steps) for job in cast_jobs]
    out = pl.pallas_call(
        functools.partial(_ffn_kernel, n_jobs=len(cast_jobs)),
        grid=(steps,),
        in_specs=[pl.BlockSpec((tm, D_MODEL), lambda i: (i, 0)),
                  _resident((1, D_MODEL)),
                  _resident(w_gu.shape),
                  _resident(w_down.shape)] + [s[0] for s in specs],
        out_specs=[pl.BlockSpec((tm, D_MODEL), lambda i: (i, 0))] + [s[1] for s in specs],
        out_shape=[jax.ShapeDtypeStruct((N_TOK, D_MODEL), F32)] + [s[2] for s in specs],
        compiler_params=_params(("arbitrary",), VMEM_LIMIT),
        name="ffn",
    )(x, g.reshape(1, D_MODEL), w_gu, w_down, *[job[0] for job in cast_jobs])
    return out[0], out[1:]


def _proj_kernel(x_ref, g_ref, w_ref, cs_ref, o_ref, *, plan, gs):
    assert gs in (HEAD_DIM, MXU_N)
    h = _rms(x_ref[...], g_ref[...]).astype(BF)
    n_chunks = len(plan) // 2
    lo = lax.broadcasted_iota(jnp.int32, (x_ref.shape[0], LANES), 1) < HEAD_DIM

    def main(j):
        return jnp.dot(h, w_ref[:, MXU_N * j:MXU_N * (j + 1)], preferred_element_type=F32)

    acc_next = main(0)
    out = 0
    for j in range(n_chunks):
        cols = slice(MXU_N * j, MXU_N * (j + 1))
        acc = acc_next
        if j + 1 < n_chunks:
            acc_next = main(j + 1)
        y = acc * cs_ref[:, cols]
        halves = plan[2 * j:2 * j + 2]
        if gs == MXU_N and any(normed for normed, _ in halves):
            inv_chunk = lax.rsqrt(jnp.mean(acc * acc, axis=1, keepdims=True) + RMS_EPS)
        for half, (normed, dup) in enumerate(halves):
            lanes = slice(LANES * half, LANES * (half + 1))
            yh = y[:, lanes]
            if normed and gs == MXU_N:
                yh = yh * inv_chunk
            elif normed:
                sq = acc[:, lanes] * acc[:, lanes]
                s_lo = jnp.sum(jnp.where(lo, sq, 0.0), axis=1, keepdims=True)
                s_hi = jnp.sum(jnp.where(lo, 0.0, sq), axis=1, keepdims=True)
                yh = yh * lax.rsqrt(jnp.where(lo, s_lo, s_hi) * (1.0 / gs) + RMS_EPS)
            if dup:
                swapped = pltpu.roll(yh, HEAD_DIM, axis=1)
                o_ref[out] = jnp.where(lo, yh, swapped).astype(BF)
                o_ref[out + 1] = jnp.where(lo, swapped, yh).astype(BF)
                out += 2
            else:
                o_ref[out] = yh.astype(BF)
                out += 1


def _proj(x, g, w, colscale, plan, gs, *, tm=1024):
    rows = x.shape[0]
    wout = w.shape[1]
    assert wout == LANES * len(plan) and len(plan) % 2 == 0
    c = sum(2 if dup else 1 for _, dup in plan)
    return pl.pallas_call(
        functools.partial(_proj_kernel, plan=tuple(plan), gs=gs),
        grid=(rows // tm,),
        in_specs=[pl.BlockSpec((tm, D_MODEL), lambda i: (i, 0)),
                  _resident((1, D_MODEL)),
                  _resident(w.shape),
                  _resident((1, wout))],
        out_specs=pl.BlockSpec((c, tm, LANES), lambda i: (0, i, 0)),
        out_shape=jax.ShapeDtypeStruct((c, rows, LANES), BF),
        compiler_params=_params(("parallel",), VMEM_LIMIT),
        name="proj",
    )(x, g.reshape(1, D_MODEL), w, colscale.reshape(1, wout).astype(F32))


def _swa_block(q_blocks, kp, kc, vp, vc, has_prev, slopes_ref, sinks_ref, *, kv_div, max_dist):
    row = lax.broadcasted_iota(jnp.int32, (BLOCK, 2 * BLOCK), 0)
    col = lax.broadcasted_iota(jnp.int32, (BLOCK, 2 * BLOCK), 1)
    dist = row + BLOCK - col
    valid = (dist >= 0) & (dist <= max_dist)
    if has_prev is not True:
        valid = valid & ((col >= BLOCK) | has_prev)
    negmask = jnp.where(valid, 0.0, -jnp.inf)
    distf = dist.astype(F32)
    lo = lax.broadcasted_iota(jnp.int32, (BLOCK, LANES), 1) < HEAD_DIM

    n_groups = len(q_blocks) // kv_div
    heads_per_group = 2 * kv_div
    scores = []
    for g in range(n_groups):
        parts = []
        for p in range(g * kv_div, (g + 1) * kv_div):
            q2 = q_blocks[p].astype(F32)
            parts += [jnp.where(lo, q2, 0.0), jnp.where(lo, 0.0, q2)]
        q_stack = jnp.concatenate(parts, axis=0).astype(BF)
        scores.append(jnp.concatenate(
            [lax.dot_general(q_stack, kp[g], NT_DIMS, preferred_element_type=F32),
             lax.dot_general(q_stack, kc[g], NT_DIMS, preferred_element_type=F32)], axis=1))
    soft = []
    for g in range(n_groups):
        res = []
        for j in range(heads_per_group):
            h = g * heads_per_group + j
            s = scores[g][j * BLOCK:(j + 1) * BLOCK] - slopes_ref[h] * distf + negmask
            m = jnp.maximum(jnp.max(s, axis=1, keepdims=True), sinks_ref[h])
            pe = jnp.exp2(s - m)
            res.append((pe.astype(BF), jnp.sum(pe, axis=1, keepdims=True) + jnp.exp2(sinks_ref[h] - m)))
        soft.append(res)
    outs = []
    for g in range(n_groups):
        pb = jnp.concatenate([r[0] for r in soft[g]], axis=0)
        pv = (jnp.dot(pb[:, :BLOCK], vp[g], preferred_element_type=F32)
              + jnp.dot(pb[:, BLOCK:], vc[g], preferred_element_type=F32))
        for jp in range(kv_div):
            o0 = pv[(2 * jp) * BLOCK:(2 * jp + 1) * BLOCK] / soft[g][2 * jp][1]
            o1 = pv[(2 * jp + 1) * BLOCK:(2 * jp + 2) * BLOCK] / soft[g][2 * jp + 1][1]
            outs.append(jnp.where(lo, o0, o1))
    return outs


DIL_ORDER = tuple(sorted(C_PATTERNS, key=lambda wd: -wd[1]))
DIL_UNROLL = 8
DIL_AHEAD = 2
DIL_BASE = 4
DIL_Q = SEQ // DIL_BASE
DIL_CONVERT_ROWS = DIL_BASE * BLOCK


def _dilated_kernel(slopes_ref, q_ref, k_ref, v_ref, o_ref, qn_s, tq_s, tk_s, tv_s, q0_s, q1_s, k_s, v_s,
                    acc_r, m_r, l_r, acc_n, m_n, l_n):
    assert all(d == 1 or d % DIL_BASE == 0 for _, d in DIL_ORDER) and DIL_ORDER[-1][1] == 1
    p = pl.program_id(1)
    lo = lax.broadcasted_iota(jnp.int32, (BLOCK, LANES), 1) < HEAD_DIM

    def convert(c, carry):
        rows = pl.ds(pl.multiple_of(c * DIL_CONVERT_ROWS, DIL_CONVERT_ROWS), DIL_CONVERT_ROWS)
        q_nat = q_ref[0, rows, :].astype(F32)
        lo_c = lax.broadcasted_iota(jnp.int32, (DIL_CONVERT_ROWS, LANES), 1) < HEAD_DIM
        qn_s[0, rows, :] = jnp.where(lo_c, q_nat, 0.0).astype(BF)
        qn_s[1, rows, :] = jnp.where(lo_c, 0.0, q_nat).astype(BF)
        tq_s[...] = q_nat
        tk_s[...] = k_ref[0, rows, :].astype(F32)
        tv_s[...] = v_ref[0, rows, :].astype(F32)
        for rho in range(DIL_BASE):
            src = pl.ds(rho, BLOCK, stride=DIL_BASE)
            dst = pl.ds(pl.multiple_of(rho * DIL_Q + c * BLOCK, BLOCK), BLOCK)
            q = tq_s[src, :]
            q0_s[dst, :] = jnp.where(lo, q, 0.0)
            q1_s[dst, :] = jnp.where(lo, 0.0, q)
            k_s[dst, :] = tk_s[src, :]
            v_s[dst, :] = tv_s[src, :]
        return carry

    lax.fori_loop(0, SEQ // DIL_CONVERT_ROWS, convert, 0)

    row = lax.broadcasted_iota(jnp.int32, (BLOCK, 2 * BLOCK), 0)
    col = lax.broadcasted_iota(jnp.int32, (BLOCK, 2 * BLOCK), 1)
    dist = row + BLOCK - col
    distf = dist.astype(F32)
    no_prev = jnp.where(col < BLOCK, -jnp.inf, 0.0)

    def bcast2(a0, a1):
        return jnp.where(lo, jnp.broadcast_to(a0, (BLOCK, LANES)), jnp.broadcast_to(a1, (BLOCK, LANES)))

    for pi, (window, d) in enumerate(DIL_ORDER):
        first, last = pi == 0, pi == len(DIL_ORDER) - 1
        natural = d == 1
        nb = SEQ // d // BLOCK
        band = (dist >= 0) & (dist <= window // d)
        bias = [jnp.where(band, (-float(d) * slopes_ref[2 * p + hh]) * distf, -jnp.inf) for hh in range(2)]
        acc_s, m_s, l_s = (acc_n, m_n, l_n) if natural else (acc_r, m_r, l_r)

        if natural and not first:
            for rho in range(DIL_BASE):
                src, dst = pl.ds(rho * DIL_Q, DIL_Q), pl.ds(rho, DIL_Q, stride=DIL_BASE)
                acc_n[dst, :] = acc_r[src, :]
                m_n[dst, :] = m_r[src, :]
                l_n[dst, :] = l_r[src, :]

        def rows_of(r, n, n_blocks=1, d=d, natural=natural):
            size = n_blocks * BLOCK
            if natural:
                return pl.ds(pl.multiple_of(BLOCK * n, BLOCK), size)
            inner = d // DIL_BASE
            start = (r % DIL_BASE) * DIL_Q + r // DIL_BASE + inner * BLOCK * n
            return pl.ds(start, size, stride=inner) if inner > 1 else pl.ds(pl.multiple_of(start, BLOCK), size)

        def step(it, carry, nb=nb, bias=bias, first=first, last=last, natural=natural, rows_of=rows_of,
                 acc_s=acc_s, m_s=m_s, l_s=l_s):
            assert DIL_UNROLL % nb == 0 or nb % DIL_UNROLL == 0
            load_k = (lambda rr: k_ref[0, rr, :]) if natural else (lambda rr: k_s[rr, :].astype(BF))
            load_v = (lambda rr: v_ref[0, rr, :]) if natural else (lambda rr: v_s[rr, :].astype(BF))

            def scores(u):
                t = it * DIL_UNROLL + u
                r, n = t // nb, t % nb
                prev = (u % nb != 0) if nb <= DIL_UNROLL else (True if u else None)
                rows = rows_of(r, n)
                if prev is True:
                    kv_rows = [rows_of(r, n - 1, 2)]
                elif prev is None:
                    kv_rows = [rows_of(r, jnp.maximum(n - 1, 0)), rows]
                else:
                    kv_rows = [rows]
                if natural:
                    qh = jnp.concatenate([qn_s[0, rows, :], qn_s[1, rows, :]], axis=0)
                else:
                    qh = jnp.concatenate([q0_s[rows, :], q1_s[rows, :]], axis=0).astype(BF)
                s = jnp.concatenate([lax.dot_general(qh, load_k(rr), NT_DIMS, preferred_element_type=F32)
                                     for rr in kv_rows], axis=1)
                return n, prev, rows, kv_rows, (s[:BLOCK], s[BLOCK:])

            def softmax_pv(blk):
                n, prev, rows, kv_rows, s = blk
                ms, ls, pes = [], [], []
                for hh in range(2):
                    sh = s[hh] + (bias[hh][:, BLOCK:] if prev is False else bias[hh])
                    if prev is None:
                        sh = sh + jnp.where(n == 0, no_prev, 0.0)
                    m = jnp.max(sh, axis=1, keepdims=True)
                    pe = jnp.exp2(sh - m)
                    ms.append(m)
                    ls.append(jnp.sum(pe, axis=1, keepdims=True))
                    pes.append(pe.astype(BF))
                pb = jnp.concatenate(pes, axis=0)
                v = jnp.concatenate([load_v(rr) for rr in kv_rows], axis=0) if len(kv_rows) > 1 else load_v(kv_rows[0])
                pv = jnp.dot(pb, v, preferred_element_type=F32)
                return rows, ms, ls, (pv[:BLOCK], pv[BLOCK:])

            def merge(rows, ms, ls, pv):
                m2 = bcast2(ms[0], ms[1])
                l2 = bcast2(ls[0], ls[1])
                acc2 = jnp.where(lo, pv[0], pv[1])
                if not first:
                    m_old = m_s[rows, :]
                    m_new = jnp.maximum(m_old, m2)
                    a_old, a_new = jnp.exp2(m_old - m_new), jnp.exp2(m2 - m_new)
                    l2 = a_old * l_s[rows, :] + a_new * l2
                    acc2 = a_old * acc_s[rows, :] + a_new * acc2
                    m2 = m_new
                if last:
                    o_ref[0, rows, :] = (acc2 / l2).astype(BF)
                else:
                    m_s[rows, :] = m2
                    l_s[rows, :] = l2
                    acc_s[rows, :] = acc2

            pending = {u: scores(u) for u in range(DIL_AHEAD)}
            done = None
            for u in range(DIL_UNROLL):
                if u + DIL_AHEAD < DIL_UNROLL:
                    pending[u + DIL_AHEAD] = scores(u + DIL_AHEAD)
                cur = softmax_pv(pending.pop(u))
                if done is not None:
                    merge(*done)
                done = cur
            merge(*done)
            return carry

        lax.fori_loop(0, SEQ // BLOCK // DIL_UNROLL, step, 0)


def _dilated(qkv, slopes):
    n_pairs = C_HEADS // 2
    seq_f32 = pltpu.VMEM((SEQ, LANES), F32)
    chunk_f32 = pltpu.VMEM((DIL_CONVERT_ROWS, LANES), F32)
    return pl.pallas_call(
        _dilated_kernel,
        grid=(BATCH, n_pairs),
        in_specs=[pl.BlockSpec(memory_space=pltpu.SMEM),
                  pl.BlockSpec((1, SEQ, LANES), lambda b, p: (p, b, 0)),
                  pl.BlockSpec((1, SEQ, LANES), lambda b, p: (n_pairs + p, b, 0)),
                  pl.BlockSpec((1, SEQ, LANES), lambda b, p: (2 * n_pairs + p, b, 0))],
        out_specs=pl.BlockSpec((1, SEQ, LANES), lambda b, p: (p, b, 0)),
        out_shape=jax.ShapeDtypeStruct((n_pairs, N_TOK, LANES), BF),
        scratch_shapes=[pltpu.VMEM((2, SEQ, LANES), BF)] + [chunk_f32] * 3 + [seq_f32] * 10,
        compiler_params=_params(("parallel", "parallel"), VMEM_LIMIT),
        name="dilated",
    )(slopes, qkv, qkv, qkv)


SB_QB = MXU_N
SB_FIRST_TILES = 2
SB_DEAD_LOG2 = -150.0


def _sb_matrix():
    idx = np.arange(SB_QB)
    return jnp.asarray(-(idx[:, None] > idx[None, :]).astype(np.float32), dtype=BF)


def _sb_unit(q2, load_k, load_v, iq, uo, fill=lambda: None):
    lo = lax.broadcasted_iota(jnp.int32, (SB_QB, LANES), 1) < HEAD_DIM
    q_stack = jnp.concatenate([jnp.where(lo, q2, 0.0), jnp.where(lo, 0.0, q2)], axis=0).astype(BF)
    rel1 = (lax.broadcasted_iota(jnp.int32, (SB_QB, SB_QB), 1)
            - lax.broadcasted_iota(jnp.int32, (SB_QB, SB_QB), 0))
    rel = jnp.concatenate([rel1, rel1], axis=0)

    def scores(first, n_tiles):
        return lax.dot_general(q_stack, load_k(first, n_tiles), NT_DIMS,
                               preferred_element_type=F32)

    def walk(first, n_tiles, carry, masked):
        z, c, o = carry
        z_next = scores(jnp.maximum(first - 1, 0), 1)
        if masked:
            fill()
        order = list(reversed(range(n_tiles)))
        sps, es, stricts, totals = [], {}, {}, {}
        for t in order:
            zt = z[:, t * SB_QB:(t + 1) * SB_QB]
            sp = jnp.maximum(zt, 0.0) + jnp.log2(1.0 + jnp.exp2(-jnp.abs(zt)))
            es[t] = zt - sp
            if masked:
                stricts[t] = rel < (iq - first - t) * SB_QB
                sp = jnp.where(stricts[t], sp, 0.0)
            sps.append(sp.astype(BF))
            totals[t] = jnp.sum(sp, axis=1, keepdims=True)
        w_all = jnp.dot(jnp.concatenate(sps, axis=0), uo, preferred_element_type=F32)
        if masked:
            fill()
        parts = {}
        for idx, t in enumerate(order):
            w = w_all[idx * 2 * SB_QB:(idx + 1) * 2 * SB_QB]
            a = jnp.exp2(es[t] + jnp.concatenate([c] * (SB_QB // LANES), axis=1) + w)
            if masked:
                a = jnp.where(stricts[t], a, 0.0)
            parts[t] = a.astype(BF)
            c = c - totals[t]
        pv = jnp.dot(jnp.concatenate([parts[t] for t in range(n_tiles)], axis=1), load_v(first, n_tiles),
                     preferred_element_type=F32)
        if masked:
            fill()
        return z_next, c, o + jnp.where(lo, pv[:SB_QB], pv[SB_QB:])

    def alive(c):
        return jnp.max(c) > SB_DEAD_LOG2

    def body(state):
        g = state[0]
        z, c, o = walk(g, 1, state[2:], False)
        return g - 1, alive(c), z, c, o

    first = jnp.maximum(iq - 1, 0)
    zeros = (jnp.zeros((2 * SB_QB, LANES), F32), jnp.zeros((SB_QB, LANES), F32))
    z, c, o = walk(first, SB_FIRST_TILES, (scores(first, SB_FIRST_TILES),) + zeros, True)
    state = lax.while_loop(lambda st: (st[0] >= 0) & st[1], body, (first - 1, alive(c), z, c, o))
    return state[4]


def _mix_xattn_kernel(*refs):
    x_ref = refs[0]
    wm_ref, g_ref, wq_ref, cs_ref, kv_ref, wo_ref, o_ref = refs[-7:]
    mixed = jnp.concatenate([r[c] for r in refs[1:-7] for c in range(r.shape[0])], axis=1)
    xv = x_ref[...] + jnp.dot(mixed, wm_ref[...], preferred_element_type=F32)
    h = _rms(xv, g_ref[...]).astype(BF)
    heads = range(X_HEADS)
    cols = [slice(X_HEAD_DIM * hd, X_HEAD_DIM * (hd + 1)) for hd in heads]
    acc = [jnp.dot(h, wq_ref[:, cols[hd]], preferred_element_type=F32) for hd in heads]
    ms = [jnp.mean(acc[hd] * acc[hd], axis=1, keepdims=True) for hd in heads]
    q = [(acc[hd] * cs_ref[:, cols[hd]] * lax.rsqrt(ms[hd] + RMS_EPS)).astype(BF) for hd in heads]
    s = [lax.dot_general(q[hd], jnp.concatenate([kv_ref[2 * hd], kv_ref[2 * hd + 1]], axis=1), NT_DIMS,
                         preferred_element_type=F32) for hd in heads]
    pe, l = [], []
    for hd in heads:
        e = jnp.exp2(s[hd] - jnp.max(s[hd], axis=1, keepdims=True))
        l.append(jnp.sum(e, axis=1, keepdims=True))
        pe.append(e.astype(BF))
    v0 = 2 * X_HEADS
    pv = [jnp.dot(pe[hd], jnp.concatenate([kv_ref[v0 + 2 * hd], kv_ref[v0 + 2 * hd + 1]], axis=1),
                  preferred_element_type=F32) for hd in heads]
    o = jnp.concatenate([(pv[hd] / l[hd]).astype(BF) for hd in heads], axis=1)
    o_ref[...] = xv + jnp.dot(o, wo_ref[...], preferred_element_type=F32)


def _mix_xattn(x, mixer_heads, w_mix, g, w_q, q_colscale, kv, w_o, *, tm=1024):
    tiles_per_batch = SEQ // tm
    in_specs = [pl.BlockSpec((tm, D_MODEL), lambda i: (i, 0))]
    in_specs += [pl.BlockSpec((mh.shape[0], tm, LANES), lambda i: (0, i, 0)) for mh in mixer_heads]
    in_specs += [_resident(w_mix.shape),
                 _resident((1, D_MODEL)),
                 _resident(w_q.shape),
                 _resident((1, D_MODEL)),
                 pl.BlockSpec((4 * X_HEADS, MEM_LEN, LANES), lambda i: (0, i // tiles_per_batch, 0)),
                 _resident(w_o.shape)]
    return pl.pallas_call(
        _mix_xattn_kernel, grid=(N_TOK // tm,),
        in_specs=in_specs,
        out_specs=pl.BlockSpec((tm, D_MODEL), lambda i: (i, 0)),
        out_shape=jax.ShapeDtypeStruct((N_TOK, D_MODEL), F32),
        compiler_params=_params(("parallel",), VMEM_LIMIT),
        name="mix_xattn",
    )(x, *mixer_heads, w_mix, g.reshape(1, D_MODEL), w_q,
      q_colscale.reshape(1, D_MODEL).astype(F32), kv, w_o)


EVEN_TILE = 512


def _even_tail_kernel(slopes_ref, sinks_ref, x_ref, swq_ref, swkp_ref, swkc_ref, swvp_ref, swvc_ref,
                      sbq_ref, sbk_ref, sbv_ref, uo_ref, wm_ref, g_ref, wq_ref, cs_ref, kv_ref, wo_ref,
                      o_ref, heads_s, *, kv_div, max_dist):
    i = pl.program_id(0)
    tiles_per_seq = SEQ // EVEN_TILE
    t_in_seq = jnp.minimum(i, N_TOK // EVEN_TILE - 1) % tiles_per_seq
    wr, rd = i % 2, (i + 1) % 2
    n_pairs_a, n_pairs_b = swq_ref.shape[0], sbq_ref.shape[0]

    @pl.when(i == 0)
    def _():
        heads_s[1] = jnp.zeros(heads_s.shape[1:], BF)

    heads = range(X_HEADS)
    cols = [slice(X_HEAD_DIM * hd, X_HEAD_DIM * (hd + 1)) for hd in heads]
    st = {"xv": [None] * X_HEADS, "q": [None] * X_HEADS, "pe": [None] * X_HEADS, "l": [None] * X_HEADS,
          "o": [None] * X_HEADS}

    def project(c):
        if c == 0:
            st["mixed"] = jnp.concatenate([heads_s[rd, k] for k in range(n_pairs_a + n_pairs_b)], axis=1)
        st["xv"][c] = x_ref[:, cols[c]] + jnp.dot(st["mixed"], wm_ref[:, cols[c]], preferred_element_type=F32)

    def q_proj(hd):
        if hd == 0:
            xv = jnp.concatenate(st["xv"], axis=1)
            st["h"] = _rms(xv, g_ref[...]).astype(BF)
        acc = jnp.dot(st["h"], wq_ref[:, cols[hd]], preferred_element_type=F32)
        ms = jnp.mean(acc * acc, axis=1, keepdims=True)
        st["q"][hd] = (acc * cs_ref[:, cols[hd]] * lax.rsqrt(ms + RMS_EPS)).astype(BF)

    def scores(hd):
        kh = jnp.concatenate([kv_ref[2 * hd], kv_ref[2 * hd + 1]], axis=1)
        s = lax.dot_general(st["q"][hd], kh, NT_DIMS, preferred_element_type=F32)
        e = jnp.exp2(s - jnp.max(s, axis=1, keepdims=True))
        st["l"][hd] = jnp.sum(e, axis=1, keepdims=True)
        st["pe"][hd] = e.astype(BF)

    def values(hd):
        v0 = 2 * X_HEADS
        vh = jnp.concatenate([kv_ref[v0 + 2 * hd], kv_ref[v0 + 2 * hd + 1]], axis=1)
        st["o"][hd] = (jnp.dot(st["pe"][hd], vh, preferred_element_type=F32) / st["l"][hd]).astype(BF)

    def out_proj(c):
        if c == 0:
            st["oc"] = jnp.concatenate(st["o"], axis=1)
        o_ref[:, cols[c]] = st["xv"][c] + jnp.dot(st["oc"], wo_ref[:, cols[c]], preferred_element_type=F32)

    chain = [functools.partial(f, k) for f in (project, q_proj, scores, values, out_proj) for k in heads]

    def fill():
        if chain:
            chain.pop(0)()

    uo = uo_ref[...]
    blocks_per_tile = EVEN_TILE // SB_QB

    def sb_unit(p, j):
        rows = slice(SB_QB * j, SB_QB * (j + 1))
        load = lambda ref: (lambda first, n: ref[p, pl.ds(pl.multiple_of(first * SB_QB, SB_QB), n * SB_QB), :])
        o = _sb_unit(sbq_ref[p, rows, :].astype(F32), load(sbk_ref), load(sbv_ref),
                     blocks_per_tile * t_in_seq + j, uo, fill)
        heads_s[wr, n_pairs_a + p, rows, :] = o.astype(BF)

    for p in range(n_pairs_b):
        for j in range(blocks_per_tile):
            sb_unit(p, j)
    while chain:
        fill()

    n_kv = swkc_ref.shape[0]
    for m in range(EVEN_TILE // BLOCK):
        rows = slice(BLOCK * m, BLOCK * (m + 1))
        if m == 0:
            kp, vp, has_prev = [swkp_ref[g] for g in range(n_kv)], [swvp_ref[g] for g in range(n_kv)], t_in_seq > 0
        else:
            prows = slice(BLOCK * (m - 1), BLOCK * m)
            kp, vp, has_prev = [swkc_ref[g, prows, :] for g in range(n_kv)], [swvc_ref[g, prows, :] for g in range(n_kv)], True
        outs = _swa_block([swq_ref[p, rows, :] for p in range(n_pairs_a)], kp, [swkc_ref[g, rows, :] for g in range(n_kv)],
                          vp, [swvc_ref[g, rows, :] for g in range(n_kv)], has_prev, slopes_ref, sinks_ref,
                          kv_div=kv_div, max_dist=max_dist)
        for p in range(n_pairs_a):
            heads_s[wr, p, rows, :] = outs[p].astype(BF)


def _even_tail(x, p, slopes, sinks, w_mix, g, w_q, q_colscale, kv, w_o):
    n_pairs_a, n_kv, n_pairs_b = A_Q_HEADS // 2, A_KV_HEADS, B_HEADS // 2
    assert p.shape[0] == n_pairs_a + 2 * n_kv + 3 * n_pairs_b and n_pairs_a == n_pairs_b == 2 * n_kv
    n_tiles = N_TOK // EVEN_TILE
    tiles_per_seq = SEQ // EVEN_TILE
    blocks_per_tile = EVEN_TILE // BLOCK
    att = lambda i: jnp.minimum(i, n_tiles - 1)
    mix = lambda i: jnp.maximum(i - 1, 0)
    tile_rows = lambda size, idx: pl.BlockSpec((size, EVEN_TILE, LANES), lambda i: (idx, att(i), 0))
    prev_block = lambda idx: pl.BlockSpec(
        (n_kv, BLOCK, LANES), lambda i: (idx, jnp.maximum(blocks_per_tile * att(i) - 1, 0), 0))
    whole_seq = lambda idx: pl.BlockSpec((n_pairs_b, SEQ, LANES), lambda i: (idx, att(i) // tiles_per_seq, 0),
                                         pipeline_mode=pl.Buffered(1))
    smem = pl.BlockSpec(memory_space=pltpu.SMEM)
    return pl.pallas_call(
        functools.partial(_even_tail_kernel, kv_div=n_pairs_a // n_kv, max_dist=A_WINDOW - 1),
        grid=(n_tiles + 1,),
        in_specs=[smem, smem,
                  pl.BlockSpec((EVEN_TILE, D_MODEL), lambda i: (mix(i), 0)),
                  tile_rows(n_pairs_a, 0), prev_block(2), tile_rows(n_kv, 2), prev_block(3), tile_rows(n_kv, 3),
                  tile_rows(n_pairs_b, 2), whole_seq(3), whole_seq(4),
                  _resident((SB_QB, SB_QB)),
                  _resident(w_mix.shape), _resident((1, D_MODEL)), _resident(w_q.shape), _resident((1, D_MODEL)),
                  pl.BlockSpec((4 * X_HEADS, MEM_LEN, LANES), lambda i: (0, mix(i) // tiles_per_seq, 0)),
                  _resident(w_o.shape)],
        out_specs=pl.BlockSpec((EVEN_TILE, D_MODEL), lambda i: (mix(i), 0)),
        out_shape=jax.ShapeDtypeStruct((N_TOK, D_MODEL), F32),
        scratch_shapes=[pltpu.VMEM((2, n_pairs_a + n_pairs_b, EVEN_TILE, LANES), BF)],
        compiler_params=_params(("arbitrary",), VMEM_LIMIT),
        name="even_tail",
    )(slopes, sinks, x, p, p, p, p, p, p, p, p, _sb_matrix(), w_mix, g.reshape(1, D_MODEL), w_q,
      q_colscale.reshape(1, D_MODEL).astype(F32), kv, w_o)


def _alibi_log2(n_heads):
    return jnp.asarray(LOG2_E * 2.0 ** (-8.0 * np.arange(1, n_heads + 1) / n_heads), dtype=F32)


def _even_projection(x, norm_g, w_in, q_gain, k_gain):
    hd = HEAD_DIM
    scale = hd ** -0.5 * LOG2_E
    ones = lambda n: jnp.ones((n,), F32)
    cs = jnp.concatenate([jnp.tile(q_gain, A_Q_HEADS) * scale, jnp.tile(k_gain, A_KV_HEADS), ones(128),
                          ones(512) * scale, ones(1024)])
    plan = [(True, False)] * 4 + [(True, True), (False, True)] + [(False, False)] * 12
    return _proj(x, norm_g, w_in, cs, plan, hd)


def _odd_mixer_heads(x, norm_g, w_in, q_gain, k_gain):
    hd = HEAD_DIM
    cs = jnp.concatenate([jnp.tile(q_gain, C_HEADS) * (hd ** -0.5 * LOG2_E), jnp.tile(k_gain, C_HEADS),
                          jnp.ones((C_HEADS * hd,), F32)])
    plan = [(True, False)] * 16 + [(False, False)] * 8
    p = _proj(x, norm_g, w_in, cs, plan, hd)
    return [_dilated(p, _alibi_log2(C_HEADS))]


def _memory_kv(mem2d, mem_g, w_kv, k_gain):
    cs_kv = jnp.concatenate([jnp.tile(k_gain, X_HEADS), jnp.ones((D_MODEL,), F32)])
    plan = [(True, False)] * 8 + [(False, False)] * 8
    return _proj(mem2d, mem_g, w_kv, cs_kv, plan, X_HEAD_DIM, tm=256)


def kernel(x, mem, ffn1_norm, ffn1_w_gu, ffn1_w_down, mix_norm, ev_w_in, ev_q_gain, ev_k_gain, ev_sinks, ev_w_out, od_w_in, od_q_gain, od_k_gain, od_w_out, xa_norm, xa_mem_norm, xa_w_q, xa_w_kv, xa_q_gain, xa_k_gain, xa_w_o, ffn2_norm, ffn2_w_gu, ffn2_w_down):
    x = x.reshape(N_TOK, D_MODEL)
    mem2d = mem.reshape(BATCH * MEM_LEN, D_MODEL)
    w_gu, w_down = _cast_now((ffn1_w_gu, 0)), _cast_now((ffn1_w_down, 0))
    for layer in range(DEPTH):
        j = layer // 2
        even = layer % 2 == 0
        w_in3, w_mix3 = (ev_w_in, ev_w_out) if even else (od_w_in, od_w_out)
        jobs = [(w_in3, j), (w_mix3, j), (xa_w_q, layer), (xa_w_kv, layer), (xa_w_o, layer),
                (ffn2_w_gu, layer), (ffn2_w_down, layer)]
        x, (w_in, w_mix, w_q, w_kv, w_o, w_gu, w_down) = _ffn(x, ffn1_norm[layer], w_gu, w_down, jobs)
        kv = _memory_kv(mem2d, xa_mem_norm[layer], w_kv, xa_k_gain[layer])
        cs_q = jnp.tile(xa_q_gain[layer], X_HEADS) * (X_HEAD_DIM ** -0.5 * LOG2_E)
        if even:
            p = _even_projection(x, mix_norm[layer], w_in, ev_q_gain[j], ev_k_gain[j])
            x = _even_tail(x, p, _alibi_log2(A_Q_HEADS), ev_sinks[j].astype(F32) * LOG2_E, w_mix,
                           xa_norm[layer], w_q, cs_q, kv, w_o)
        else:
            heads = _odd_mixer_heads(x, mix_norm[layer], w_in, od_q_gain[j], od_k_gain[j])
            x = _mix_xattn(x, heads, w_mix, xa_norm[layer], w_q, cs_q, kv, w_o)
        jobs = [(ffn1_w_gu, layer + 1), (ffn1_w_down, layer + 1)] if layer + 1 < DEPTH else []
        x, next_ffn1 = _ffn(x, ffn2_norm[layer], w_gu, w_down, jobs)
        if next_ffn1:
            w_gu, w_down = next_ffn1
    return x.reshape(BATCH, SEQ, D_MODEL)
```

```python
import functools

import numpy as np
import jax
import jax.numpy as jnp
from jax import lax
from jax.experimental import pallas as pl
from jax.experimental.pallas import tpu as pltpu

D_MODEL = 1024
BATCH = 4
SEQ = 4096
N_TOK = BATCH * SEQ
DEPTH = 2
HEAD_DIM = 64
BLOCK = 128
A_Q_HEADS = 8
A_KV_HEADS = 2
A_WINDOW = 128
B_HEADS = 8
C_HEADS = 16
C_PATTERNS = ((128, 1), (512, 4), (2048, 16))
MEM_LEN = 256
X_HEADS = 4
X_HEAD_DIM = D_MODEL // X_HEADS
D_FF = 2816
RMS_EPS = 1e-6

LANES = 128
MXU_N = 256
VMEM_LIMIT = 56 * 1024 * 1024

BF = jnp.bfloat16
F32 = jnp.float32
NT_DIMS = (((1,), (1,)), ((), ()))
LOG2_E = 1.4426950408889634


def _params(sem, vmem=None):
    return pltpu.CompilerParams(dimension_semantics=sem, vmem_limit_bytes=vmem)


def _resident(shape):
    nd = len(shape)
    return pl.BlockSpec(shape, lambda *_: (0,) * nd, pipeline_mode=pl.Buffered(1))


BF16_SUBLANES = 16


def _cast_specs(job, steps):
    w3, layer = job
    _, r, c = w3.shape
    rb = next(rb for rb in range(BF16_SUBLANES, r + 1, BF16_SUBLANES) if r % rb == 0 and r // rb <= steps)
    last = r // rb - 1
    return (pl.BlockSpec((None, rb, c), lambda i: (layer, jnp.minimum(i, last), 0)),
            pl.BlockSpec((rb, c), lambda i: (jnp.minimum(i, last), 0)),
            jax.ShapeDtypeStruct((r, c), BF))


def _run_cast_jobs(in_refs, out_refs):
    for src, dst in zip(in_refs, out_refs):
        dst[...] = src[...].astype(BF)


def _cast_kernel(w_ref, o_ref):
    _run_cast_jobs([w_ref], [o_ref])


def _cast_now(job, *, rows=128):
    steps = job[0].shape[1] // rows
    in_spec, out_spec, out_shape = _cast_specs(job, steps)
    return pl.pallas_call(
        _cast_kernel, grid=(steps,), in_specs=[in_spec], out_specs=out_spec, out_shape=out_shape,
        compiler_params=_params(("arbitrary",)),
        name="cast",
    )(job[0])


def _rms(xv, g):
    ms = jnp.mean(xv * xv, axis=-1, keepdims=True)
    return xv * lax.rsqrt(ms + RMS_EPS) * g


FFN_SPLIT = (D_FF // MXU_N + 1) // 2 * MXU_N
FFN_CHUNKS = ((0, FFN_SPLIT), (FFN_SPLIT, D_FF))


def _ffn_kernel(*refs, n_jobs):
    x_ref, g_ref, wgu_ref, wd_ref = refs[:4]
    o_ref = refs[4 + n_jobs]
    xv = x_ref[...]
    h = _rms(xv, g_ref[...]).astype(BF)
    acc = jnp.zeros_like(xv)
    for c0, c1 in FFN_CHUNKS:
        gate = jnp.dot(h, wgu_ref[:, c0:c1], preferred_element_type=F32)
        up = jnp.dot(h, wgu_ref[:, D_FF + c0:D_FF + c1], preferred_element_type=F32)
        act = (gate * jax.nn.sigmoid(gate) * up).astype(BF)
        acc = acc + jnp.dot(act, wd_ref[c0:c1, :], preferred_element_type=F32)
    o_ref[...] = xv + 0.5 * acc
    _run_cast_jobs(refs[4:4 + n_jobs], refs[5 + n_jobs:])


def _ffn(x, g, w_gu, w_down, cast_jobs=(), *, tm=512):
    steps = N_TOK // tm
    specs = [_cast_specs(job, steps) for job in cast_jobs]
    out = pl.pallas_call(
        functools.partial(_ffn_kernel, n_jobs=len(cast_jobs)),
        grid=(steps,),
        in_specs=[pl.BlockSpec((tm, D_MODEL), lambda i: (i, 0)),
                  _resident((1, D_MODEL)),
                  _resident(w_gu.shape),
                  _resident(w_down.shape)] + [s[0] for s in specs],
        out_specs=[pl.BlockSpec((tm, D_MODEL), lambda i: (i, 0))] + [s[1] for s in specs],
        out_shape=[jax.ShapeDtypeStruct((N_TOK, D_MODEL), F32)] + [s[2] for s in specs],
        compiler_params=_params(("arbitrary",), VMEM_LIMIT),
        name="ffn",
    )(x, g.reshape(1, D_MODEL), w_gu, w_down, *[job[0] for job in cast_jobs])
    return out[0], out[1:]


def _proj_kernel(x_ref, g_ref, w_ref, cs_ref, o_ref, *, plan, gs):
    assert gs in (HEAD_DIM, MXU_N)
    h = _rms(x_ref[...], g_ref[...]).astype(BF)
    n_chunks = len(plan) // 2
    lo = lax.broadcasted_iota(jnp.int32, (x_ref.shape[0], LANES), 1) < HEAD_DIM

    def main(j):
        return jnp.dot(h, w_ref[:, MXU_N * j:MXU_N * (j + 1)], preferred_element_type=F32)

    acc_next = main(0)
    out = 0
    for j in range(n_chunks):
        cols = slice(MXU_N * j, MXU_N * (j + 1))
        acc = acc_next
        if j + 1 < n_chunks:
            acc_next = main(j + 1)
        y = acc * cs_ref[:, cols]
        halves = plan[2 * j:2 * j + 2]
        if gs == MXU_N and any(normed for normed, _ in halves):
            inv_chunk = lax.rsqrt(jnp.mean(acc * acc, axis=1, keepdims=True) + RMS_EPS)
        for half, (normed, dup) in enumerate(halves):
            lanes = slice(LANES * half, LANES * (half + 1))
            yh = y[:, lanes]
            if normed and gs == MXU_N:
                yh = yh * inv_chunk
            elif normed:
                sq = acc[:, lanes] * acc[:, lanes]
                s_lo = jnp.sum(jnp.where(lo, sq, 0.0), axis=1, keepdims=True)
                s_hi = jnp.sum(jnp.where(lo, 0.0, sq), axis=1, keepdims=True)
                yh = yh * lax.rsqrt(jnp.where(lo, s_lo, s_hi) * (1.0 / gs) + RMS_EPS)
            if dup:
                swapped = pltpu.roll(yh, HEAD_DIM, axis=1)
                o_ref[out] = jnp.where(lo, yh, swapped).astype(BF)
                o_ref[out + 1] = jnp.where(lo, swapped, yh).astype(BF)
                out += 2
            else:
                o_ref[out] = yh.astype(BF)
                out += 1


def _proj(x, g, w, colscale, plan, gs, *, tm=1024):
    rows = x.shape[0]
    wout = w.shape[1]
    assert wout == LANES * len(plan) and len(plan) % 2 == 0
    c = sum(2 if dup else 1 for _, dup in plan)
    return pl.pallas_call(
        functools.partial(_proj_kernel, plan=tuple(plan), gs=gs),
        grid=(rows // tm,),
        in_specs=[pl.BlockSpec((tm, D_MODEL), lambda i: (i, 0)),
                  _resident((1, D_MODEL)),
                  _resident(w.shape),
                  _resident((1, wout))],
        out_specs=pl.BlockSpec((c, tm, LANES), lambda i: (0, i, 0)),
        out_shape=jax.ShapeDtypeStruct((c, rows, LANES), BF),
        compiler_params=_params(("parallel",), VMEM_LIMIT),
        name="proj",
    )(x, g.reshape(1, D_MODEL), w, colscale.reshape(1, wout).astype(F32))


def _swa_block(q_blocks, kp, kc, vp, vc, has_prev, slopes_ref, sinks_ref, *, kv_div, max_dist):
    row = lax.broadcasted_iota(jnp.int32, (BLOCK, 2 * BLOCK), 0)
    col = lax.broadcasted_iota(jnp.int32, (BLOCK, 2 * BLOCK), 1)
    dist = row + BLOCK - col
    valid = (dist >= 0) & (dist <= max_dist)
    if has_prev is not True:
        valid = valid & ((col >= BLOCK) | has_prev)
    negmask = jnp.where(valid, 0.0, -jnp.inf)
    distf = dist.astype(F32)
    lo = lax.broadcasted_iota(jnp.int32, (BLOCK, LANES), 1) < HEAD_DIM

    n_groups = len(q_blocks) // kv_div
    heads_per_group = 2 * kv_div
    scores = []
    for g in range(n_groups):
        parts = []
        for p in range(g * kv_div, (g + 1) * kv_div):
            q2 = q_blocks[p].astype(F32)
            parts += [jnp.where(lo, q2, 0.0), jnp.where(lo, 0.0, q2)]
        q_stack = jnp.concatenate(parts, axis=0).astype(BF)
        scores.append(jnp.concatenate(
            [lax.dot_general(q_stack, kp[g], NT_DIMS, preferred_element_type=F32),
             lax.dot_general(q_stack, kc[g], NT_DIMS, preferred_element_type=F32)], axis=1))
    soft = []
    for g in range(n_groups):
        res = []
        for j in range(heads_per_group):
            h = g * heads_per_group + j
            s = scores[g][j * BLOCK:(j + 1) * BLOCK] - slopes_ref[h] * distf + negmask
            m = jnp.maximum(jnp.max(s, axis=1, keepdims=True), sinks_ref[h])
            pe = jnp.exp2(s - m)
            res.append((pe.astype(BF), jnp.sum(pe, axis=1, keepdims=True) + jnp.exp2(sinks_ref[h] - m)))
        soft.append(res)
    outs = []
    for g in range(n_groups):
        pb = jnp.concatenate([r[0] for r in soft[g]], axis=0)
        pv = (jnp.dot(pb[:, :BLOCK], vp[g], preferred_element_type=F32)
              + jnp.dot(pb[:, BLOCK:], vc[g], preferred_element_type=F32))
        for jp in range(kv_div):
            o0 = pv[(2 * jp) * BLOCK:(2 * jp + 1) * BLOCK] / soft[g][2 * jp][1]
            o1 = pv[(2 * jp + 1) * BLOCK:(2 * jp + 2) * BLOCK] / soft[g][2 * jp + 1][1]
            outs.append(jnp.where(lo, o0, o1))
    return outs


DIL_ORDER = tuple(sorted(C_PATTERNS, key=lambda wd: -wd[1]))
DIL_UNROLL = 8
DIL_AHEAD = 2
DIL_BASE = 4
DIL_Q = SEQ // DIL_BASE
DIL_CONVERT_ROWS = DIL_BASE * BLOCK


def _dilated_kernel(slopes_ref, q_ref, k_ref, v_ref, o_ref, qn_s, tq_s, tk_s, tv_s, q0_s, q1_s, k_s, v_s,
                    acc_r, m_r, l_r, acc_n, m_n, l_n):
    assert all(d == 1 or d % DIL_BASE == 0 for _, d in DIL_ORDER) and DIL_ORDER[-1][1] == 1
    p = pl.program_id(1)
    lo = lax.broadcasted_iota(jnp.int32, (BLOCK, LANES), 1) < HEAD_DIM

    def convert(c, carry):
        rows = pl.ds(pl.multiple_of(c * DIL_CONVERT_ROWS, DIL_CONVERT_ROWS), DIL_CONVERT_ROWS)
        q_nat = q_ref[0, rows, :].astype(F32)
        lo_c = lax.broadcasted_iota(jnp.int32, (DIL_CONVERT_ROWS, LANES), 1) < HEAD_DIM
        qn_s[0, rows, :] = jnp.where(lo_c, q_nat, 0.0).astype(BF)
        qn_s[1, rows, :] = jnp.where(lo_c, 0.0, q_nat).astype(BF)
        tq_s[...] = q_nat
        tk_s[...] = k_ref[0, rows, :].astype(F32)
        tv_s[...] = v_ref[0, rows, :].astype(F32)
        for rho in range(DIL_BASE):
            src = pl.ds(rho, BLOCK, stride=DIL_BASE)
            dst = pl.ds(pl.multiple_of(rho * DIL_Q + c * BLOCK, BLOCK), BLOCK)
            q = tq_s[src, :]
            q0_s[dst, :] = jnp.where(lo, q, 0.0)
            q1_s[dst, :] = jnp.where(lo, 0.0, q)
            k_s[dst, :] = tk_s[src, :]
            v_s[dst, :] = tv_s[src, :]
        return carry

    lax.fori_loop(0, SEQ // DIL_CONVERT_ROWS, convert, 0)

    row = lax.broadcasted_iota(jnp.int32, (BLOCK, 2 * BLOCK), 0)
    col = lax.broadcasted_iota(jnp.int32, (BLOCK, 2 * BLOCK), 1)
    dist = row + BLOCK - col
    distf = dist.astype(F32)
    no_prev = jnp.where(col < BLOCK, -jnp.inf, 0.0)

    def bcast2(a0, a1):
        return jnp.where(lo, jnp.broadcast_to(a0, (BLOCK, LANES)), jnp.broadcast_to(a1, (BLOCK, LANES)))

    for pi, (window, d) in enumerate(DIL_ORDER):
        first, last = pi == 0, pi == len(DIL_ORDER) - 1
        natural = d == 1
        nb = SEQ // d // BLOCK
        band = (dist >= 0) & (dist <= window // d)
        bias = [jnp.where(band, (-float(d) * slopes_ref[2 * p + hh]) * distf, -jnp.inf) for hh in range(2)]
        acc_s, m_s, l_s = (acc_n, m_n, l_n) if natural else (acc_r, m_r, l_r)

        if natural and not first:
            for rho in range(DIL_BASE):
                src, dst = pl.ds(rho * DIL_Q, DIL_Q), pl.ds(rho, DIL_Q, stride=DIL_BASE)
                acc_n[dst, :] = acc_r[src, :]
                m_n[dst, :] = m_r[src, :]
                l_n[dst, :] = l_r[src, :]

        def rows_of(r, n, n_blocks=1, d=d, natural=natural):
            size = n_blocks * BLOCK
            if natural:
                return pl.ds(pl.multiple_of(BLOCK * n, BLOCK), size)
            inner = d // DIL_BASE
            start = (r % DIL_BASE) * DIL_Q + r // DIL_BASE + inner * BLOCK * n
            return pl.ds(start, size, stride=inner) if inner > 1 else pl.ds(pl.multiple_of(start, BLOCK), size)

        def step(it, carry, nb=nb, bias=bias, first=first, last=last, natural=natural, rows_of=rows_of,
                 acc_s=acc_s, m_s=m_s, l_s=l_s):
            assert DIL_UNROLL % nb == 0 or nb % DIL_UNROLL == 0
            load_k = (lambda rr: k_ref[0, rr, :]) if natural else (lambda rr: k_s[rr, :].astype(BF))
            load_v = (lambda rr: v_ref[0, rr, :]) if natural else (lambda rr: v_s[rr, :].astype(BF))

            def scores(u):
                t = it * DIL_UNROLL + u
                r, n = t // nb, t % nb
                prev = (u % nb != 0) if nb <= DIL_UNROLL else (True if u else None)
                rows = rows_of(r, n)
                if prev is True:
                    kv_rows = [rows_of(r, n - 1, 2)]
                elif prev is None:
                    kv_rows = [rows_of(r, jnp.maximum(n - 1, 0)), rows]
                else:
                    kv_rows = [rows]
                if natural:
                    qh = jnp.concatenate([qn_s[0, rows, :], qn_s[1, rows, :]], axis=0)
                else:
                    qh = jnp.concatenate([q0_s[rows, :], q1_s[rows, :]], axis=0).astype(BF)
                s = jnp.concatenate([lax.dot_general(qh, load_k(rr), NT_DIMS, preferred_element_type=F32)
                                     for rr in kv_rows], axis=1)
                return n, prev, rows, kv_rows, (s[:BLOCK], s[BLOCK:])

            def softmax_pv(blk):
                n, prev, rows, kv_rows, s = blk
                ms, ls, pes = [], [], []
                for hh in range(2):
                    sh = s[hh] + (bias[hh][:, BLOCK:] if prev is False else bias[hh])
                    if prev is None:
                        sh = sh + jnp.where(n == 0, no_prev, 0.0)
                    m = jnp.max(sh, axis=1, keepdims=True)
                    pe = jnp.exp2(sh - m)
                    ms.append(m)
                    ls.append(jnp.sum(pe, axis=1, keepdims=True))
                    pes.append(pe.astype(BF))
                pb = jnp.concatenate(pes, axis=0)
                v = jnp.concatenate([load_v(rr) for rr in kv_rows], axis=0) if len(kv_rows) > 1 else load_v(kv_rows[0])
                pv = jnp.dot(pb, v, preferred_element_type=F32)
                return rows, ms, ls, (pv[:BLOCK], pv[BLOCK:])

            def merge(rows, ms, ls, pv):
                m2 = bcast2(ms[0], ms[1])
                l2 = bcast2(ls[0], ls[1])
                acc2 = jnp.where(lo, pv[0], pv[1])
                if not first:
                    m_old = m_s[rows, :]
                    m_new = jnp.maximum(m_old, m2)
                    a_old, a_new = jnp.exp2(m_old - m_new), jnp.exp2(m2 - m_new)
                    l2 = a_old * l_s[rows, :] + a_new * l2
                    acc2 = a_old * acc_s[rows, :] + a_new * acc2
                    m2 = m_new
                if last:
                    o_ref[0, rows, :] = (acc2 / l2).astype(BF)
                else:
                    m_s[rows, :] = m2
                    l_s[rows, :] = l2
                    acc_s[rows, :] = acc2

            pending = {u: scores(u) for u in range(DIL_AHEAD)}
            done = None
            for u in range(DIL_UNROLL):
                if u + DIL_AHEAD < DIL_UNROLL:
                    pending[u + DIL_AHEAD] = scores(u + DIL_AHEAD)
                cur = softmax_pv(pending.pop(u))
                if done is not None:
                    merge(*done)
                done = cur
            merge(*done)
            return carry

        lax.fori_loop(0, SEQ // BLOCK // DIL_UNROLL, step, 0)


def _dilated(qkv, slopes):
    n_pairs = C_HEADS // 2
    seq_f32 = pltpu.VMEM((SEQ, LANES), F32)
    chunk_f32 = pltpu.VMEM((DIL_CONVERT_ROWS, LANES), F32)
    return pl.pallas_call(
        _dilated_kernel,
        grid=(BATCH, n_pairs),
        in_specs=[pl.BlockSpec(memory_space=pltpu.SMEM),
                  pl.BlockSpec((1, SEQ, LANES), lambda b, p: (p, b, 0)),
                  pl.BlockSpec((1, SEQ, LANES), lambda b, p: (n_pairs + p, b, 0)),
                  pl.BlockSpec((1, SEQ, LANES), lambda b, p: (2 * n_pairs + p, b, 0))],
        out_specs=pl.BlockSpec((1, SEQ, LANES), lambda b, p: (p, b, 0)),
        out_shape=jax.ShapeDtypeStruct((n_pairs, N_TOK, LANES), BF),
        scratch_shapes=[pltpu.VMEM((2, SEQ, LANES), BF)] + [chunk_f32] * 3 + [seq_f32] * 10,
        compiler_params=_params(("parallel", "parallel"), VMEM_LIMIT),
        name="dilated",
    )(slopes, qkv, qkv, qkv)


SB_QB = MXU_N
SB_FIRST_TILES = 2
SB_DEAD_LOG2 = -150.0


def _sb_matrix():
    idx = np.arange(SB_QB)
    return jnp.asarray(-(idx[:, None] > idx[None, :]).astype(np.float32), dtype=BF)


def _sb_unit(q2, load_k, load_v, iq, uo, fill=lambda: None):
    lo = lax.broadcasted_iota(jnp.int32, (SB_QB, LANES), 1) < HEAD_DIM
    q_stack = jnp.concatenate([jnp.where(lo, q2, 0.0), jnp.where(lo, 0.0, q2)], axis=0).astype(BF)
    rel1 = (lax.broadcasted_iota(jnp.int32, (SB_QB, SB_QB), 1)
            - lax.broadcasted_iota(jnp.int32, (SB_QB, SB_QB), 0))
    rel = jnp.concatenate([rel1, rel1], axis=0)

    def scores(first, n_tiles):
        return lax.dot_general(q_stack, load_k(first, n_tiles), NT_DIMS,
                               preferred_element_type=F32)

    def walk(first, n_tiles, carry, masked):
        z, c, o = carry
        z_next = scores(jnp.maximum(first - 1, 0), 1)
        if masked:
            fill()
        order = list(reversed(range(n_tiles)))
        ws, es, stricts, totals = {}, {}, {}, {}
        for t in order:
            zt = z[:, t * SB_QB:(t + 1) * SB_QB]
            sp = jnp.maximum(zt, 0.0) + jnp.log2(1.0 + jnp.exp2(-jnp.abs(zt)))
            es[t] = zt - sp
            if masked:
                stricts[t] = rel < (iq - first - t) * SB_QB
                sp = jnp.where(stricts[t], sp, 0.0)
            ws[t] = jnp.dot(sp.astype(BF), uo, preferred_element_type=F32)
            totals[t] = jnp.sum(sp, axis=1, keepdims=True)
        if masked:
            fill()
        parts = {}
        for t in order:
            a = jnp.exp2(es[t] + jnp.concatenate([c] * (SB_QB // LANES), axis=1) + ws[t])
            if masked:
                a = jnp.where(stricts[t], a, 0.0)
            parts[t] = a.astype(BF)
            c = c - totals[t]
        pv = jnp.dot(jnp.concatenate([parts[t] for t in range(n_tiles)], axis=1), load_v(first, n_tiles),
                     preferred_element_type=F32)
        if masked:
            fill()
        return z_next, c, o + jnp.where(lo, pv[:SB_QB], pv[SB_QB:])

    def alive(c):
        return jnp.max(c) > SB_DEAD_LOG2

    def body(state):
        g = state[0]
        z, c, o = walk(g, 1, state[2:], False)
        return g - 1, alive(c), z, c, o

    first = jnp.maximum(iq - 1, 0)
    zeros = (jnp.zeros((2 * SB_QB, LANES), F32), jnp.zeros((SB_QB, LANES), F32))
    z, c, o = walk(first, SB_FIRST_TILES, (scores(first, SB_FIRST_TILES),) + zeros, True)
    state = lax.while_loop(lambda st: (st[0] >= 0) & st[1], body, (first - 1, alive(c), z, c, o))
    return state[4]


def _mix_xattn_kernel(*refs):
    x_ref = refs[0]
    wm_ref, g_ref, wq_ref, cs_ref, kv_ref, wo_ref, o_ref = refs[-7:]
    mixed = jnp.concatenate([r[c] for r in refs[1:-7] for c in range(r.shape[0])], axis=1)
    xv = x_ref[...] + jnp.dot(mixed, wm_ref[...], preferred_element_type=F32)
    h = _rms(xv, g_ref[...]).astype(BF)
    heads = range(X_HEADS)
    cols = [slice(X_HEAD_DIM * hd, X_HEAD_DIM * (hd + 1)) for hd in heads]
    acc = [jnp.dot(h, wq_ref[:, cols[hd]], preferred_element_type=F32) for hd in heads]
    ms = [jnp.mean(acc[hd] * acc[hd], axis=1, keepdims=True) for hd in heads]
    q = [(acc[hd] * cs_ref[:, cols[hd]] * lax.rsqrt(ms[hd] + RMS_EPS)).astype(BF) for hd in heads]
    s = [lax.dot_general(q[hd], jnp.concatenate([kv_ref[2 * hd], kv_ref[2 * hd + 1]], axis=1), NT_DIMS,
                         preferred_element_type=F32) for hd in heads]
    pe, l = [], []
    for hd in heads:
        e = jnp.exp2(s[hd] - jnp.max(s[hd], axis=1, keepdims=True))
        l.append(jnp.sum(e, axis=1, keepdims=True))
        pe.append(e.astype(BF))
    v0 = 2 * X_HEADS
    pv = [jnp.dot(pe[hd], jnp.concatenate([kv_ref[v0 + 2 * hd], kv_ref[v0 + 2 * hd + 1]], axis=1),
                  preferred_element_type=F32) for hd in heads]
    o = jnp.concatenate([(pv[hd] / l[hd]).astype(BF) for hd in heads], axis=1)
    o_ref[...] = xv + jnp.dot(o, wo_ref[...], preferred_element_type=F32)


def _mix_xattn(x, mixer_heads, w_mix, g, w_q, q_colscale, kv, w_o, *, tm=1024):
    tiles_per_batch = SEQ // tm
    in_specs = [pl.BlockSpec((tm, D_MODEL), lambda i: (i, 0))]
    in_specs += [pl.BlockSpec((mh.shape[0], tm, LANES), lambda i: (0, i, 0)) for mh in mixer_heads]
    in_specs += [_resident(w_mix.shape),
                 _resident((1, D_MODEL)),
                 _resident(w_q.shape),
                 _resident((1, D_MODEL)),
                 pl.BlockSpec((4 * X_HEADS, MEM_LEN, LANES), lambda i: (0, i // tiles_per_batch, 0)),
                 _resident(w_o.shape)]
    return pl.pallas_call(
        _mix_xattn_kernel, grid=(N_TOK // tm,),
        in_specs=in_specs,
        out_specs=pl.BlockSpec((tm, D_MODEL), lambda i: (i, 0)),
        out_shape=jax.ShapeDtypeStruct((N_TOK, D_MODEL), F32),
        compiler_params=_params(("parallel",), VMEM_LIMIT),
        name="mix_xattn",
    )(x, *mixer_heads, w_mix, g.reshape(1, D_MODEL), w_q,
      q_colscale.reshape(1, D_MODEL).astype(F32), kv, w_o)


EVEN_TILE = 512


def _even_tail_kernel(slopes_ref, sinks_ref, x_ref, swq_ref, swkp_ref, swkc_ref, swvp_ref, swvc_ref,
                      sbq_ref, sbk_ref, sbv_ref, uo_ref, wm_ref, g_ref, wq_ref, cs_ref, kv_ref, wo_ref,
                      o_ref, heads_s, *, kv_div, max_dist):
    i = pl.program_id(0)
    tiles_per_seq = SEQ // EVEN_TILE
    t_in_seq = jnp.minimum(i, N_TOK // EVEN_TILE - 1) % tiles_per_seq
    wr, rd = i % 2, (i + 1) % 2
    n_pairs_a, n_pairs_b = swq_ref.shape[0], sbq_ref.shape[0]

    @pl.when(i == 0)
    def _():
        heads_s[1] = jnp.zeros(heads_s.shape[1:], BF)

    heads = range(X_HEADS)
    cols = [slice(X_HEAD_DIM * hd, X_HEAD_DIM * (hd + 1)) for hd in heads]
    st = {"xv": [None] * X_HEADS, "q": [None] * X_HEADS, "pe": [None] * X_HEADS, "l": [None] * X_HEADS,
          "o": [None] * X_HEADS}

    def project(c):
        if c == 0:
            st["mixed"] = jnp.concatenate([heads_s[rd, k] for k in range(n_pairs_a + n_pairs_b)], axis=1)
        st["xv"][c] = x_ref[:, cols[c]] + jnp.dot(st["mixed"], wm_ref[:, cols[c]], preferred_element_type=F32)

    def q_proj(hd):
        if hd == 0:
            xv = jnp.concatenate(st["xv"], axis=1)
            st["h"] = _rms(xv, g_ref[...]).astype(BF)
        acc = jnp.dot(st["h"], wq_ref[:, cols[hd]], preferred_element_type=F32)
        ms = jnp.mean(acc * acc, axis=1, keepdims=True)
        st["q"][hd] = (acc * cs_ref[:, cols[hd]] * lax.rsqrt(ms + RMS_EPS)).astype(BF)

    def scores(hd):
        kh = jnp.concatenate([kv_ref[2 * hd], kv_ref[2 * hd + 1]], axis=1)
        s = lax.dot_general(st["q"][hd], kh, NT_DIMS, preferred_element_type=F32)
        e = jnp.exp2(s - jnp.max(s, axis=1, keepdims=True))
        st["l"][hd] = jnp.sum(e, axis=1, keepdims=True)
        st["pe"][hd] = e.astype(BF)

    def values(hd):
        v0 = 2 * X_HEADS
        vh = jnp.concatenate([kv_ref[v0 + 2 * hd], kv_ref[v0 + 2 * hd + 1]], axis=1)
        st["o"][hd] = (jnp.dot(st["pe"][hd], vh, preferred_element_type=F32) / st["l"][hd]).astype(BF)

    def out_proj(c):
        if c == 0:
            st["oc"] = jnp.concatenate(st["o"], axis=1)
        o_ref[:, cols[c]] = st["xv"][c] + jnp.dot(st["oc"], wo_ref[:, cols[c]], preferred_element_type=F32)

    chain = [functools.partial(f, k) for f in (project, q_proj, scores, values, out_proj) for k in heads]

    def fill():
        if chain:
            chain.pop(0)()

    uo = uo_ref[...]
    blocks_per_tile = EVEN_TILE // SB_QB

    def sb_unit(p, j):
        rows = slice(SB_QB * j, SB_QB * (j + 1))
        load = lambda ref: (lambda first, n: ref[p, pl.ds(pl.multiple_of(first * SB_QB, SB_QB), n * SB_QB), :])
        o = _sb_unit(sbq_ref[p, rows, :].astype(F32), load(sbk_ref), load(sbv_ref),
                     blocks_per_tile * t_in_seq + j, uo, fill)
        heads_s[wr, n_pairs_a + p, rows, :] = o.astype(BF)

    for p in range(n_pairs_b):
        for j in range(blocks_per_tile):
            sb_unit(p, j)
    while chain:
        fill()

    n_kv = swkc_ref.shape[0]
    for m in range(EVEN_TILE // BLOCK):
        rows = slice(BLOCK * m, BLOCK * (m + 1))
        if m == 0:
            kp, vp, has_prev = [swkp_ref[g] for g in range(n_kv)], [swvp_ref[g] for g in range(n_kv)], t_in_seq > 0
        else:
            prows = slice(BLOCK * (m - 1), BLOCK * m)
            kp, vp, has_prev = [swkc_ref[g, prows, :] for g in range(n_kv)], [swvc_ref[g, prows, :] for g in range(n_kv)], True
        outs = _swa_block([swq_ref[p, rows, :] for p in range(n_pairs_a)], kp, [swkc_ref[g, rows, :] for g in range(n_kv)],
                          vp, [swvc_ref[g, rows, :] for g in range(n_kv)], has_prev, slopes_ref, sinks_ref,
                          kv_div=kv_div, max_dist=max_dist)
        for p in range(n_pairs_a):
            heads_s[wr, p, rows, :] = outs[p].astype(BF)


def _even_tail(x, p, slopes, sinks, w_mix, g, w_q, q_colscale, kv, w_o):
    n_pairs_a, n_kv, n_pairs_b = A_Q_HEADS // 2, A_KV_HEADS, B_HEADS // 2
    assert p.shape[0] == n_pairs_a + 2 * n_kv + 3 * n_pairs_b and n_pairs_a == n_pairs_b == 2 * n_kv
    n_tiles = N_TOK // EVEN_TILE
    tiles_per_seq = SEQ // EVEN_TILE
    blocks_per_tile = EVEN_TILE // BLOCK
    att = lambda i: jnp.minimum(i, n_tiles - 1)
    mix = lambda i: jnp.maximum(i - 1, 0)
    tile_rows = lambda size, idx: pl.BlockSpec((size, EVEN_TILE, LANES), lambda i: (idx, att(i), 0))
    prev_block = lambda idx: pl.BlockSpec(
        (n_kv, BLOCK, LANES), lambda i: (idx, jnp.maximum(blocks_per_tile * att(i) - 1, 0), 0))
    whole_seq = lambda idx: pl.BlockSpec((n_pairs_b, SEQ, LANES), lambda i: (idx, att(i) // tiles_per_seq, 0),
                                         pipeline_mode=pl.Buffered(1))
    smem = pl.BlockSpec(memory_space=pltpu.SMEM)
    return pl.pallas_call(
        functools.partial(_even_tail_kernel, kv_div=n_pairs_a // n_kv, max_dist=A_WINDOW - 1),
        grid=(n_tiles + 1,),
        in_specs=[smem, smem,
                  pl.BlockSpec((EVEN_TILE, D_MODEL), lambda i: (mix(i), 0)),
                  tile_rows(n_pairs_a, 0), prev_block(2), tile_rows(n_kv, 2), prev_block(3), tile_rows(n_kv, 3),
                  tile_rows(n_pairs_b, 2), whole_seq(3), whole_seq(4),
                  _resident((SB_QB, SB_QB)),
                  _resident(w_mix.shape), _resident((1, D_MODEL)), _resident(w_q.shape), _resident((1, D_MODEL)),
                  pl.BlockSpec((4 * X_HEADS, MEM_LEN, LANES), lambda i: (0, mix(i) // tiles_per_seq, 0)),
                  _resident(w_o.shape)],
        out_specs=pl.BlockSpec((EVEN_TILE, D_MODEL), lambda i: (mix(i), 0)),
        out_shape=jax.ShapeDtypeStruct((N_TOK, D_MODEL), F32),
        scratch_shapes=[pltpu.VMEM((2, n_pairs_a + n_pairs_b, EVEN_TILE, LANES), BF)],
        compiler_params=_params(("arbitrary",), VMEM_LIMIT),
        name="even_tail",
    )(slopes, sinks, x, p, p, p, p, p, p, p, p, _sb_matrix(), w_mix, g.reshape(1, D_MODEL), w_q,
      q_colscale.reshape(1, D_MODEL).astype(F32), kv, w_o)


def _alibi_log2(n_heads):
    return jnp.asarray(LOG2_E * 2.0 ** (-8.0 * np.arange(1, n_heads + 1) / n_heads), dtype=F32)


def _even_projection(x, norm_g, w_in, q_gain, k_gain):
    hd = HEAD_DIM
    a_q, a_kv, b_w = A_Q_HEADS * hd, A_KV_HEADS * hd, B_HEADS * hd
    scale = hd ** -0.5 * LOG2_E
    ones = lambda n: jnp.ones((n,), F32)
    cs = jnp.concatenate([jnp.tile(q_gain, A_Q_HEADS) * scale, jnp.tile(k_gain, A_KV_HEADS), ones(a_kv),
                          ones(b_w) * scale, ones(2 * b_w)])
    plan = ([(True, False)] * (a_q // LANES) + [(True, True)] * (a_kv // LANES) + [(False, True)] * (a_kv // LANES)
            + [(False, False)] * (3 * b_w // LANES))
    return _proj(x, norm_g, w_in, cs, plan, hd)


def _odd_mixer_heads(x, norm_g, w_in, q_gain, k_gain):
    hd = HEAD_DIM
    cs = jnp.concatenate([jnp.tile(q_gain, C_HEADS) * (hd ** -0.5 * LOG2_E), jnp.tile(k_gain, C_HEADS),
                          jnp.ones((C_HEADS * hd,), F32)])
    head_blocks = C_HEADS * hd // LANES
    plan = [(True, False)] * (2 * head_blocks) + [(False, False)] * head_blocks
    p = _proj(x, norm_g, w_in, cs, plan, hd)
    return [_dilated(p, _alibi_log2(C_HEADS))]


def _memory_kv(mem2d, mem_g, w_kv, k_gain):
    cs_kv = jnp.concatenate([jnp.tile(k_gain, X_HEADS), jnp.ones((D_MODEL,), F32)])
    plan = [(True, False)] * (D_MODEL // LANES) + [(False, False)] * (D_MODEL // LANES)
    return _proj(mem2d, mem_g, w_kv, cs_kv, plan, X_HEAD_DIM, tm=MEM_LEN)


def kernel(x, mem, ffn1_norm, ffn1_w_gu, ffn1_w_down, mix_norm, ev_w_in, ev_q_gain, ev_k_gain, ev_sinks, ev_w_out, od_w_in, od_q_gain, od_k_gain, od_w_out, xa_norm, xa_mem_norm, xa_w_q, xa_w_kv, xa_q_gain, xa_k_gain, xa_w_o, ffn2_norm, ffn2_w_gu, ffn2_w_down):
    x = x.reshape(N_TOK, D_MODEL)
    mem2d = mem.reshape(BATCH * MEM_LEN, D_MODEL)
    w_gu, w_down = _cast_now((ffn1_w_gu, 0)), _cast_now((ffn1_w_down, 0))
    for layer in range(DEPTH):
        j = layer // 2
        even = layer % 2 == 0
        w_in3, w_mix3 = (ev_w_in, ev_w_out) if even else (od_w_in, od_w_out)
        jobs = [(w_in3, j), (w_mix3, j), (xa_w_q, layer), (xa_w_kv, layer), (xa_w_o, layer),
                (ffn2_w_gu, layer), (ffn2_w_down, layer)]
        x, (w_in, w_mix, w_q, w_kv, w_o, w_gu, w_down) = _ffn(x, ffn1_norm[layer], w_gu, w_down, jobs)
        kv = _memory_kv(mem2d, xa_mem_norm[layer], w_kv, xa_k_gain[layer])
        cs_q = jnp.tile(xa_q_gain[layer], X_HEADS) * (X_HEAD_DIM ** -0.5 * LOG2_E)
        if even:
            p = _even_projection(x, mix_norm[layer], w_in, ev_q_gain[j], ev_k_gain[j])
            x = _even_tail(x, p, _alibi_log2(A_Q_HEADS), ev_sinks[j].astype(F32) * LOG2_E, w_mix,
                           xa_norm[layer], w_q, cs_q, kv, w_o)
        else:
            heads = _odd_mixer_heads(x, mix_norm[layer], w_in, od_q_gain[j], od_k_gain[j])
            x = _mix_xattn(x, heads, w_mix, xa_norm[layer], w_q, cs_q, kv, w_o)
        jobs = [(ffn1_w_gu, layer + 1), (ffn1_w_down, layer + 1)] if layer + 1 < DEPTH else []
        x, next_ffn1 = _ffn(x, ffn2_norm[layer], w_gu, w_down, jobs)
        if next_ffn1:
            w_gu, w_down = next_ffn1
    return x.reshape(BATCH, SEQ, D_MODEL)
```

```python
import functools

import numpy as np
import jax
import jax.numpy as jnp
from jax import lax
from jax.experimental import pallas as pl
from jax.experimental.pallas import tpu as pltpu

D_MODEL = 1024
BATCH = 4
SEQ = 4096
N_TOK = BATCH * SEQ
DEPTH = 2
HEAD_DIM = 64
BLOCK = 128
A_Q_HEADS = 8
A_KV_HEADS = 2
A_WINDOW = 128
B_HEADS = 8
C_HEADS = 16
C_PATTERNS = ((128, 1), (512, 4), (2048, 16))
MEM_LEN = 256
X_HEADS = 4
X_HEAD_DIM = D_MODEL // X_HEADS
D_FF = 2816
RMS_EPS = 1e-6

LANES = 128
MXU_N = 256
VMEM_LIMIT = 56 * 1024 * 1024

BF = jnp.bfloat16
F32 = jnp.float32
NT_DIMS = (((1,), (1,)), ((), ()))
LOG2_E = 1.4426950408889634


def _params(sem, vmem=None):
    return pltpu.CompilerParams(dimension_semantics=sem, vmem_limit_bytes=vmem)


def _resident(shape):
    nd = len(shape)
    return pl.BlockSpec(shape, lambda *_: (0,) * nd, pipeline_mode=pl.Buffered(1))


BF16_SUBLANES = 16


def _cast_specs(job, steps):
    w3, layer = job
    _, r, c = w3.shape
    rb = next(rb for rb in range(BF16_SUBLANES, r + 1, BF16_SUBLANES) if r % rb == 0 and r // rb <= steps)
    last = r // rb - 1
    return (pl.BlockSpec((None, rb, c), lambda i: (layer, jnp.minimum(i, last), 0)),
            pl.BlockSpec((rb, c), lambda i: (jnp.minimum(i, last), 0)),
            jax.ShapeDtypeStruct((r, c), BF))


def _run_cast_jobs(in_refs, out_refs):
    for src, dst in zip(in_refs, out_refs):
        dst[...] = src[...].astype(BF)


def _cast_kernel(w_ref, o_ref):
    _run_cast_jobs([w_ref], [o_ref])


def _cast_now(job, *, rows=128):
    steps = job[0].shape[1] // rows
    in_spec, out_spec, out_shape = _cast_specs(job, steps)
    return pl.pallas_call(
        _cast_kernel, grid=(steps,), in_specs=[in_spec], out_specs=out_spec, out_shape=out_shape,
        compiler_params=_params(("arbitrary",)),
        name="cast",
    )(job[0])


def _rms(xv, g):
    ms = jnp.mean(xv * xv, axis=-1, keepdims=True)
    return xv * lax.rsqrt(ms + RMS_EPS) * g


FFN_SPLIT = (D_FF // MXU_N + 1) // 2 * MXU_N
FFN_CHUNKS = ((0, FFN_SPLIT), (FFN_SPLIT, D_FF))


def _ffn_kernel(*refs, n_jobs):
    x_ref, g_ref, wgu_ref, wd_ref = refs[:4]
    o_ref = refs[4 + n_jobs]
    xv = x_ref[...]
    h = _rms(xv, g_ref[...]).astype(BF)
    acc = jnp.zeros_like(xv)
    for c0, c1 in FFN_CHUNKS:
        gate = jnp.dot(h, wgu_ref[:, c0:c1], preferred_element_type=F32)
        up = jnp.dot(h, wgu_ref[:, D_FF + c0:D_FF + c1], preferred_element_type=F32)
        act = (gate * jax.nn.sigmoid(gate) * up).astype(BF)
        acc = acc + jnp.dot(act, wd_ref[c0:c1, :], preferred_element_type=F32)
    o_ref[...] = xv + 0.5 * acc
    _run_cast_jobs(refs[4:4 + n_jobs], refs[5 + n_jobs:])


def _ffn(x, g, w_gu, w_down, cast_jobs=(), *, tm=512):
    steps = N_TOK // tm
    specs = [_cast_specs(job, steps) for job in cast_jobs]
    out = pl.pallas_call(
        functools.partial(_ffn_kernel, n_jobs=len(cast_jobs)),
        grid=(steps,),
        in_specs=[pl.BlockSpec((tm, D_MODEL), lambda i: (i, 0)),
                  _resident((1, D_MODEL)),
                  _resident(w_gu.shape),
                  _resident(w_down.shape)] + [s[0] for s in specs],
        out_specs=[pl.BlockSpec((tm, D_MODEL), lambda i: (i, 0))] + [s[1] for s in specs],
        out_shape=[jax.ShapeDtypeStruct((N_TOK, D_MODEL), F32)] + [s[2] for s in specs],
        compiler_params=_params(("arbitrary",), VMEM_LIMIT),
        name="ffn",
    )(x, g.reshape(1, D_MODEL), w_gu, w_down, *[job[0] for job in cast_jobs])
    return out[0], out[1:]


def _proj_kernel(x_ref, g_ref, w_ref, cs_ref, o_ref, *, plan, gs):
    assert gs in (HEAD_DIM, MXU_N)
    h = _rms(x_ref[...], g_ref[...]).astype(BF)
    n_chunks = len(plan) // 2
    lo = lax.broadcasted_iota(jnp.int32, (x_ref.shape[0], LANES), 1) < HEAD_DIM

    def main(j):
        return jnp.dot(h, w_ref[:, MXU_N * j:MXU_N * (j + 1)], preferred_element_type=F32)

    acc_next = main(0)
    out = 0
    for j in range(n_chunks):
        cols = slice(MXU_N * j, MXU_N * (j + 1))
        acc = acc_next
        if j + 1 < n_chunks:
            acc_next = main(j + 1)
        y = acc * cs_ref[:, cols]
        halves = plan[2 * j:2 * j + 2]
        if gs == MXU_N and any(normed for normed, _ in halves):
            inv_chunk = lax.rsqrt(jnp.mean(acc * acc, axis=1, keepdims=True) + RMS_EPS)
        for half, (normed, dup) in enumerate(halves):
            lanes = slice(LANES * half, LANES * (half + 1))
            yh = y[:, lanes]
            if normed and gs == MXU_N:
                yh = yh * inv_chunk
            elif normed:
                sq = acc[:, lanes] * acc[:, lanes]
                s_lo = jnp.sum(jnp.where(lo, sq, 0.0), axis=1, keepdims=True)
                s_hi = jnp.sum(jnp.where(lo, 0.0, sq), axis=1, keepdims=True)
                yh = yh * lax.rsqrt(jnp.where(lo, s_lo, s_hi) * (1.0 / gs) + RMS_EPS)
            if dup:
                swapped = pltpu.roll(yh, HEAD_DIM, axis=1)
                o_ref[out] = jnp.where(lo, yh, swapped).astype(BF)
                o_ref[out + 1] = jnp.where(lo, swapped, yh).astype(BF)
                out += 2
            else:
                o_ref[out] = yh.astype(BF)
                out += 1


def _proj(x, g, w, colscale, plan, gs, *, tm=1024):
    rows = x.shape[0]
    wout = w.shape[1]
    assert wout == LANES * len(plan) and len(plan) % 2 == 0
    c = sum(2 if dup else 1 for _, dup in plan)
    return pl.pallas_call(
        functools.partial(_proj_kernel, plan=tuple(plan), gs=gs),
        grid=(rows // tm,),
        in_specs=[pl.BlockSpec((tm, D_MODEL), lambda i: (i, 0)),
                  _resident((1, D_MODEL)),
                  _resident(w.shape),
                  _resident((1, wout))],
        out_specs=pl.BlockSpec((c, tm, LANES), lambda i: (0, i, 0)),
        out_shape=jax.ShapeDtypeStruct((c, rows, LANES), BF),
        compiler_params=_params(("parallel",), VMEM_LIMIT),
        name="proj",
    )(x, g.reshape(1, D_MODEL), w, colscale.reshape(1, wout).astype(F32))


def _swa_block(q_blocks, kp, kc, vp, vc, has_prev, slopes_ref, sinks_ref, *, kv_div, max_dist):
    row = lax.broadcasted_iota(jnp.int32, (BLOCK, 2 * BLOCK), 0)
    col = lax.broadcasted_iota(jnp.int32, (BLOCK, 2 * BLOCK), 1)
    dist = row + BLOCK - col
    valid = (dist >= 0) & (dist <= max_dist)
    if has_prev is not True:
        valid = valid & ((col >= BLOCK) | has_prev)
    negmask = jnp.where(valid, 0.0, -jnp.inf)
    distf = dist.astype(F32)
    lo = lax.broadcasted_iota(jnp.int32, (BLOCK, LANES), 1) < HEAD_DIM

    n_groups = len(q_blocks) // kv_div
    heads_per_group = 2 * kv_div
    scores = []
    for g in range(n_groups):
        parts = []
        for p in range(g * kv_div, (g + 1) * kv_div):
            q2 = q_blocks[p].astype(F32)
            parts += [jnp.where(lo, q2, 0.0), jnp.where(lo, 0.0, q2)]
        q_stack = jnp.concatenate(parts, axis=0).astype(BF)
        scores.append(jnp.concatenate(
            [lax.dot_general(q_stack, kp[g], NT_DIMS, preferred_element_type=F32),
             lax.dot_general(q_stack, kc[g], NT_DIMS, preferred_element_type=F32)], axis=1))
    soft = []
    for g in range(n_groups):
        res = []
        for j in range(heads_per_group):
            h = g * heads_per_group + j
            s = scores[g][j * BLOCK:(j + 1) * BLOCK] - slopes_ref[h] * distf + negmask
            m = jnp.maximum(jnp.max(s, axis=1, keepdims=True), sinks_ref[h])
            pe = jnp.exp2(s - m)
            res.append((pe.astype(BF), jnp.sum(pe, axis=1, keepdims=True) + jnp.exp2(sinks_ref[h] - m)))
        soft.append(res)
    outs = []
    for g in range(n_groups):
        pb = jnp.concatenate([r[0] for r in soft[g]], axis=0)
        pv = (jnp.dot(pb[:, :BLOCK], vp[g], preferred_element_type=F32)
              + jnp.dot(pb[:, BLOCK:], vc[g], preferred_element_type=F32))
        for jp in range(kv_div):
            o0 = pv[(2 * jp) * BLOCK:(2 * jp + 1) * BLOCK] / soft[g][2 * jp][1]
            o1 = pv[(2 * jp + 1) * BLOCK:(2 * jp + 2) * BLOCK] / soft[g][2 * jp + 1][1]
            outs.append(jnp.where(lo, o0, o1))
    return outs


DIL_ORDER = tuple(sorted(C_PATTERNS, key=lambda wd: -wd[1]))
DIL_UNROLL = 8
DIL_AHEAD = 2
DIL_BASE = 4
DIL_Q = SEQ // DIL_BASE
DIL_CONVERT_ROWS = DIL_BASE * BLOCK


def _dilated_kernel(slopes_ref, q_ref, k_ref, v_ref, o_ref, qn_s, tq_s, tk_s, tv_s, q0_s, q1_s, k_s, v_s,
                    acc_r, m_r, l_r, acc_n, m_n, l_n):
    assert all(d == 1 or d % DIL_BASE == 0 for _, d in DIL_ORDER) and DIL_ORDER[-1][1] == 1
    p = pl.program_id(1)
    lo = lax.broadcasted_iota(jnp.int32, (BLOCK, LANES), 1) < HEAD_DIM

    def convert(c, carry):
        rows = pl.ds(pl.multiple_of(c * DIL_CONVERT_ROWS, DIL_CONVERT_ROWS), DIL_CONVERT_ROWS)
        q_nat = q_ref[0, rows, :].astype(F32)
        lo_c = lax.broadcasted_iota(jnp.int32, (DIL_CONVERT_ROWS, LANES), 1) < HEAD_DIM
        qn_s[0, rows, :] = jnp.where(lo_c, q_nat, 0.0).astype(BF)
        qn_s[1, rows, :] = jnp.where(lo_c, 0.0, q_nat).astype(BF)
        tq_s[...] = q_nat
        tk_s[...] = k_ref[0, rows, :].astype(F32)
        tv_s[...] = v_ref[0, rows, :].astype(F32)
        for rho in range(DIL_BASE):
            src = pl.ds(rho, BLOCK, stride=DIL_BASE)
            dst = pl.ds(pl.multiple_of(rho * DIL_Q + c * BLOCK, BLOCK), BLOCK)
            q = tq_s[src, :]
            q0_s[dst, :] = jnp.where(lo, q, 0.0)
            q1_s[dst, :] = jnp.where(lo, 0.0, q)
            k_s[dst, :] = tk_s[src, :]
            v_s[dst, :] = tv_s[src, :]
        return carry

    lax.fori_loop(0, SEQ // DIL_CONVERT_ROWS, convert, 0)

    row = lax.broadcasted_iota(jnp.int32, (BLOCK, 2 * BLOCK), 0)
    col = lax.broadcasted_iota(jnp.int32, (BLOCK, 2 * BLOCK), 1)
    dist = row + BLOCK - col
    distf = dist.astype(F32)
    no_prev = jnp.where(col < BLOCK, -jnp.inf, 0.0)

    def bcast2(a0, a1):
        return jnp.where(lo, jnp.broadcast_to(a0, (BLOCK, LANES)), jnp.broadcast_to(a1, (BLOCK, LANES)))

    for pi, (window, d) in enumerate(DIL_ORDER):
        first, last = pi == 0, pi == len(DIL_ORDER) - 1
        natural = d == 1
        nb = SEQ // d // BLOCK
        band = (dist >= 0) & (dist <= window // d)
        bias = [jnp.where(band, (-float(d) * slopes_ref[2 * p + hh]) * distf, -jnp.inf) for hh in range(2)]
        acc_s, m_s, l_s = (acc_n, m_n, l_n) if natural else (acc_r, m_r, l_r)
        to_natural = not natural and not last and DIL_ORDER[pi + 1][1] == 1
        assert (not to_natural or d == DIL_BASE) and (not natural or first or DIL_ORDER[pi - 1][1] == DIL_BASE)
        acc_o, m_o, l_o = (acc_n, m_n, l_n) if to_natural else (acc_s, m_s, l_s)

        def rows_of(r, n, n_blocks=1, d=d, natural=natural):
            size = n_blocks * BLOCK
            if natural:
                return pl.ds(pl.multiple_of(BLOCK * n, BLOCK), size)
            inner = d // DIL_BASE
            start = (r % DIL_BASE) * DIL_Q + r // DIL_BASE + inner * BLOCK * n
            return pl.ds(start, size, stride=inner) if inner > 1 else pl.ds(pl.multiple_of(start, BLOCK), size)

        def step(it, carry, nb=nb, bias=bias, first=first, last=last, natural=natural, rows_of=rows_of,
                 acc_s=acc_s, m_s=m_s, l_s=l_s, acc_o=acc_o, m_o=m_o, l_o=l_o, to_natural=to_natural):
            assert DIL_UNROLL % nb == 0 or nb % DIL_UNROLL == 0
            load_k = (lambda rr: k_ref[0, rr, :]) if natural else (lambda rr: k_s[rr, :].astype(BF))
            load_v = (lambda rr: v_ref[0, rr, :]) if natural else (lambda rr: v_s[rr, :].astype(BF))

            def scores(u):
                t = it * DIL_UNROLL + u
                r, n = t // nb, t % nb
                prev = (u % nb != 0) if nb <= DIL_UNROLL else (True if u else None)
                rows = rows_of(r, n)
                out_rows = pl.ds(DIL_BASE * BLOCK * n + r, BLOCK, stride=DIL_BASE) if to_natural else rows
                if prev is True:
                    kv_rows = [rows_of(r, n - 1, 2)]
                elif prev is None:
                    kv_rows = [rows_of(r, jnp.maximum(n - 1, 0)), rows]
                else:
                    kv_rows = [rows]
                if natural:
                    qh = jnp.concatenate([qn_s[0, rows, :], qn_s[1, rows, :]], axis=0)
                else:
                    qh = jnp.concatenate([q0_s[rows, :], q1_s[rows, :]], axis=0).astype(BF)
                s = jnp.concatenate([lax.dot_general(qh, load_k(rr), NT_DIMS, preferred_element_type=F32)
                                     for rr in kv_rows], axis=1)
                return n, prev, rows, out_rows, kv_rows, (s[:BLOCK], s[BLOCK:])

            def softmax_pv(blk):
                n, prev, rows, out_rows, kv_rows, s = blk
                ms, ls, pes = [], [], []
                for hh in range(2):
                    sh = s[hh] + (bias[hh][:, BLOCK:] if prev is False else bias[hh])
                    if prev is None:
                        sh = sh + jnp.where(n == 0, no_prev, 0.0)
                    m = jnp.max(sh, axis=1, keepdims=True)
                    pe = jnp.exp2(sh - m)
                    ms.append(m)
                    ls.append(jnp.sum(pe, axis=1, keepdims=True))
                    pes.append(pe.astype(BF))
                pb = jnp.concatenate(pes, axis=0)
                v = jnp.concatenate([load_v(rr) for rr in kv_rows], axis=0) if len(kv_rows) > 1 else load_v(kv_rows[0])
                pv = jnp.dot(pb, v, preferred_element_type=F32)
                return rows, out_rows, ms, ls, (pv[:BLOCK], pv[BLOCK:])

            def merge(rows, out_rows, ms, ls, pv):
                m2 = bcast2(ms[0], ms[1])
                l2 = bcast2(ls[0], ls[1])
                acc2 = jnp.where(lo, pv[0], pv[1])
                if not first:
                    m_old = m_s[rows, :]
                    m_new = jnp.maximum(m_old, m2)
                    a_old, a_new = jnp.exp2(m_old - m_new), jnp.exp2(m2 - m_new)
                    l2 = a_old * l_s[rows, :] + a_new * l2
                    acc2 = a_old * acc_s[rows, :] + a_new * acc2
                    m2 = m_new
                if last:
                    o_ref[0, rows, :] = (acc2 / l2).astype(BF)
                else:
                    m_o[out_rows, :] = m2
                    l_o[out_rows, :] = l2
                    acc_o[out_rows, :] = acc2

            pending = {u: scores(u) for u in range(DIL_AHEAD)}
            done = None
            for u in range(DIL_UNROLL):
                if u + DIL_AHEAD < DIL_UNROLL:
                    pending[u + DIL_AHEAD] = scores(u + DIL_AHEAD)
                cur = softmax_pv(pending.pop(u))
                if done is not None:
                    merge(*done)
                done = cur
            merge(*done)
            return carry

        lax.fori_loop(0, SEQ // BLOCK // DIL_UNROLL, step, 0)


def _dilated(qkv, slopes):
    n_pairs = C_HEADS // 2
    seq_f32 = pltpu.VMEM((SEQ, LANES), F32)
    chunk_f32 = pltpu.VMEM((DIL_CONVERT_ROWS, LANES), F32)
    return pl.pallas_call(
        _dilated_kernel,
        grid=(BATCH, n_pairs),
        in_specs=[pl.BlockSpec(memory_space=pltpu.SMEM),
                  pl.BlockSpec((1, SEQ, LANES), lambda b, p: (p, b, 0)),
                  pl.BlockSpec((1, SEQ, LANES), lambda b, p: (n_pairs + p, b, 0)),
                  pl.BlockSpec((1, SEQ, LANES), lambda b, p: (2 * n_pairs + p, b, 0))],
        out_specs=pl.BlockSpec((1, SEQ, LANES), lambda b, p: (p, b, 0)),
        out_shape=jax.ShapeDtypeStruct((n_pairs, N_TOK, LANES), BF),
        scratch_shapes=[pltpu.VMEM((2, SEQ, LANES), BF)] + [chunk_f32] * 3 + [seq_f32] * 10,
        compiler_params=_params(("parallel", "parallel"), VMEM_LIMIT),
        name="dilated",
    )(slopes, qkv, qkv, qkv)


SB_QB = MXU_N
SB_FIRST_TILES = 2
SB_DEAD_LOG2 = -150.0


def _sb_matrix():
    idx = np.arange(SB_QB)
    return jnp.asarray(-(idx[:, None] > idx[None, :]).astype(np.float32), dtype=BF)


def _sb_unit(q2, load_k, load_v, iq, uo, fill=lambda: None):
    lo = lax.broadcasted_iota(jnp.int32, (SB_QB, LANES), 1) < HEAD_DIM
    q_stack = jnp.concatenate([jnp.where(lo, q2, 0.0), jnp.where(lo, 0.0, q2)], axis=0).astype(BF)
    rel1 = (lax.broadcasted_iota(jnp.int32, (SB_QB, SB_QB), 1)
            - lax.broadcasted_iota(jnp.int32, (SB_QB, SB_QB), 0))
    rel = jnp.concatenate([rel1, rel1], axis=0)

    def scores(first, n_tiles):
        return lax.dot_general(q_stack, load_k(first, n_tiles), NT_DIMS,
                               preferred_element_type=F32)

    def walk(first, n_tiles, carry, masked):
        z, c, o = carry
        z_next = scores(jnp.maximum(first - 1, 0), 1)
        if masked:
            fill()
        order = list(reversed(range(n_tiles)))
        ws, es, stricts, totals = {}, {}, {}, {}
        for t in order:
            zt = z[:, t * SB_QB:(t + 1) * SB_QB]
            sp = jnp.maximum(zt, 0.0) + jnp.log2(1.0 + jnp.exp2(-jnp.abs(zt)))
            es[t] = zt - sp
            if masked:
                stricts[t] = rel < (iq - first - t) * SB_QB
                sp = jnp.where(stricts[t], sp, 0.0)
            ws[t] = jnp.dot(sp.astype(BF), uo, preferred_element_type=F32)
            totals[t] = jnp.sum(sp, axis=1, keepdims=True)
        if masked:
            fill()
        parts = {}
        for t in order:
            a = jnp.exp2(es[t] + jnp.concatenate([c] * (SB_QB // LANES), axis=1) + ws[t])
            if masked:
                a = jnp.where(stricts[t], a, 0.0)
            parts[t] = a.astype(BF)
            c = c - totals[t]
        pv = jnp.dot(jnp.concatenate([parts[t] for t in range(n_tiles)], axis=1), load_v(first, n_tiles),
                     preferred_element_type=F32)
        if masked:
            fill()
        return z_next, c, o + jnp.where(lo, pv[:SB_QB], pv[SB_QB:])

    def alive(c):
        return jnp.max(c) > SB_DEAD_LOG2

    def body(state):
        g = state[0]
        z, c, o = walk(g, 1, state[2:], False)
        return g - 1, alive(c), z, c, o

    first = jnp.maximum(iq - 1, 0)
    zeros = (jnp.zeros((2 * SB_QB, LANES), F32), jnp.zeros((SB_QB, LANES), F32))
    z, c, o = walk(first, SB_FIRST_TILES, (scores(first, SB_FIRST_TILES),) + zeros, True)
    state = lax.while_loop(lambda st: (st[0] >= 0) & st[1], body, (first - 1, alive(c), z, c, o))
    return state[4]


def _mix_xattn_kernel(*refs):
    x_ref = refs[0]
    wm_ref, g_ref, wq_ref, cs_ref, kv_ref, wo_ref, o_ref = refs[-7:]
    mixed = jnp.concatenate([r[c] for r in refs[1:-7] for c in range(r.shape[0])], axis=1)
    xv = x_ref[...] + jnp.dot(mixed, wm_ref[...], preferred_element_type=F32)
    h = _rms(xv, g_ref[...]).astype(BF)
    heads = range(X_HEADS)
    cols = [slice(X_HEAD_DIM * hd, X_HEAD_DIM * (hd + 1)) for hd in heads]
    acc = [jnp.dot(h, wq_ref[:, cols[hd]], preferred_element_type=F32) for hd in heads]
    ms = [jnp.mean(acc[hd] * acc[hd], axis=1, keepdims=True) for hd in heads]
    q = [(acc[hd] * cs_ref[:, cols[hd]] * lax.rsqrt(ms[hd] + RMS_EPS)).astype(BF) for hd in heads]
    s = [lax.dot_general(q[hd], jnp.concatenate([kv_ref[2 * hd], kv_ref[2 * hd + 1]], axis=1), NT_DIMS,
                         preferred_element_type=F32) for hd in heads]
    pe, l = [], []
    for hd in heads:
        e = jnp.exp2(s[hd] - jnp.max(s[hd], axis=1, keepdims=True))
        l.append(jnp.sum(e, axis=1, keepdims=True))
        pe.append(e.astype(BF))
    v0 = 2 * X_HEADS
    pv = [jnp.dot(pe[hd], jnp.concatenate([kv_ref[v0 + 2 * hd], kv_ref[v0 + 2 * hd + 1]], axis=1),
                  preferred_element_type=F32) for hd in heads]
    o = jnp.concatenate([(pv[hd] / l[hd]).astype(BF) for hd in heads], axis=1)
    o_ref[...] = xv + jnp.dot(o, wo_ref[...], preferred_element_type=F32)


def _mix_xattn(x, mixer_heads, w_mix, g, w_q, q_colscale, kv, w_o, *, tm=1024):
    tiles_per_batch = SEQ // tm
    in_specs = [pl.BlockSpec((tm, D_MODEL), lambda i: (i, 0))]
    in_specs += [pl.BlockSpec((mh.shape[0], tm, LANES), lambda i: (0, i, 0)) for mh in mixer_heads]
    in_specs += [_resident(w_mix.shape),
                 _resident((1, D_MODEL)),
                 _resident(w_q.shape),
                 _resident((1, D_MODEL)),
                 pl.BlockSpec((4 * X_HEADS, MEM_LEN, LANES), lambda i: (0, i // tiles_per_batch, 0)),
                 _resident(w_o.shape)]
    return pl.pallas_call(
        _mix_xattn_kernel, grid=(N_TOK // tm,),
        in_specs=in_specs,
        out_specs=pl.BlockSpec((tm, D_MODEL), lambda i: (i, 0)),
        out_shape=jax.ShapeDtypeStruct((N_TOK, D_MODEL), F32),
        compiler_params=_params(("parallel",), VMEM_LIMIT),
        name="mix_xattn",
    )(x, *mixer_heads, w_mix, g.reshape(1, D_MODEL), w_q,
      q_colscale.reshape(1, D_MODEL).astype(F32), kv, w_o)


EVEN_TILE = 512


def _even_tail_kernel(slopes_ref, sinks_ref, x_ref, swq_ref, swkp_ref, swkc_ref, swvp_ref, swvc_ref,
                      sbq_ref, sbk_ref, sbv_ref, uo_ref, wm_ref, g_ref, wq_ref, cs_ref, kv_ref, wo_ref,
                      o_ref, heads_s, *, kv_div, max_dist):
    i = pl.program_id(0)
    tiles_per_seq = SEQ // EVEN_TILE
    t_in_seq = jnp.minimum(i, N_TOK // EVEN_TILE - 1) % tiles_per_seq
    wr, rd = i % 2, (i + 1) % 2
    n_pairs_a, n_pairs_b = swq_ref.shape[0], sbq_ref.shape[0]

    @pl.when(i == 0)
    def _():
        heads_s[1] = jnp.zeros(heads_s.shape[1:], BF)

    def mix_steps():
        heads = range(X_HEADS)
        cols = [slice(X_HEAD_DIM * hd, X_HEAD_DIM * (hd + 1)) for hd in heads]
        st = {"xv": [None] * X_HEADS, "q": [None] * X_HEADS, "pe": [None] * X_HEADS, "l": [None] * X_HEADS,
              "o": [None] * X_HEADS}

        def project(c):
            if c == 0:
                st["mixed"] = jnp.concatenate([heads_s[rd, k] for k in range(n_pairs_a + n_pairs_b)], axis=1)
            st["xv"][c] = x_ref[:, cols[c]] + jnp.dot(st["mixed"], wm_ref[:, cols[c]], preferred_element_type=F32)

        def q_proj(hd):
            if hd == 0:
                xv = jnp.concatenate(st["xv"], axis=1)
                st["h"] = _rms(xv, g_ref[...]).astype(BF)
            acc = jnp.dot(st["h"], wq_ref[:, cols[hd]], preferred_element_type=F32)
            ms = jnp.mean(acc * acc, axis=1, keepdims=True)
            st["q"][hd] = (acc * cs_ref[:, cols[hd]] * lax.rsqrt(ms + RMS_EPS)).astype(BF)

        def scores(hd):
            kh = jnp.concatenate([kv_ref[2 * hd], kv_ref[2 * hd + 1]], axis=1)
            s = lax.dot_general(st["q"][hd], kh, NT_DIMS, preferred_element_type=F32)
            e = jnp.exp2(s - jnp.max(s, axis=1, keepdims=True))
            st["l"][hd] = jnp.sum(e, axis=1, keepdims=True)
            st["pe"][hd] = e.astype(BF)

        def values(hd):
            v0 = 2 * X_HEADS
            vh = jnp.concatenate([kv_ref[v0 + 2 * hd], kv_ref[v0 + 2 * hd + 1]], axis=1)
            st["o"][hd] = (jnp.dot(st["pe"][hd], vh, preferred_element_type=F32) / st["l"][hd]).astype(BF)

        def out_proj(c):
            if c == 0:
                st["oc"] = jnp.concatenate(st["o"], axis=1)
            o_ref[:, cols[c]] = st["xv"][c] + jnp.dot(st["oc"], wo_ref[:, cols[c]], preferred_element_type=F32)

        return [functools.partial(f, k) for f in (project, q_proj, scores, values, out_proj) for k in heads]

    def attend_and_mix():
        chain = mix_steps()

        def fill():
            if chain:
                chain.pop(0)()

        uo = uo_ref[...]
        blocks_per_tile = EVEN_TILE // SB_QB

        def sb_unit(p, j):
            rows = slice(SB_QB * j, SB_QB * (j + 1))
            load = lambda ref: (lambda first, n: ref[p, pl.ds(pl.multiple_of(first * SB_QB, SB_QB), n * SB_QB), :])
            o = _sb_unit(sbq_ref[p, rows, :].astype(F32), load(sbk_ref), load(sbv_ref),
                         blocks_per_tile * t_in_seq + j, uo, fill)
            heads_s[wr, n_pairs_a + p, rows, :] = o.astype(BF)

        for p in range(n_pairs_b):
            for j in range(blocks_per_tile):
                sb_unit(p, j)
        while chain:
            fill()

        n_kv = swkc_ref.shape[0]
        for m in range(EVEN_TILE // BLOCK):
            rows = slice(BLOCK * m, BLOCK * (m + 1))
            if m == 0:
                kp, vp, has_prev = [swkp_ref[g] for g in range(n_kv)], [swvp_ref[g] for g in range(n_kv)], t_in_seq > 0
            else:
                prows = slice(BLOCK * (m - 1), BLOCK * m)
                kp, vp, has_prev = ([swkc_ref[g, prows, :] for g in range(n_kv)],
                                    [swvc_ref[g, prows, :] for g in range(n_kv)], True)
            outs = _swa_block([swq_ref[p, rows, :] for p in range(n_pairs_a)], kp,
                              [swkc_ref[g, rows, :] for g in range(n_kv)], vp, [swvc_ref[g, rows, :] for g in range(n_kv)],
                              has_prev, slopes_ref, sinks_ref, kv_div=kv_div, max_dist=max_dist)
            for p in range(n_pairs_a):
                heads_s[wr, p, rows, :] = outs[p].astype(BF)

    n_tiles = N_TOK // EVEN_TILE
    pl.when(i < n_tiles)(attend_and_mix)

    @pl.when(i == n_tiles)
    def _():
        for step in mix_steps():
            step()


def _even_tail(x, p, slopes, sinks, w_mix, g, w_q, q_colscale, kv, w_o):
    n_pairs_a, n_kv, n_pairs_b = A_Q_HEADS // 2, A_KV_HEADS, B_HEADS // 2
    assert p.shape[0] == n_pairs_a + 2 * n_kv + 3 * n_pairs_b and n_pairs_a == n_pairs_b == 2 * n_kv
    n_tiles = N_TOK // EVEN_TILE
    tiles_per_seq = SEQ // EVEN_TILE
    blocks_per_tile = EVEN_TILE // BLOCK
    att = lambda i: jnp.minimum(i, n_tiles - 1)
    mix = lambda i: jnp.maximum(i - 1, 0)
    tile_rows = lambda size, idx: pl.BlockSpec((size, EVEN_TILE, LANES), lambda i: (idx, att(i), 0))
    prev_block = lambda idx: pl.BlockSpec(
        (n_kv, BLOCK, LANES), lambda i: (idx, jnp.maximum(blocks_per_tile * att(i) - 1, 0), 0))
    whole_seq = lambda idx: pl.BlockSpec((n_pairs_b, SEQ, LANES), lambda i: (idx, att(i) // tiles_per_seq, 0),
                                         pipeline_mode=pl.Buffered(1))
    smem = pl.BlockSpec(memory_space=pltpu.SMEM)
    return pl.pallas_call(
        functools.partial(_even_tail_kernel, kv_div=n_pairs_a // n_kv, max_dist=A_WINDOW - 1),
        grid=(n_tiles + 1,),
        in_specs=[smem, smem,
                  pl.BlockSpec((EVEN_TILE, D_MODEL), lambda i: (mix(i), 0)),
                  tile_rows(n_pairs_a, 0), prev_block(2), tile_rows(n_kv, 2), prev_block(3), tile_rows(n_kv, 3),
                  tile_rows(n_pairs_b, 2), whole_seq(3), whole_seq(4),
                  _resident((SB_QB, SB_QB)),
                  _resident(w_mix.shape), _resident((1, D_MODEL)), _resident(w_q.shape), _resident((1, D_MODEL)),
                  pl.BlockSpec((4 * X_HEADS, MEM_LEN, LANES), lambda i: (0, mix(i) // tiles_per_seq, 0)),
                  _resident(w_o.shape)],
        out_specs=pl.BlockSpec((EVEN_TILE, D_MODEL), lambda i: (mix(i), 0)),
        out_shape=jax.ShapeDtypeStruct((N_TOK, D_MODEL), F32),
        scratch_shapes=[pltpu.VMEM((2, n_pairs_a + n_pairs_b, EVEN_TILE, LANES), BF)],
        compiler_params=_params(("arbitrary",), VMEM_LIMIT),
        name="even_tail",
    )(slopes, sinks, x, p, p, p, p, p, p, p, p, _sb_matrix(), w_mix, g.reshape(1, D_MODEL), w_q,
      q_colscale.reshape(1, D_MODEL).astype(F32), kv, w_o)


def _alibi_log2(n_heads):
    return jnp.asarray(LOG2_E * 2.0 ** (-8.0 * np.arange(1, n_heads + 1) / n_heads), dtype=F32)


def _even_projection(x, norm_g, w_in, q_gain, k_gain):
    hd = HEAD_DIM
    a_q, a_kv, b_w = A_Q_HEADS * hd, A_KV_HEADS * hd, B_HEADS * hd
    scale = hd ** -0.5 * LOG2_E
    ones = lambda n: jnp.ones((n,), F32)
    cs = jnp.concatenate([jnp.tile(q_gain, A_Q_HEADS) * scale, jnp.tile(k_gain, A_KV_HEADS), ones(a_kv),
                          ones(b_w) * scale, ones(2 * b_w)])
    plan = ([(True, False)] * (a_q // LANES) + [(True, True)] * (a_kv // LANES) + [(False, True)] * (a_kv // LANES)
            + [(False, False)] * (3 * b_w // LANES))
    return _proj(x, norm_g, w_in, cs, plan, hd)


def _odd_mixer_heads(x, norm_g, w_in, q_gain, k_gain):
    hd = HEAD_DIM
    cs = jnp.concatenate([jnp.tile(q_gain, C_HEADS) * (hd ** -0.5 * LOG2_E), jnp.tile(k_gain, C_HEADS),
                          jnp.ones((C_HEADS * hd,), F32)])
    head_blocks = C_HEADS * hd // LANES
    plan = [(True, False)] * (2 * head_blocks) + [(False, False)] * head_blocks
    p = _proj(x, norm_g, w_in, cs, plan, hd)
    return [_dilated(p, _alibi_log2(C_HEADS))]


def _memory_kv(mem2d, mem_g, w_kv, k_gain):
    cs_kv = jnp.concatenate([jnp.tile(k_gain, X_HEADS), jnp.ones((D_MODEL,), F32)])
    plan = [(True, False)] * (D_MODEL // LANES) + [(False, False)] * (D_MODEL // LANES)
    return _proj(mem2d, mem_g, w_kv, cs_kv, plan, X_HEAD_DIM, tm=MEM_LEN)


def kernel(x, mem, ffn1_norm, ffn1_w_gu, ffn1_w_down, mix_norm, ev_w_in, ev_q_gain, ev_k_gain, ev_sinks, ev_w_out, od_w_in, od_q_gain, od_k_gain, od_w_out, xa_norm, xa_mem_norm, xa_w_q, xa_w_kv, xa_q_gain, xa_k_gain, xa_w_o, ffn2_norm, ffn2_w_gu, ffn2_w_down):
    x = x.reshape(N_TOK, D_MODEL)
    mem2d = mem.reshape(BATCH * MEM_LEN, D_MODEL)
    w_gu, w_down = _cast_now((ffn1_w_gu, 0)), _cast_now((ffn1_w_down, 0))
    for layer in range(DEPTH):
        j = layer // 2
        even = layer % 2 == 0
        w_in3, w_mix3 = (ev_w_in, ev_w_out) if even else (od_w_in, od_w_out)
        jobs = [(w_in3, j), (w_mix3, j), (xa_w_q, layer), (xa_w_kv, layer), (xa_w_o, layer),
                (ffn2_w_gu, layer), (ffn2_w_down, layer)]
        x, (w_in, w_mix, w_q, w_kv, w_o, w_gu, w_down) = _ffn(x, ffn1_norm[layer], w_gu, w_down, jobs)
        kv = _memory_kv(mem2d, xa_mem_norm[layer], w_kv, xa_k_gain[layer])
        cs_q = jnp.tile(xa_q_gain[layer], X_HEADS) * (X_HEAD_DIM ** -0.5 * LOG2_E)
        if even:
            p = _even_projection(x, mix_norm[layer], w_in, ev_q_gain[j], ev_k_gain[j])
            x = _even_tail(x, p, _alibi_log2(A_Q_HEADS), ev_sinks[j].astype(F32) * LOG2_E, w_mix,
                           xa_norm[layer], w_q, cs_q, kv, w_o)
        else:
            heads = _odd_mixer_heads(x, mix_norm[layer], w_in, od_q_gain[j], od_k_gain[j])
            x = _mix_xattn(x, heads, w_mix, xa_norm[layer], w_q, cs_q, kv, w_o)
        jobs = [(ffn1_w_gu, layer + 1), (ffn1_w_down, layer + 1)] if layer + 1 < DEPTH else []
        x, next_ffn1 = _ffn(x, ffn2_norm[layer], w_gu, w_down, jobs)
        if next_ffn1:
            w_gu, w_down = next_ffn1
    return x.reshape(BATCH, SEQ, D_MODEL)
```

```python
import functools

import numpy as np
import jax
import jax.numpy as jnp
from jax import lax
from jax.experimental import pallas as pl
from jax.experimental.pallas import tpu as pltpu

D_MODEL = 1024
BATCH = 4
SEQ = 4096
N_TOK = BATCH * SEQ
DEPTH = 2
HEAD_DIM = 64
BLOCK = 128
A_Q_HEADS = 8
A_KV_HEADS = 2
A_WINDOW = 128
B_HEADS = 8
C_HEADS = 16
C_PATTERNS = ((128, 1), (512, 4), (2048, 16))
MEM_LEN = 256
X_HEADS = 4
X_HEAD_DIM = D_MODEL // X_HEADS
D_FF = 2816
RMS_EPS = 1e-6

LANES = 128
MXU_N = 256
VMEM_LIMIT = 56 * 1024 * 1024

BF = jnp.bfloat16
F32 = jnp.float32
NT_DIMS = (((1,), (1,)), ((), ()))
LOG2_E = 1.4426950408889634


def _params(sem, vmem=None):
    return pltpu.CompilerParams(dimension_semantics=sem, vmem_limit_bytes=vmem)


def _resident(shape):
    nd = len(shape)
    return pl.BlockSpec(shape, lambda *_: (0,) * nd, pipeline_mode=pl.Buffered(1))


BF16_SUBLANES = 16


def _cast_specs(job, steps):
    w3, layer = job
    _, r, c = w3.shape
    rb = next(rb for rb in range(BF16_SUBLANES, r + 1, BF16_SUBLANES) if r % rb == 0 and r // rb <= steps)
    last = r // rb - 1
    return (pl.BlockSpec((None, rb, c), lambda i: (layer, jnp.minimum(i, last), 0)),
            pl.BlockSpec((rb, c), lambda i: (jnp.minimum(i, last), 0)),
            jax.ShapeDtypeStruct((r, c), BF))


def _run_cast_jobs(in_refs, out_refs):
    for src, dst in zip(in_refs, out_refs):
        dst[...] = src[...].astype(BF)


def _cast_kernel(w_ref, o_ref):
    _run_cast_jobs([w_ref], [o_ref])


def _cast_now(job, *, rows=128):
    steps = job[0].shape[1] // rows
    in_spec, out_spec, out_shape = _cast_specs(job, steps)
    return pl.pallas_call(
        _cast_kernel, grid=(steps,), in_specs=[in_spec], out_specs=out_spec, out_shape=out_shape,
        compiler_params=_params(("arbitrary",)),
        name="cast",
    )(job[0])


def _rms(xv, g):
    ms = jnp.mean(xv * xv, axis=-1, keepdims=True)
    return xv * lax.rsqrt(ms + RMS_EPS) * g


FFN_SPLIT = (D_FF // MXU_N + 1) // 2 * MXU_N
FFN_CHUNKS = ((0, FFN_SPLIT), (FFN_SPLIT, D_FF))


def _ffn_kernel(*refs, n_jobs):
    x_ref, g_ref, wgu_ref, wd_ref = refs[:4]
    o_ref = refs[4 + n_jobs]
    xv = x_ref[...]
    h = _rms(xv, g_ref[...]).astype(BF)
    acc = jnp.zeros_like(xv)
    for c0, c1 in FFN_CHUNKS:
        gate = jnp.dot(h, wgu_ref[:, c0:c1], preferred_element_type=F32)
        up = jnp.dot(h, wgu_ref[:, D_FF + c0:D_FF + c1], preferred_element_type=F32)
        act = (gate * jax.nn.sigmoid(gate) * up).astype(BF)
        acc = acc + jnp.dot(act, wd_ref[c0:c1, :], preferred_element_type=F32)
    o_ref[...] = xv + 0.5 * acc
    _run_cast_jobs(refs[4:4 + n_jobs], refs[5 + n_jobs:])


def _ffn(x, g, w_gu, w_down, cast_jobs=(), *, tm=512):
    steps = N_TOK // tm
    specs = [_cast_specs(job, steps) for job in cast_jobs]
    out = pl.pallas_call(
        functools.partial(_ffn_kernel, n_jobs=len(cast_jobs)),
        grid=(steps,),
        in_specs=[pl.BlockSpec((tm, D_MODEL), lambda i: (i, 0)),
                  _resident((1, D_MODEL)),
                  _resident(w_gu.shape),
                  _resident(w_down.shape)] + [s[0] for s in specs],
        out_specs=[pl.BlockSpec((tm, D_MODEL), lambda i: (i, 0))] + [s[1] for s in specs],
        out_shape=[jax.ShapeDtypeStruct((N_TOK, D_MODEL), F32)] + [s[2] for s in specs],
        compiler_params=_params(("arbitrary",), VMEM_LIMIT),
        name="ffn",
    )(x, g.reshape(1, D_MODEL), w_gu, w_down, *[job[0] for job in cast_jobs])
    return out[0], out[1:]


def _proj_kernel(x_ref, g_ref, w_ref, cs_ref, o_ref, *, plan, gs):
    assert gs in (HEAD_DIM, MXU_N)
    h = _rms(x_ref[...], g_ref[...]).astype(BF)
    n_chunks = len(plan) // 2
    lo = lax.broadcasted_iota(jnp.int32, (x_ref.shape[0], LANES), 1) < HEAD_DIM

    def main(j):
        return jnp.dot(h, w_ref[:, MXU_N * j:MXU_N * (j + 1)], preferred_element_type=F32)

    acc_next = main(0)
    out = 0
    for j in range(n_chunks):
        cols = slice(MXU_N * j, MXU_N * (j + 1))
        acc = acc_next
        if j + 1 < n_chunks:
            acc_next = main(j + 1)
        y = acc * cs_ref[:, cols]
        halves = plan[2 * j:2 * j + 2]
        if gs == MXU_N and any(normed for normed, _ in halves):
            inv_chunk = lax.rsqrt(jnp.mean(acc * acc, axis=1, keepdims=True) + RMS_EPS)
        for half, (normed, dup) in enumerate(halves):
            lanes = slice(LANES * half, LANES * (half + 1))
            yh = y[:, lanes]
            if normed and gs == MXU_N:
                yh = yh * inv_chunk
            elif normed:
                sq = acc[:, lanes] * acc[:, lanes]
                s_lo = jnp.sum(jnp.where(lo, sq, 0.0), axis=1, keepdims=True)
                s_hi = jnp.sum(jnp.where(lo, 0.0, sq), axis=1, keepdims=True)
                yh = yh * lax.rsqrt(jnp.where(lo, s_lo, s_hi) * (1.0 / gs) + RMS_EPS)
            if dup:
                swapped = pltpu.roll(yh, HEAD_DIM, axis=1)
                o_ref[out] = jnp.where(lo, yh, swapped).astype(BF)
                o_ref[out + 1] = jnp.where(lo, swapped, yh).astype(BF)
                out += 2
            else:
                o_ref[out] = yh.astype(BF)
                out += 1


def _proj(x, g, w, colscale, plan, gs, *, tm=1024):
    rows = x.shape[0]
    wout = w.shape[1]
    assert wout == LANES * len(plan) and len(plan) % 2 == 0
    c = sum(2 if dup else 1 for _, dup in plan)
    return pl.pallas_call(
        functools.partial(_proj_kernel, plan=tuple(plan), gs=gs),
        grid=(rows // tm,),
        in_specs=[pl.BlockSpec((tm, D_MODEL), lambda i: (i, 0)),
                  _resident((1, D_MODEL)),
                  _resident(w.shape),
                  _resident((1, wout))],
        out_specs=pl.BlockSpec((c, tm, LANES), lambda i: (0, i, 0)),
        out_shape=jax.ShapeDtypeStruct((c, rows, LANES), BF),
        compiler_params=_params(("parallel",), VMEM_LIMIT),
        name="proj",
    )(x, g.reshape(1, D_MODEL), w, colscale.reshape(1, wout).astype(F32))


def _swa_block(q_blocks, kp, kc, vp, vc, has_prev, slopes_ref, sinks_ref, *, kv_div, max_dist):
    row = lax.broadcasted_iota(jnp.int32, (BLOCK, 2 * BLOCK), 0)
    col = lax.broadcasted_iota(jnp.int32, (BLOCK, 2 * BLOCK), 1)
    dist = row + BLOCK - col
    valid = (dist >= 0) & (dist <= max_dist)
    if has_prev is not True:
        valid = valid & ((col >= BLOCK) | has_prev)
    negmask = jnp.where(valid, 0.0, -jnp.inf)
    distf = dist.astype(F32)
    lo = lax.broadcasted_iota(jnp.int32, (BLOCK, LANES), 1) < HEAD_DIM

    n_groups = len(q_blocks) // kv_div
    heads_per_group = 2 * kv_div
    scores = []
    for g in range(n_groups):
        parts = []
        for p in range(g * kv_div, (g + 1) * kv_div):
            q2 = q_blocks[p].astype(F32)
            parts += [jnp.where(lo, q2, 0.0), jnp.where(lo, 0.0, q2)]
        q_stack = jnp.concatenate(parts, axis=0).astype(BF)
        scores.append(jnp.concatenate(
            [lax.dot_general(q_stack, kp[g], NT_DIMS, preferred_element_type=F32),
             lax.dot_general(q_stack, kc[g], NT_DIMS, preferred_element_type=F32)], axis=1))
    soft = []
    for g in range(n_groups):
        res = []
        for j in range(heads_per_group):
            h = g * heads_per_group + j
            s = scores[g][j * BLOCK:(j + 1) * BLOCK] - slopes_ref[h] * distf + negmask
            m = jnp.maximum(jnp.max(s, axis=1, keepdims=True), sinks_ref[h])
            pe = jnp.exp2(s - m)
            res.append((pe.astype(BF), jnp.sum(pe, axis=1, keepdims=True) + jnp.exp2(sinks_ref[h] - m)))
        soft.append(res)
    outs = []
    for g in range(n_groups):
        pb = jnp.concatenate([r[0] for r in soft[g]], axis=0)
        pv = (jnp.dot(pb[:, :BLOCK], vp[g], preferred_element_type=F32)
              + jnp.dot(pb[:, BLOCK:], vc[g], preferred_element_type=F32))
        for jp in range(kv_div):
            o0 = pv[(2 * jp) * BLOCK:(2 * jp + 1) * BLOCK] / soft[g][2 * jp][1]
            o1 = pv[(2 * jp + 1) * BLOCK:(2 * jp + 2) * BLOCK] / soft[g][2 * jp + 1][1]
            outs.append(jnp.where(lo, o0, o1))
    return outs


DIL_ORDER = tuple(sorted(C_PATTERNS, key=lambda wd: -wd[1]))
DIL_UNROLL = 16
DIL_AHEAD = 2
DIL_BASE = 4
DIL_Q = SEQ // DIL_BASE
DIL_CONVERT_ROWS = DIL_BASE * BLOCK


def _dilated_kernel(slopes_ref, q_ref, k_ref, v_ref, o_ref, qn_s, tq_s, tk_s, tv_s, q0_s, q1_s, k_s, v_s,
                    acc_r, m_r, l_r, acc_n, m_n, l_n):
    assert all(d == 1 or d % DIL_BASE == 0 for _, d in DIL_ORDER) and DIL_ORDER[-1][1] == 1
    p = pl.program_id(1)
    lo = lax.broadcasted_iota(jnp.int32, (BLOCK, LANES), 1) < HEAD_DIM

    def convert(c, carry):
        rows = pl.ds(pl.multiple_of(c * DIL_CONVERT_ROWS, DIL_CONVERT_ROWS), DIL_CONVERT_ROWS)
        q_nat = q_ref[0, rows, :].astype(F32)
        lo_c = lax.broadcasted_iota(jnp.int32, (DIL_CONVERT_ROWS, LANES), 1) < HEAD_DIM
        qn_s[0, rows, :] = jnp.where(lo_c, q_nat, 0.0).astype(BF)
        qn_s[1, rows, :] = jnp.where(lo_c, 0.0, q_nat).astype(BF)
        tq_s[...] = q_nat
        tk_s[...] = k_ref[0, rows, :].astype(F32)
        tv_s[...] = v_ref[0, rows, :].astype(F32)
        for rho in range(DIL_BASE):
            src = pl.ds(rho, BLOCK, stride=DIL_BASE)
            dst = pl.ds(pl.multiple_of(rho * DIL_Q + c * BLOCK, BLOCK), BLOCK)
            q = tq_s[src, :]
            q0_s[dst, :] = jnp.where(lo, q, 0.0)
            q1_s[dst, :] = jnp.where(lo, 0.0, q)
            k_s[dst, :] = tk_s[src, :]
            v_s[dst, :] = tv_s[src, :]
        return carry

    lax.fori_loop(0, SEQ // DIL_CONVERT_ROWS, convert, 0)

    row = lax.broadcasted_iota(jnp.int32, (BLOCK, 2 * BLOCK), 0)
    col = lax.broadcasted_iota(jnp.int32, (BLOCK, 2 * BLOCK), 1)
    dist = row + BLOCK - col
    distf = dist.astype(F32)
    no_prev = jnp.where(col < BLOCK, -jnp.inf, 0.0)

    def bcast2(a0, a1):
        return jnp.where(lo, jnp.broadcast_to(a0, (BLOCK, LANES)), jnp.broadcast_to(a1, (BLOCK, LANES)))

    for pi, (window, d) in enumerate(DIL_ORDER):
        first, last = pi == 0, pi == len(DIL_ORDER) - 1
        natural = d == 1
        nb = SEQ // d // BLOCK
        band = (dist >= 0) & (dist <= window // d)
        bias = [jnp.where(band, (-float(d) * slopes_ref[2 * p + hh]) * distf, -jnp.inf) for hh in range(2)]
        acc_s, m_s, l_s = (acc_n, m_n, l_n) if natural else (acc_r, m_r, l_r)
        to_natural = not natural and not last and DIL_ORDER[pi + 1][1] == 1
        assert (not to_natural or d == DIL_BASE) and (not natural or first or DIL_ORDER[pi - 1][1] == DIL_BASE)
        acc_o, m_o, l_o = (acc_n, m_n, l_n) if to_natural else (acc_s, m_s, l_s)

        def rows_of(r, n, n_blocks=1, d=d, natural=natural):
            size = n_blocks * BLOCK
            if natural:
                return pl.ds(pl.multiple_of(BLOCK * n, BLOCK), size)
            inner = d // DIL_BASE
            start = (r % DIL_BASE) * DIL_Q + r // DIL_BASE + inner * BLOCK * n
            return pl.ds(start, size, stride=inner) if inner > 1 else pl.ds(pl.multiple_of(start, BLOCK), size)

        def step(it, carry, nb=nb, bias=bias, first=first, last=last, natural=natural, rows_of=rows_of,
                 acc_s=acc_s, m_s=m_s, l_s=l_s, acc_o=acc_o, m_o=m_o, l_o=l_o, to_natural=to_natural):
            assert DIL_UNROLL % nb == 0 or nb % DIL_UNROLL == 0
            load_k = (lambda rr: k_ref[0, rr, :]) if natural else (lambda rr: k_s[rr, :].astype(BF))
            load_v = (lambda rr: v_ref[0, rr, :]) if natural else (lambda rr: v_s[rr, :].astype(BF))

            def scores(u):
                t = it * DIL_UNROLL + u
                r, n = t // nb, t % nb
                prev = (u % nb != 0) if nb <= DIL_UNROLL else (True if u else None)
                rows = rows_of(r, n)
                out_rows = pl.ds(DIL_BASE * BLOCK * n + r, BLOCK, stride=DIL_BASE) if to_natural else rows
                if prev is True:
                    kv_rows = [rows_of(r, n - 1, 2)]
                elif prev is None:
                    kv_rows = [rows_of(r, jnp.maximum(n - 1, 0)), rows]
                else:
                    kv_rows = [rows]
                if natural:
                    qh = jnp.concatenate([qn_s[0, rows, :], qn_s[1, rows, :]], axis=0)
                else:
                    qh = jnp.concatenate([q0_s[rows, :], q1_s[rows, :]], axis=0).astype(BF)
                s = jnp.concatenate([lax.dot_general(qh, load_k(rr), NT_DIMS, preferred_element_type=F32)
                                     for rr in kv_rows], axis=1)
                return n, prev, rows, out_rows, kv_rows, (s[:BLOCK], s[BLOCK:])

            def softmax_pv(blk):
                n, prev, rows, out_rows, kv_rows, s = blk
                ms, ls, pes = [], [], []
                for hh in range(2):
                    sh = s[hh] + (bias[hh][:, BLOCK:] if prev is False else bias[hh])
                    if prev is None:
                        sh = sh + jnp.where(n == 0, no_prev, 0.0)
                    m = jnp.max(sh, axis=1, keepdims=True)
                    pe = jnp.exp2(sh - m)
                    ms.append(m)
                    ls.append(jnp.sum(pe, axis=1, keepdims=True))
                    pes.append(pe.astype(BF))
                pb = jnp.concatenate(pes, axis=0)
                v = jnp.concatenate([load_v(rr) for rr in kv_rows], axis=0) if len(kv_rows) > 1 else load_v(kv_rows[0])
                pv = jnp.dot(pb, v, preferred_element_type=F32)
                return rows, out_rows, ms, ls, (pv[:BLOCK], pv[BLOCK:])

            def merge(rows, out_rows, ms, ls, pv):
                m2 = bcast2(ms[0], ms[1])
                l2 = bcast2(ls[0], ls[1])
                acc2 = jnp.where(lo, pv[0], pv[1])
                if not first:
                    m_old = m_s[rows, :]
                    m_new = jnp.maximum(m_old, m2)
                    a_old, a_new = jnp.exp2(m_old - m_new), jnp.exp2(m2 - m_new)
                    l2 = a_old * l_s[rows, :] + a_new * l2
                    acc2 = a_old * acc_s[rows, :] + a_new * acc2
                    m2 = m_new
                if last:
                    o_ref[0, rows, :] = (acc2 / l2).astype(BF)
                else:
                    m_o[out_rows, :] = m2
                    l_o[out_rows, :] = l2
                    acc_o[out_rows, :] = acc2

            pending = {u: scores(u) for u in range(DIL_AHEAD)}
            done = None
            for u in range(DIL_UNROLL):
                if u + DIL_AHEAD < DIL_UNROLL:
                    pending[u + DIL_AHEAD] = scores(u + DIL_AHEAD)
                cur = softmax_pv(pending.pop(u))
                if done is not None:
                    merge(*done)
                done = cur
            merge(*done)
            return carry

        lax.fori_loop(0, SEQ // BLOCK // DIL_UNROLL, step, 0)


def _dilated(qkv, slopes):
    n_pairs = C_HEADS // 2
    seq_f32 = pltpu.VMEM((SEQ, LANES), F32)
    chunk_f32 = pltpu.VMEM((DIL_CONVERT_ROWS, LANES), F32)
    return pl.pallas_call(
        _dilated_kernel,
        grid=(BATCH, n_pairs),
        in_specs=[pl.BlockSpec(memory_space=pltpu.SMEM),
                  pl.BlockSpec((1, SEQ, LANES), lambda b, p: (p, b, 0)),
                  pl.BlockSpec((1, SEQ, LANES), lambda b, p: (n_pairs + p, b, 0)),
                  pl.BlockSpec((1, SEQ, LANES), lambda b, p: (2 * n_pairs + p, b, 0))],
        out_specs=pl.BlockSpec((1, SEQ, LANES), lambda b, p: (p, b, 0)),
        out_shape=jax.ShapeDtypeStruct((n_pairs, N_TOK, LANES), BF),
        scratch_shapes=[pltpu.VMEM((2, SEQ, LANES), BF)] + [chunk_f32] * 3 + [seq_f32] * 10,
        compiler_params=_params(("parallel", "parallel"), VMEM_LIMIT),
        name="dilated",
    )(slopes, qkv, qkv, qkv)


SB_QB = MXU_N
SB_FIRST_TILES = 2
SB_DEAD_LOG2 = -150.0


def _sb_matrix():
    idx = np.arange(SB_QB)
    return jnp.asarray(-(idx[:, None] > idx[None, :]).astype(np.float32), dtype=BF)


def _sb_unit(q2, load_k, load_v, iq, uo, fill=lambda: None):
    lo = lax.broadcasted_iota(jnp.int32, (SB_QB, LANES), 1) < HEAD_DIM
    q_stack = jnp.concatenate([jnp.where(lo, q2, 0.0), jnp.where(lo, 0.0, q2)], axis=0).astype(BF)
    rel1 = (lax.broadcasted_iota(jnp.int32, (SB_QB, SB_QB), 1)
            - lax.broadcasted_iota(jnp.int32, (SB_QB, SB_QB), 0))
    rel = jnp.concatenate([rel1, rel1], axis=0)

    def scores(first, n_tiles):
        return lax.dot_general(q_stack, load_k(first, n_tiles), NT_DIMS,
                               preferred_element_type=F32)

    def walk(first, n_tiles, carry, masked):
        z, c, o = carry
        z_next = scores(jnp.maximum(first - 1, 0), 1)
        if masked:
            fill()
        order = list(reversed(range(n_tiles)))
        ws, es, stricts, totals = {}, {}, {}, {}
        for t in order:
            zt = z[:, t * SB_QB:(t + 1) * SB_QB]
            sp = jnp.maximum(zt, 0.0) + jnp.log2(1.0 + jnp.exp2(-jnp.abs(zt)))
            es[t] = zt - sp
            if masked:
                stricts[t] = rel < (iq - first - t) * SB_QB
                sp = jnp.where(stricts[t], sp, 0.0)
            ws[t] = jnp.dot(sp.astype(BF), uo, preferred_element_type=F32)
            totals[t] = jnp.sum(sp, axis=1, keepdims=True)
        if masked:
            fill()
        parts = {}
        for t in order:
            a = jnp.exp2(es[t] + jnp.concatenate([c] * (SB_QB // LANES), axis=1) + ws[t])
            if masked:
                a = jnp.where(stricts[t], a, 0.0)
            parts[t] = a.astype(BF)
            c = c - totals[t]
        pv = jnp.dot(jnp.concatenate([parts[t] for t in range(n_tiles)], axis=1), load_v(first, n_tiles),
                     preferred_element_type=F32)
        if masked:
            fill()
        return z_next, c, o + jnp.where(lo, pv[:SB_QB], pv[SB_QB:])

    def alive(c):
        return jnp.max(c) > SB_DEAD_LOG2

    def body(state):
        g = state[0]
        z, c, o = walk(g, 1, state[2:], False)
        return g - 1, alive(c), z, c, o

    first = jnp.maximum(iq - 1, 0)
    zeros = (jnp.zeros((2 * SB_QB, LANES), F32), jnp.zeros((SB_QB, LANES), F32))
    z, c, o = walk(first, SB_FIRST_TILES, (scores(first, SB_FIRST_TILES),) + zeros, True)
    state = lax.while_loop(lambda st: (st[0] >= 0) & st[1], body, (first - 1, alive(c), z, c, o))
    return state[4]


def _mix_xattn_kernel(*refs):
    x_ref = refs[0]
    wm_ref, g_ref, wq_ref, cs_ref, kv_ref, wo_ref, o_ref = refs[-7:]
    mixed = jnp.concatenate([r[c] for r in refs[1:-7] for c in range(r.shape[0])], axis=1)
    xv = x_ref[...] + jnp.dot(mixed, wm_ref[...], preferred_element_type=F32)
    h = _rms(xv, g_ref[...]).astype(BF)
    heads = range(X_HEADS)
    cols = [slice(X_HEAD_DIM * hd, X_HEAD_DIM * (hd + 1)) for hd in heads]
    acc = [jnp.dot(h, wq_ref[:, cols[hd]], preferred_element_type=F32) for hd in heads]
    ms = [jnp.mean(acc[hd] * acc[hd], axis=1, keepdims=True) for hd in heads]
    q = [(acc[hd] * cs_ref[:, cols[hd]] * lax.rsqrt(ms[hd] + RMS_EPS)).astype(BF) for hd in heads]
    s = [lax.dot_general(q[hd], jnp.concatenate([kv_ref[2 * hd], kv_ref[2 * hd + 1]], axis=1), NT_DIMS,
                         preferred_element_type=F32) for hd in heads]
    pe, l = [], []
    for hd in heads:
        e = jnp.exp2(s[hd] - jnp.max(s[hd], axis=1, keepdims=True))
        l.append(jnp.sum(e, axis=1, keepdims=True))
        pe.append(e.astype(BF))
    v0 = 2 * X_HEADS
    pv = [jnp.dot(pe[hd], jnp.concatenate([kv_ref[v0 + 2 * hd], kv_ref[v0 + 2 * hd + 1]], axis=1),
                  preferred_element_type=F32) for hd in heads]
    o = jnp.concatenate([(pv[hd] / l[hd]).astype(BF) for hd in heads], axis=1)
    o_ref[...] = xv + jnp.dot(o, wo_ref[...], preferred_element_type=F32)


def _mix_xattn(x, mixer_heads, w_mix, g, w_q, q_colscale, kv, w_o, *, tm=1024):
    tiles_per_batch = SEQ // tm
    in_specs = [pl.BlockSpec((tm, D_MODEL), lambda i: (i, 0))]
    in_specs += [pl.BlockSpec((mh.shape[0], tm, LANES), lambda i: (0, i, 0)) for mh in mixer_heads]
    in_specs += [_resident(w_mix.shape),
                 _resident((1, D_MODEL)),
                 _resident(w_q.shape),
                 _resident((1, D_MODEL)),
                 pl.BlockSpec((4 * X_HEADS, MEM_LEN, LANES), lambda i: (0, i // tiles_per_batch, 0)),
                 _resident(w_o.shape)]
    return pl.pallas_call(
        _mix_xattn_kernel, grid=(N_TOK // tm,),
        in_specs=in_specs,
        out_specs=pl.BlockSpec((tm, D_MODEL), lambda i: (i, 0)),
        out_shape=jax.ShapeDtypeStruct((N_TOK, D_MODEL), F32),
        compiler_params=_params(("parallel",), VMEM_LIMIT),
        name="mix_xattn",
    )(x, *mixer_heads, w_mix, g.reshape(1, D_MODEL), w_q,
      q_colscale.reshape(1, D_MODEL).astype(F32), kv, w_o)


EVEN_TILE = 512


def _even_tail_kernel(slopes_ref, sinks_ref, x_ref, swq_ref, swkp_ref, swkc_ref, swvp_ref, swvc_ref,
                      sbq_ref, sbk_ref, sbv_ref, uo_ref, wm_ref, g_ref, wq_ref, cs_ref, kv_ref, wo_ref,
                      o_ref, heads_s, *, kv_div, max_dist):
    i = pl.program_id(0)
    tiles_per_seq = SEQ // EVEN_TILE
    t_in_seq = jnp.minimum(i, N_TOK // EVEN_TILE - 1) % tiles_per_seq
    wr, rd = i % 2, (i + 1) % 2
    n_pairs_a, n_pairs_b = swq_ref.shape[0], sbq_ref.shape[0]

    @pl.when(i == 0)
    def _():
        heads_s[1] = jnp.zeros(heads_s.shape[1:], BF)

    def mix_steps():
        heads = range(X_HEADS)
        cols = [slice(X_HEAD_DIM * hd, X_HEAD_DIM * (hd + 1)) for hd in heads]
        st = {"xv": [None] * X_HEADS, "q": [None] * X_HEADS, "pe": [None] * X_HEADS, "l": [None] * X_HEADS,
              "o": [None] * X_HEADS}

        def project(c):
            if c == 0:
                st["mixed"] = jnp.concatenate([heads_s[rd, k] for k in range(n_pairs_a + n_pairs_b)], axis=1)
            st["xv"][c] = x_ref[:, cols[c]] + jnp.dot(st["mixed"], wm_ref[:, cols[c]], preferred_element_type=F32)

        def q_proj(hd):
            if hd == 0:
                xv = jnp.concatenate(st["xv"], axis=1)
                st["h"] = _rms(xv, g_ref[...]).astype(BF)
            acc = jnp.dot(st["h"], wq_ref[:, cols[hd]], preferred_element_type=F32)
            ms = jnp.mean(acc * acc, axis=1, keepdims=True)
            st["q"][hd] = (acc * cs_ref[:, cols[hd]] * lax.rsqrt(ms + RMS_EPS)).astype(BF)

        def scores(hd):
            kh = jnp.concatenate([kv_ref[2 * hd], kv_ref[2 * hd + 1]], axis=1)
            s = lax.dot_general(st["q"][hd], kh, NT_DIMS, preferred_element_type=F32)
            e = jnp.exp2(s - jnp.max(s, axis=1, keepdims=True))
            st["l"][hd] = jnp.sum(e, axis=1, keepdims=True)
            st["pe"][hd] = e.astype(BF)

        def values(hd):
            v0 = 2 * X_HEADS
            vh = jnp.concatenate([kv_ref[v0 + 2 * hd], kv_ref[v0 + 2 * hd + 1]], axis=1)
            st["o"][hd] = (jnp.dot(st["pe"][hd], vh, preferred_element_type=F32) / st["l"][hd]).astype(BF)

        def out_proj(c):
            if c == 0:
                st["oc"] = jnp.concatenate(st["o"], axis=1)
            o_ref[:, cols[c]] = st["xv"][c] + jnp.dot(st["oc"], wo_ref[:, cols[c]], preferred_element_type=F32)

        return [functools.partial(f, k) for f in (project, q_proj, scores, values, out_proj) for k in heads]

    def attend_and_mix():
        chain = mix_steps()

        def fill():
            if chain:
                chain.pop(0)()

        uo = uo_ref[...]
        blocks_per_tile = EVEN_TILE // SB_QB

        def sb_unit(p, j):
            rows = slice(SB_QB * j, SB_QB * (j + 1))
            load = lambda ref: (lambda first, n: ref[p, pl.ds(pl.multiple_of(first * SB_QB, SB_QB), n * SB_QB), :])
            o = _sb_unit(sbq_ref[p, rows, :].astype(F32), load(sbk_ref), load(sbv_ref),
                         blocks_per_tile * t_in_seq + j, uo, fill)
            heads_s[wr, n_pairs_a + p, rows, :] = o.astype(BF)

        for p in range(n_pairs_b):
            for j in range(blocks_per_tile):
                sb_unit(p, j)
        while chain:
            fill()

        n_kv = swkc_ref.shape[0]
        for m in range(EVEN_TILE // BLOCK):
            rows = slice(BLOCK * m, BLOCK * (m + 1))
            if m == 0:
                kp, vp, has_prev = [swkp_ref[g] for g in range(n_kv)], [swvp_ref[g] for g in range(n_kv)], t_in_seq > 0
            else:
                prows = slice(BLOCK * (m - 1), BLOCK * m)
                kp, vp, has_prev = ([swkc_ref[g, prows, :] for g in range(n_kv)],
                                    [swvc_ref[g, prows, :] for g in range(n_kv)], True)
            outs = _swa_block([swq_ref[p, rows, :] for p in range(n_pairs_a)], kp,
                              [swkc_ref[g, rows, :] for g in range(n_kv)], vp, [swvc_ref[g, rows, :] for g in range(n_kv)],
                              has_prev, slopes_ref, sinks_ref, kv_div=kv_div, max_dist=max_dist)
            for p in range(n_pairs_a):
                heads_s[wr, p, rows, :] = outs[p].astype(BF)

    n_tiles = N_TOK // EVEN_TILE
    pl.when(i < n_tiles)(attend_and_mix)

    @pl.when(i == n_tiles)
    def _():
        for step in mix_steps():
            step()


def _even_tail(x, p, slopes, sinks, w_mix, g, w_q, q_colscale, kv, w_o):
    n_pairs_a, n_kv, n_pairs_b = A_Q_HEADS // 2, A_KV_HEADS, B_HEADS // 2
    assert p.shape[0] == n_pairs_a + 2 * n_kv + 3 * n_pairs_b and n_pairs_a == n_pairs_b == 2 * n_kv
    n_tiles = N_TOK // EVEN_TILE
    tiles_per_seq = SEQ // EVEN_TILE
    blocks_per_tile = EVEN_TILE // BLOCK
    att = lambda i: jnp.minimum(i, n_tiles - 1)
    mix = lambda i: jnp.maximum(i - 1, 0)
    tile_rows = lambda size, idx: pl.BlockSpec((size, EVEN_TILE, LANES), lambda i: (idx, att(i), 0))
    prev_block = lambda idx: pl.BlockSpec(
        (n_kv, BLOCK, LANES), lambda i: (idx, jnp.maximum(blocks_per_tile * att(i) - 1, 0), 0))
    whole_seq = lambda idx: pl.BlockSpec((n_pairs_b, SEQ, LANES), lambda i: (idx, att(i) // tiles_per_seq, 0),
                                         pipeline_mode=pl.Buffered(1))
    smem = pl.BlockSpec(memory_space=pltpu.SMEM)
    return pl.pallas_call(
        functools.partial(_even_tail_kernel, kv_div=n_pairs_a // n_kv, max_dist=A_WINDOW - 1),
        grid=(n_tiles + 1,),
        in_specs=[smem, smem,
                  pl.BlockSpec((EVEN_TILE, D_MODEL), lambda i: (mix(i), 0)),
                  tile_rows(n_pairs_a, 0), prev_block(2), tile_rows(n_kv, 2), prev_block(3), tile_rows(n_kv, 3),
                  tile_rows(n_pairs_b, 2), whole_seq(3), whole_seq(4),
                  _resident((SB_QB, SB_QB)),
                  _resident(w_mix.shape), _resident((1, D_MODEL)), _resident(w_q.shape), _resident((1, D_MODEL)),
                  pl.BlockSpec((4 * X_HEADS, MEM_LEN, LANES), lambda i: (0, mix(i) // tiles_per_seq, 0)),
                  _resident(w_o.shape)],
        out_specs=pl.BlockSpec((EVEN_TILE, D_MODEL), lambda i: (mix(i), 0)),
        out_shape=jax.ShapeDtypeStruct((N_TOK, D_MODEL), F32),
        scratch_shapes=[pltpu.VMEM((2, n_pairs_a + n_pairs_b, EVEN_TILE, LANES), BF)],
        compiler_params=_params(("arbitrary",), VMEM_LIMIT),
        name="even_tail",
    )(slopes, sinks, x, p, p, p, p, p, p, p, p, _sb_matrix(), w_mix, g.reshape(1, D_MODEL), w_q,
      q_colscale.reshape(1, D_MODEL).astype(F32), kv, w_o)


def _alibi_log2(n_heads):
    return jnp.asarray(LOG2_E * 2.0 ** (-8.0 * np.arange(1, n_heads + 1) / n_heads), dtype=F32)


def _even_projection(x, norm_g, w_in, q_gain, k_gain):
    hd = HEAD_DIM
    a_q, a_kv, b_w = A_Q_HEADS * hd, A_KV_HEADS * hd, B_HEADS * hd
    scale = hd ** -0.5 * LOG2_E
    ones = lambda n: jnp.ones((n,), F32)
    cs = jnp.concatenate([jnp.tile(q_gain, A_Q_HEADS) * scale, jnp.tile(k_gain, A_KV_HEADS), ones(a_kv),
                          ones(b_w) * scale, ones(2 * b_w)])
    plan = ([(True, False)] * (a_q // LANES) + [(True, True)] * (a_kv // LANES) + [(False, True)] * (a_kv // LANES)
            + [(False, False)] * (3 * b_w // LANES))
    return _proj(x, norm_g, w_in, cs, plan, hd)


def _odd_mixer_heads(x, norm_g, w_in, q_gain, k_gain):
    hd = HEAD_DIM
    cs = jnp.concatenate([jnp.tile(q_gain, C_HEADS) * (hd ** -0.5 * LOG2_E), jnp.tile(k_gain, C_HEADS),
                          jnp.ones((C_HEADS * hd,), F32)])
    head_blocks = C_HEADS * hd // LANES
    plan = [(True, False)] * (2 * head_blocks) + [(False, False)] * head_blocks
    p = _proj(x, norm_g, w_in, cs, plan, hd)
    return [_dilated(p, _alibi_log2(C_HEADS))]


def _memory_kv(mem2d, mem_g, w_kv, k_gain):
    cs_kv = jnp.concatenate([jnp.tile(k_gain, X_HEADS), jnp.ones((D_MODEL,), F32)])
    plan = [(True, False)] * (D_MODEL // LANES) + [(False, False)] * (D_MODEL // LANES)
    return _proj(mem2d, mem_g, w_kv, cs_kv, plan, X_HEAD_DIM, tm=MEM_LEN)


def kernel(x, mem, ffn1_norm, ffn1_w_gu, ffn1_w_down, mix_norm, ev_w_in, ev_q_gain, ev_k_gain, ev_sinks, ev_w_out, od_w_in, od_q_gain, od_k_gain, od_w_out, xa_norm, xa_mem_norm, xa_w_q, xa_w_kv, xa_q_gain, xa_k_gain, xa_w_o, ffn2_norm, ffn2_w_gu, ffn2_w_down):
    x = x.reshape(N_TOK, D_MODEL)
    mem2d = mem.reshape(BATCH * MEM_LEN, D_MODEL)
    w_gu, w_down = _cast_now((ffn1_w_gu, 0)), _cast_now((ffn1_w_down, 0))
    for layer in range(DEPTH):
        j = layer // 2
        even = layer % 2 == 0
        w_in3, w_mix3 = (ev_w_in, ev_w_out) if even else (od_w_in, od_w_out)
        jobs = [(w_in3, j), (w_mix3, j), (xa_w_q, layer), (xa_w_kv, layer), (xa_w_o, layer),
                (ffn2_w_gu, layer), (ffn2_w_down, layer)]
        x, (w_in, w_mix, w_q, w_kv, w_o, w_gu, w_down) = _ffn(x, ffn1_norm[layer], w_gu, w_down, jobs)
        kv = _memory_kv(mem2d, xa_mem_norm[layer], w_kv, xa_k_gain[layer])
        cs_q = jnp.tile(xa_q_gain[layer], X_HEADS) * (X_HEAD_DIM ** -0.5 * LOG2_E)
        if even:
            p = _even_projection(x, mix_norm[layer], w_in, ev_q_gain[j], ev_k_gain[j])
            x = _even_tail(x, p, _alibi_log2(A_Q_HEADS), ev_sinks[j].astype(F32) * LOG2_E, w_mix,
                           xa_norm[layer], w_q, cs_q, kv, w_o)
        else:
            heads = _odd_mixer_heads(x, mix_norm[layer], w_in, od_q_gain[j], od_k_gain[j])
            x = _mix_xattn(x, heads, w_mix, xa_norm[layer], w_q, cs_q, kv, w_o)
        jobs = [(ffn1_w_gu, layer + 1), (ffn1_w_down, layer + 1)] if layer + 1 < DEPTH else []
        x, next_ffn1 = _ffn(x, ffn2_norm[layer], w_gu, w_down, jobs)
        if next_ffn1:
            w_gu, w_down = next_ffn1
    return x.reshape(BATCH, SEQ, D_MODEL)
```

```python
import functools

import numpy as np
import jax
import jax.numpy as jnp
from jax import lax
from jax.experimental import pallas as pl
from jax.experimental.pallas import tpu as pltpu

D_MODEL = 1024
BATCH = 4
SEQ = 4096
N_TOK = BATCH * SEQ
DEPTH = 2
HEAD_DIM = 64
BLOCK = 128
A_Q_HEADS = 8
A_KV_HEADS = 2
A_WINDOW = 128
B_HEADS = 8
C_HEADS = 16
C_PATTERNS = ((128, 1), (512, 4), (2048, 16))
MEM_LEN = 256
X_HEADS = 4
X_HEAD_DIM = D_MODEL // X_HEADS
D_FF = 2816
RMS_EPS = 1e-6

LANES = 128
MXU_N = 256
VMEM_LIMIT = 56 * 1024 * 1024

BF = jnp.bfloat16
F32 = jnp.float32
NT_DIMS = (((1,), (1,)), ((), ()))
LOG2_E = 1.4426950408889634


def _params(sem, vmem=None):
    return pltpu.CompilerParams(dimension_semantics=sem, vmem_limit_bytes=vmem)


def _resident(shape):
    nd = len(shape)
    return pl.BlockSpec(shape, lambda *_: (0,) * nd, pipeline_mode=pl.Buffered(1))


BF16_SUBLANES = 16


def _cast_specs(job, steps):
    w3, layer = job
    _, r, c = w3.shape
    rb = next(rb for rb in range(BF16_SUBLANES, r + 1, BF16_SUBLANES) if r % rb == 0 and r // rb <= steps)
    last = r // rb - 1
    return (pl.BlockSpec((None, rb, c), lambda i: (layer, jnp.minimum(i, last), 0)),
            pl.BlockSpec((rb, c), lambda i: (jnp.minimum(i, last), 0)),
            jax.ShapeDtypeStruct((r, c), BF))


def _run_cast_jobs(in_refs, out_refs):
    for src, dst in zip(in_refs, out_refs):
        dst[...] = src[...].astype(BF)


def _cast_kernel(w_ref, o_ref):
    _run_cast_jobs([w_ref], [o_ref])


def _cast_now(job, *, rows=128):
    steps = job[0].shape[1] // rows
    in_spec, out_spec, out_shape = _cast_specs(job, steps)
    return pl.pallas_call(
        _cast_kernel, grid=(steps,), in_specs=[in_spec], out_specs=out_spec, out_shape=out_shape,
        compiler_params=_params(("arbitrary",)),
        name="cast",
    )(job[0])


def _rms(xv, g):
    ms = jnp.mean(xv * xv, axis=-1, keepdims=True)
    return xv * lax.rsqrt(ms + RMS_EPS) * g


FFN_SPLIT = (D_FF // MXU_N + 1) // 2 * MXU_N
FFN_CHUNKS = ((0, FFN_SPLIT), (FFN_SPLIT, D_FF))


def _ffn_kernel(*refs, n_jobs):
    x_ref, g_ref, wgu_ref, wd_ref = refs[:4]
    o_ref = refs[4 + n_jobs]
    xv = x_ref[...]
    h = _rms(xv, g_ref[...]).astype(BF)
    acc = jnp.zeros_like(xv)
    for c0, c1 in FFN_CHUNKS:
        gate = jnp.dot(h, wgu_ref[:, c0:c1], preferred_element_type=F32)
        up = jnp.dot(h, wgu_ref[:, D_FF + c0:D_FF + c1], preferred_element_type=F32)
        act = (gate * jax.nn.sigmoid(gate) * up).astype(BF)
        acc = acc + jnp.dot(act, wd_ref[c0:c1, :], preferred_element_type=F32)
    o_ref[...] = xv + 0.5 * acc
    _run_cast_jobs(refs[4:4 + n_jobs], refs[5 + n_jobs:])


def _ffn(x, g, w_gu, w_down, cast_jobs=(), *, tm=512):
    steps = N_TOK // tm
    specs = [_cast_specs(job, steps) for job in cast_jobs]
    out = pl.pallas_call(
        functools.partial(_ffn_kernel, n_jobs=len(cast_jobs)),
        grid=(steps,),
        in_specs=[pl.BlockSpec((tm, D_MODEL), lambda i: (i, 0)),
                  _resident((1, D_MODEL)),
                  _resident(w_gu.shape),
                  _resident(w_down.shape)] + [s[0] for s in specs],
        out_specs=[pl.BlockSpec((tm, D_MODEL), lambda i: (i, 0))] + [s[1] for s in specs],
        out_shape=[jax.ShapeDtypeStruct((N_TOK, D_MODEL), F32)] + [s[2] for s in specs],
        compiler_params=_params(("arbitrary",), VMEM_LIMIT),
        name="ffn",
    )(x, g.reshape(1, D_MODEL), w_gu, w_down, *[job[0] for job in cast_jobs])
    return out[0], out[1:]


def _proj_kernel(x_ref, g_ref, w_ref, cs_ref, o_ref, *, plan, gs):
    assert gs in (HEAD_DIM, MXU_N)
    h = _rms(x_ref[...], g_ref[...]).astype(BF)
    n_chunks = len(plan) // 2
    lo = lax.broadcasted_iota(jnp.int32, (x_ref.shape[0], LANES), 1) < HEAD_DIM

    def main(j):
        return jnp.dot(h, w_ref[:, MXU_N * j:MXU_N * (j + 1)], preferred_element_type=F32)

    acc_next = main(0)
    out = 0
    for j in range(n_chunks):
        cols = slice(MXU_N * j, MXU_N * (j + 1))
        acc = acc_next
        if j + 1 < n_chunks:
            acc_next = main(j + 1)
        y = acc * cs_ref[:, cols]
        halves = plan[2 * j:2 * j + 2]
        if gs == MXU_N and any(normed for normed, _ in halves):
            inv_chunk = lax.rsqrt(jnp.mean(acc * acc, axis=1, keepdims=True) + RMS_EPS)
        for half, (normed, dup) in enumerate(halves):
            lanes = slice(LANES * half, LANES * (half + 1))
            yh = y[:, lanes]
            if normed and gs == MXU_N:
                yh = yh * inv_chunk
            elif normed:
                sq = acc[:, lanes] * acc[:, lanes]
                s_lo = jnp.sum(jnp.where(lo, sq, 0.0), axis=1, keepdims=True)
                s_hi = jnp.sum(jnp.where(lo, 0.0, sq), axis=1, keepdims=True)
                yh = yh * lax.rsqrt(jnp.where(lo, s_lo, s_hi) * (1.0 / gs) + RMS_EPS)
            if dup:
                swapped = pltpu.roll(yh, HEAD_DIM, axis=1)
                o_ref[out] = jnp.where(lo, yh, swapped).astype(BF)
                o_ref[out + 1] = jnp.where(lo, swapped, yh).astype(BF)
                out += 2
            else:
                o_ref[out] = yh.astype(BF)
                out += 1


def _proj(x, g, w, colscale, plan, gs, *, tm=1024):
    rows = x.shape[0]
    wout = w.shape[1]
    assert wout == LANES * len(plan) and len(plan) % 2 == 0
    c = sum(2 if dup else 1 for _, dup in plan)
    return pl.pallas_call(
        functools.partial(_proj_kernel, plan=tuple(plan), gs=gs),
        grid=(rows // tm,),
        in_specs=[pl.BlockSpec((tm, D_MODEL), lambda i: (i, 0)),
                  _resident((1, D_MODEL)),
                  _resident(w.shape),
                  _resident((1, wout))],
        out_specs=pl.BlockSpec((c, tm, LANES), lambda i: (0, i, 0)),
        out_shape=jax.ShapeDtypeStruct((c, rows, LANES), BF),
        compiler_params=_params(("parallel",), VMEM_LIMIT),
        name="proj",
    )(x, g.reshape(1, D_MODEL), w, colscale.reshape(1, wout).astype(F32))


def _swa_block(q_blocks, kp, kc, vp, vc, has_prev, slopes_ref, sinks_ref, *, kv_div, max_dist):
    row = lax.broadcasted_iota(jnp.int32, (BLOCK, 2 * BLOCK), 0)
    col = lax.broadcasted_iota(jnp.int32, (BLOCK, 2 * BLOCK), 1)
    dist = row + BLOCK - col
    valid = (dist >= 0) & (dist <= max_dist)
    if has_prev is not True:
        valid = valid & ((col >= BLOCK) | has_prev)
    negmask = jnp.where(valid, 0.0, -jnp.inf)
    distf = dist.astype(F32)
    lo = lax.broadcasted_iota(jnp.int32, (BLOCK, LANES), 1) < HEAD_DIM

    n_groups = len(q_blocks) // kv_div
    heads_per_group = 2 * kv_div
    scores = []
    for g in range(n_groups):
        parts = []
        for p in range(g * kv_div, (g + 1) * kv_div):
            q2 = q_blocks[p].astype(F32)
            parts += [jnp.where(lo, q2, 0.0), jnp.where(lo, 0.0, q2)]
        q_stack = jnp.concatenate(parts, axis=0).astype(BF)
        scores.append(jnp.concatenate(
            [lax.dot_general(q_stack, kp[g], NT_DIMS, preferred_element_type=F32),
             lax.dot_general(q_stack, kc[g], NT_DIMS, preferred_element_type=F32)], axis=1))
    soft = []
    for g in range(n_groups):
        res = []
        for j in range(heads_per_group):
            h = g * heads_per_group + j
            s = scores[g][j * BLOCK:(j + 1) * BLOCK] - slopes_ref[h] * distf + negmask
            m = jnp.maximum(jnp.max(s, axis=1, keepdims=True), sinks_ref[h])
            pe = jnp.exp2(s - m)
            res.append((pe.astype(BF), jnp.sum(pe, axis=1, keepdims=True) + jnp.exp2(sinks_ref[h] - m)))
        soft.append(res)
    outs = []
    for g in range(n_groups):
        pb = jnp.concatenate([r[0] for r in soft[g]], axis=0)
        pv = (jnp.dot(pb[:, :BLOCK], vp[g], preferred_element_type=F32)
              + jnp.dot(pb[:, BLOCK:], vc[g], preferred_element_type=F32))
        for jp in range(kv_div):
            o0 = pv[(2 * jp) * BLOCK:(2 * jp + 1) * BLOCK] / soft[g][2 * jp][1]
            o1 = pv[(2 * jp + 1) * BLOCK:(2 * jp + 2) * BLOCK] / soft[g][2 * jp + 1][1]
            outs.append(jnp.where(lo, o0, o1))
    return outs


DIL_ORDER = tuple(sorted(C_PATTERNS, key=lambda wd: -wd[1]))
DIL_UNROLL = 16
DIL_AHEAD = 2
DIL_BASE = 4
DIL_Q = SEQ // DIL_BASE
DIL_CONVERT_ROWS = DIL_BASE * BLOCK


def _dilated_kernel(slopes_ref, q_ref, k_ref, v_ref, o_ref, qn_s, tq_s, tk_s, tv_s, q0_s, q1_s, k_s, v_s,
                    acc_r, m_r, l_r, acc_n, m_n, l_n):
    assert all(d == 1 or d % DIL_BASE == 0 for _, d in DIL_ORDER) and DIL_ORDER[-1][1] == 1
    p = pl.program_id(1)
    lo = lax.broadcasted_iota(jnp.int32, (BLOCK, LANES), 1) < HEAD_DIM

    def convert(c, carry):
        rows = pl.ds(pl.multiple_of(c * DIL_CONVERT_ROWS, DIL_CONVERT_ROWS), DIL_CONVERT_ROWS)
        q_nat = q_ref[0, rows, :].astype(F32)
        lo_c = lax.broadcasted_iota(jnp.int32, (DIL_CONVERT_ROWS, LANES), 1) < HEAD_DIM
        qn_s[0, rows, :] = jnp.where(lo_c, q_nat, 0.0).astype(BF)
        qn_s[1, rows, :] = jnp.where(lo_c, 0.0, q_nat).astype(BF)
        tq_s[...] = q_nat
        tk_s[...] = k_ref[0, rows, :].astype(F32)
        tv_s[...] = v_ref[0, rows, :].astype(F32)
        for rho in range(DIL_BASE):
            src = pl.ds(rho, BLOCK, stride=DIL_BASE)
            dst = pl.ds(pl.multiple_of(rho * DIL_Q + c * BLOCK, BLOCK), BLOCK)
            q = tq_s[src, :]
            q0_s[dst, :] = jnp.where(lo, q, 0.0)
            q1_s[dst, :] = jnp.where(lo, 0.0, q)
            k_s[dst, :] = tk_s[src, :]
            v_s[dst, :] = tv_s[src, :]
        return carry

    lax.fori_loop(0, SEQ // DIL_CONVERT_ROWS, convert, 0)

    row = lax.broadcasted_iota(jnp.int32, (BLOCK, 2 * BLOCK), 0)
    col = lax.broadcasted_iota(jnp.int32, (BLOCK, 2 * BLOCK), 1)
    dist = row + BLOCK - col
    distf = dist.astype(F32)
    no_prev = jnp.where(col < BLOCK, -jnp.inf, 0.0)

    def bcast2(a0, a1):
        return jnp.where(lo, jnp.broadcast_to(a0, (BLOCK, LANES)), jnp.broadcast_to(a1, (BLOCK, LANES)))

    for pi, (window, d) in enumerate(DIL_ORDER):
        first, last = pi == 0, pi == len(DIL_ORDER) - 1
        natural = d == 1
        nb = SEQ // d // BLOCK
        band = (dist >= 0) & (dist <= window // d)
        bias = [jnp.where(band, (-float(d) * slopes_ref[2 * p + hh]) * distf, -jnp.inf) for hh in range(2)]
        acc_s, m_s, l_s = (acc_n, m_n, l_n) if natural else (acc_r, m_r, l_r)
        to_natural = not natural and not last and DIL_ORDER[pi + 1][1] == 1
        assert (not to_natural or d == DIL_BASE) and (not natural or first or DIL_ORDER[pi - 1][1] == DIL_BASE)
        acc_o, m_o, l_o = (acc_n, m_n, l_n) if to_natural else (acc_s, m_s, l_s)

        def rows_of(r, n, n_blocks=1, d=d, natural=natural):
            size = n_blocks * BLOCK
            if natural:
                return pl.ds(pl.multiple_of(BLOCK * n, BLOCK), size)
            inner = d // DIL_BASE
            start = (r % DIL_BASE) * DIL_Q + r // DIL_BASE + inner * BLOCK * n
            return pl.ds(start, size, stride=inner) if inner > 1 else pl.ds(pl.multiple_of(start, BLOCK), size)

        def step(it, carry, nb=nb, bias=bias, first=first, last=last, natural=natural, rows_of=rows_of,
                 acc_s=acc_s, m_s=m_s, l_s=l_s, acc_o=acc_o, m_o=m_o, l_o=l_o, to_natural=to_natural):
            assert DIL_UNROLL % nb == 0 or nb % DIL_UNROLL == 0
            load_k = (lambda rr: k_ref[0, rr, :]) if natural else (lambda rr: k_s[rr, :].astype(BF))
            load_v = (lambda rr: v_ref[0, rr, :]) if natural else (lambda rr: v_s[rr, :].astype(BF))

            def scores(u):
                t = it * DIL_UNROLL + u
                r, n = t // nb, t % nb
                prev = (u % nb != 0) if nb <= DIL_UNROLL else (True if u else None)
                rows = rows_of(r, n)
                out_rows = pl.ds(DIL_BASE * BLOCK * n + r, BLOCK, stride=DIL_BASE) if to_natural else rows
                if prev is True:
                    kv_rows = [rows_of(r, n - 1, 2)]
                elif prev is None:
                    kv_rows = [rows_of(r, jnp.maximum(n - 1, 0)), rows]
                else:
                    kv_rows = [rows]
                if natural:
                    qh = jnp.concatenate([qn_s[0, rows, :], qn_s[1, rows, :]], axis=0)
                else:
                    qh = jnp.concatenate([q0_s[rows, :], q1_s[rows, :]], axis=0).astype(BF)
                s = jnp.concatenate([lax.dot_general(qh, load_k(rr), NT_DIMS, preferred_element_type=F32)
                                     for rr in kv_rows], axis=1)
                return n, prev, rows, out_rows, kv_rows, (s[:BLOCK], s[BLOCK:])

            def softmax_pv(blk):
                n, prev, rows, out_rows, kv_rows, s = blk
                ms, ls, pes = [], [], []
                for hh in range(2):
                    sh = s[hh] + (bias[hh][:, BLOCK:] if prev is False else bias[hh])
                    if prev is None:
                        sh = sh + jnp.where(n == 0, no_prev, 0.0)
                    m = jnp.max(sh, axis=1, keepdims=True)
                    pe = jnp.exp2(sh - m)
                    ms.append(m)
                    ls.append(jnp.sum(pe, axis=1, keepdims=True))
                    pes.append(pe.astype(BF))
                pb = jnp.concatenate(pes, axis=0)
                v = jnp.concatenate([load_v(rr) for rr in kv_rows], axis=0) if len(kv_rows) > 1 else load_v(kv_rows[0])
                pv = jnp.dot(pb, v, preferred_element_type=F32)
                return rows, out_rows, ms, ls, (pv[:BLOCK], pv[BLOCK:])

            def merge(rows, out_rows, ms, ls, pv):
                m2 = bcast2(ms[0], ms[1])
                l2 = bcast2(ls[0], ls[1])
                acc2 = jnp.where(lo, pv[0], pv[1])
                if not first:
                    m_old = m_s[rows, :]
                    m_new = jnp.maximum(m_old, m2)
                    a_old, a_new = jnp.exp2(m_old - m_new), jnp.exp2(m2 - m_new)
                    l2 = a_old * l_s[rows, :] + a_new * l2
                    acc2 = a_old * acc_s[rows, :] + a_new * acc2
                    m2 = m_new
                if last:
                    o_ref[0, rows, :] = (acc2 / l2).astype(BF)
                else:
                    m_o[out_rows, :] = m2
                    l_o[out_rows, :] = l2
                    acc_o[out_rows, :] = acc2

            pending = {u: scores(u) for u in range(DIL_AHEAD)}
            done = None
            for u in range(DIL_UNROLL):
                if u + DIL_AHEAD < DIL_UNROLL:
                    pending[u + DIL_AHEAD] = scores(u + DIL_AHEAD)
                cur = softmax_pv(pending.pop(u))
                if done is not None:
                    merge(*done)
                done = cur
            merge(*done)
            return carry

        lax.fori_loop(0, SEQ // BLOCK // DIL_UNROLL, step, 0)


def _dilated(qkv, slopes):
    n_pairs = C_HEADS // 2
    seq_f32 = pltpu.VMEM((SEQ, LANES), F32)
    chunk_f32 = pltpu.VMEM((DIL_CONVERT_ROWS, LANES), F32)
    return pl.pallas_call(
        _dilated_kernel,
        grid=(BATCH, n_pairs),
        in_specs=[pl.BlockSpec(memory_space=pltpu.SMEM),
                  pl.BlockSpec((1, SEQ, LANES), lambda b, p: (p, b, 0)),
                  pl.BlockSpec((1, SEQ, LANES), lambda b, p: (n_pairs + p, b, 0)),
                  pl.BlockSpec((1, SEQ, LANES), lambda b, p: (2 * n_pairs + p, b, 0))],
        out_specs=pl.BlockSpec((1, SEQ, LANES), lambda b, p: (p, b, 0)),
        out_shape=jax.ShapeDtypeStruct((n_pairs, N_TOK, LANES), BF),
        scratch_shapes=[pltpu.VMEM((2, SEQ, LANES), BF)] + [chunk_f32] * 3 + [seq_f32] * 10,
        compiler_params=_params(("parallel", "parallel"), VMEM_LIMIT),
        name="dilated",
    )(slopes, qkv, qkv, qkv)


SB_QB = MXU_N
SB_FIRST_TILES = 2
SB_DEAD_LOG2 = -150.0


def _sb_matrix():
    idx = np.arange(SB_QB)
    return jnp.asarray(-(idx[:, None] > idx[None, :]).astype(np.float32), dtype=BF)


def _sb_unit(q2, load_k, load_v, iq, uo, fill=lambda: None):
    lo = lax.broadcasted_iota(jnp.int32, (SB_QB, LANES), 1) < HEAD_DIM
    q_stack = jnp.concatenate([jnp.where(lo, q2, 0.0), jnp.where(lo, 0.0, q2)], axis=0).astype(BF)
    rel1 = (lax.broadcasted_iota(jnp.int32, (SB_QB, SB_QB), 1)
            - lax.broadcasted_iota(jnp.int32, (SB_QB, SB_QB), 0))
    rel = jnp.concatenate([rel1, rel1], axis=0)

    def scores(first, n_tiles):
        return lax.dot_general(q_stack, load_k(first, n_tiles), NT_DIMS,
                               preferred_element_type=F32)

    def walk(first, n_tiles, carry, masked):
        z, c, o = carry
        z_next = scores(jnp.maximum(first - 1, 0), 1)
        if masked:
            fill()
        order = list(reversed(range(n_tiles)))
        ws, es, stricts, totals = {}, {}, {}, {}
        for t in order:
            zt = z[:, t * SB_QB:(t + 1) * SB_QB]
            sp = jnp.maximum(zt, 0.0) + jnp.log2(1.0 + jnp.exp2(-jnp.abs(zt)))
            es[t] = zt - sp
            if masked:
                stricts[t] = rel < (iq - first - t) * SB_QB
                sp = jnp.where(stricts[t], sp, 0.0)
            ws[t] = jnp.dot(sp.astype(BF), uo, preferred_element_type=F32)
            totals[t] = jnp.sum(sp, axis=1, keepdims=True)
        if masked:
            fill()
        parts = {}
        for t in order:
            a = jnp.exp2(es[t] + jnp.concatenate([c] * (SB_QB // LANES), axis=1) + ws[t])
            if masked:
                a = jnp.where(stricts[t], a, 0.0)
            parts[t] = a.astype(BF)
            c = c - totals[t]
        pv = jnp.dot(jnp.concatenate([parts[t] for t in range(n_tiles)], axis=1), load_v(first, n_tiles),
                     preferred_element_type=F32)
        if masked:
            fill()
        return z_next, c, o + jnp.where(lo, pv[:SB_QB], pv[SB_QB:])

    def alive(c):
        return jnp.max(c) > SB_DEAD_LOG2

    def body(state):
        g = state[0]
        z, c, o = walk(g, 1, state[2:], False)
        return g - 1, alive(c), z, c, o

    first = jnp.maximum(iq - 1, 0)
    zeros = (jnp.zeros((2 * SB_QB, LANES), F32), jnp.zeros((SB_QB, LANES), F32))
    z, c, o = walk(first, SB_FIRST_TILES, (scores(first, SB_FIRST_TILES),) + zeros, True)

    def finish():
        state = lax.while_loop(lambda st: (st[0] >= 0) & st[1], body, (first - 1, alive(c), z, c, o))
        return state[4]

    return finish


def _mix_xattn_kernel(*refs):
    x_ref = refs[0]
    wm_ref, g_ref, wq_ref, cs_ref, kv_ref, wo_ref, o_ref = refs[-7:]
    mixed = jnp.concatenate([r[c] for r in refs[1:-7] for c in range(r.shape[0])], axis=1)
    xv = x_ref[...] + jnp.dot(mixed, wm_ref[...], preferred_element_type=F32)
    h = _rms(xv, g_ref[...]).astype(BF)
    heads = range(X_HEADS)
    cols = [slice(X_HEAD_DIM * hd, X_HEAD_DIM * (hd + 1)) for hd in heads]
    acc = [jnp.dot(h, wq_ref[:, cols[hd]], preferred_element_type=F32) for hd in heads]
    ms = [jnp.mean(acc[hd] * acc[hd], axis=1, keepdims=True) for hd in heads]
    q = [(acc[hd] * cs_ref[:, cols[hd]] * lax.rsqrt(ms[hd] + RMS_EPS)).astype(BF) for hd in heads]
    s = [lax.dot_general(q[hd], jnp.concatenate([kv_ref[2 * hd], kv_ref[2 * hd + 1]], axis=1), NT_DIMS,
                         preferred_element_type=F32) for hd in heads]
    pe, l = [], []
    for hd in heads:
        e = jnp.exp2(s[hd] - jnp.max(s[hd], axis=1, keepdims=True))
        l.append(jnp.sum(e, axis=1, keepdims=True))
        pe.append(e.astype(BF))
    v0 = 2 * X_HEADS
    pv = [jnp.dot(pe[hd], jnp.concatenate([kv_ref[v0 + 2 * hd], kv_ref[v0 + 2 * hd + 1]], axis=1),
                  preferred_element_type=F32) for hd in heads]
    o = jnp.concatenate([(pv[hd] / l[hd]).astype(BF) for hd in heads], axis=1)
    o_ref[...] = xv + jnp.dot(o, wo_ref[...], preferred_element_type=F32)


def _mix_xattn(x, mixer_heads, w_mix, g, w_q, q_colscale, kv, w_o, *, tm=1024):
    tiles_per_batch = SEQ // tm
    in_specs = [pl.BlockSpec((tm, D_MODEL), lambda i: (i, 0))]
    in_specs += [pl.BlockSpec((mh.shape[0], tm, LANES), lambda i: (0, i, 0)) for mh in mixer_heads]
    in_specs += [_resident(w_mix.shape),
                 _resident((1, D_MODEL)),
                 _resident(w_q.shape),
                 _resident((1, D_MODEL)),
                 pl.BlockSpec((4 * X_HEADS, MEM_LEN, LANES), lambda i: (0, i // tiles_per_batch, 0)),
                 _resident(w_o.shape)]
    return pl.pallas_call(
        _mix_xattn_kernel, grid=(N_TOK // tm,),
        in_specs=in_specs,
        out_specs=pl.BlockSpec((tm, D_MODEL), lambda i: (i, 0)),
        out_shape=jax.ShapeDtypeStruct((N_TOK, D_MODEL), F32),
        compiler_params=_params(("parallel",), VMEM_LIMIT),
        name="mix_xattn",
    )(x, *mixer_heads, w_mix, g.reshape(1, D_MODEL), w_q,
      q_colscale.reshape(1, D_MODEL).astype(F32), kv, w_o)


EVEN_TILE = 512


def _even_tail_kernel(slopes_ref, sinks_ref, x_ref, swq_ref, swkp_ref, swkc_ref, swvp_ref, swvc_ref,
                      sbq_ref, sbk_ref, sbv_ref, uo_ref, wm_ref, g_ref, wq_ref, cs_ref, kv_ref, wo_ref,
                      o_ref, heads_s, *, kv_div, max_dist):
    i = pl.program_id(0)
    tiles_per_seq = SEQ // EVEN_TILE
    t_in_seq = jnp.minimum(i, N_TOK // EVEN_TILE - 1) % tiles_per_seq
    wr, rd = i % 2, (i + 1) % 2
    n_pairs_a, n_pairs_b = swq_ref.shape[0], sbq_ref.shape[0]

    @pl.when(i == 0)
    def _():
        heads_s[1] = jnp.zeros(heads_s.shape[1:], BF)

    def mix_steps():
        heads = range(X_HEADS)
        cols = [slice(X_HEAD_DIM * hd, X_HEAD_DIM * (hd + 1)) for hd in heads]
        st = {"xv": [None] * X_HEADS, "q": [None] * X_HEADS, "pe": [None] * X_HEADS, "l": [None] * X_HEADS,
              "o": [None] * X_HEADS}

        def project(c):
            if c == 0:
                st["mixed"] = jnp.concatenate([heads_s[rd, k] for k in range(n_pairs_a + n_pairs_b)], axis=1)
            st["xv"][c] = x_ref[:, cols[c]] + jnp.dot(st["mixed"], wm_ref[:, cols[c]], preferred_element_type=F32)

        def q_proj(hd):
            if hd == 0:
                xv = jnp.concatenate(st["xv"], axis=1)
                st["h"] = _rms(xv, g_ref[...]).astype(BF)
            acc = jnp.dot(st["h"], wq_ref[:, cols[hd]], preferred_element_type=F32)
            ms = jnp.mean(acc * acc, axis=1, keepdims=True)
            st["q"][hd] = (acc * cs_ref[:, cols[hd]] * lax.rsqrt(ms + RMS_EPS)).astype(BF)

        def scores(hd):
            kh = jnp.concatenate([kv_ref[2 * hd], kv_ref[2 * hd + 1]], axis=1)
            s = lax.dot_general(st["q"][hd], kh, NT_DIMS, preferred_element_type=F32)
            e = jnp.exp2(s - jnp.max(s, axis=1, keepdims=True))
            st["l"][hd] = jnp.sum(e, axis=1, keepdims=True)
            st["pe"][hd] = e.astype(BF)

        def values(hd):
            v0 = 2 * X_HEADS
            vh = jnp.concatenate([kv_ref[v0 + 2 * hd], kv_ref[v0 + 2 * hd + 1]], axis=1)
            st["o"][hd] = (jnp.dot(st["pe"][hd], vh, preferred_element_type=F32) / st["l"][hd]).astype(BF)

        def out_proj(c):
            if c == 0:
                st["oc"] = jnp.concatenate(st["o"], axis=1)
            o_ref[:, cols[c]] = st["xv"][c] + jnp.dot(st["oc"], wo_ref[:, cols[c]], preferred_element_type=F32)

        return [functools.partial(f, k) for f in (project, q_proj, scores, values, out_proj) for k in heads]

    def attend_and_mix():
        chain = mix_steps()

        def fill():
            if chain:
                chain.pop(0)()

        uo = uo_ref[...]
        blocks_per_tile = EVEN_TILE // SB_QB

        def sb_first_pass(p, j):
            rows = slice(SB_QB * j, SB_QB * (j + 1))
            load = lambda ref: (lambda first, n: ref[p, pl.ds(pl.multiple_of(first * SB_QB, SB_QB), n * SB_QB), :])
            return _sb_unit(sbq_ref[p, rows, :].astype(F32), load(sbk_ref), load(sbv_ref),
                            blocks_per_tile * t_in_seq + j, uo, fill)

        units = [(p, j) for p in range(n_pairs_b) for j in range(blocks_per_tile)]
        finishers = [sb_first_pass(p, j) for p, j in units]
        while chain:
            fill()

        n_kv = swkc_ref.shape[0]
        for m in range(EVEN_TILE // BLOCK):
            rows = slice(BLOCK * m, BLOCK * (m + 1))
            if m == 0:
                kp, vp, has_prev = [swkp_ref[g] for g in range(n_kv)], [swvp_ref[g] for g in range(n_kv)], t_in_seq > 0
            else:
                prows = slice(BLOCK * (m - 1), BLOCK * m)
                kp, vp, has_prev = ([swkc_ref[g, prows, :] for g in range(n_kv)],
                                    [swvc_ref[g, prows, :] for g in range(n_kv)], True)
            outs = _swa_block([swq_ref[p, rows, :] for p in range(n_pairs_a)], kp,
                              [swkc_ref[g, rows, :] for g in range(n_kv)], vp, [swvc_ref[g, rows, :] for g in range(n_kv)],
                              has_prev, slopes_ref, sinks_ref, kv_div=kv_div, max_dist=max_dist)
            for p in range(n_pairs_a):
                heads_s[wr, p, rows, :] = outs[p].astype(BF)

        for (p, j), finish in zip(units, finishers):
            heads_s[wr, n_pairs_a + p, SB_QB * j:SB_QB * (j + 1), :] = finish().astype(BF)

    n_tiles = N_TOK // EVEN_TILE
    pl.when(i < n_tiles)(attend_and_mix)

    @pl.when(i == n_tiles)
    def _():
        for step in mix_steps():
            step()


def _even_tail(x, p, slopes, sinks, w_mix, g, w_q, q_colscale, kv, w_o):
    n_pairs_a, n_kv, n_pairs_b = A_Q_HEADS // 2, A_KV_HEADS, B_HEADS // 2
    assert p.shape[0] == n_pairs_a + 2 * n_kv + 3 * n_pairs_b and n_pairs_a == n_pairs_b == 2 * n_kv
    n_tiles = N_TOK // EVEN_TILE
    tiles_per_seq = SEQ // EVEN_TILE
    blocks_per_tile = EVEN_TILE // BLOCK
    att = lambda i: jnp.minimum(i, n_tiles - 1)
    mix = lambda i: jnp.maximum(i - 1, 0)
    tile_rows = lambda size, idx: pl.BlockSpec((size, EVEN_TILE, LANES), lambda i: (idx, att(i), 0))
    prev_block = lambda idx: pl.BlockSpec(
        (n_kv, BLOCK, LANES), lambda i: (idx, jnp.maximum(blocks_per_tile * att(i) - 1, 0), 0))
    whole_seq = lambda idx: pl.BlockSpec((n_pairs_b, SEQ, LANES), lambda i: (idx, att(i) // tiles_per_seq, 0),
                                         pipeline_mode=pl.Buffered(1))
    smem = pl.BlockSpec(memory_space=pltpu.SMEM)
    return pl.pallas_call(
        functools.partial(_even_tail_kernel, kv_div=n_pairs_a // n_kv, max_dist=A_WINDOW - 1),
        grid=(n_tiles + 1,),
        in_specs=[smem, smem,
                  pl.BlockSpec((EVEN_TILE, D_MODEL), lambda i: (mix(i), 0)),
                  tile_rows(n_pairs_a, 0), prev_block(2), tile_rows(n_kv, 2), prev_block(3), tile_rows(n_kv, 3),
                  tile_rows(n_pairs_b, 2), whole_seq(3), whole_seq(4),
                  _resident((SB_QB, SB_QB)),
                  _resident(w_mix.shape), _resident((1, D_MODEL)), _resident(w_q.shape), _resident((1, D_MODEL)),
                  pl.BlockSpec((4 * X_HEADS, MEM_LEN, LANES), lambda i: (0, mix(i) // tiles_per_seq, 0)),
                  _resident(w_o.shape)],
        out_specs=pl.BlockSpec((EVEN_TILE, D_MODEL), lambda i: (mix(i), 0)),
        out_shape=jax.ShapeDtypeStruct((N_TOK, D_MODEL), F32),
        scratch_shapes=[pltpu.VMEM((2, n_pairs_a + n_pairs_b, EVEN_TILE, LANES), BF)],
        compiler_params=_params(("arbitrary",), VMEM_LIMIT),
        name="even_tail",
    )(slopes, sinks, x, p, p, p, p, p, p, p, p, _sb_matrix(), w_mix, g.reshape(1, D_MODEL), w_q,
      q_colscale.reshape(1, D_MODEL).astype(F32), kv, w_o)


def _alibi_log2(n_heads):
    return jnp.asarray(LOG2_E * 2.0 ** (-8.0 * np.arange(1, n_heads + 1) / n_heads), dtype=F32)


def _even_projection(x, norm_g, w_in, q_gain, k_gain):
    hd = HEAD_DIM
    a_q, a_kv, b_w = A_Q_HEADS * hd, A_KV_HEADS * hd, B_HEADS * hd
    scale = hd ** -0.5 * LOG2_E
    ones = lambda n: jnp.ones((n,), F32)
    cs = jnp.concatenate([jnp.tile(q_gain, A_Q_HEADS) * scale, jnp.tile(k_gain, A_KV_HEADS), ones(a_kv),
                          ones(b_w) * scale, ones(2 * b_w)])
    plan = ([(True, False)] * (a_q // LANES) + [(True, True)] * (a_kv // LANES) + [(False, True)] * (a_kv // LANES)
            + [(False, False)] * (3 * b_w // LANES))
    return _proj(x, norm_g, w_in, cs, plan, hd)


def _odd_mixer_heads(x, norm_g, w_in, q_gain, k_gain):
    hd = HEAD_DIM
    cs = jnp.concatenate([jnp.tile(q_gain, C_HEADS) * (hd ** -0.5 * LOG2_E), jnp.tile(k_gain, C_HEADS),
                          jnp.ones((C_HEADS * hd,), F32)])
    head_blocks = C_HEADS * hd // LANES
    plan = [(True, False)] * (2 * head_blocks) + [(False, False)] * head_blocks
    p = _proj(x, norm_g, w_in, cs, plan, hd)
    return [_dilated(p, _alibi_log2(C_HEADS))]


def _memory_kv(mem2d, mem_g, w_kv, k_gain):
    cs_kv = jnp.concatenate([jnp.tile(k_gain, X_HEADS), jnp.ones((D_MODEL,), F32)])
    plan = [(True, False)] * (D_MODEL // LANES) + [(False, False)] * (D_MODEL // LANES)
    return _proj(mem2d, mem_g, w_kv, cs_kv, plan, X_HEAD_DIM, tm=MEM_LEN)


def kernel(x, mem, ffn1_norm, ffn1_w_gu, ffn1_w_down, mix_norm, ev_w_in, ev_q_gain, ev_k_gain, ev_sinks, ev_w_out, od_w_in, od_q_gain, od_k_gain, od_w_out, xa_norm, xa_mem_norm, xa_w_q, xa_w_kv, xa_q_gain, xa_k_gain, xa_w_o, ffn2_norm, ffn2_w_gu, ffn2_w_down):
    x = x.reshape(N_TOK, D_MODEL)
    mem2d = mem.reshape(BATCH * MEM_LEN, D_MODEL)
    w_gu, w_down = _cast_now((ffn1_w_gu, 0)), _cast_now((ffn1_w_down, 0))
    for layer in range(DEPTH):
        j = layer // 2
        even = layer % 2 == 0
        w_in3, w_mix3 = (ev_w_in, ev_w_out) if even else (od_w_in, od_w_out)
        jobs = [(w_in3, j), (w_mix3, j), (xa_w_q, layer), (xa_w_kv, layer), (xa_w_o, layer),
                (ffn2_w_gu, layer), (ffn2_w_down, layer)]
        x, (w_in, w_mix, w_q, w_kv, w_o, w_gu, w_down) = _ffn(x, ffn1_norm[layer], w_gu, w_down, jobs)
        kv = _memory_kv(mem2d, xa_mem_norm[layer], w_kv, xa_k_gain[layer])
        cs_q = jnp.tile(xa_q_gain[layer], X_HEADS) * (X_HEAD_DIM ** -0.5 * LOG2_E)
        if even:
            p = _even_projection(x, mix_norm[layer], w_in, ev_q_gain[j], ev_k_gain[j])
            x = _even_tail(x, p, _alibi_log2(A_Q_HEADS), ev_sinks[j].astype(F32) * LOG2_E, w_mix,
                           xa_norm[layer], w_q, cs_q, kv, w_o)
        else:
            heads = _odd_mixer_heads(x, mix_norm[layer], w_in, od_q_gain[j], od_k_gain[j])
            x = _mix_xattn(x, heads, w_mix, xa_norm[layer], w_q, cs_q, kv, w_o)
        jobs = [(ffn1_w_gu, layer + 1), (ffn1_w_down, layer + 1)] if layer + 1 < DEPTH else []
        x, next_ffn1 = _ffn(x, ffn2_norm[layer], w_gu, w_down, jobs)
        if next_ffn1:
            w_gu, w_down = next_ffn1
    return x.reshape(BATCH, SEQ, D_MODEL)
```

```python
import functools

import numpy as np
import jax
import jax.numpy as jnp
from jax import lax
from jax.experimental import pallas as pl
from jax.experimental.pallas import tpu as pltpu

D_MODEL = 1024
BATCH = 4
SEQ = 4096
N_TOK = BATCH * SEQ
DEPTH = 2
HEAD_DIM = 64
BLOCK = 128
A_Q_HEADS = 8
A_KV_HEADS = 2
A_WINDOW = 128
B_HEADS = 8
C_HEADS = 16
C_PATTERNS = ((128, 1), (512, 4), (2048, 16))
MEM_LEN = 256
X_HEADS = 4
X_HEAD_DIM = D_MODEL // X_HEADS
D_FF = 2816
RMS_EPS = 1e-6

LANES = 128
MXU_N = 256
VMEM_LIMIT = 56 * 1024 * 1024

BF = jnp.bfloat16
F32 = jnp.float32
NT_DIMS = (((1,), (1,)), ((), ()))
LOG2_E = 1.4426950408889634


def _params(sem, vmem=None):
    return pltpu.CompilerParams(dimension_semantics=sem, vmem_limit_bytes=vmem)


def _resident(shape):
    nd = len(shape)
    return pl.BlockSpec(shape, lambda *_: (0,) * nd, pipeline_mode=pl.Buffered(1))


BF16_SUBLANES = 16


def _cast_specs(job, steps):
    w3, layer = job
    _, r, c = w3.shape
    rb = next(rb for rb in range(BF16_SUBLANES, r + 1, BF16_SUBLANES) if r % rb == 0 and r // rb <= steps)
    last = r // rb - 1
    return (pl.BlockSpec((None, rb, c), lambda i: (layer, jnp.minimum(i, last), 0)),
            pl.BlockSpec((rb, c), lambda i: (jnp.minimum(i, last), 0)),
            jax.ShapeDtypeStruct((r, c), BF))


def _run_cast_jobs(in_refs, out_refs):
    for src, dst in zip(in_refs, out_refs):
        dst[...] = src[...].astype(BF)


def _cast_kernel(w_ref, o_ref):
    _run_cast_jobs([w_ref], [o_ref])


def _cast_now(job, *, rows=128):
    steps = job[0].shape[1] // rows
    in_spec, out_spec, out_shape = _cast_specs(job, steps)
    return pl.pallas_call(
        _cast_kernel, grid=(steps,), in_specs=[in_spec], out_specs=out_spec, out_shape=out_shape,
        compiler_params=_params(("arbitrary",)),
        name="cast",
    )(job[0])


def _rms(xv, g):
    ms = jnp.mean(xv * xv, axis=-1, keepdims=True)
    return xv * lax.rsqrt(ms + RMS_EPS) * g


FFN_SPLIT = (D_FF // MXU_N + 1) // 2 * MXU_N
FFN_CHUNKS = ((0, FFN_SPLIT), (FFN_SPLIT, D_FF))


def _ffn_kernel(*refs, n_jobs):
    x_ref, g_ref, wgu_ref, wd_ref = refs[:4]
    o_ref = refs[4 + n_jobs]
    xv = x_ref[...]
    h = _rms(xv, g_ref[...]).astype(BF)
    acc = jnp.zeros_like(xv)
    for c0, c1 in FFN_CHUNKS:
        gate = jnp.dot(h, wgu_ref[:, c0:c1], preferred_element_type=F32)
        up = jnp.dot(h, wgu_ref[:, D_FF + c0:D_FF + c1], preferred_element_type=F32)
        act = (gate * jax.nn.sigmoid(gate) * up).astype(BF)
        acc = acc + jnp.dot(act, wd_ref[c0:c1, :], preferred_element_type=F32)
    o_ref[...] = xv + 0.5 * acc
    _run_cast_jobs(refs[4:4 + n_jobs], refs[5 + n_jobs:])


def _ffn(x, g, w_gu, w_down, cast_jobs=(), *, tm=512):
    steps = N_TOK // tm
    specs = [_cast_specs(job, steps) for job in cast_jobs]
    out = pl.pallas_call(
        functools.partial(_ffn_kernel, n_jobs=len(cast_jobs)),
        grid=(steps,),
        in_specs=[pl.BlockSpec((tm, D_MODEL), lambda i: (i, 0)),
                  _resident((1, D_MODEL)),
                  _resident(w_gu.shape),
                  _resident(w_down.shape)] + [s[0] for s in specs],
        out_specs=[pl.BlockSpec((tm, D_MODEL), lambda i: (i, 0))] + [s[1] for s in specs],
        out_shape=[jax.ShapeDtypeStruct((N_TOK, D_MODEL), F32)] + [s[2] for s in specs],
        compiler_params=_params(("arbitrary",), VMEM_LIMIT),
        name="ffn",
    )(x, g.reshape(1, D_MODEL), w_gu, w_down, *[job[0] for job in cast_jobs])
    return out[0], out[1:]


def _proj_kernel(x_ref, g_ref, w_ref, cs_ref, o_ref, *, plan, gs):
    assert gs in (HEAD_DIM, MXU_N)
    h = _rms(x_ref[...], g_ref[...]).astype(BF)
    n_chunks = len(plan) // 2
    lo = lax.broadcasted_iota(jnp.int32, (x_ref.shape[0], LANES), 1) < HEAD_DIM

    def main(j):
        return jnp.dot(h, w_ref[:, MXU_N * j:MXU_N * (j + 1)], preferred_element_type=F32)

    acc_next = main(0)
    out = 0
    for j in range(n_chunks):
        cols = slice(MXU_N * j, MXU_N * (j + 1))
        acc = acc_next
        if j + 1 < n_chunks:
            acc_next = main(j + 1)
        y = acc * cs_ref[:, cols]
        halves = plan[2 * j:2 * j + 2]
        if gs == MXU_N and any(normed for normed, _ in halves):
            inv_chunk = lax.rsqrt(jnp.mean(acc * acc, axis=1, keepdims=True) + RMS_EPS)
        for half, (normed, dup) in enumerate(halves):
            lanes = slice(LANES * half, LANES * (half + 1))
            yh = y[:, lanes]
            if normed and gs == MXU_N:
                yh = yh * inv_chunk
            elif normed:
                sq = acc[:, lanes] * acc[:, lanes]
                s_lo = jnp.sum(jnp.where(lo, sq, 0.0), axis=1, keepdims=True)
                s_hi = jnp.sum(jnp.where(lo, 0.0, sq), axis=1, keepdims=True)
                yh = yh * lax.rsqrt(jnp.where(lo, s_lo, s_hi) * (1.0 / gs) + RMS_EPS)
            if dup:
                swapped = pltpu.roll(yh, HEAD_DIM, axis=1)
                o_ref[out] = jnp.where(lo, yh, swapped).astype(BF)
                o_ref[out + 1] = jnp.where(lo, swapped, yh).astype(BF)
                out += 2
            else:
                o_ref[out] = yh.astype(BF)
                out += 1


def _proj(x, g, w, colscale, plan, gs, *, tm=1024):
    rows = x.shape[0]
    wout = w.shape[1]
    assert wout == LANES * len(plan) and len(plan) % 2 == 0
    c = sum(2 if dup else 1 for _, dup in plan)
    return pl.pallas_call(
        functools.partial(_proj_kernel, plan=tuple(plan), gs=gs),
        grid=(rows // tm,),
        in_specs=[pl.BlockSpec((tm, D_MODEL), lambda i: (i, 0)),
                  _resident((1, D_MODEL)),
                  _resident(w.shape),
                  _resident((1, wout))],
        out_specs=pl.BlockSpec((c, tm, LANES), lambda i: (0, i, 0)),
        out_shape=jax.ShapeDtypeStruct((c, rows, LANES), BF),
        compiler_params=_params(("parallel",), VMEM_LIMIT),
        name="proj",
    )(x, g.reshape(1, D_MODEL), w, colscale.reshape(1, wout).astype(F32))


def _swa_block(q_blocks, kp, kc, vp, vc, has_prev, slopes_ref, sinks_ref, *, kv_div, max_dist):
    row = lax.broadcasted_iota(jnp.int32, (BLOCK, 2 * BLOCK), 0)
    col = lax.broadcasted_iota(jnp.int32, (BLOCK, 2 * BLOCK), 1)
    dist = row + BLOCK - col
    valid = (dist >= 0) & (dist <= max_dist)
    if has_prev is not True:
        valid = valid & ((col >= BLOCK) | has_prev)
    negmask = jnp.where(valid, 0.0, -jnp.inf)
    distf = dist.astype(F32)
    lo = lax.broadcasted_iota(jnp.int32, (BLOCK, LANES), 1) < HEAD_DIM

    n_groups = len(q_blocks) // kv_div
    heads_per_group = 2 * kv_div
    scores = []
    for g in range(n_groups):
        parts = []
        for p in range(g * kv_div, (g + 1) * kv_div):
            q2 = q_blocks[p].astype(F32)
            parts += [jnp.where(lo, q2, 0.0), jnp.where(lo, 0.0, q2)]
        q_stack = jnp.concatenate(parts, axis=0).astype(BF)
        scores.append(jnp.concatenate(
            [lax.dot_general(q_stack, kp[g], NT_DIMS, preferred_element_type=F32),
             lax.dot_general(q_stack, kc[g], NT_DIMS, preferred_element_type=F32)], axis=1))
    soft = []
    for g in range(n_groups):
        res = []
        for j in range(heads_per_group):
            h = g * heads_per_group + j
            s = scores[g][j * BLOCK:(j + 1) * BLOCK] - slopes_ref[h] * distf + negmask
            m = jnp.maximum(jnp.max(s, axis=1, keepdims=True), sinks_ref[h])
            pe = jnp.exp2(s - m)
            res.append((pe.astype(BF), jnp.sum(pe, axis=1, keepdims=True) + jnp.exp2(sinks_ref[h] - m)))
        soft.append(res)
    outs = []
    for g in range(n_groups):
        pb = jnp.concatenate([r[0] for r in soft[g]], axis=0)
        pv = (jnp.dot(pb[:, :BLOCK], vp[g], preferred_element_type=F32)
              + jnp.dot(pb[:, BLOCK:], vc[g], preferred_element_type=F32))
        for jp in range(kv_div):
            o0 = pv[(2 * jp) * BLOCK:(2 * jp + 1) * BLOCK] / soft[g][2 * jp][1]
            o1 = pv[(2 * jp + 1) * BLOCK:(2 * jp + 2) * BLOCK] / soft[g][2 * jp + 1][1]
            outs.append(jnp.where(lo, o0, o1))
    return outs


DIL_ORDER = tuple(sorted(C_PATTERNS, key=lambda wd: -wd[1]))
DIL_UNROLL = 16
DIL_AHEAD = 2
DIL_BASE = 4
DIL_Q = SEQ // DIL_BASE
DIL_CONVERT_ROWS = DIL_BASE * BLOCK


def _dilated_kernel(slopes_ref, q_ref, k_ref, v_ref, o_ref, qn_s, tq_s, tk_s, tv_s, q0_s, q1_s, k_s, v_s,
                    acc_r, m_r, l_r, acc_n, m_n, l_n):
    assert all(d == 1 or d % DIL_BASE == 0 for _, d in DIL_ORDER) and DIL_ORDER[-1][1] == 1
    p = pl.program_id(1)
    lo = lax.broadcasted_iota(jnp.int32, (BLOCK, LANES), 1) < HEAD_DIM

    def convert(c, carry):
        rows = pl.ds(pl.multiple_of(c * DIL_CONVERT_ROWS, DIL_CONVERT_ROWS), DIL_CONVERT_ROWS)
        q_nat = q_ref[0, rows, :].astype(F32)
        lo_c = lax.broadcasted_iota(jnp.int32, (DIL_CONVERT_ROWS, LANES), 1) < HEAD_DIM
        qn_s[0, rows, :] = jnp.where(lo_c, q_nat, 0.0).astype(BF)
        qn_s[1, rows, :] = jnp.where(lo_c, 0.0, q_nat).astype(BF)
        tq_s[...] = q_nat
        tk_s[...] = k_ref[0, rows, :].astype(F32)
        tv_s[...] = v_ref[0, rows, :].astype(F32)
        for rho in range(DIL_BASE):
            src = pl.ds(rho, BLOCK, stride=DIL_BASE)
            dst = pl.ds(pl.multiple_of(rho * DIL_Q + c * BLOCK, BLOCK), BLOCK)
            q = tq_s[src, :]
            q0_s[dst, :] = jnp.where(lo, q, 0.0)
            q1_s[dst, :] = jnp.where(lo, 0.0, q)
            k_s[dst, :] = tk_s[src, :]
            v_s[dst, :] = tv_s[src, :]
        return carry

    lax.fori_loop(0, SEQ // DIL_CONVERT_ROWS, convert, 0)

    row = lax.broadcasted_iota(jnp.int32, (BLOCK, 2 * BLOCK), 0)
    col = lax.broadcasted_iota(jnp.int32, (BLOCK, 2 * BLOCK), 1)
    dist = row + BLOCK - col
    distf = dist.astype(F32)
    no_prev = jnp.where(col < BLOCK, -jnp.inf, 0.0)

    def bcast2(a0, a1):
        return jnp.where(lo, jnp.broadcast_to(a0, (BLOCK, LANES)), jnp.broadcast_to(a1, (BLOCK, LANES)))

    for pi, (window, d) in enumerate(DIL_ORDER):
        first, last = pi == 0, pi == len(DIL_ORDER) - 1
        natural = d == 1
        nb = SEQ // d // BLOCK
        band = (dist >= 0) & (dist <= window // d)
        bias = [jnp.where(band, (-float(d) * slopes_ref[2 * p + hh]) * distf, -jnp.inf) for hh in range(2)]
        acc_s, m_s, l_s = (acc_n, m_n, l_n) if natural else (acc_r, m_r, l_r)
        to_natural = not natural and not last and DIL_ORDER[pi + 1][1] == 1
        assert (not to_natural or d == DIL_BASE) and (not natural or first or DIL_ORDER[pi - 1][1] == DIL_BASE)
        acc_o, m_o, l_o = (acc_n, m_n, l_n) if to_natural else (acc_s, m_s, l_s)

        def rows_of(r, n, n_blocks=1, d=d, natural=natural):
            size = n_blocks * BLOCK
            if natural:
                return pl.ds(pl.multiple_of(BLOCK * n, BLOCK), size)
            inner = d // DIL_BASE
            start = (r % DIL_BASE) * DIL_Q + r // DIL_BASE + inner * BLOCK * n
            return pl.ds(start, size, stride=inner) if inner > 1 else pl.ds(pl.multiple_of(start, BLOCK), size)

        def step(it, carry, nb=nb, bias=bias, first=first, last=last, natural=natural, rows_of=rows_of,
                 acc_s=acc_s, m_s=m_s, l_s=l_s, acc_o=acc_o, m_o=m_o, l_o=l_o, to_natural=to_natural):
            assert DIL_UNROLL % nb == 0 or nb % DIL_UNROLL == 0
            load_k = (lambda rr: k_ref[0, rr, :]) if natural else (lambda rr: k_s[rr, :].astype(BF))
            load_v = (lambda rr: v_ref[0, rr, :]) if natural else (lambda rr: v_s[rr, :].astype(BF))

            def scores(u):
                t = it * DIL_UNROLL + u
                r, n = t // nb, t % nb
                prev = (u % nb != 0) if nb <= DIL_UNROLL else (True if u else None)
                rows = rows_of(r, n)
                out_rows = pl.ds(DIL_BASE * BLOCK * n + r, BLOCK, stride=DIL_BASE) if to_natural else rows
                if prev is True:
                    kv_rows = [rows_of(r, n - 1, 2)]
                elif prev is None:
                    kv_rows = [rows_of(r, jnp.maximum(n - 1, 0)), rows]
                else:
                    kv_rows = [rows]
                if natural:
                    qh = jnp.concatenate([qn_s[0, rows, :], qn_s[1, rows, :]], axis=0)
                else:
                    qh = jnp.concatenate([q0_s[rows, :], q1_s[rows, :]], axis=0).astype(BF)
                s = jnp.concatenate([lax.dot_general(qh, load_k(rr), NT_DIMS, preferred_element_type=F32)
                                     for rr in kv_rows], axis=1)
                return n, prev, rows, out_rows, kv_rows, (s[:BLOCK], s[BLOCK:])

            def softmax_pv(blk):
                n, prev, rows, out_rows, kv_rows, s = blk
                ms, ls, pes = [], [], []
                for hh in range(2):
                    sh = s[hh] + (bias[hh][:, BLOCK:] if prev is False else bias[hh])
                    if prev is None:
                        sh = sh + jnp.where(n == 0, no_prev, 0.0)
                    m = jnp.max(sh, axis=1, keepdims=True)
                    pe = jnp.exp2(sh - m)
                    ms.append(m)
                    ls.append(jnp.sum(pe, axis=1, keepdims=True))
                    pes.append(pe.astype(BF))
                pb = jnp.concatenate(pes, axis=0)
                v = jnp.concatenate([load_v(rr) for rr in kv_rows], axis=0) if len(kv_rows) > 1 else load_v(kv_rows[0])
                pv = jnp.dot(pb, v, preferred_element_type=F32)
                return rows, out_rows, ms, ls, (pv[:BLOCK], pv[BLOCK:])

            def merge(rows, out_rows, ms, ls, pv):
                m2 = bcast2(ms[0], ms[1])
                l2 = bcast2(ls[0], ls[1])
                acc2 = jnp.where(lo, pv[0], pv[1])
                if not first:
                    m_old = m_s[rows, :]
                    m_new = jnp.maximum(m_old, m2)
                    a_old, a_new = jnp.exp2(m_old - m_new), jnp.exp2(m2 - m_new)
                    l2 = a_old * l_s[rows, :] + a_new * l2
                    acc2 = a_old * acc_s[rows, :] + a_new * acc2
                    m2 = m_new
                if last:
                    o_ref[0, rows, :] = (acc2 / l2).astype(BF)
                else:
                    m_o[out_rows, :] = m2
                    l_o[out_rows, :] = l2
                    acc_o[out_rows, :] = acc2

            pending = {u: scores(u) for u in range(DIL_AHEAD)}
            done = None
            for u in range(DIL_UNROLL):
                if u + DIL_AHEAD < DIL_UNROLL:
                    pending[u + DIL_AHEAD] = scores(u + DIL_AHEAD)
                cur = softmax_pv(pending.pop(u))
                if done is not None:
                    merge(*done)
                done = cur
            merge(*done)
            return carry

        lax.fori_loop(0, SEQ // BLOCK // DIL_UNROLL, step, 0)


def _dilated(qkv, slopes):
    n_pairs = C_HEADS // 2
    seq_f32 = pltpu.VMEM((SEQ, LANES), F32)
    chunk_f32 = pltpu.VMEM((DIL_CONVERT_ROWS, LANES), F32)
    return pl.pallas_call(
        _dilated_kernel,
        grid=(BATCH, n_pairs),
        in_specs=[pl.BlockSpec(memory_space=pltpu.SMEM),
                  pl.BlockSpec((1, SEQ, LANES), lambda b, p: (p, b, 0)),
                  pl.BlockSpec((1, SEQ, LANES), lambda b, p: (n_pairs + p, b, 0)),
                  pl.BlockSpec((1, SEQ, LANES), lambda b, p: (2 * n_pairs + p, b, 0))],
        out_specs=pl.BlockSpec((1, SEQ, LANES), lambda b, p: (p, b, 0)),
        out_shape=jax.ShapeDtypeStruct((n_pairs, N_TOK, LANES), BF),
        scratch_shapes=[pltpu.VMEM((2, SEQ, LANES), BF)] + [chunk_f32] * 3 + [seq_f32] * 10,
        compiler_params=_params(("parallel", "parallel"), VMEM_LIMIT),
        name="dilated",
    )(slopes, qkv, qkv, qkv)


SB_QB = MXU_N
SB_FIRST_TILES = 2
SB_DEAD_LOG2 = -150.0


def _sb_matrix():
    idx = np.arange(SB_QB)
    return jnp.asarray(-(idx[:, None] > idx[None, :]).astype(np.float32), dtype=BF)


def _sb_unit(q2, load_k, load_v, iq, uo):
    lo = lax.broadcasted_iota(jnp.int32, (SB_QB, LANES), 1) < HEAD_DIM
    q_stack = jnp.concatenate([jnp.where(lo, q2, 0.0), jnp.where(lo, 0.0, q2)], axis=0).astype(BF)
    rel1 = (lax.broadcasted_iota(jnp.int32, (SB_QB, SB_QB), 1)
            - lax.broadcasted_iota(jnp.int32, (SB_QB, SB_QB), 0))
    rel = jnp.concatenate([rel1, rel1], axis=0)

    def scores(first, n_tiles):
        return lax.dot_general(q_stack, load_k(first, n_tiles), NT_DIMS,
                               preferred_element_type=F32)

    def walk(first, n_tiles, carry, masked):
        z, c, o = carry
        z_next = scores(jnp.maximum(first - 1, 0), 1)
        order = list(reversed(range(n_tiles)))
        ws, es, stricts, totals = {}, {}, {}, {}
        for t in order:
            zt = z[:, t * SB_QB:(t + 1) * SB_QB]
            sp = jnp.maximum(zt, 0.0) + jnp.log2(1.0 + jnp.exp2(-jnp.abs(zt)))
            es[t] = zt - sp
            if masked:
                stricts[t] = rel < (iq - first - t) * SB_QB
                sp = jnp.where(stricts[t], sp, 0.0)
            ws[t] = jnp.dot(sp.astype(BF), uo, preferred_element_type=F32)
            totals[t] = jnp.sum(sp, axis=1, keepdims=True)
        parts = {}
        for t in order:
            a = jnp.exp2(es[t] + jnp.concatenate([c] * (SB_QB // LANES), axis=1) + ws[t])
            if masked:
                a = jnp.where(stricts[t], a, 0.0)
            parts[t] = a.astype(BF)
            c = c - totals[t]
        pv = jnp.dot(jnp.concatenate([parts[t] for t in range(n_tiles)], axis=1), load_v(first, n_tiles),
                     preferred_element_type=F32)
        return z_next, c, o + jnp.where(lo, pv[:SB_QB], pv[SB_QB:])

    def alive(c):
        return jnp.max(c) > SB_DEAD_LOG2

    def body(state):
        g = state[0]
        z, c, o = walk(g, 1, state[2:], False)
        return g - 1, alive(c), z, c, o

    first = jnp.maximum(iq - 1, 0)
    zeros = (jnp.zeros((2 * SB_QB, LANES), F32), jnp.zeros((SB_QB, LANES), F32))
    z, c, o = walk(first, SB_FIRST_TILES, (scores(first, SB_FIRST_TILES),) + zeros, True)

    def finish():
        state = lax.while_loop(lambda st: (st[0] >= 0) & st[1], body, (first - 1, alive(c), z, c, o))
        return state[4]

    return finish


def _mix_xattn_kernel(*refs):
    x_ref = refs[0]
    wm_ref, g_ref, wq_ref, cs_ref, kv_ref, wo_ref, o_ref = refs[-7:]
    mixed = jnp.concatenate([r[c] for r in refs[1:-7] for c in range(r.shape[0])], axis=1)
    xv = x_ref[...] + jnp.dot(mixed, wm_ref[...], preferred_element_type=F32)
    h = _rms(xv, g_ref[...]).astype(BF)
    heads = range(X_HEADS)
    cols = [slice(X_HEAD_DIM * hd, X_HEAD_DIM * (hd + 1)) for hd in heads]
    acc = [jnp.dot(h, wq_ref[:, cols[hd]], preferred_element_type=F32) for hd in heads]
    ms = [jnp.mean(acc[hd] * acc[hd], axis=1, keepdims=True) for hd in heads]
    q = [(acc[hd] * cs_ref[:, cols[hd]] * lax.rsqrt(ms[hd] + RMS_EPS)).astype(BF) for hd in heads]
    s = [lax.dot_general(q[hd], jnp.concatenate([kv_ref[2 * hd], kv_ref[2 * hd + 1]], axis=1), NT_DIMS,
                         preferred_element_type=F32) for hd in heads]
    pe, l = [], []
    for hd in heads:
        e = jnp.exp2(s[hd] - jnp.max(s[hd], axis=1, keepdims=True))
        l.append(jnp.sum(e, axis=1, keepdims=True))
        pe.append(e.astype(BF))
    v0 = 2 * X_HEADS
    pv = [jnp.dot(pe[hd], jnp.concatenate([kv_ref[v0 + 2 * hd], kv_ref[v0 + 2 * hd + 1]], axis=1),
                  preferred_element_type=F32) for hd in heads]
    o = jnp.concatenate([(pv[hd] / l[hd]).astype(BF) for hd in heads], axis=1)
    o_ref[...] = xv + jnp.dot(o, wo_ref[...], preferred_element_type=F32)


def _mix_xattn(x, mixer_heads, w_mix, g, w_q, q_colscale, kv, w_o, *, tm=1024):
    tiles_per_batch = SEQ // tm
    in_specs = [pl.BlockSpec((tm, D_MODEL), lambda i: (i, 0))]
    in_specs += [pl.BlockSpec((mh.shape[0], tm, LANES), lambda i: (0, i, 0)) for mh in mixer_heads]
    in_specs += [_resident(w_mix.shape),
                 _resident((1, D_MODEL)),
                 _resident(w_q.shape),
                 _resident((1, D_MODEL)),
                 pl.BlockSpec((4 * X_HEADS, MEM_LEN, LANES), lambda i: (0, i // tiles_per_batch, 0)),
                 _resident(w_o.shape)]
    return pl.pallas_call(
        _mix_xattn_kernel, grid=(N_TOK // tm,),
        in_specs=in_specs,
        out_specs=pl.BlockSpec((tm, D_MODEL), lambda i: (i, 0)),
        out_shape=jax.ShapeDtypeStruct((N_TOK, D_MODEL), F32),
        compiler_params=_params(("parallel",), VMEM_LIMIT),
        name="mix_xattn",
    )(x, *mixer_heads, w_mix, g.reshape(1, D_MODEL), w_q,
      q_colscale.reshape(1, D_MODEL).astype(F32), kv, w_o)


EVEN_TILE = 512


def _even_tail_kernel(slopes_ref, sinks_ref, x_ref, swq_ref, swkp_ref, swkc_ref, swvp_ref, swvc_ref,
                      sbq_ref, sbk_ref, sbv_ref, uo_ref, wm_ref, g_ref, wq_ref, cs_ref, kv_ref, wo_ref,
                      o_ref, heads_s, *, kv_div, max_dist):
    i = pl.program_id(0)
    tiles_per_seq = SEQ // EVEN_TILE
    t_in_seq = jnp.minimum(i, N_TOK // EVEN_TILE - 1) % tiles_per_seq
    wr, rd = i % 2, (i + 1) % 2
    n_pairs_a, n_pairs_b = swq_ref.shape[0], sbq_ref.shape[0]

    @pl.when(i == 0)
    def _():
        heads_s[1] = jnp.zeros(heads_s.shape[1:], BF)

    def mix_steps():
        heads = range(X_HEADS)
        cols = [slice(X_HEAD_DIM * hd, X_HEAD_DIM * (hd + 1)) for hd in heads]
        st = {"xv": [None] * X_HEADS, "q": [None] * X_HEADS, "pe": [None] * X_HEADS, "l": [None] * X_HEADS,
              "o": [None] * X_HEADS}

        def project(c):
            if c == 0:
                st["mixed"] = jnp.concatenate([heads_s[rd, k] for k in range(n_pairs_a + n_pairs_b)], axis=1)
            st["xv"][c] = x_ref[:, cols[c]] + jnp.dot(st["mixed"], wm_ref[:, cols[c]], preferred_element_type=F32)

        def q_proj(hd):
            if hd == 0:
                xv = jnp.concatenate(st["xv"], axis=1)
                st["h"] = _rms(xv, g_ref[...]).astype(BF)
            acc = jnp.dot(st["h"], wq_ref[:, cols[hd]], preferred_element_type=F32)
            ms = jnp.mean(acc * acc, axis=1, keepdims=True)
            st["q"][hd] = (acc * cs_ref[:, cols[hd]] * lax.rsqrt(ms + RMS_EPS)).astype(BF)

        def scores(hd):
            kh = jnp.concatenate([kv_ref[2 * hd], kv_ref[2 * hd + 1]], axis=1)
            s = lax.dot_general(st["q"][hd], kh, NT_DIMS, preferred_element_type=F32)
            e = jnp.exp2(s - jnp.max(s, axis=1, keepdims=True))
            st["l"][hd] = jnp.sum(e, axis=1, keepdims=True)
            st["pe"][hd] = e.astype(BF)

        def values(hd):
            v0 = 2 * X_HEADS
            vh = jnp.concatenate([kv_ref[v0 + 2 * hd], kv_ref[v0 + 2 * hd + 1]], axis=1)
            st["o"][hd] = (jnp.dot(st["pe"][hd], vh, preferred_element_type=F32) / st["l"][hd]).astype(BF)

        def out_proj(c):
            if c == 0:
                st["oc"] = jnp.concatenate(st["o"], axis=1)
            o_ref[:, cols[c]] = st["xv"][c] + jnp.dot(st["oc"], wo_ref[:, cols[c]], preferred_element_type=F32)

        return [functools.partial(f, k) for f in (project, q_proj, scores, values, out_proj) for k in heads]

    def attend_and_mix():
        for step in mix_steps():
            step()

        uo = uo_ref[...]
        blocks_per_tile = EVEN_TILE // SB_QB

        def sb_first_pass(p, j):
            rows = slice(SB_QB * j, SB_QB * (j + 1))
            load = lambda ref: (lambda first, n: ref[p, pl.ds(pl.multiple_of(first * SB_QB, SB_QB), n * SB_QB), :])
            return _sb_unit(sbq_ref[p, rows, :].astype(F32), load(sbk_ref), load(sbv_ref),
                            blocks_per_tile * t_in_seq + j, uo)

        units = [(p, j) for p in range(n_pairs_b) for j in range(blocks_per_tile)]
        finishers = [sb_first_pass(p, j) for p, j in units]

        n_kv = swkc_ref.shape[0]
        for m in range(EVEN_TILE // BLOCK):
            rows = slice(BLOCK * m, BLOCK * (m + 1))
            if m == 0:
                kp, vp, has_prev = [swkp_ref[g] for g in range(n_kv)], [swvp_ref[g] for g in range(n_kv)], t_in_seq > 0
            else:
                prows = slice(BLOCK * (m - 1), BLOCK * m)
                kp, vp, has_prev = ([swkc_ref[g, prows, :] for g in range(n_kv)],
                                    [swvc_ref[g, prows, :] for g in range(n_kv)], True)
            outs = _swa_block([swq_ref[p, rows, :] for p in range(n_pairs_a)], kp,
                              [swkc_ref[g, rows, :] for g in range(n_kv)], vp, [swvc_ref[g, rows, :] for g in range(n_kv)],
                              has_prev, slopes_ref, sinks_ref, kv_div=kv_div, max_dist=max_dist)
            for p in range(n_pairs_a):
                heads_s[wr, p, rows, :] = outs[p].astype(BF)

        for (p, j), finish in zip(units, finishers):
            heads_s[wr, n_pairs_a + p, SB_QB * j:SB_QB * (j + 1), :] = finish().astype(BF)

    n_tiles = N_TOK // EVEN_TILE
    pl.when(i < n_tiles)(attend_and_mix)

    @pl.when(i == n_tiles)
    def _():
        for step in mix_steps():
            step()


def _even_tail(x, p, slopes, sinks, w_mix, g, w_q, q_colscale, kv, w_o):
    n_pairs_a, n_kv, n_pairs_b = A_Q_HEADS // 2, A_KV_HEADS, B_HEADS // 2
    assert p.shape[0] == n_pairs_a + 2 * n_kv + 3 * n_pairs_b and n_pairs_a == n_pairs_b == 2 * n_kv
    n_tiles = N_TOK // EVEN_TILE
    tiles_per_seq = SEQ // EVEN_TILE
    blocks_per_tile = EVEN_TILE // BLOCK
    att = lambda i: jnp.minimum(i, n_tiles - 1)
    mix = lambda i: jnp.maximum(i - 1, 0)
    tile_rows = lambda size, idx: pl.BlockSpec((size, EVEN_TILE, LANES), lambda i: (idx, att(i), 0))
    prev_block = lambda idx: pl.BlockSpec(
        (n_kv, BLOCK, LANES), lambda i: (idx, jnp.maximum(blocks_per_tile * att(i) - 1, 0), 0))
    whole_seq = lambda idx: pl.BlockSpec((n_pairs_b, SEQ, LANES), lambda i: (idx, att(i) // tiles_per_seq, 0),
                                         pipeline_mode=pl.Buffered(1))
    smem = pl.BlockSpec(memory_space=pltpu.SMEM)
    return pl.pallas_call(
        functools.partial(_even_tail_kernel, kv_div=n_pairs_a // n_kv, max_dist=A_WINDOW - 1),
        grid=(n_tiles + 1,),
        in_specs=[smem, smem,
                  pl.BlockSpec((EVEN_TILE, D_MODEL), lambda i: (mix(i), 0)),
                  tile_rows(n_pairs_a, 0), prev_block(2), tile_rows(n_kv, 2), prev_block(3), tile_rows(n_kv, 3),
                  tile_rows(n_pairs_b, 2), whole_seq(3), whole_seq(4),
                  _resident((SB_QB, SB_QB)),
                  _resident(w_mix.shape), _resident((1, D_MODEL)), _resident(w_q.shape), _resident((1, D_MODEL)),
                  pl.BlockSpec((4 * X_HEADS, MEM_LEN, LANES), lambda i: (0, mix(i) // tiles_per_seq, 0)),
                  _resident(w_o.shape)],
        out_specs=pl.BlockSpec((EVEN_TILE, D_MODEL), lambda i: (mix(i), 0)),
        out_shape=jax.ShapeDtypeStruct((N_TOK, D_MODEL), F32),
        scratch_shapes=[pltpu.VMEM((2, n_pairs_a + n_pairs_b, EVEN_TILE, LANES), BF)],
        compiler_params=_params(("arbitrary",), VMEM_LIMIT),
        name="even_tail",
    )(slopes, sinks, x, p, p, p, p, p, p, p, p, _sb_matrix(), w_mix, g.reshape(1, D_MODEL), w_q,
      q_colscale.reshape(1, D_MODEL).astype(F32), kv, w_o)


def _alibi_log2(n_heads):
    return jnp.asarray(LOG2_E * 2.0 ** (-8.0 * np.arange(1, n_heads + 1) / n_heads), dtype=F32)


def _even_projection(x, norm_g, w_in, q_gain, k_gain):
    hd = HEAD_DIM
    a_q, a_kv, b_w = A_Q_HEADS * hd, A_KV_HEADS * hd, B_HEADS * hd
    scale = hd ** -0.5 * LOG2_E
    ones = lambda n: jnp.ones((n,), F32)
    cs = jnp.concatenate([jnp.tile(q_gain, A_Q_HEADS) * scale, jnp.tile(k_gain, A_KV_HEADS), ones(a_kv),
                          ones(b_w) * scale, ones(2 * b_w)])
    plan = ([(True, False)] * (a_q // LANES) + [(True, True)] * (a_kv // LANES) + [(False, True)] * (a_kv // LANES)
            + [(False, False)] * (3 * b_w // LANES))
    return _proj(x, norm_g, w_in, cs, plan, hd)


def _odd_mixer_heads(x, norm_g, w_in, q_gain, k_gain):
    hd = HEAD_DIM
    cs = jnp.concatenate([jnp.tile(q_gain, C_HEADS) * (hd ** -0.5 * LOG2_E), jnp.tile(k_gain, C_HEADS),
                          jnp.ones((C_HEADS * hd,), F32)])
    head_blocks = C_HEADS * hd // LANES
    plan = [(True, False)] * (2 * head_blocks) + [(False, False)] * head_blocks
    p = _proj(x, norm_g, w_in, cs, plan, hd)
    return [_dilated(p, _alibi_log2(C_HEADS))]


def _memory_kv(mem2d, mem_g, w_kv, k_gain):
    cs_kv = jnp.concatenate([jnp.tile(k_gain, X_HEADS), jnp.ones((D_MODEL,), F32)])
    plan = [(True, False)] * (D_MODEL // LANES) + [(False, False)] * (D_MODEL // LANES)
    return _proj(mem2d, mem_g, w_kv, cs_kv, plan, X_HEAD_DIM, tm=MEM_LEN)


def kernel(x, mem, ffn1_norm, ffn1_w_gu, ffn1_w_down, mix_norm, ev_w_in, ev_q_gain, ev_k_gain, ev_sinks, ev_w_out, od_w_in, od_q_gain, od_k_gain, od_w_out, xa_norm, xa_mem_norm, xa_w_q, xa_w_kv, xa_q_gain, xa_k_gain, xa_w_o, ffn2_norm, ffn2_w_gu, ffn2_w_down):
    x = x.reshape(N_TOK, D_MODEL)
    mem2d = mem.reshape(BATCH * MEM_LEN, D_MODEL)
    w_gu, w_down = _cast_now((ffn1_w_gu, 0)), _cast_now((ffn1_w_down, 0))
    for layer in range(DEPTH):
        j = layer // 2
        even = layer % 2 == 0
        w_in3, w_mix3 = (ev_w_in, ev_w_out) if even else (od_w_in, od_w_out)
        jobs = [(w_in3, j), (w_mix3, j), (xa_w_q, layer), (xa_w_kv, layer), (xa_w_o, layer),
                (ffn2_w_gu, layer), (ffn2_w_down, layer)]
        x, (w_in, w_mix, w_q, w_kv, w_o, w_gu, w_down) = _ffn(x, ffn1_norm[layer], w_gu, w_down, jobs)
        kv = _memory_kv(mem2d, xa_mem_norm[layer], w_kv, xa_k_gain[layer])
        cs_q = jnp.tile(xa_q_gain[layer], X_HEADS) * (X_HEAD_DIM ** -0.5 * LOG2_E)
        if even:
            p = _even_projection(x, mix_norm[layer], w_in, ev_q_gain[j], ev_k_gain[j])
            x = _even_tail(x, p, _alibi_log2(A_Q_HEADS), ev_sinks[j].astype(F32) * LOG2_E, w_mix,
                           xa_norm[layer], w_q, cs_q, kv, w_o)
        else:
            heads = _odd_mixer_heads(x, mix_norm[layer], w_in, od_q_gain[j], od_k_gain[j])
            x = _mix_xattn(x, heads, w_mix, xa_norm[layer], w_q, cs_q, kv, w_o)
        jobs = [(ffn1_w_gu, layer + 1), (ffn1_w_down, layer + 1)] if layer + 1 < DEPTH else []
        x, next_ffn1 = _ffn(x, ffn2_norm[layer], w_gu, w_down, jobs)
        if next_ffn1:
            w_gu, w_down = next_ffn1
    return x.reshape(BATCH, SEQ, D_MODEL)
```

```python
import functools

import numpy as np
import jax
import jax.numpy as jnp
from jax import lax
from jax.experimental import pallas as pl
from jax.experimental.pallas import tpu as pltpu

D_MODEL = 1024
BATCH = 4
SEQ = 4096
N_TOK = BATCH * SEQ
DEPTH = 2
HEAD_DIM = 64
BLOCK = 128
A_Q_HEADS = 8
A_KV_HEADS = 2
A_WINDOW = 128
B_HEADS = 8
C_HEADS = 16
C_PATTERNS = ((128, 1), (512, 4), (2048, 16))
MEM_LEN = 256
X_HEADS = 4
X_HEAD_DIM = D_MODEL // X_HEADS
D_FF = 2816
RMS_EPS = 1e-6

LANES = 128
MXU_N = 256
VMEM_LIMIT = 56 * 1024 * 1024

BF = jnp.bfloat16
F32 = jnp.float32
NT_DIMS = (((1,), (1,)), ((), ()))
LOG2_E = 1.4426950408889634


def _params(sem, vmem=None):
    return pltpu.CompilerParams(dimension_semantics=sem, vmem_limit_bytes=vmem)


def _resident(shape):
    nd = len(shape)
    return pl.BlockSpec(shape, lambda *_: (0,) * nd, pipeline_mode=pl.Buffered(1))


BF16_SUBLANES = 16


def _cast_specs(job, steps):
    w3, layer = job
    _, r, c = w3.shape
    rb = next(rb for rb in range(BF16_SUBLANES, r + 1, BF16_SUBLANES) if r % rb == 0 and r // rb <= steps)
    last = r // rb - 1
    return (pl.BlockSpec((None, rb, c), lambda i: (layer, jnp.minimum(i, last), 0)),
            pl.BlockSpec((rb, c), lambda i: (jnp.minimum(i, last), 0)),
            jax.ShapeDtypeStruct((r, c), BF))


def _run_cast_jobs(in_refs, out_refs):
    for src, dst in zip(in_refs, out_refs):
        dst[...] = src[...].astype(BF)


def _cast_kernel(w_ref, o_ref):
    _run_cast_jobs([w_ref], [o_ref])


def _cast_now(job, *, rows=128):
    steps = job[0].shape[1] // rows
    in_spec, out_spec, out_shape = _cast_specs(job, steps)
    return pl.pallas_call(
        _cast_kernel, grid=(steps,), in_specs=[in_spec], out_specs=out_spec, out_shape=out_shape,
        compiler_params=_params(("arbitrary",)),
        name="cast",
    )(job[0])


def _rms(xv, g):
    ms = jnp.mean(xv * xv, axis=-1, keepdims=True)
    return xv * lax.rsqrt(ms + RMS_EPS) * g


FFN_SPLIT = (D_FF // MXU_N + 1) // 2 * MXU_N
FFN_CHUNKS = ((0, FFN_SPLIT), (FFN_SPLIT, D_FF))


def _ffn_kernel(*refs, n_jobs):
    x_ref, g_ref, wgu_ref, wd_ref = refs[:4]
    o_ref = refs[4 + n_jobs]
    xv = x_ref[...]
    h = _rms(xv, g_ref[...]).astype(BF)
    acc = jnp.zeros_like(xv)
    for c0, c1 in FFN_CHUNKS:
        gate = jnp.dot(h, wgu_ref[:, c0:c1], preferred_element_type=F32)
        up = jnp.dot(h, wgu_ref[:, D_FF + c0:D_FF + c1], preferred_element_type=F32)
        act = (gate * jax.nn.sigmoid(gate) * up).astype(BF)
        acc = acc + jnp.dot(act, wd_ref[c0:c1, :], preferred_element_type=F32)
    o_ref[...] = xv + 0.5 * acc
    _run_cast_jobs(refs[4:4 + n_jobs], refs[5 + n_jobs:])


def _ffn(x, g, w_gu, w_down, cast_jobs=(), *, tm=512):
    steps = N_TOK // tm
    specs = [_cast_specs(job, steps) for job in cast_jobs]
    out = pl.pallas_call(
        functools.partial(_ffn_kernel, n_jobs=len(cast_jobs)),
        grid=(steps,),
        in_specs=[pl.BlockSpec((tm, D_MODEL), lambda i: (i, 0)),
                  _resident((1, D_MODEL)),
                  _resident(w_gu.shape),
                  _resident(w_down.shape)] + [s[0] for s in specs],
        out_specs=[pl.BlockSpec((tm, D_MODEL), lambda i: (i, 0))] + [s[1] for s in specs],
        out_shape=[jax.ShapeDtypeStruct((N_TOK, D_MODEL), F32)] + [s[2] for s in specs],
        compiler_params=_params(("arbitrary",), VMEM_LIMIT),
        name="ffn",
    )(x, g.reshape(1, D_MODEL), w_gu, w_down, *[job[0] for job in cast_jobs])
    return out[0], out[1:]


def _proj_kernel(x_ref, g_ref, w_ref, cs_ref, o_ref, *, plan, gs):
    assert gs in (HEAD_DIM, MXU_N)
    h = _rms(x_ref[...], g_ref[...]).astype(BF)
    n_chunks = len(plan) // 2
    lo = lax.broadcasted_iota(jnp.int32, (x_ref.shape[0], LANES), 1) < HEAD_DIM

    def main(j):
        return jnp.dot(h, w_ref[:, MXU_N * j:MXU_N * (j + 1)], preferred_element_type=F32)

    acc_next = main(0)
    out = 0
    for j in range(n_chunks):
        cols = slice(MXU_N * j, MXU_N * (j + 1))
        acc = acc_next
        if j + 1 < n_chunks:
            acc_next = main(j + 1)
        y = acc * cs_ref[:, cols]
        halves = plan[2 * j:2 * j + 2]
        if gs == MXU_N and any(normed for normed, _ in halves):
            inv_chunk = lax.rsqrt(jnp.mean(acc * acc, axis=1, keepdims=True) + RMS_EPS)
        for half, (normed, dup) in enumerate(halves):
            lanes = slice(LANES * half, LANES * (half + 1))
            yh = y[:, lanes]
            if normed and gs == MXU_N:
                yh = yh * inv_chunk
            elif normed:
                sq = acc[:, lanes] * acc[:, lanes]
                s_lo = jnp.sum(jnp.where(lo, sq, 0.0), axis=1, keepdims=True)
                s_hi = jnp.sum(jnp.where(lo, 0.0, sq), axis=1, keepdims=True)
                yh = yh * lax.rsqrt(jnp.where(lo, s_lo, s_hi) * (1.0 / gs) + RMS_EPS)
            if dup:
                swapped = pltpu.roll(yh, HEAD_DIM, axis=1)
                o_ref[out] = jnp.where(lo, yh, swapped).astype(BF)
                o_ref[out + 1] = jnp.where(lo, swapped, yh).astype(BF)
                out += 2
            else:
                o_ref[out] = yh.astype(BF)
                out += 1


def _proj(x, g, w, colscale, plan, gs, *, tm=1024):
    rows = x.shape[0]
    wout = w.shape[1]
    assert wout == LANES * len(plan) and len(plan) % 2 == 0
    c = sum(2 if dup else 1 for _, dup in plan)
    return pl.pallas_call(
        functools.partial(_proj_kernel, plan=tuple(plan), gs=gs),
        grid=(rows // tm,),
        in_specs=[pl.BlockSpec((tm, D_MODEL), lambda i: (i, 0)),
                  _resident((1, D_MODEL)),
                  _resident(w.shape),
                  _resident((1, wout))],
        out_specs=pl.BlockSpec((c, tm, LANES), lambda i: (0, i, 0)),
        out_shape=jax.ShapeDtypeStruct((c, rows, LANES), BF),
        compiler_params=_params(("parallel",), VMEM_LIMIT),
        name="proj",
    )(x, g.reshape(1, D_MODEL), w, colscale.reshape(1, wout).astype(F32))


def _swa_block(q_blocks, kp, kc, vp, vc, has_prev, slopes_ref, sinks_ref, *, kv_div, max_dist):
    row = lax.broadcasted_iota(jnp.int32, (BLOCK, 2 * BLOCK), 0)
    col = lax.broadcasted_iota(jnp.int32, (BLOCK, 2 * BLOCK), 1)
    dist = row + BLOCK - col
    valid = (dist >= 0) & (dist <= max_dist)
    if has_prev is not True:
        valid = valid & ((col >= BLOCK) | has_prev)
    negmask = jnp.where(valid, 0.0, -jnp.inf)
    distf = dist.astype(F32)
    lo = lax.broadcasted_iota(jnp.int32, (BLOCK, LANES), 1) < HEAD_DIM

    n_groups = len(q_blocks) // kv_div
    heads_per_group = 2 * kv_div
    scores = []
    for g in range(n_groups):
        parts = []
        for p in range(g * kv_div, (g + 1) * kv_div):
            q2 = q_blocks[p].astype(F32)
            parts += [jnp.where(lo, q2, 0.0), jnp.where(lo, 0.0, q2)]
        q_stack = jnp.concatenate(parts, axis=0).astype(BF)
        scores.append(jnp.concatenate(
            [lax.dot_general(q_stack, kp[g], NT_DIMS, preferred_element_type=F32),
             lax.dot_general(q_stack, kc[g], NT_DIMS, preferred_element_type=F32)], axis=1))
    soft = []
    for g in range(n_groups):
        res = []
        for j in range(heads_per_group):
            h = g * heads_per_group + j
            s = scores[g][j * BLOCK:(j + 1) * BLOCK] - slopes_ref[h] * distf + negmask
            m = jnp.maximum(jnp.max(s, axis=1, keepdims=True), sinks_ref[h])
            pe = jnp.exp2(s - m)
            res.append((pe.astype(BF), jnp.sum(pe, axis=1, keepdims=True) + jnp.exp2(sinks_ref[h] - m)))
        soft.append(res)
    outs = []
    for g in range(n_groups):
        pb = jnp.concatenate([r[0] for r in soft[g]], axis=0)
        pv = (jnp.dot(pb[:, :BLOCK], vp[g], preferred_element_type=F32)
              + jnp.dot(pb[:, BLOCK:], vc[g], preferred_element_type=F32))
        for jp in range(kv_div):
            o0 = pv[(2 * jp) * BLOCK:(2 * jp + 1) * BLOCK] / soft[g][2 * jp][1]
            o1 = pv[(2 * jp + 1) * BLOCK:(2 * jp + 2) * BLOCK] / soft[g][2 * jp + 1][1]
            outs.append(jnp.where(lo, o0, o1))
    return outs


DIL_ORDER = tuple(sorted(C_PATTERNS, key=lambda wd: -wd[1]))
DIL_UNROLL = 16
DIL_AHEAD = 2
DIL_BASE = 4
DIL_Q = SEQ // DIL_BASE
DIL_CONVERT_ROWS = DIL_BASE * BLOCK


def _dilated_kernel(slopes_ref, q_ref, k_ref, v_ref, o_ref, qn_s, tq_s, tk_s, tv_s, q0_s, q1_s, k_s, v_s,
                    acc_r, m_r, l_r, acc_n, m_n, l_n):
    assert all(d == 1 or d % DIL_BASE == 0 for _, d in DIL_ORDER) and DIL_ORDER[-1][1] == 1
    p = pl.program_id(1)
    lo = lax.broadcasted_iota(jnp.int32, (BLOCK, LANES), 1) < HEAD_DIM

    def convert(c, carry):
        rows = pl.ds(pl.multiple_of(c * DIL_CONVERT_ROWS, DIL_CONVERT_ROWS), DIL_CONVERT_ROWS)
        q_nat = q_ref[0, rows, :].astype(F32)
        lo_c = lax.broadcasted_iota(jnp.int32, (DIL_CONVERT_ROWS, LANES), 1) < HEAD_DIM
        qn_s[0, rows, :] = jnp.where(lo_c, q_nat, 0.0).astype(BF)
        qn_s[1, rows, :] = jnp.where(lo_c, 0.0, q_nat).astype(BF)
        tq_s[...] = q_nat
        tk_s[...] = k_ref[0, rows, :].astype(F32)
        tv_s[...] = v_ref[0, rows, :].astype(F32)
        for rho in range(DIL_BASE):
            src = pl.ds(rho, BLOCK, stride=DIL_BASE)
            dst = pl.ds(pl.multiple_of(rho * DIL_Q + c * BLOCK, BLOCK), BLOCK)
            q = tq_s[src, :]
            q0_s[dst, :] = jnp.where(lo, q, 0.0)
            q1_s[dst, :] = jnp.where(lo, 0.0, q)
            k_s[dst, :] = tk_s[src, :]
            v_s[dst, :] = tv_s[src, :]
        return carry

    lax.fori_loop(0, SEQ // DIL_CONVERT_ROWS, convert, 0)

    row = lax.broadcasted_iota(jnp.int32, (BLOCK, 2 * BLOCK), 0)
    col = lax.broadcasted_iota(jnp.int32, (BLOCK, 2 * BLOCK), 1)
    dist = row + BLOCK - col
    distf = dist.astype(F32)
    no_prev = jnp.where(col < BLOCK, -jnp.inf, 0.0)

    def bcast2(a0, a1):
        return jnp.where(lo, jnp.broadcast_to(a0, (BLOCK, LANES)), jnp.broadcast_to(a1, (BLOCK, LANES)))

    for pi, (window, d) in enumerate(DIL_ORDER):
        first, last = pi == 0, pi == len(DIL_ORDER) - 1
        natural = d == 1
        nb = SEQ // d // BLOCK
        band = (dist >= 0) & (dist <= window // d)
        bias = [jnp.where(band, (-float(d) * slopes_ref[2 * p + hh]) * distf, -jnp.inf) for hh in range(2)]
        acc_s, m_s, l_s = (acc_n, m_n, l_n) if natural else (acc_r, m_r, l_r)
        to_natural = not natural and not last and DIL_ORDER[pi + 1][1] == 1
        assert (not to_natural or d == DIL_BASE) and (not natural or first or DIL_ORDER[pi - 1][1] == DIL_BASE)
        acc_o, m_o, l_o = (acc_n, m_n, l_n) if to_natural else (acc_s, m_s, l_s)

        def rows_of(r, n, n_blocks=1, d=d, natural=natural):
            size = n_blocks * BLOCK
            if natural:
                return pl.ds(pl.multiple_of(BLOCK * n, BLOCK), size)
            inner = d // DIL_BASE
            start = (r % DIL_BASE) * DIL_Q + r // DIL_BASE + inner * BLOCK * n
            return pl.ds(start, size, stride=inner) if inner > 1 else pl.ds(pl.multiple_of(start, BLOCK), size)

        def step(it, carry, nb=nb, bias=bias, first=first, last=last, natural=natural, rows_of=rows_of,
                 acc_s=acc_s, m_s=m_s, l_s=l_s, acc_o=acc_o, m_o=m_o, l_o=l_o, to_natural=to_natural):
            assert DIL_UNROLL % nb == 0 or nb % DIL_UNROLL == 0
            load_k = (lambda rr: k_ref[0, rr, :]) if natural else (lambda rr: k_s[rr, :].astype(BF))
            load_v = (lambda rr: v_ref[0, rr, :]) if natural else (lambda rr: v_s[rr, :].astype(BF))

            def scores(u):
                t = it * DIL_UNROLL + u
                r, n = t // nb, t % nb
                prev = (u % nb != 0) if nb <= DIL_UNROLL else (True if u else None)
                rows = rows_of(r, n)
                out_rows = pl.ds(DIL_BASE * BLOCK * n + r, BLOCK, stride=DIL_BASE) if to_natural else rows
                if prev is True:
                    kv_rows = [rows_of(r, n - 1, 2)]
                elif prev is None:
                    kv_rows = [rows_of(r, jnp.maximum(n - 1, 0)), rows]
                else:
                    kv_rows = [rows]
                if natural:
                    qh = jnp.concatenate([qn_s[0, rows, :], qn_s[1, rows, :]], axis=0)
                else:
                    qh = jnp.concatenate([q0_s[rows, :], q1_s[rows, :]], axis=0).astype(BF)
                s = jnp.concatenate([lax.dot_general(qh, load_k(rr), NT_DIMS, preferred_element_type=F32)
                                     for rr in kv_rows], axis=1)
                return n, prev, rows, out_rows, kv_rows, (s[:BLOCK], s[BLOCK:])

            def softmax_pv(blk):
                n, prev, rows, out_rows, kv_rows, s = blk
                ms, ls, pes = [], [], []
                for hh in range(2):
                    sh = s[hh] + (bias[hh][:, BLOCK:] if prev is False else bias[hh])
                    if prev is None:
                        sh = sh + jnp.where(n == 0, no_prev, 0.0)
                    m = jnp.max(sh, axis=1, keepdims=True)
                    pe = jnp.exp2(sh - m)
                    ms.append(m)
                    ls.append(jnp.sum(pe, axis=1, keepdims=True))
                    pes.append(pe.astype(BF))
                pb = jnp.concatenate(pes, axis=0)
                v = jnp.concatenate([load_v(rr) for rr in kv_rows], axis=0) if len(kv_rows) > 1 else load_v(kv_rows[0])
                pv = jnp.dot(pb, v, preferred_element_type=F32)
                return rows, out_rows, ms, ls, (pv[:BLOCK], pv[BLOCK:])

            def merge(rows, out_rows, ms, ls, pv):
                m2 = bcast2(ms[0], ms[1])
                l2 = bcast2(ls[0], ls[1])
                acc2 = jnp.where(lo, pv[0], pv[1])
                if not first:
                    m_old = m_s[rows, :]
                    m_new = jnp.maximum(m_old, m2)
                    a_old, a_new = jnp.exp2(m_old - m_new), jnp.exp2(m2 - m_new)
                    l2 = a_old * l_s[rows, :] + a_new * l2
                    acc2 = a_old * acc_s[rows, :] + a_new * acc2
                    m2 = m_new
                if last:
                    o_ref[0, rows, :] = (acc2 / l2).astype(BF)
                else:
                    m_o[out_rows, :] = m2
                    l_o[out_rows, :] = l2
                    acc_o[out_rows, :] = acc2

            pending = {u: scores(u) for u in range(DIL_AHEAD)}
            done = None
            for u in range(DIL_UNROLL):
                if u + DIL_AHEAD < DIL_UNROLL:
                    pending[u + DIL_AHEAD] = scores(u + DIL_AHEAD)
                cur = softmax_pv(pending.pop(u))
                if done is not None:
                    merge(*done)
                done = cur
            merge(*done)
            return carry

        lax.fori_loop(0, SEQ // BLOCK // DIL_UNROLL, step, 0)


def _dilated(qkv, slopes):
    n_pairs = C_HEADS // 2
    seq_f32 = pltpu.VMEM((SEQ, LANES), F32)
    chunk_f32 = pltpu.VMEM((DIL_CONVERT_ROWS, LANES), F32)
    return pl.pallas_call(
        _dilated_kernel,
        grid=(BATCH, n_pairs),
        in_specs=[pl.BlockSpec(memory_space=pltpu.SMEM),
                  pl.BlockSpec((1, SEQ, LANES), lambda b, p: (p, b, 0)),
                  pl.BlockSpec((1, SEQ, LANES), lambda b, p: (n_pairs + p, b, 0)),
                  pl.BlockSpec((1, SEQ, LANES), lambda b, p: (2 * n_pairs + p, b, 0))],
        out_specs=pl.BlockSpec((1, SEQ, LANES), lambda b, p: (p, b, 0)),
        out_shape=jax.ShapeDtypeStruct((n_pairs, N_TOK, LANES), BF),
        scratch_shapes=[pltpu.VMEM((2, SEQ, LANES), BF)] + [chunk_f32] * 3 + [seq_f32] * 10,
        compiler_params=_params(("parallel", "parallel"), VMEM_LIMIT),
        name="dilated",
    )(slopes, qkv, qkv, qkv)


SB_QB = MXU_N
SB_FIRST_TILES = 2
SB_DEAD_LOG2 = -150.0


def _sb_matrix():
    idx = np.arange(SB_QB)
    return jnp.asarray(-(idx[:, None] > idx[None, :]).astype(np.float32), dtype=BF)


def _sb_unit(q2, load_k, load_v, iq, uo):
    lo = lax.broadcasted_iota(jnp.int32, (SB_QB, LANES), 1) < HEAD_DIM
    q_stack = jnp.concatenate([jnp.where(lo, q2, 0.0), jnp.where(lo, 0.0, q2)], axis=0).astype(BF)
    rel1 = (lax.broadcasted_iota(jnp.int32, (SB_QB, SB_QB), 1)
            - lax.broadcasted_iota(jnp.int32, (SB_QB, SB_QB), 0))
    rel = jnp.concatenate([rel1, rel1], axis=0)

    def scores(first, n_tiles):
        return lax.dot_general(q_stack, load_k(first, n_tiles), NT_DIMS,
                               preferred_element_type=F32)

    def walk(first, n_tiles, carry, masked):
        c, o = carry
        z = scores(first, n_tiles)
        order = list(reversed(range(n_tiles)))
        ws, es, stricts, totals = {}, {}, {}, {}
        for t in order:
            zt = z[:, t * SB_QB:(t + 1) * SB_QB]
            sp = jnp.maximum(zt, 0.0) + jnp.log2(1.0 + jnp.exp2(-jnp.abs(zt)))
            es[t] = zt - sp
            if masked:
                stricts[t] = rel < (iq - first - t) * SB_QB
                sp = jnp.where(stricts[t], sp, 0.0)
            ws[t] = jnp.dot(sp.astype(BF), uo, preferred_element_type=F32)
            totals[t] = jnp.sum(sp, axis=1, keepdims=True)
        parts = {}
        for t in order:
            a = jnp.exp2(es[t] + jnp.concatenate([c] * (SB_QB // LANES), axis=1) + ws[t])
            if masked:
                a = jnp.where(stricts[t], a, 0.0)
            parts[t] = a.astype(BF)
            c = c - totals[t]
        pv = jnp.dot(jnp.concatenate([parts[t] for t in range(n_tiles)], axis=1), load_v(first, n_tiles),
                     preferred_element_type=F32)
        return c, o + jnp.where(lo, pv[:SB_QB], pv[SB_QB:])

    def alive(c):
        return jnp.max(c) > SB_DEAD_LOG2

    def body(state):
        g = state[0]
        c, o = walk(g, 1, state[2:], False)
        return g - 1, alive(c), c, o

    first = jnp.maximum(iq - 1, 0)
    zeros = (jnp.zeros((2 * SB_QB, LANES), F32), jnp.zeros((SB_QB, LANES), F32))
    c, o = walk(first, SB_FIRST_TILES, zeros, True)

    def finish():
        state = lax.while_loop(lambda st: (st[0] >= 0) & st[1], body, (first - 1, alive(c), c, o))
        return state[3]

    return finish


def _mix_xattn_kernel(*refs):
    x_ref = refs[0]
    wm_ref, g_ref, wq_ref, cs_ref, kv_ref, wo_ref, o_ref = refs[-7:]
    mixed = jnp.concatenate([r[c] for r in refs[1:-7] for c in range(r.shape[0])], axis=1)
    xv = x_ref[...] + jnp.dot(mixed, wm_ref[...], preferred_element_type=F32)
    h = _rms(xv, g_ref[...]).astype(BF)
    heads = range(X_HEADS)
    cols = [slice(X_HEAD_DIM * hd, X_HEAD_DIM * (hd + 1)) for hd in heads]
    acc = [jnp.dot(h, wq_ref[:, cols[hd]], preferred_element_type=F32) for hd in heads]
    ms = [jnp.mean(acc[hd] * acc[hd], axis=1, keepdims=True) for hd in heads]
    q = [(acc[hd] * cs_ref[:, cols[hd]] * lax.rsqrt(ms[hd] + RMS_EPS)).astype(BF) for hd in heads]
    s = [lax.dot_general(q[hd], jnp.concatenate([kv_ref[2 * hd], kv_ref[2 * hd + 1]], axis=1), NT_DIMS,
                         preferred_element_type=F32) for hd in heads]
    pe, l = [], []
    for hd in heads:
        e = jnp.exp2(s[hd] - jnp.max(s[hd], axis=1, keepdims=True))
        l.append(jnp.sum(e, axis=1, keepdims=True))
        pe.append(e.astype(BF))
    v0 = 2 * X_HEADS
    pv = [jnp.dot(pe[hd], jnp.concatenate([kv_ref[v0 + 2 * hd], kv_ref[v0 + 2 * hd + 1]], axis=1),
                  preferred_element_type=F32) for hd in heads]
    o = jnp.concatenate([(pv[hd] / l[hd]).astype(BF) for hd in heads], axis=1)
    o_ref[...] = xv + jnp.dot(o, wo_ref[...], preferred_element_type=F32)


def _mix_xattn(x, mixer_heads, w_mix, g, w_q, q_colscale, kv, w_o, *, tm=1024):
    tiles_per_batch = SEQ // tm
    in_specs = [pl.BlockSpec((tm, D_MODEL), lambda i: (i, 0))]
    in_specs += [pl.BlockSpec((mh.shape[0], tm, LANES), lambda i: (0, i, 0)) for mh in mixer_heads]
    in_specs += [_resident(w_mix.shape),
                 _resident((1, D_MODEL)),
                 _resident(w_q.shape),
                 _resident((1, D_MODEL)),
                 pl.BlockSpec((4 * X_HEADS, MEM_LEN, LANES), lambda i: (0, i // tiles_per_batch, 0)),
                 _resident(w_o.shape)]
    return pl.pallas_call(
        _mix_xattn_kernel, grid=(N_TOK // tm,),
        in_specs=in_specs,
        out_specs=pl.BlockSpec((tm, D_MODEL), lambda i: (i, 0)),
        out_shape=jax.ShapeDtypeStruct((N_TOK, D_MODEL), F32),
        compiler_params=_params(("parallel",), VMEM_LIMIT),
        name="mix_xattn",
    )(x, *mixer_heads, w_mix, g.reshape(1, D_MODEL), w_q,
      q_colscale.reshape(1, D_MODEL).astype(F32), kv, w_o)


EVEN_TILE = 512


def _even_tail_kernel(slopes_ref, sinks_ref, x_ref, swq_ref, swkp_ref, swkc_ref, swvp_ref, swvc_ref,
                      sbq_ref, sbk_ref, sbv_ref, uo_ref, wm_ref, g_ref, wq_ref, cs_ref, kv_ref, wo_ref,
                      o_ref, heads_s, *, kv_div, max_dist):
    i = pl.program_id(0)
    tiles_per_seq = SEQ // EVEN_TILE
    t_in_seq = jnp.minimum(i, N_TOK // EVEN_TILE - 1) % tiles_per_seq
    wr, rd = i % 2, (i + 1) % 2
    n_pairs_a, n_pairs_b = swq_ref.shape[0], sbq_ref.shape[0]

    @pl.when(i == 0)
    def _():
        heads_s[1] = jnp.zeros(heads_s.shape[1:], BF)

    def mix_steps():
        heads = range(X_HEADS)
        cols = [slice(X_HEAD_DIM * hd, X_HEAD_DIM * (hd + 1)) for hd in heads]
        st = {"xv": [None] * X_HEADS, "q": [None] * X_HEADS, "pe": [None] * X_HEADS, "l": [None] * X_HEADS,
              "o": [None] * X_HEADS}

        def project(c):
            if c == 0:
                st["mixed"] = jnp.concatenate([heads_s[rd, k] for k in range(n_pairs_a + n_pairs_b)], axis=1)
            st["xv"][c] = x_ref[:, cols[c]] + jnp.dot(st["mixed"], wm_ref[:, cols[c]], preferred_element_type=F32)

        def q_proj(hd):
            if hd == 0:
                xv = jnp.concatenate(st["xv"], axis=1)
                st["h"] = _rms(xv, g_ref[...]).astype(BF)
            acc = jnp.dot(st["h"], wq_ref[:, cols[hd]], preferred_element_type=F32)
            ms = jnp.mean(acc * acc, axis=1, keepdims=True)
            st["q"][hd] = (acc * cs_ref[:, cols[hd]] * lax.rsqrt(ms + RMS_EPS)).astype(BF)

        def scores(hd):
            kh = jnp.concatenate([kv_ref[2 * hd], kv_ref[2 * hd + 1]], axis=1)
            s = lax.dot_general(st["q"][hd], kh, NT_DIMS, preferred_element_type=F32)
            e = jnp.exp2(s - jnp.max(s, axis=1, keepdims=True))
            st["l"][hd] = jnp.sum(e, axis=1, keepdims=True)
            st["pe"][hd] = e.astype(BF)

        def values(hd):
            v0 = 2 * X_HEADS
            vh = jnp.concatenate([kv_ref[v0 + 2 * hd], kv_ref[v0 + 2 * hd + 1]], axis=1)
            st["o"][hd] = (jnp.dot(st["pe"][hd], vh, preferred_element_type=F32) / st["l"][hd]).astype(BF)

        def out_proj(c):
            if c == 0:
                st["oc"] = jnp.concatenate(st["o"], axis=1)
            o_ref[:, cols[c]] = st["xv"][c] + jnp.dot(st["oc"], wo_ref[:, cols[c]], preferred_element_type=F32)

        return [functools.partial(f, k) for f in (project, q_proj, scores, values, out_proj) for k in heads]

    def attend_and_mix():
        for step in mix_steps():
            step()

        uo = uo_ref[...]
        blocks_per_tile = EVEN_TILE // SB_QB

        def sb_first_pass(p, j):
            rows = slice(SB_QB * j, SB_QB * (j + 1))
            load = lambda ref: (lambda first, n: ref[p, pl.ds(pl.multiple_of(first * SB_QB, SB_QB), n * SB_QB), :])
            return _sb_unit(sbq_ref[p, rows, :].astype(F32), load(sbk_ref), load(sbv_ref),
                            blocks_per_tile * t_in_seq + j, uo)

        units = [(p, j) for p in range(n_pairs_b) for j in range(blocks_per_tile)]
        finishers = [sb_first_pass(p, j) for p, j in units]

        n_kv = swkc_ref.shape[0]
        for m in range(EVEN_TILE // BLOCK):
            rows = slice(BLOCK * m, BLOCK * (m + 1))
            if m == 0:
                kp, vp, has_prev = [swkp_ref[g] for g in range(n_kv)], [swvp_ref[g] for g in range(n_kv)], t_in_seq > 0
            else:
                prows = slice(BLOCK * (m - 1), BLOCK * m)
                kp, vp, has_prev = ([swkc_ref[g, prows, :] for g in range(n_kv)],
                                    [swvc_ref[g, prows, :] for g in range(n_kv)], True)
            outs = _swa_block([swq_ref[p, rows, :] for p in range(n_pairs_a)], kp,
                              [swkc_ref[g, rows, :] for g in range(n_kv)], vp, [swvc_ref[g, rows, :] for g in range(n_kv)],
                              has_prev, slopes_ref, sinks_ref, kv_div=kv_div, max_dist=max_dist)
            for p in range(n_pairs_a):
                heads_s[wr, p, rows, :] = outs[p].astype(BF)

        for (p, j), finish in zip(units, finishers):
            heads_s[wr, n_pairs_a + p, SB_QB * j:SB_QB * (j + 1), :] = finish().astype(BF)

    n_tiles = N_TOK // EVEN_TILE
    pl.when(i < n_tiles)(attend_and_mix)

    @pl.when(i == n_tiles)
    def _():
        for step in mix_steps():
            step()


def _even_tail(x, p, slopes, sinks, w_mix, g, w_q, q_colscale, kv, w_o):
    n_pairs_a, n_kv, n_pairs_b = A_Q_HEADS // 2, A_KV_HEADS, B_HEADS // 2
    assert p.shape[0] == n_pairs_a + 2 * n_kv + 3 * n_pairs_b and n_pairs_a == n_pairs_b == 2 * n_kv
    n_tiles = N_TOK // EVEN_TILE
    tiles_per_seq = SEQ // EVEN_TILE
    blocks_per_tile = EVEN_TILE // BLOCK
    att = lambda i: jnp.minimum(i, n_tiles - 1)
    mix = lambda i: jnp.maximum(i - 1, 0)
    tile_rows = lambda size, idx: pl.BlockSpec((size, EVEN_TILE, LANES), lambda i: (idx, att(i), 0))
    prev_block = lambda idx: pl.BlockSpec(
        (n_kv, BLOCK, LANES), lambda i: (idx, jnp.maximum(blocks_per_tile * att(i) - 1, 0), 0))
    whole_seq = lambda idx: pl.BlockSpec((n_pairs_b, SEQ, LANES), lambda i: (idx, att(i) // tiles_per_seq, 0),
                                         pipeline_mode=pl.Buffered(1))
    smem = pl.BlockSpec(memory_space=pltpu.SMEM)
    return pl.pallas_call(
        functools.partial(_even_tail_kernel, kv_div=n_pairs_a // n_kv, max_dist=A_WINDOW - 1),
        grid=(n_tiles + 1,),
        in_specs=[smem, smem,
                  pl.BlockSpec((EVEN_TILE, D_MODEL), lambda i: (mix(i), 0)),
                  tile_rows(n_pairs_a, 0), prev_block(2), tile_rows(n_kv, 2), prev_block(3), tile_rows(n_kv, 3),
                  tile_rows(n_pairs_b, 2), whole_seq(3), whole_seq(4),
                  _resident((SB_QB, SB_QB)),
                  _resident(w_mix.shape), _resident((1, D_MODEL)), _resident(w_q.shape), _resident((1, D_MODEL)),
                  pl.BlockSpec((4 * X_HEADS, MEM_LEN, LANES), lambda i: (0, mix(i) // tiles_per_seq, 0)),
                  _resident(w_o.shape)],
        out_specs=pl.BlockSpec((EVEN_TILE, D_MODEL), lambda i: (mix(i), 0)),
        out_shape=jax.ShapeDtypeStruct((N_TOK, D_MODEL), F32),
        scratch_shapes=[pltpu.VMEM((2, n_pairs_a + n_pairs_b, EVEN_TILE, LANES), BF)],
        compiler_params=_params(("arbitrary",), VMEM_LIMIT),
        name="even_tail",
    )(slopes, sinks, x, p, p, p, p, p, p, p, p, _sb_matrix(), w_mix, g.reshape(1, D_MODEL), w_q,
      q_colscale.reshape(1, D_MODEL).astype(F32), kv, w_o)


def _alibi_log2(n_heads):
    return jnp.asarray(LOG2_E * 2.0 ** (-8.0 * np.arange(1, n_heads + 1) / n_heads), dtype=F32)


def _even_projection(x, norm_g, w_in, q_gain, k_gain):
    hd = HEAD_DIM
    a_q, a_kv, b_w = A_Q_HEADS * hd, A_KV_HEADS * hd, B_HEADS * hd
    scale = hd ** -0.5 * LOG2_E
    ones = lambda n: jnp.ones((n,), F32)
    cs = jnp.concatenate([jnp.tile(q_gain, A_Q_HEADS) * scale, jnp.tile(k_gain, A_KV_HEADS), ones(a_kv),
                          ones(b_w) * scale, ones(2 * b_w)])
    plan = ([(True, False)] * (a_q // LANES) + [(True, True)] * (a_kv // LANES) + [(False, True)] * (a_kv // LANES)
            + [(False, False)] * (3 * b_w // LANES))
    return _proj(x, norm_g, w_in, cs, plan, hd)


def _odd_mixer_heads(x, norm_g, w_in, q_gain, k_gain):
    hd = HEAD_DIM
    cs = jnp.concatenate([jnp.tile(q_gain, C_HEADS) * (hd ** -0.5 * LOG2_E), jnp.tile(k_gain, C_HEADS),
                          jnp.ones((C_HEADS * hd,), F32)])
    head_blocks = C_HEADS * hd // LANES
    plan = [(True, False)] * (2 * head_blocks) + [(False, False)] * head_blocks
    p = _proj(x, norm_g, w_in, cs, plan, hd)
    return [_dilated(p, _alibi_log2(C_HEADS))]


def _memory_kv(mem2d, mem_g, w_kv, k_gain):
    cs_kv = jnp.concatenate([jnp.tile(k_gain, X_HEADS), jnp.ones((D_MODEL,), F32)])
    plan = [(True, False)] * (D_MODEL // LANES) + [(False, False)] * (D_MODEL // LANES)
    return _proj(mem2d, mem_g, w_kv, cs_kv, plan, X_HEAD_DIM, tm=MEM_LEN)


def kernel(x, mem, ffn1_norm, ffn1_w_gu, ffn1_w_down, mix_norm, ev_w_in, ev_q_gain, ev_k_gain, ev_sinks, ev_w_out, od_w_in, od_q_gain, od_k_gain, od_w_out, xa_norm, xa_mem_norm, xa_w_q, xa_w_kv, xa_q_gain, xa_k_gain, xa_w_o, ffn2_norm, ffn2_w_gu, ffn2_w_down):
    x = x.reshape(N_TOK, D_MODEL)
    mem2d = mem.reshape(BATCH * MEM_LEN, D_MODEL)
    w_gu, w_down = _cast_now((ffn1_w_gu, 0)), _cast_now((ffn1_w_down, 0))
    for layer in range(DEPTH):
        j = layer // 2
        even = layer % 2 == 0
        w_in3, w_mix3 = (ev_w_in, ev_w_out) if even else (od_w_in, od_w_out)
        jobs = [(w_in3, j), (w_mix3, j), (xa_w_q, layer), (xa_w_kv, layer), (xa_w_o, layer),
                (ffn2_w_gu, layer), (ffn2_w_down, layer)]
        x, (w_in, w_mix, w_q, w_kv, w_o, w_gu, w_down) = _ffn(x, ffn1_norm[layer], w_gu, w_down, jobs)
        kv = _memory_kv(mem2d, xa_mem_norm[layer], w_kv, xa_k_gain[layer])
        cs_q = jnp.tile(xa_q_gain[layer], X_HEADS) * (X_HEAD_DIM ** -0.5 * LOG2_E)
        if even:
            p = _even_projection(x, mix_norm[layer], w_in, ev_q_gain[j], ev_k_gain[j])
            x = _even_tail(x, p, _alibi_log2(A_Q_HEADS), ev_sinks[j].astype(F32) * LOG2_E, w_mix,
                           xa_norm[layer], w_q, cs_q, kv, w_o)
        else:
            heads = _odd_mixer_heads(x, mix_norm[layer], w_in, od_q_gain[j], od_k_gain[j])
            x = _mix_xattn(x, heads, w_mix, xa_norm[layer], w_q, cs_q, kv, w_o)
        jobs = [(ffn1_w_gu, layer + 1), (ffn1_w_down, layer + 1)] if layer + 1 < DEPTH else []
        x, next_ffn1 = _ffn(x, ffn2_norm[layer], w_gu, w_down, jobs)
        if next_ffn1:
            w_gu, w_down = next_ffn1
    return x.reshape(BATCH, SEQ, D_MODEL)
```

```python
import functools

import numpy as np
import jax
import jax.numpy as jnp
from jax import lax
from jax.experimental import pallas as pl
from jax.experimental.pallas import tpu as pltpu

D_MODEL = 1024
BATCH = 4
SEQ = 4096
N_TOK = BATCH * SEQ
DEPTH = 2
HEAD_DIM = 64
BLOCK = 128
A_Q_HEADS = 8
A_KV_HEADS = 2
A_WINDOW = 128
B_HEADS = 8
C_HEADS = 16
C_PATTERNS = ((128, 1), (512, 4), (2048, 16))
MEM_LEN = 256
X_HEADS = 4
X_HEAD_DIM = D_MODEL // X_HEADS
D_FF = 2816
RMS_EPS = 1e-6

LANES = 128
MXU_N = 256
VMEM_LIMIT = 56 * 1024 * 1024

BF = jnp.bfloat16
F32 = jnp.float32
NT_DIMS = (((1,), (1,)), ((), ()))
LOG2_E = 1.4426950408889634


def _params(sem, vmem=None):
    return pltpu.CompilerParams(dimension_semantics=sem, vmem_limit_bytes=vmem)


def _resident(shape):
    nd = len(shape)
    return pl.BlockSpec(shape, lambda *_: (0,) * nd, pipeline_mode=pl.Buffered(1))


BF16_SUBLANES = 16


def _cast_specs(job, steps):
    w3, layer = job
    _, r, c = w3.shape
    rb = next(rb for rb in range(BF16_SUBLANES, r + 1, BF16_SUBLANES) if r % rb == 0 and r // rb <= steps)
    last = r // rb - 1
    return (pl.BlockSpec((None, rb, c), lambda i: (layer, jnp.minimum(i, last), 0)),
            pl.BlockSpec((rb, c), lambda i: (jnp.minimum(i, last), 0)),
            jax.ShapeDtypeStruct((r, c), BF))


def _run_cast_jobs(in_refs, out_refs):
    for src, dst in zip(in_refs, out_refs):
        dst[...] = src[...].astype(BF)


def _cast_kernel(w_ref, o_ref):
    _run_cast_jobs([w_ref], [o_ref])


def _cast_now(job, *, rows=128):
    steps = job[0].shape[1] // rows
    in_spec, out_spec, out_shape = _cast_specs(job, steps)
    return pl.pallas_call(
        _cast_kernel, grid=(steps,), in_specs=[in_spec], out_specs=out_spec, out_shape=out_shape,
        compiler_params=_params(("arbitrary",)),
        name="cast",
    )(job[0])


def _rms(xv, g):
    ms = jnp.mean(xv * xv, axis=-1, keepdims=True)
    return xv * lax.rsqrt(ms + RMS_EPS) * g


FFN_SPLIT = (D_FF // MXU_N + 1) // 2 * MXU_N
FFN_CHUNKS = ((0, FFN_SPLIT), (FFN_SPLIT, D_FF))


def _ffn_kernel(*refs, n_jobs):
    x_ref, g_ref, wgu_ref, wd_ref = refs[:4]
    o_ref = refs[4 + n_jobs]
    xv = x_ref[...]
    h = _rms(xv, g_ref[...]).astype(BF)
    acc = jnp.zeros_like(xv)
    for c0, c1 in FFN_CHUNKS:
        gate = jnp.dot(h, wgu_ref[:, c0:c1], preferred_element_type=F32)
        up = jnp.dot(h, wgu_ref[:, D_FF + c0:D_FF + c1], preferred_element_type=F32)
        act = (gate * jax.nn.sigmoid(gate) * up).astype(BF)
        acc = acc + jnp.dot(act, wd_ref[c0:c1, :], preferred_element_type=F32)
    o_ref[...] = xv + 0.5 * acc
    _run_cast_jobs(refs[4:4 + n_jobs], refs[5 + n_jobs:])


def _ffn(x, g, w_gu, w_down, cast_jobs=(), *, tm=512):
    steps = N_TOK // tm
    specs = [_cast_specs(job, steps) for job in cast_jobs]
    out = pl.pallas_call(
        functools.partial(_ffn_kernel, n_jobs=len(cast_jobs)),
        grid=(steps,),
        in_specs=[pl.BlockSpec((tm, D_MODEL), lambda i: (i, 0)),
                  _resident((1, D_MODEL)),
                  _resident(w_gu.shape),
                  _resident(w_down.shape)] + [s[0] for s in specs],
        out_specs=[pl.BlockSpec((tm, D_MODEL), lambda i: (i, 0))] + [s[1] for s in specs],
        out_shape=[jax.ShapeDtypeStruct((N_TOK, D_MODEL), F32)] + [s[2] for s in specs],
        compiler_params=_params(("arbitrary",), VMEM_LIMIT),
        name="ffn",
    )(x, g.reshape(1, D_MODEL), w_gu, w_down, *[job[0] for job in cast_jobs])
    return out[0], out[1:]


def _proj_kernel(x_ref, g_ref, w_ref, cs_ref, o_ref, *, plan, gs):
    assert gs in (HEAD_DIM, MXU_N)
    h = _rms(x_ref[...], g_ref[...]).astype(BF)
    n_chunks = len(plan) // 2
    lo = lax.broadcasted_iota(jnp.int32, (x_ref.shape[0], LANES), 1) < HEAD_DIM

    def main(j):
        return jnp.dot(h, w_ref[:, MXU_N * j:MXU_N * (j + 1)], preferred_element_type=F32)

    acc_next = main(0)
    out = 0
    for j in range(n_chunks):
        cols = slice(MXU_N * j, MXU_N * (j + 1))
        acc = acc_next
        if j + 1 < n_chunks:
            acc_next = main(j + 1)
        y = acc * cs_ref[:, cols]
        halves = plan[2 * j:2 * j + 2]
        if gs == MXU_N and any(normed for normed, _ in halves):
            inv_chunk = lax.rsqrt(jnp.mean(acc * acc, axis=1, keepdims=True) + RMS_EPS)
        for half, (normed, dup) in enumerate(halves):
            lanes = slice(LANES * half, LANES * (half + 1))
            yh = y[:, lanes]
            if normed and gs == MXU_N:
                yh = yh * inv_chunk
            elif normed:
                sq = acc[:, lanes] * acc[:, lanes]
                s_lo = jnp.sum(jnp.where(lo, sq, 0.0), axis=1, keepdims=True)
                s_hi = jnp.sum(jnp.where(lo, 0.0, sq), axis=1, keepdims=True)
                yh = yh * lax.rsqrt(jnp.where(lo, s_lo, s_hi) * (1.0 / gs) + RMS_EPS)
            if dup:
                swapped = pltpu.roll(yh, HEAD_DIM, axis=1)
                o_ref[out] = jnp.where(lo, yh, swapped).astype(BF)
                o_ref[out + 1] = jnp.where(lo, swapped, yh).astype(BF)
                out += 2
            else:
                o_ref[out] = yh.astype(BF)
                out += 1


def _proj(x, g, w, colscale, plan, gs, *, tm=1024):
    rows = x.shape[0]
    wout = w.shape[1]
    assert wout == LANES * len(plan) and len(plan) % 2 == 0
    c = sum(2 if dup else 1 for _, dup in plan)
    return pl.pallas_call(
        functools.partial(_proj_kernel, plan=tuple(plan), gs=gs),
        grid=(rows // tm,),
        in_specs=[pl.BlockSpec((tm, D_MODEL), lambda i: (i, 0)),
                  _resident((1, D_MODEL)),
                  _resident(w.shape),
                  _resident((1, wout))],
        out_specs=pl.BlockSpec((c, tm, LANES), lambda i: (0, i, 0)),
        out_shape=jax.ShapeDtypeStruct((c, rows, LANES), BF),
        compiler_params=_params(("parallel",), VMEM_LIMIT),
        name="proj",
    )(x, g.reshape(1, D_MODEL), w, colscale.reshape(1, wout).astype(F32))


def _swa_block(q_blocks, kp, kc, vp, vc, has_prev, slopes_ref, sinks_ref, *, kv_div, max_dist):
    row = lax.broadcasted_iota(jnp.int32, (BLOCK, 2 * BLOCK), 0)
    col = lax.broadcasted_iota(jnp.int32, (BLOCK, 2 * BLOCK), 1)
    dist = row + BLOCK - col
    valid = (dist >= 0) & (dist <= max_dist)
    if has_prev is not True:
        valid = valid & ((col >= BLOCK) | has_prev)
    negmask = jnp.where(valid, 0.0, -jnp.inf)
    distf = dist.astype(F32)
    lo = lax.broadcasted_iota(jnp.int32, (BLOCK, LANES), 1) < HEAD_DIM

    n_groups = len(q_blocks) // kv_div
    heads_per_group = 2 * kv_div
    scores = []
    for g in range(n_groups):
        parts = []
        for p in range(g * kv_div, (g + 1) * kv_div):
            q2 = q_blocks[p].astype(F32)
            parts += [jnp.where(lo, q2, 0.0), jnp.where(lo, 0.0, q2)]
        q_stack = jnp.concatenate(parts, axis=0).astype(BF)
        scores.append(jnp.concatenate(
            [lax.dot_general(q_stack, kp[g], NT_DIMS, preferred_element_type=F32),
             lax.dot_general(q_stack, kc[g], NT_DIMS, preferred_element_type=F32)], axis=1))
    soft = []
    for g in range(n_groups):
        res = []
        for j in range(heads_per_group):
            h = g * heads_per_group + j
            s = scores[g][j * BLOCK:(j + 1) * BLOCK] - slopes_ref[h] * distf + negmask
            m = jnp.maximum(jnp.max(s, axis=1, keepdims=True), sinks_ref[h])
            pe = jnp.exp2(s - m)
            res.append((pe.astype(BF), jnp.sum(pe, axis=1, keepdims=True) + jnp.exp2(sinks_ref[h] - m)))
        soft.append(res)
    outs = []
    for g in range(n_groups):
        pb = jnp.concatenate([r[0] for r in soft[g]], axis=0)
        pv = (jnp.dot(pb[:, :BLOCK], vp[g], preferred_element_type=F32)
              + jnp.dot(pb[:, BLOCK:], vc[g], preferred_element_type=F32))
        for jp in range(kv_div):
            o0 = pv[(2 * jp) * BLOCK:(2 * jp + 1) * BLOCK] / soft[g][2 * jp][1]
            o1 = pv[(2 * jp + 1) * BLOCK:(2 * jp + 2) * BLOCK] / soft[g][2 * jp + 1][1]
            outs.append(jnp.where(lo, o0, o1))
    return outs


DIL_ORDER = tuple(sorted(C_PATTERNS, key=lambda wd: -wd[1]))
DIL_UNROLL = 16
DIL_AHEAD = 2
DIL_BASE = 4
DIL_Q = SEQ // DIL_BASE
DIL_CONVERT_ROWS = DIL_BASE * BLOCK


def _dilated_kernel(slopes_ref, q_ref, k_ref, v_ref, o_ref, qn_s, tq_s, tk_s, tv_s, q0_s, q1_s, k_s, v_s,
                    acc_r, m_r, l_r, acc_n, m_n, l_n):
    assert all(d == 1 or d % DIL_BASE == 0 for _, d in DIL_ORDER) and DIL_ORDER[-1][1] == 1
    p = pl.program_id(1)
    lo = lax.broadcasted_iota(jnp.int32, (BLOCK, LANES), 1) < HEAD_DIM

    def convert(c, carry):
        rows = pl.ds(pl.multiple_of(c * DIL_CONVERT_ROWS, DIL_CONVERT_ROWS), DIL_CONVERT_ROWS)
        q_nat = q_ref[0, rows, :].astype(F32)
        lo_c = lax.broadcasted_iota(jnp.int32, (DIL_CONVERT_ROWS, LANES), 1) < HEAD_DIM
        qn_s[0, rows, :] = jnp.where(lo_c, q_nat, 0.0).astype(BF)
        qn_s[1, rows, :] = jnp.where(lo_c, 0.0, q_nat).astype(BF)
        tq_s[...] = q_nat
        tk_s[...] = k_ref[0, rows, :].astype(F32)
        tv_s[...] = v_ref[0, rows, :].astype(F32)
        for rho in range(DIL_BASE):
            src = pl.ds(rho, BLOCK, stride=DIL_BASE)
            dst = pl.ds(pl.multiple_of(rho * DIL_Q + c * BLOCK, BLOCK), BLOCK)
            q = tq_s[src, :]
            q0_s[dst, :] = jnp.where(lo, q, 0.0)
            q1_s[dst, :] = jnp.where(lo, 0.0, q)
            k_s[dst, :] = tk_s[src, :]
            v_s[dst, :] = tv_s[src, :]
        return carry

    lax.fori_loop(0, SEQ // DIL_CONVERT_ROWS, convert, 0)

    row = lax.broadcasted_iota(jnp.int32, (BLOCK, 2 * BLOCK), 0)
    col = lax.broadcasted_iota(jnp.int32, (BLOCK, 2 * BLOCK), 1)
    dist = row + BLOCK - col
    distf = dist.astype(F32)
    no_prev = jnp.where(col < BLOCK, -jnp.inf, 0.0)

    def bcast2(a0, a1):
        return jnp.where(lo, jnp.broadcast_to(a0, (BLOCK, LANES)), jnp.broadcast_to(a1, (BLOCK, LANES)))

    for pi, (window, d) in enumerate(DIL_ORDER):
        first, last = pi == 0, pi == len(DIL_ORDER) - 1
        natural = d == 1
        nb = SEQ // d // BLOCK
        band = (dist >= 0) & (dist <= window // d)
        bias = [jnp.where(band, (-float(d) * slopes_ref[2 * p + hh]) * distf, -jnp.inf) for hh in range(2)]
        acc_s, m_s, l_s = (acc_n, m_n, l_n) if natural else (acc_r, m_r, l_r)
        to_natural = not natural and not last and DIL_ORDER[pi + 1][1] == 1
        assert (not to_natural or d == DIL_BASE) and (not natural or first or DIL_ORDER[pi - 1][1] == DIL_BASE)
        acc_o, m_o, l_o = (acc_n, m_n, l_n) if to_natural else (acc_s, m_s, l_s)

        def rows_of(r, n, n_blocks=1, d=d, natural=natural):
            size = n_blocks * BLOCK
            if natural:
                return pl.ds(pl.multiple_of(BLOCK * n, BLOCK), size)
            inner = d // DIL_BASE
            start = (r % DIL_BASE) * DIL_Q + r // DIL_BASE + inner * BLOCK * n
            return pl.ds(start, size, stride=inner) if inner > 1 else pl.ds(pl.multiple_of(start, BLOCK), size)

        def step(it, carry, nb=nb, bias=bias, first=first, last=last, natural=natural, rows_of=rows_of,
                 acc_s=acc_s, m_s=m_s, l_s=l_s, acc_o=acc_o, m_o=m_o, l_o=l_o, to_natural=to_natural):
            assert DIL_UNROLL % nb == 0 or nb % DIL_UNROLL == 0
            load_k = (lambda rr: k_ref[0, rr, :]) if natural else (lambda rr: k_s[rr, :].astype(BF))
            load_v = (lambda rr: v_ref[0, rr, :]) if natural else (lambda rr: v_s[rr, :].astype(BF))

            def scores(u):
                t = it * DIL_UNROLL + u
                r, n = t // nb, t % nb
                prev = (u % nb != 0) if nb <= DIL_UNROLL else (True if u else None)
                rows = rows_of(r, n)
                out_rows = pl.ds(DIL_BASE * BLOCK * n + r, BLOCK, stride=DIL_BASE) if to_natural else rows
                if prev is True:
                    kv_rows = [rows_of(r, n - 1, 2)]
                elif prev is None:
                    kv_rows = [rows_of(r, jnp.maximum(n - 1, 0)), rows]
                else:
                    kv_rows = [rows]
                if natural:
                    qh = jnp.concatenate([qn_s[0, rows, :], qn_s[1, rows, :]], axis=0)
                else:
                    qh = jnp.concatenate([q0_s[rows, :], q1_s[rows, :]], axis=0).astype(BF)
                s = jnp.concatenate([lax.dot_general(qh, load_k(rr), NT_DIMS, preferred_element_type=F32)
                                     for rr in kv_rows], axis=1)
                return n, prev, rows, out_rows, kv_rows, (s[:BLOCK], s[BLOCK:])

            def softmax_pv(blk):
                n, prev, rows, out_rows, kv_rows, s = blk
                ms, ls, pes = [], [], []
                for hh in range(2):
                    sh = s[hh] + (bias[hh][:, BLOCK:] if prev is False else bias[hh])
                    if prev is None:
                        sh = sh + jnp.where(n == 0, no_prev, 0.0)
                    m = jnp.max(sh, axis=1, keepdims=True)
                    pe = jnp.exp2(sh - m)
                    ms.append(m)
                    ls.append(jnp.sum(pe, axis=1, keepdims=True))
                    pes.append(pe.astype(BF))
                pb = jnp.concatenate(pes, axis=0)
                v = jnp.concatenate([load_v(rr) for rr in kv_rows], axis=0) if len(kv_rows) > 1 else load_v(kv_rows[0])
                pv = jnp.dot(pb, v, preferred_element_type=F32)
                return rows, out_rows, ms, ls, (pv[:BLOCK], pv[BLOCK:])

            def merge(rows, out_rows, ms, ls, pv):
                m2 = bcast2(ms[0], ms[1])
                l2 = bcast2(ls[0], ls[1])
                acc2 = jnp.where(lo, pv[0], pv[1])
                if not first:
                    m_old = m_s[rows, :]
                    m_new = jnp.maximum(m_old, m2)
                    a_old, a_new = jnp.exp2(m_old - m_new), jnp.exp2(m2 - m_new)
                    l2 = a_old * l_s[rows, :] + a_new * l2
                    acc2 = a_old * acc_s[rows, :] + a_new * acc2
                    m2 = m_new
                if last:
                    o_ref[0, rows, :] = (acc2 / l2).astype(BF)
                else:
                    m_o[out_rows, :] = m2
                    l_o[out_rows, :] = l2
                    acc_o[out_rows, :] = acc2

            pending = {u: scores(u) for u in range(DIL_AHEAD)}
            done = None
            for u in range(DIL_UNROLL):
                if u + DIL_AHEAD < DIL_UNROLL:
                    pending[u + DIL_AHEAD] = scores(u + DIL_AHEAD)
                cur = softmax_pv(pending.pop(u))
                if done is not None:
                    merge(*done)
                done = cur
            merge(*done)
            return carry

        lax.fori_loop(0, SEQ // BLOCK // DIL_UNROLL, step, 0)


def _dilated(qkv, slopes):
    n_pairs = C_HEADS // 2
    seq_f32 = pltpu.VMEM((SEQ, LANES), F32)
    chunk_f32 = pltpu.VMEM((DIL_CONVERT_ROWS, LANES), F32)
    return pl.pallas_call(
        _dilated_kernel,
        grid=(BATCH, n_pairs),
        in_specs=[pl.BlockSpec(memory_space=pltpu.SMEM),
                  pl.BlockSpec((1, SEQ, LANES), lambda b, p: (p, b, 0)),
                  pl.BlockSpec((1, SEQ, LANES), lambda b, p: (n_pairs + p, b, 0)),
                  pl.BlockSpec((1, SEQ, LANES), lambda b, p: (2 * n_pairs + p, b, 0))],
        out_specs=pl.BlockSpec((1, SEQ, LANES), lambda b, p: (p, b, 0)),
        out_shape=jax.ShapeDtypeStruct((n_pairs, N_TOK, LANES), BF),
        scratch_shapes=[pltpu.VMEM((2, SEQ, LANES), BF)] + [chunk_f32] * 3 + [seq_f32] * 10,
        compiler_params=_params(("parallel", "parallel"), VMEM_LIMIT),
        name="dilated",
    )(slopes, qkv, qkv, qkv)


SB_QB = MXU_N
SB_FIRST_TILES = 2
SB_DEAD_LOG2 = -150.0


def _sb_matrix():
    idx = np.arange(SB_QB)
    return jnp.asarray(-(idx[:, None] > idx[None, :]).astype(np.float32), dtype=BF)


def _sb_unit(q2, load_k, load_v, iq, uo):
    lo = lax.broadcasted_iota(jnp.int32, (SB_QB, LANES), 1) < HEAD_DIM
    q_stack = jnp.concatenate([jnp.where(lo, q2, 0.0), jnp.where(lo, 0.0, q2)], axis=0).astype(BF)
    rel1 = (lax.broadcasted_iota(jnp.int32, (SB_QB, SB_QB), 1)
            - lax.broadcasted_iota(jnp.int32, (SB_QB, SB_QB), 0))
    rel = jnp.concatenate([rel1, rel1], axis=0)

    def scores(first, n_tiles):
        return lax.dot_general(q_stack, load_k(first, n_tiles), NT_DIMS,
                               preferred_element_type=F32)

    def walk(first, n_tiles, carry, masked):
        c, o = carry
        z = scores(first, n_tiles)
        order = list(reversed(range(n_tiles)))
        ws, es, stricts, totals = {}, {}, {}, {}
        for t in order:
            zt = z[:, t * SB_QB:(t + 1) * SB_QB]
            sp = jnp.maximum(zt, 0.0) + jnp.log2(1.0 + jnp.exp2(-jnp.abs(zt)))
            es[t] = zt - sp
            if masked:
                stricts[t] = rel < (iq - first - t) * SB_QB
                sp = jnp.where(stricts[t], sp, 0.0)
            ws[t] = jnp.dot(sp.astype(BF), uo, preferred_element_type=F32)
            totals[t] = jnp.sum(sp, axis=1, keepdims=True)
        parts = {}
        for t in order:
            a = jnp.exp2(es[t] + jnp.concatenate([c] * (SB_QB // LANES), axis=1) + ws[t])
            if masked:
                a = jnp.where(stricts[t], a, 0.0)
            parts[t] = a.astype(BF)
            c = c - totals[t]
        pv = jnp.dot(jnp.concatenate([parts[t] for t in range(n_tiles)], axis=1), load_v(first, n_tiles),
                     preferred_element_type=F32)
        return c, o + jnp.where(lo, pv[:SB_QB], pv[SB_QB:])

    def alive(c):
        return jnp.max(c) > SB_DEAD_LOG2

    def body(state):
        g = state[0]
        c, o = walk(g, 1, state[2:], False)
        return g - 1, alive(c), c, o

    first = jnp.maximum(iq - 1, 0)
    zeros = (jnp.zeros((2 * SB_QB, LANES), F32), jnp.zeros((SB_QB, LANES), F32))
    c, o = walk(first, SB_FIRST_TILES, zeros, True)
    more = alive(c)

    def finish():
        state = lax.while_loop(lambda st: (st[0] >= 0) & st[1], body, (first - 1, more, c, o))
        return state[3]

    return finish


def _mix_xattn_kernel(*refs):
    x_ref = refs[0]
    wm_ref, g_ref, wq_ref, cs_ref, kv_ref, wo_ref, o_ref = refs[-7:]
    mixed = jnp.concatenate([r[c] for r in refs[1:-7] for c in range(r.shape[0])], axis=1)
    xv = x_ref[...] + jnp.dot(mixed, wm_ref[...], preferred_element_type=F32)
    h = _rms(xv, g_ref[...]).astype(BF)
    heads = range(X_HEADS)
    cols = [slice(X_HEAD_DIM * hd, X_HEAD_DIM * (hd + 1)) for hd in heads]
    acc = [jnp.dot(h, wq_ref[:, cols[hd]], preferred_element_type=F32) for hd in heads]
    ms = [jnp.mean(acc[hd] * acc[hd], axis=1, keepdims=True) for hd in heads]
    q = [(acc[hd] * cs_ref[:, cols[hd]] * lax.rsqrt(ms[hd] + RMS_EPS)).astype(BF) for hd in heads]
    s = [lax.dot_general(q[hd], jnp.concatenate([kv_ref[2 * hd], kv_ref[2 * hd + 1]], axis=1), NT_DIMS,
                         preferred_element_type=F32) for hd in heads]
    pe, l = [], []
    for hd in heads:
        e = jnp.exp2(s[hd] - jnp.max(s[hd], axis=1, keepdims=True))
        l.append(jnp.sum(e, axis=1, keepdims=True))
        pe.append(e.astype(BF))
    v0 = 2 * X_HEADS
    pv = [jnp.dot(pe[hd], jnp.concatenate([kv_ref[v0 + 2 * hd], kv_ref[v0 + 2 * hd + 1]], axis=1),
                  preferred_element_type=F32) for hd in heads]
    o = jnp.concatenate([(pv[hd] / l[hd]).astype(BF) for hd in heads], axis=1)
    o_ref[...] = xv + jnp.dot(o, wo_ref[...], preferred_element_type=F32)


def _mix_xattn(x, mixer_heads, w_mix, g, w_q, q_colscale, kv, w_o, *, tm=1024):
    tiles_per_batch = SEQ // tm
    in_specs = [pl.BlockSpec((tm, D_MODEL), lambda i: (i, 0))]
    in_specs += [pl.BlockSpec((mh.shape[0], tm, LANES), lambda i: (0, i, 0)) for mh in mixer_heads]
    in_specs += [_resident(w_mix.shape),
                 _resident((1, D_MODEL)),
                 _resident(w_q.shape),
                 _resident((1, D_MODEL)),
                 pl.BlockSpec((4 * X_HEADS, MEM_LEN, LANES), lambda i: (0, i // tiles_per_batch, 0)),
                 _resident(w_o.shape)]
    return pl.pallas_call(
        _mix_xattn_kernel, grid=(N_TOK // tm,),
        in_specs=in_specs,
        out_specs=pl.BlockSpec((tm, D_MODEL), lambda i: (i, 0)),
        out_shape=jax.ShapeDtypeStruct((N_TOK, D_MODEL), F32),
        compiler_params=_params(("parallel",), VMEM_LIMIT),
        name="mix_xattn",
    )(x, *mixer_heads, w_mix, g.reshape(1, D_MODEL), w_q,
      q_colscale.reshape(1, D_MODEL).astype(F32), kv, w_o)


EVEN_TILE = 512


def _even_tail_kernel(slopes_ref, sinks_ref, x_ref, swq_ref, swkp_ref, swkc_ref, swvp_ref, swvc_ref,
                      sbq_ref, sbk_ref, sbv_ref, uo_ref, wm_ref, g_ref, wq_ref, cs_ref, kv_ref, wo_ref,
                      o_ref, heads_s, *, kv_div, max_dist):
    i = pl.program_id(0)
    tiles_per_seq = SEQ // EVEN_TILE
    t_in_seq = jnp.minimum(i, N_TOK // EVEN_TILE - 1) % tiles_per_seq
    wr, rd = i % 2, (i + 1) % 2
    n_pairs_a, n_pairs_b = swq_ref.shape[0], sbq_ref.shape[0]

    @pl.when(i == 0)
    def _():
        heads_s[1] = jnp.zeros(heads_s.shape[1:], BF)

    def mix_steps():
        heads = range(X_HEADS)
        cols = [slice(X_HEAD_DIM * hd, X_HEAD_DIM * (hd + 1)) for hd in heads]
        st = {"xv": [None] * X_HEADS, "q": [None] * X_HEADS, "pe": [None] * X_HEADS, "l": [None] * X_HEADS,
              "o": [None] * X_HEADS}

        def project(c):
            if c == 0:
                st["mixed"] = jnp.concatenate([heads_s[rd, k] for k in range(n_pairs_a + n_pairs_b)], axis=1)
            st["xv"][c] = x_ref[:, cols[c]] + jnp.dot(st["mixed"], wm_ref[:, cols[c]], preferred_element_type=F32)

        def q_proj(hd):
            if hd == 0:
                xv = jnp.concatenate(st["xv"], axis=1)
                st["h"] = _rms(xv, g_ref[...]).astype(BF)
            acc = jnp.dot(st["h"], wq_ref[:, cols[hd]], preferred_element_type=F32)
            ms = jnp.mean(acc * acc, axis=1, keepdims=True)
            st["q"][hd] = (acc * cs_ref[:, cols[hd]] * lax.rsqrt(ms + RMS_EPS)).astype(BF)

        def scores(hd):
            kh = jnp.concatenate([kv_ref[2 * hd], kv_ref[2 * hd + 1]], axis=1)
            s = lax.dot_general(st["q"][hd], kh, NT_DIMS, preferred_element_type=F32)
            e = jnp.exp2(s - jnp.max(s, axis=1, keepdims=True))
            st["l"][hd] = jnp.sum(e, axis=1, keepdims=True)
            st["pe"][hd] = e.astype(BF)

        def values(hd):
            v0 = 2 * X_HEADS
            vh = jnp.concatenate([kv_ref[v0 + 2 * hd], kv_ref[v0 + 2 * hd + 1]], axis=1)
            st["o"][hd] = (jnp.dot(st["pe"][hd], vh, preferred_element_type=F32) / st["l"][hd]).astype(BF)

        def out_proj(c):
            if c == 0:
                st["oc"] = jnp.concatenate(st["o"], axis=1)
            o_ref[:, cols[c]] = st["xv"][c] + jnp.dot(st["oc"], wo_ref[:, cols[c]], preferred_element_type=F32)

        return [functools.partial(f, k) for f in (project, q_proj, scores, values, out_proj) for k in heads]

    def attend_and_mix():
        for step in mix_steps():
            step()

        uo = uo_ref[...]
        blocks_per_tile = EVEN_TILE // SB_QB

        def sb_first_pass(p, j):
            rows = slice(SB_QB * j, SB_QB * (j + 1))
            load = lambda ref: (lambda first, n: ref[p, pl.ds(pl.multiple_of(first * SB_QB, SB_QB), n * SB_QB), :])
            return _sb_unit(sbq_ref[p, rows, :].astype(F32), load(sbk_ref), load(sbv_ref),
                            blocks_per_tile * t_in_seq + j, uo)

        units = [(p, j) for p in range(n_pairs_b) for j in range(blocks_per_tile)]
        finishers = [sb_first_pass(p, j) for p, j in units]

        n_kv = swkc_ref.shape[0]
        for m in range(EVEN_TILE // BLOCK):
            rows = slice(BLOCK * m, BLOCK * (m + 1))
            if m == 0:
                kp, vp, has_prev = [swkp_ref[g] for g in range(n_kv)], [swvp_ref[g] for g in range(n_kv)], t_in_seq > 0
            else:
                prows = slice(BLOCK * (m - 1), BLOCK * m)
                kp, vp, has_prev = ([swkc_ref[g, prows, :] for g in range(n_kv)],
                                    [swvc_ref[g, prows, :] for g in range(n_kv)], True)
            outs = _swa_block([swq_ref[p, rows, :] for p in range(n_pairs_a)], kp,
                              [swkc_ref[g, rows, :] for g in range(n_kv)], vp, [swvc_ref[g, rows, :] for g in range(n_kv)],
                              has_prev, slopes_ref, sinks_ref, kv_div=kv_div, max_dist=max_dist)
            for p in range(n_pairs_a):
                heads_s[wr, p, rows, :] = outs[p].astype(BF)

        for (p, j), finish in zip(units, finishers):
            heads_s[wr, n_pairs_a + p, SB_QB * j:SB_QB * (j + 1), :] = finish().astype(BF)

    n_tiles = N_TOK // EVEN_TILE
    pl.when(i < n_tiles)(attend_and_mix)

    @pl.when(i == n_tiles)
    def _():
        for step in mix_steps():
            step()


def _even_tail(x, p, slopes, sinks, w_mix, g, w_q, q_colscale, kv, w_o):
    n_pairs_a, n_kv, n_pairs_b = A_Q_HEADS // 2, A_KV_HEADS, B_HEADS // 2
    assert p.shape[0] == n_pairs_a + 2 * n_kv + 3 * n_pairs_b and n_pairs_a == n_pairs_b == 2 * n_kv
    n_tiles = N_TOK // EVEN_TILE
    tiles_per_seq = SEQ // EVEN_TILE
    blocks_per_tile = EVEN_TILE // BLOCK
    att = lambda i: jnp.minimum(i, n_tiles - 1)
    mix = lambda i: jnp.maximum(i - 1, 0)
    tile_rows = lambda size, idx: pl.BlockSpec((size, EVEN_TILE, LANES), lambda i: (idx, att(i), 0))
    prev_block = lambda idx: pl.BlockSpec(
        (n_kv, BLOCK, LANES), lambda i: (idx, jnp.maximum(blocks_per_tile * att(i) - 1, 0), 0))
    whole_seq = lambda idx: pl.BlockSpec((n_pairs_b, SEQ, LANES), lambda i: (idx, att(i) // tiles_per_seq, 0),
                                         pipeline_mode=pl.Buffered(1))
    smem = pl.BlockSpec(memory_space=pltpu.SMEM)
    return pl.pallas_call(
        functools.partial(_even_tail_kernel, kv_div=n_pairs_a // n_kv, max_dist=A_WINDOW - 1),
        grid=(n_tiles + 1,),
        in_specs=[smem, smem,
                  pl.BlockSpec((EVEN_TILE, D_MODEL), lambda i: (mix(i), 0)),
                  tile_rows(n_pairs_a, 0), prev_block(2), tile_rows(n_kv, 2), prev_block(3), tile_rows(n_kv, 3),
                  tile_rows(n_pairs_b, 2), whole_seq(3), whole_seq(4),
                  _resident((SB_QB, SB_QB)),
                  _resident(w_mix.shape), _resident((1, D_MODEL)), _resident(w_q.shape), _resident((1, D_MODEL)),
                  pl.BlockSpec((4 * X_HEADS, MEM_LEN, LANES), lambda i: (0, mix(i) // tiles_per_seq, 0)),
                  _resident(w_o.shape)],
        out_specs=pl.BlockSpec((EVEN_TILE, D_MODEL), lambda i: (mix(i), 0)),
        out_shape=jax.ShapeDtypeStruct((N_TOK, D_MODEL), F32),
        scratch_shapes=[pltpu.VMEM((2, n_pairs_a + n_pairs_b, EVEN_TILE, LANES), BF)],
        compiler_params=_params(("arbitrary",), VMEM_LIMIT),
        name="even_tail",
    )(slopes, sinks, x, p, p, p, p, p, p, p, p, _sb_matrix(), w_mix, g.reshape(1, D_MODEL), w_q,
      q_colscale.reshape(1, D_MODEL).astype(F32), kv, w_o)


def _alibi_log2(n_heads):
    return jnp.asarray(LOG2_E * 2.0 ** (-8.0 * np.arange(1, n_heads + 1) / n_heads), dtype=F32)


def _even_projection(x, norm_g, w_in, q_gain, k_gain):
    hd = HEAD_DIM
    a_q, a_kv, b_w = A_Q_HEADS * hd, A_KV_HEADS * hd, B_HEADS * hd
    scale = hd ** -0.5 * LOG2_E
    ones = lambda n: jnp.ones((n,), F32)
    cs = jnp.concatenate([jnp.tile(q_gain, A_Q_HEADS) * scale, jnp.tile(k_gain, A_KV_HEADS), ones(a_kv),
                          ones(b_w) * scale, ones(2 * b_w)])
    plan = ([(True, False)] * (a_q // LANES) + [(True, True)] * (a_kv // LANES) + [(False, True)] * (a_kv // LANES)
            + [(False, False)] * (3 * b_w // LANES))
    return _proj(x, norm_g, w_in, cs, plan, hd)


def _odd_mixer_heads(x, norm_g, w_in, q_gain, k_gain):
    hd = HEAD_DIM
    cs = jnp.concatenate([jnp.tile(q_gain, C_HEADS) * (hd ** -0.5 * LOG2_E), jnp.tile(k_gain, C_HEADS),
                          jnp.ones((C_HEADS * hd,), F32)])
    head_blocks = C_HEADS * hd // LANES
    plan = [(True, False)] * (2 * head_blocks) + [(False, False)] * head_blocks
    p = _proj(x, norm_g, w_in, cs, plan, hd)
    return [_dilated(p, _alibi_log2(C_HEADS))]


def _memory_kv(mem2d, mem_g, w_kv, k_gain):
    cs_kv = jnp.concatenate([jnp.tile(k_gain, X_HEADS), jnp.ones((D_MODEL,), F32)])
    plan = [(True, False)] * (D_MODEL // LANES) + [(False, False)] * (D_MODEL // LANES)
    return _proj(mem2d, mem_g, w_kv, cs_kv, plan, X_HEAD_DIM, tm=MEM_LEN)


def kernel(x, mem, ffn1_norm, ffn1_w_gu, ffn1_w_down, mix_norm, ev_w_in, ev_q_gain, ev_k_gain, ev_sinks, ev_w_out, od_w_in, od_q_gain, od_k_gain, od_w_out, xa_norm, xa_mem_norm, xa_w_q, xa_w_kv, xa_q_gain, xa_k_gain, xa_w_o, ffn2_norm, ffn2_w_gu, ffn2_w_down):
    x = x.reshape(N_TOK, D_MODEL)
    mem2d = mem.reshape(BATCH * MEM_LEN, D_MODEL)
    w_gu, w_down = _cast_now((ffn1_w_gu, 0)), _cast_now((ffn1_w_down, 0))
    for layer in range(DEPTH):
        j = layer // 2
        even = layer % 2 == 0
        w_in3, w_mix3 = (ev_w_in, ev_w_out) if even else (od_w_in, od_w_out)
        jobs = [(w_in3, j), (w_mix3, j), (xa_w_q, layer), (xa_w_kv, layer), (xa_w_o, layer),
                (ffn2_w_gu, layer), (ffn2_w_down, layer)]
        x, (w_in, w_mix, w_q, w_kv, w_o, w_gu, w_down) = _ffn(x, ffn1_norm[layer], w_gu, w_down, jobs)
        kv = _memory_kv(mem2d, xa_mem_norm[layer], w_kv, xa_k_gain[layer])
        cs_q = jnp.tile(xa_q_gain[layer], X_HEADS) * (X_HEAD_DIM ** -0.5 * LOG2_E)
        if even:
            p = _even_projection(x, mix_norm[layer], w_in, ev_q_gain[j], ev_k_gain[j])
            x = _even_tail(x, p, _alibi_log2(A_Q_HEADS), ev_sinks[j].astype(F32) * LOG2_E, w_mix,
                           xa_norm[layer], w_q, cs_q, kv, w_o)
        else:
            heads = _odd_mixer_heads(x, mix_norm[layer], w_in, od_q_gain[j], od_k_gain[j])
            x = _mix_xattn(x, heads, w_mix, xa_norm[layer], w_q, cs_q, kv, w_o)
        jobs = [(ffn1_w_gu, layer + 1), (ffn1_w_down, layer + 1)] if layer + 1 < DEPTH else []
        x, next_ffn1 = _ffn(x, ffn2_norm[layer], w_gu, w_down, jobs)
        if next_ffn1:
            w_gu, w_down = next_ffn1
    return x.reshape(BATCH, SEQ, D_MODEL)
```

```python
import functools

import numpy as np
import jax
import jax.numpy as jnp
from jax import lax
from jax.experimental import pallas as pl
from jax.experimental.pallas import tpu as pltpu

D_MODEL = 1024
BATCH = 4
SEQ = 4096
N_TOK = BATCH * SEQ
DEPTH = 2
HEAD_DIM = 64
BLOCK = 128
A_Q_HEADS = 8
A_KV_HEADS = 2
A_WINDOW = 128
B_HEADS = 8
C_HEADS = 16
C_PATTERNS = ((128, 1), (512, 4), (2048, 16))
MEM_LEN = 256
X_HEADS = 4
X_HEAD_DIM = D_MODEL // X_HEADS
D_FF = 2816
RMS_EPS = 1e-6

LANES = 128
MXU_N = 256
VMEM_LIMIT = 56 * 1024 * 1024

BF = jnp.bfloat16
F32 = jnp.float32
NT_DIMS = (((1,), (1,)), ((), ()))
LOG2_E = 1.4426950408889634


def _params(sem, vmem=None):
    return pltpu.CompilerParams(dimension_semantics=sem, vmem_limit_bytes=vmem)


def _resident(shape):
    nd = len(shape)
    return pl.BlockSpec(shape, lambda *_: (0,) * nd, pipeline_mode=pl.Buffered(1))


BF16_SUBLANES = 16


def _cast_specs(job, steps):
    w3, layer = job
    _, r, c = w3.shape
    rb = next(rb for rb in range(BF16_SUBLANES, r + 1, BF16_SUBLANES) if r % rb == 0 and r // rb <= steps)
    last = r // rb - 1
    return (pl.BlockSpec((None, rb, c), lambda i: (layer, jnp.minimum(i, last), 0)),
            pl.BlockSpec((rb, c), lambda i: (jnp.minimum(i, last), 0)),
            jax.ShapeDtypeStruct((r, c), BF))


def _run_cast_jobs(in_refs, out_refs):
    for src, dst in zip(in_refs, out_refs):
        dst[...] = src[...].astype(BF)


def _cast_kernel(w_ref, o_ref):
    _run_cast_jobs([w_ref], [o_ref])


def _cast_now(job, *, rows=128):
    steps = job[0].shape[1] // rows
    in_spec, out_spec, out_shape = _cast_specs(job, steps)
    return pl.pallas_call(
        _cast_kernel, grid=(steps,), in_specs=[in_spec], out_specs=out_spec, out_shape=out_shape,
        compiler_params=_params(("arbitrary",)),
        name="cast",
    )(job[0])


def _rms(xv, g):
    ms = jnp.mean(xv * xv, axis=-1, keepdims=True)
    return xv * lax.rsqrt(ms + RMS_EPS) * g


FFN_SPLIT = (D_FF // MXU_N + 1) // 2 * MXU_N
FFN_CHUNKS = ((0, FFN_SPLIT), (FFN_SPLIT, D_FF))


def _ffn_kernel(*refs, n_jobs):
    x_ref, g_ref, wgu_ref, wd_ref = refs[:4]
    o_ref = refs[4 + n_jobs]
    xv = x_ref[...]
    h = _rms(xv, g_ref[...]).astype(BF)
    acc = jnp.zeros_like(xv)
    for c0, c1 in FFN_CHUNKS:
        gate = jnp.dot(h, wgu_ref[:, c0:c1], preferred_element_type=F32)
        up = jnp.dot(h, wgu_ref[:, D_FF + c0:D_FF + c1], preferred_element_type=F32)
        act = (gate * jax.nn.sigmoid(gate) * up).astype(BF)
        acc = acc + jnp.dot(act, wd_ref[c0:c1, :], preferred_element_type=F32)
    o_ref[...] = xv + 0.5 * acc
    _run_cast_jobs(refs[4:4 + n_jobs], refs[5 + n_jobs:])


def _ffn(x, g, w_gu, w_down, cast_jobs=(), *, tm=512):
    steps = N_TOK // tm
    specs = [_cast_specs(job, steps) for job in cast_jobs]
    out = pl.pallas_call(
        functools.partial(_ffn_kernel, n_jobs=len(cast_jobs)),
        grid=(steps,),
        in_specs=[pl.BlockSpec((tm, D_MODEL), lambda i: (i, 0)),
                  _resident((1, D_MODEL)),
                  _resident(w_gu.shape),
                  _resident(w_down.shape)] + [s[0] for s in specs],
        out_specs=[pl.BlockSpec((tm, D_MODEL), lambda i: (i, 0))] + [s[1] for s in specs],
        out_shape=[jax.ShapeDtypeStruct((N_TOK, D_MODEL), F32)] + [s[2] for s in specs],
        compiler_params=_params(("arbitrary",), VMEM_LIMIT),
        name="ffn",
    )(x, g.reshape(1, D_MODEL), w_gu, w_down, *[job[0] for job in cast_jobs])
    return out[0], out[1:]


def _proj_kernel(x_ref, g_ref, w_ref, cs_ref, o_ref, *, plan, gs):
    assert gs in (HEAD_DIM, MXU_N)
    h = _rms(x_ref[...], g_ref[...]).astype(BF)
    n_chunks = len(plan) // 2
    lo = lax.broadcasted_iota(jnp.int32, (x_ref.shape[0], LANES), 1) < HEAD_DIM

    def main(j):
        return jnp.dot(h, w_ref[:, MXU_N * j:MXU_N * (j + 1)], preferred_element_type=F32)

    acc_next = main(0)
    out = 0
    for j in range(n_chunks):
        cols = slice(MXU_N * j, MXU_N * (j + 1))
        acc = acc_next
        if j + 1 < n_chunks:
            acc_next = main(j + 1)
        y = acc * cs_ref[:, cols]
        halves = plan[2 * j:2 * j + 2]
        if gs == MXU_N and any(normed for normed, _ in halves):
            inv_chunk = lax.rsqrt(jnp.mean(acc * acc, axis=1, keepdims=True) + RMS_EPS)
        for half, (normed, dup) in enumerate(halves):
            lanes = slice(LANES * half, LANES * (half + 1))
            yh = y[:, lanes]
            if normed and gs == MXU_N:
                yh = yh * inv_chunk
            elif normed:
                sq = acc[:, lanes] * acc[:, lanes]
                s_lo = jnp.sum(jnp.where(lo, sq, 0.0), axis=1, keepdims=True)
                s_hi = jnp.sum(jnp.where(lo, 0.0, sq), axis=1, keepdims=True)
                yh = yh * lax.rsqrt(jnp.where(lo, s_lo, s_hi) * (1.0 / gs) + RMS_EPS)
            if dup:
                swapped = pltpu.roll(yh, HEAD_DIM, axis=1)
                o_ref[out] = jnp.where(lo, yh, swapped).astype(BF)
                o_ref[out + 1] = jnp.where(lo, swapped, yh).astype(BF)
                out += 2
            else:
                o_ref[out] = yh.astype(BF)
                out += 1


def _proj(x, g, w, colscale, plan, gs, *, tm=1024):
    rows = x.shape[0]
    wout = w.shape[1]
    assert wout == LANES * len(plan) and len(plan) % 2 == 0
    c = sum(2 if dup else 1 for _, dup in plan)
    return pl.pallas_call(
        functools.partial(_proj_kernel, plan=tuple(plan), gs=gs),
        grid=(rows // tm,),
        in_specs=[pl.BlockSpec((tm, D_MODEL), lambda i: (i, 0)),
                  _resident((1, D_MODEL)),
                  _resident(w.shape),
                  _resident((1, wout))],
        out_specs=pl.BlockSpec((c, tm, LANES), lambda i: (0, i, 0)),
        out_shape=jax.ShapeDtypeStruct((c, rows, LANES), BF),
        compiler_params=_params(("parallel",), VMEM_LIMIT),
        name="proj",
    )(x, g.reshape(1, D_MODEL), w, colscale.reshape(1, wout).astype(F32))


def _swa_block(q_blocks, kp, kc, vp, vc, has_prev, slopes_ref, sinks_ref, *, kv_div, max_dist):
    row = lax.broadcasted_iota(jnp.int32, (BLOCK, 2 * BLOCK), 0)
    col = lax.broadcasted_iota(jnp.int32, (BLOCK, 2 * BLOCK), 1)
    dist = row + BLOCK - col
    valid = (dist >= 0) & (dist <= max_dist)
    if has_prev is not True:
        valid = valid & ((col >= BLOCK) | has_prev)
    negmask = jnp.where(valid, 0.0, -jnp.inf)
    distf = dist.astype(F32)
    lo = lax.broadcasted_iota(jnp.int32, (BLOCK, LANES), 1) < HEAD_DIM

    n_groups = len(q_blocks) // kv_div
    heads_per_group = 2 * kv_div
    scores = []
    for g in range(n_groups):
        parts = []
        for p in range(g * kv_div, (g + 1) * kv_div):
            q2 = q_blocks[p].astype(F32)
            parts += [jnp.where(lo, q2, 0.0), jnp.where(lo, 0.0, q2)]
        q_stack = jnp.concatenate(parts, axis=0).astype(BF)
        scores.append(jnp.concatenate(
            [lax.dot_general(q_stack, kp[g], NT_DIMS, preferred_element_type=F32),
             lax.dot_general(q_stack, kc[g], NT_DIMS, preferred_element_type=F32)], axis=1))
    soft = []
    for g in range(n_groups):
        res = []
        for j in range(heads_per_group):
            h = g * heads_per_group + j
            s = scores[g][j * BLOCK:(j + 1) * BLOCK] - slopes_ref[h] * distf + negmask
            m = jnp.maximum(jnp.max(s, axis=1, keepdims=True), sinks_ref[h])
            pe = jnp.exp2(s - m)
            res.append((pe.astype(BF), jnp.sum(pe, axis=1, keepdims=True) + jnp.exp2(sinks_ref[h] - m)))
        soft.append(res)
    outs = []
    for g in range(n_groups):
        pb = jnp.concatenate([r[0] for r in soft[g]], axis=0)
        pv = (jnp.dot(pb[:, :BLOCK], vp[g], preferred_element_type=F32)
              + jnp.dot(pb[:, BLOCK:], vc[g], preferred_element_type=F32))
        for jp in range(kv_div):
            o0 = pv[(2 * jp) * BLOCK:(2 * jp + 1) * BLOCK] / soft[g][2 * jp][1]
            o1 = pv[(2 * jp + 1) * BLOCK:(2 * jp + 2) * BLOCK] / soft[g][2 * jp + 1][1]
            outs.append(jnp.where(lo, o0, o1))
    return outs


DIL_ORDER = tuple(sorted(C_PATTERNS, key=lambda wd: -wd[1]))
DIL_UNROLL = 16
DIL_AHEAD = 2
DIL_BASE = 4
DIL_Q = SEQ // DIL_BASE
DIL_CONVERT_ROWS = DIL_BASE * BLOCK


def _dilated_kernel(slopes_ref, q_ref, k_ref, v_ref, o_ref, qn_s, tq_s, tk_s, tv_s, q0_s, q1_s, k_s, v_s,
                    acc_r, m_r, l_r, acc_n, m_n, l_n):
    assert all(d == 1 or d % DIL_BASE == 0 for _, d in DIL_ORDER) and DIL_ORDER[-1][1] == 1
    p = pl.program_id(1)
    lo = lax.broadcasted_iota(jnp.int32, (BLOCK, LANES), 1) < HEAD_DIM

    def convert(c, carry):
        rows = pl.ds(pl.multiple_of(c * DIL_CONVERT_ROWS, DIL_CONVERT_ROWS), DIL_CONVERT_ROWS)
        q_nat = q_ref[0, rows, :].astype(F32)
        lo_c = lax.broadcasted_iota(jnp.int32, (DIL_CONVERT_ROWS, LANES), 1) < HEAD_DIM
        qn_s[0, rows, :] = jnp.where(lo_c, q_nat, 0.0).astype(BF)
        qn_s[1, rows, :] = jnp.where(lo_c, 0.0, q_nat).astype(BF)
        tq_s[...] = q_nat
        tk_s[...] = k_ref[0, rows, :].astype(F32)
        tv_s[...] = v_ref[0, rows, :].astype(F32)
        for rho in range(DIL_BASE):
            src = pl.ds(rho, BLOCK, stride=DIL_BASE)
            dst = pl.ds(pl.multiple_of(rho * DIL_Q + c * BLOCK, BLOCK), BLOCK)
            q = tq_s[src, :]
            q0_s[dst, :] = jnp.where(lo, q, 0.0)
            q1_s[dst, :] = jnp.where(lo, 0.0, q)
            k_s[dst, :] = tk_s[src, :]
            v_s[dst, :] = tv_s[src, :]
        return carry

    lax.fori_loop(0, SEQ // DIL_CONVERT_ROWS, convert, 0)

    row = lax.broadcasted_iota(jnp.int32, (BLOCK, 2 * BLOCK), 0)
    col = lax.broadcasted_iota(jnp.int32, (BLOCK, 2 * BLOCK), 1)
    dist = row + BLOCK - col
    distf = dist.astype(F32)
    no_prev = jnp.where(col < BLOCK, -jnp.inf, 0.0)

    def bcast2(a0, a1):
        return jnp.where(lo, jnp.broadcast_to(a0, (BLOCK, LANES)), jnp.broadcast_to(a1, (BLOCK, LANES)))

    for pi, (window, d) in enumerate(DIL_ORDER):
        first, last = pi == 0, pi == len(DIL_ORDER) - 1
        natural = d == 1
        nb = SEQ // d // BLOCK
        band = (dist >= 0) & (dist <= window // d)
        bias = [jnp.where(band, (-float(d) * slopes_ref[2 * p + hh]) * distf, -jnp.inf) for hh in range(2)]
        acc_s, m_s, l_s = (acc_n, m_n, l_n) if natural else (acc_r, m_r, l_r)
        to_natural = not natural and not last and DIL_ORDER[pi + 1][1] == 1
        assert (not to_natural or d == DIL_BASE) and (not natural or first or DIL_ORDER[pi - 1][1] == DIL_BASE)
        acc_o, m_o, l_o = (acc_n, m_n, l_n) if to_natural else (acc_s, m_s, l_s)

        def rows_of(r, n, n_blocks=1, d=d, natural=natural):
            size = n_blocks * BLOCK
            if natural:
                return pl.ds(pl.multiple_of(BLOCK * n, BLOCK), size)
            inner = d // DIL_BASE
            start = (r % DIL_BASE) * DIL_Q + r // DIL_BASE + inner * BLOCK * n
            return pl.ds(start, size, stride=inner) if inner > 1 else pl.ds(pl.multiple_of(start, BLOCK), size)

        def step(it, carry, nb=nb, bias=bias, first=first, last=last, natural=natural, rows_of=rows_of,
                 acc_s=acc_s, m_s=m_s, l_s=l_s, acc_o=acc_o, m_o=m_o, l_o=l_o, to_natural=to_natural):
            assert DIL_UNROLL % nb == 0 or nb % DIL_UNROLL == 0
            load_k = (lambda rr: k_ref[0, rr, :]) if natural else (lambda rr: k_s[rr, :].astype(BF))
            load_v = (lambda rr: v_ref[0, rr, :]) if natural else (lambda rr: v_s[rr, :].astype(BF))

            def scores(u):
                t = it * DIL_UNROLL + u
                r, n = t // nb, t % nb
                prev = (u % nb != 0) if nb <= DIL_UNROLL else (True if u else None)
                rows = rows_of(r, n)
                out_rows = pl.ds(DIL_BASE * BLOCK * n + r, BLOCK, stride=DIL_BASE) if to_natural else rows
                if prev is True:
                    kv_rows = [rows_of(r, n - 1, 2)]
                elif prev is None:
                    kv_rows = [rows_of(r, jnp.maximum(n - 1, 0)), rows]
                else:
                    kv_rows = [rows]
                if natural:
                    qh = jnp.concatenate([qn_s[0, rows, :], qn_s[1, rows, :]], axis=0)
                else:
                    qh = jnp.concatenate([q0_s[rows, :], q1_s[rows, :]], axis=0).astype(BF)
                s = jnp.concatenate([lax.dot_general(qh, load_k(rr), NT_DIMS, preferred_element_type=F32)
                                     for rr in kv_rows], axis=1)
                return n, prev, rows, out_rows, kv_rows, (s[:BLOCK], s[BLOCK:])

            def softmax_pv(blk):
                n, prev, rows, out_rows, kv_rows, s = blk
                ms, ls, pes = [], [], []
                for hh in range(2):
                    sh = s[hh] + (bias[hh][:, BLOCK:] if prev is False else bias[hh])
                    if prev is None:
                        sh = sh + jnp.where(n == 0, no_prev, 0.0)
                    m = jnp.max(sh, axis=1, keepdims=True)
                    pe = jnp.exp2(sh - m)
                    ms.append(m)
                    ls.append(jnp.sum(pe, axis=1, keepdims=True))
                    pes.append(pe.astype(BF))
                pb = jnp.concatenate(pes, axis=0)
                v = jnp.concatenate([load_v(rr) for rr in kv_rows], axis=0) if len(kv_rows) > 1 else load_v(kv_rows[0])
                pv = jnp.dot(pb, v, preferred_element_type=F32)
                return rows, out_rows, ms, ls, (pv[:BLOCK], pv[BLOCK:])

            def merge(rows, out_rows, ms, ls, pv):
                m2 = bcast2(ms[0], ms[1])
                l2 = bcast2(ls[0], ls[1])
                acc2 = jnp.where(lo, pv[0], pv[1])
                if not first:
                    m_old = m_s[rows, :]
                    m_new = jnp.maximum(m_old, m2)
                    a_old, a_new = jnp.exp2(m_old - m_new), jnp.exp2(m2 - m_new)
                    l2 = a_old * l_s[rows, :] + a_new * l2
                    acc2 = a_old * acc_s[rows, :] + a_new * acc2
                    m2 = m_new
                if last:
                    o_ref[0, rows, :] = (acc2 / l2).astype(BF)
                else:
                    m_o[out_rows, :] = m2
                    l_o[out_rows, :] = l2
                    acc_o[out_rows, :] = acc2

            pending = {u: scores(u) for u in range(DIL_AHEAD)}
            done = None
            for u in range(DIL_UNROLL):
                if u + DIL_AHEAD < DIL_UNROLL:
                    pending[u + DIL_AHEAD] = scores(u + DIL_AHEAD)
                cur = softmax_pv(pending.pop(u))
                if done is not None:
                    merge(*done)
                done = cur
            merge(*done)
            return carry

        lax.fori_loop(0, SEQ // BLOCK // DIL_UNROLL, step, 0)


def _dilated(qkv, slopes):
    n_pairs = C_HEADS // 2
    seq_f32 = pltpu.VMEM((SEQ, LANES), F32)
    chunk_f32 = pltpu.VMEM((DIL_CONVERT_ROWS, LANES), F32)
    return pl.pallas_call(
        _dilated_kernel,
        grid=(BATCH, n_pairs),
        in_specs=[pl.BlockSpec(memory_space=pltpu.SMEM),
                  pl.BlockSpec((1, SEQ, LANES), lambda b, p: (p, b, 0)),
                  pl.BlockSpec((1, SEQ, LANES), lambda b, p: (n_pairs + p, b, 0)),
                  pl.BlockSpec((1, SEQ, LANES), lambda b, p: (2 * n_pairs + p, b, 0))],
        out_specs=pl.BlockSpec((1, SEQ, LANES), lambda b, p: (p, b, 0)),
        out_shape=jax.ShapeDtypeStruct((n_pairs, N_TOK, LANES), BF),
        scratch_shapes=[pltpu.VMEM((2, SEQ, LANES), BF)] + [chunk_f32] * 3 + [seq_f32] * 10,
        compiler_params=_params(("parallel", "parallel"), VMEM_LIMIT),
        name="dilated",
    )(slopes, qkv, qkv, qkv)


SB_QB = MXU_N
SB_FIRST_TILES = 2
SB_DEAD_LOG2 = -150.0


def _sb_matrix():
    idx = np.arange(SB_QB)
    return jnp.asarray(-(idx[:, None] > idx[None, :]).astype(np.float32), dtype=BF)


def _sb_unit(q2, load_k, load_v, iq, uo):
    lo = lax.broadcasted_iota(jnp.int32, (SB_QB, LANES), 1) < HEAD_DIM
    q_stack = jnp.concatenate([jnp.where(lo, q2, 0.0), jnp.where(lo, 0.0, q2)], axis=0).astype(BF)
    rel1 = (lax.broadcasted_iota(jnp.int32, (SB_QB, SB_QB), 1)
            - lax.broadcasted_iota(jnp.int32, (SB_QB, SB_QB), 0))
    rel = jnp.concatenate([rel1, rel1], axis=0)

    def scores(first, n_tiles):
        return lax.dot_general(q_stack, load_k(first, n_tiles), NT_DIMS,
                               preferred_element_type=F32)

    def walk(first, n_tiles, carry, masked):
        c, o = carry
        z = scores(first, n_tiles)
        order = list(reversed(range(n_tiles)))
        ws, es, stricts, totals = {}, {}, {}, {}
        for t in order:
            zt = z[:, t * SB_QB:(t + 1) * SB_QB]
            sp = jnp.maximum(zt, 0.0) + jnp.log2(1.0 + jnp.exp2(-jnp.abs(zt)))
            es[t] = zt - sp
            if masked:
                stricts[t] = rel < (iq - first - t) * SB_QB
                sp = jnp.where(stricts[t], sp, 0.0)
            ws[t] = jnp.dot(sp.astype(BF), uo, preferred_element_type=F32)
            totals[t] = jnp.sum(sp, axis=1, keepdims=True)
        parts = {}
        for t in order:
            a = jnp.exp2(es[t] + jnp.concatenate([c] * (SB_QB // LANES), axis=1) + ws[t])
            if masked:
                a = jnp.where(stricts[t], a, 0.0)
            parts[t] = a.astype(BF)
            c = c - totals[t]
        pv = jnp.dot(jnp.concatenate([parts[t] for t in range(n_tiles)], axis=1), load_v(first, n_tiles),
                     preferred_element_type=F32)
        return c, o + jnp.where(lo, pv[:SB_QB], pv[SB_QB:])

    def alive(c):
        return jnp.max(c) > SB_DEAD_LOG2

    def body(state):
        g = state[0]
        c, o = walk(g, 1, state[2:], False)
        return g - 1, alive(c), c, o

    first = jnp.maximum(iq - 1, 0)
    zeros = (jnp.zeros((2 * SB_QB, LANES), F32), jnp.zeros((SB_QB, LANES), F32))
    c, o = walk(first, SB_FIRST_TILES, zeros, True)
    more = (first > 0) & alive(c)

    def finish():
        state = lax.while_loop(lambda st: (st[0] >= 0) & st[1], body, (first - 1, more, c, o))
        return state[3]

    return more, o, finish


def _mix_xattn_kernel(*refs):
    x_ref = refs[0]
    wm_ref, g_ref, wq_ref, cs_ref, kv_ref, wo_ref, o_ref = refs[-7:]
    mixed = jnp.concatenate([r[c] for r in refs[1:-7] for c in range(r.shape[0])], axis=1)
    xv = x_ref[...] + jnp.dot(mixed, wm_ref[...], preferred_element_type=F32)
    h = _rms(xv, g_ref[...]).astype(BF)
    heads = range(X_HEADS)
    cols = [slice(X_HEAD_DIM * hd, X_HEAD_DIM * (hd + 1)) for hd in heads]
    acc = [jnp.dot(h, wq_ref[:, cols[hd]], preferred_element_type=F32) for hd in heads]
    ms = [jnp.mean(acc[hd] * acc[hd], axis=1, keepdims=True) for hd in heads]
    q = [(acc[hd] * cs_ref[:, cols[hd]] * lax.rsqrt(ms[hd] + RMS_EPS)).astype(BF) for hd in heads]
    s = [lax.dot_general(q[hd], jnp.concatenate([kv_ref[2 * hd], kv_ref[2 * hd + 1]], axis=1), NT_DIMS,
                         preferred_element_type=F32) for hd in heads]
    pe, l = [], []
    for hd in heads:
        e = jnp.exp2(s[hd] - jnp.max(s[hd], axis=1, keepdims=True))
        l.append(jnp.sum(e, axis=1, keepdims=True))
        pe.append(e.astype(BF))
    v0 = 2 * X_HEADS
    pv = [jnp.dot(pe[hd], jnp.concatenate([kv_ref[v0 + 2 * hd], kv_ref[v0 + 2 * hd + 1]], axis=1),
                  preferred_element_type=F32) for hd in heads]
    o = jnp.concatenate([(pv[hd] / l[hd]).astype(BF) for hd in heads], axis=1)
    o_ref[...] = xv + jnp.dot(o, wo_ref[...], preferred_element_type=F32)


def _mix_xattn(x, mixer_heads, w_mix, g, w_q, q_colscale, kv, w_o, *, tm=1024):
    tiles_per_batch = SEQ // tm
    in_specs = [pl.BlockSpec((tm, D_MODEL), lambda i: (i, 0))]
    in_specs += [pl.BlockSpec((mh.shape[0], tm, LANES), lambda i: (0, i, 0)) for mh in mixer_heads]
    in_specs += [_resident(w_mix.shape),
                 _resident((1, D_MODEL)),
                 _resident(w_q.shape),
                 _resident((1, D_MODEL)),
                 pl.BlockSpec((4 * X_HEADS, MEM_LEN, LANES), lambda i: (0, i // tiles_per_batch, 0)),
                 _resident(w_o.shape)]
    return pl.pallas_call(
        _mix_xattn_kernel, grid=(N_TOK // tm,),
        in_specs=in_specs,
        out_specs=pl.BlockSpec((tm, D_MODEL), lambda i: (i, 0)),
        out_shape=jax.ShapeDtypeStruct((N_TOK, D_MODEL), F32),
        compiler_params=_params(("parallel",), VMEM_LIMIT),
        name="mix_xattn",
    )(x, *mixer_heads, w_mix, g.reshape(1, D_MODEL), w_q,
      q_colscale.reshape(1, D_MODEL).astype(F32), kv, w_o)


EVEN_TILE = 512


def _even_tail_kernel(slopes_ref, sinks_ref, x_ref, swq_ref, swkp_ref, swkc_ref, swvp_ref, swvc_ref,
                      sbq_ref, sbk_ref, sbv_ref, uo_ref, wm_ref, g_ref, wq_ref, cs_ref, kv_ref, wo_ref,
                      o_ref, heads_s, *, kv_div, max_dist):
    i = pl.program_id(0)
    tiles_per_seq = SEQ // EVEN_TILE
    t_in_seq = jnp.minimum(i, N_TOK // EVEN_TILE - 1) % tiles_per_seq
    wr, rd = i % 2, (i + 1) % 2
    n_pairs_a, n_pairs_b = swq_ref.shape[0], sbq_ref.shape[0]

    @pl.when(i == 0)
    def _():
        heads_s[1] = jnp.zeros(heads_s.shape[1:], BF)

    def mix_steps():
        heads = range(X_HEADS)
        cols = [slice(X_HEAD_DIM * hd, X_HEAD_DIM * (hd + 1)) for hd in heads]
        st = {"xv": [None] * X_HEADS, "q": [None] * X_HEADS, "pe": [None] * X_HEADS, "l": [None] * X_HEADS,
              "o": [None] * X_HEADS}

        def project(c):
            if c == 0:
                st["mixed"] = jnp.concatenate([heads_s[rd, k] for k in range(n_pairs_a + n_pairs_b)], axis=1)
            st["xv"][c] = x_ref[:, cols[c]] + jnp.dot(st["mixed"], wm_ref[:, cols[c]], preferred_element_type=F32)

        def q_proj(hd):
            if hd == 0:
                xv = jnp.concatenate(st["xv"], axis=1)
                st["h"] = _rms(xv, g_ref[...]).astype(BF)
            acc = jnp.dot(st["h"], wq_ref[:, cols[hd]], preferred_element_type=F32)
            ms = jnp.mean(acc * acc, axis=1, keepdims=True)
            st["q"][hd] = (acc * cs_ref[:, cols[hd]] * lax.rsqrt(ms + RMS_EPS)).astype(BF)

        def scores(hd):
            kh = jnp.concatenate([kv_ref[2 * hd], kv_ref[2 * hd + 1]], axis=1)
            s = lax.dot_general(st["q"][hd], kh, NT_DIMS, preferred_element_type=F32)
            e = jnp.exp2(s - jnp.max(s, axis=1, keepdims=True))
            st["l"][hd] = jnp.sum(e, axis=1, keepdims=True)
            st["pe"][hd] = e.astype(BF)

        def values(hd):
            v0 = 2 * X_HEADS
            vh = jnp.concatenate([kv_ref[v0 + 2 * hd], kv_ref[v0 + 2 * hd + 1]], axis=1)
            st["o"][hd] = (jnp.dot(st["pe"][hd], vh, preferred_element_type=F32) / st["l"][hd]).astype(BF)

        def out_proj(c):
            if c == 0:
                st["oc"] = jnp.concatenate(st["o"], axis=1)
            o_ref[:, cols[c]] = st["xv"][c] + jnp.dot(st["oc"], wo_ref[:, cols[c]], preferred_element_type=F32)

        return [functools.partial(f, k) for f in (project, q_proj, scores, values, out_proj) for k in heads]

    def attend_and_mix():
        for step in mix_steps():
            step()

        uo = uo_ref[...]
        blocks_per_tile = EVEN_TILE // SB_QB

        def sb_first_pass(p, j):
            rows = slice(SB_QB * j, SB_QB * (j + 1))
            load = lambda ref: (lambda first, n: ref[p, pl.ds(pl.multiple_of(first * SB_QB, SB_QB), n * SB_QB), :])
            return _sb_unit(sbq_ref[p, rows, :].astype(F32), load(sbk_ref), load(sbv_ref),
                            blocks_per_tile * t_in_seq + j, uo)

        def store_sb(p, j, o):
            heads_s[wr, n_pairs_a + p, SB_QB * j:SB_QB * (j + 1), :] = o.astype(BF)

        units = [(p, j) for p in range(n_pairs_b) for j in range(blocks_per_tile)]
        passes = [sb_first_pass(p, j) for p, j in units]
        for (p, j), (_, o, _) in zip(units, passes):
            store_sb(p, j, o)

        n_kv = swkc_ref.shape[0]
        for m in range(EVEN_TILE // BLOCK):
            rows = slice(BLOCK * m, BLOCK * (m + 1))
            if m == 0:
                kp, vp, has_prev = [swkp_ref[g] for g in range(n_kv)], [swvp_ref[g] for g in range(n_kv)], t_in_seq > 0
            else:
                prows = slice(BLOCK * (m - 1), BLOCK * m)
                kp, vp, has_prev = ([swkc_ref[g, prows, :] for g in range(n_kv)],
                                    [swvc_ref[g, prows, :] for g in range(n_kv)], True)
            outs = _swa_block([swq_ref[p, rows, :] for p in range(n_pairs_a)], kp,
                              [swkc_ref[g, rows, :] for g in range(n_kv)], vp, [swvc_ref[g, rows, :] for g in range(n_kv)],
                              has_prev, slopes_ref, sinks_ref, kv_div=kv_div, max_dist=max_dist)
            for p in range(n_pairs_a):
                heads_s[wr, p, rows, :] = outs[p].astype(BF)

        @pl.when(functools.reduce(jnp.logical_or, [more for more, _, _ in passes]))
        def _():
            for (p, j), (_, _, finish) in zip(units, passes):
                store_sb(p, j, finish())

    n_tiles = N_TOK // EVEN_TILE
    pl.when(i < n_tiles)(attend_and_mix)

    @pl.when(i == n_tiles)
    def _():
        for step in mix_steps():
            step()


def _even_tail(x, p, slopes, sinks, w_mix, g, w_q, q_colscale, kv, w_o):
    n_pairs_a, n_kv, n_pairs_b = A_Q_HEADS // 2, A_KV_HEADS, B_HEADS // 2
    assert p.shape[0] == n_pairs_a + 2 * n_kv + 3 * n_pairs_b and n_pairs_a == n_pairs_b == 2 * n_kv
    n_tiles = N_TOK // EVEN_TILE
    tiles_per_seq = SEQ // EVEN_TILE
    blocks_per_tile = EVEN_TILE // BLOCK
    att = lambda i: jnp.minimum(i, n_tiles - 1)
    mix = lambda i: jnp.maximum(i - 1, 0)
    tile_rows = lambda size, idx: pl.BlockSpec((size, EVEN_TILE, LANES), lambda i: (idx, att(i), 0))
    prev_block = lambda idx: pl.BlockSpec(
        (n_kv, BLOCK, LANES), lambda i: (idx, jnp.maximum(blocks_per_tile * att(i) - 1, 0), 0))
    whole_seq = lambda idx: pl.BlockSpec((n_pairs_b, SEQ, LANES), lambda i: (idx, att(i) // tiles_per_seq, 0),
                                         pipeline_mode=pl.Buffered(1))
    smem = pl.BlockSpec(memory_space=pltpu.SMEM)
    return pl.pallas_call(
        functools.partial(_even_tail_kernel, kv_div=n_pairs_a // n_kv, max_dist=A_WINDOW - 1),
        grid=(n_tiles + 1,),
        in_specs=[smem, smem,
                  pl.BlockSpec((EVEN_TILE, D_MODEL), lambda i: (mix(i), 0)),
                  tile_rows(n_pairs_a, 0), prev_block(2), tile_rows(n_kv, 2), prev_block(3), tile_rows(n_kv, 3),
                  tile_rows(n_pairs_b, 2), whole_seq(3), whole_seq(4),
                  _resident((SB_QB, SB_QB)),
                  _resident(w_mix.shape), _resident((1, D_MODEL)), _resident(w_q.shape), _resident((1, D_MODEL)),
                  pl.BlockSpec((4 * X_HEADS, MEM_LEN, LANES), lambda i: (0, mix(i) // tiles_per_seq, 0)),
                  _resident(w_o.shape)],
        out_specs=pl.BlockSpec((EVEN_TILE, D_MODEL), lambda i: (mix(i), 0)),
        out_shape=jax.ShapeDtypeStruct((N_TOK, D_MODEL), F32),
        scratch_shapes=[pltpu.VMEM((2, n_pairs_a + n_pairs_b, EVEN_TILE, LANES), BF)],
        compiler_params=_params(("arbitrary",), VMEM_LIMIT),
        name="even_tail",
    )(slopes, sinks, x, p, p, p, p, p, p, p, p, _sb_matrix(), w_mix, g.reshape(1, D_MODEL), w_q,
      q_colscale.reshape(1, D_MODEL).astype(F32), kv, w_o)


def _alibi_log2(n_heads):
    return jnp.asarray(LOG2_E * 2.0 ** (-8.0 * np.arange(1, n_heads + 1) / n_heads), dtype=F32)


def _even_projection(x, norm_g, w_in, q_gain, k_gain):
    hd = HEAD_DIM
    a_q, a_kv, b_w = A_Q_HEADS * hd, A_KV_HEADS * hd, B_HEADS * hd
    scale = hd ** -0.5 * LOG2_E
    ones = lambda n: jnp.ones((n,), F32)
    cs = jnp.concatenate([jnp.tile(q_gain, A_Q_HEADS) * scale, jnp.tile(k_gain, A_KV_HEADS), ones(a_kv),
                          ones(b_w) * scale, ones(2 * b_w)])
    plan = ([(True, False)] * (a_q // LANES) + [(True, True)] * (a_kv // LANES) + [(False, True)] * (a_kv // LANES)
            + [(False, False)] * (3 * b_w // LANES))
    return _proj(x, norm_g, w_in, cs, plan, hd)


def _odd_mixer_heads(x, norm_g, w_in, q_gain, k_gain):
    hd = HEAD_DIM
    cs = jnp.concatenate([jnp.tile(q_gain, C_HEADS) * (hd ** -0.5 * LOG2_E), jnp.tile(k_gain, C_HEADS),
                          jnp.ones((C_HEADS * hd,), F32)])
    head_blocks = C_HEADS * hd // LANES
    plan = [(True, False)] * (2 * head_blocks) + [(False, False)] * head_blocks
    p = _proj(x, norm_g, w_in, cs, plan, hd)
    return [_dilated(p, _alibi_log2(C_HEADS))]


def _memory_kv(mem2d, mem_g, w_kv, k_gain):
    cs_kv = jnp.concatenate([jnp.tile(k_gain, X_HEADS), jnp.ones((D_MODEL,), F32)])
    plan = [(True, False)] * (D_MODEL // LANES) + [(False, False)] * (D_MODEL // LANES)
    return _proj(mem2d, mem_g, w_kv, cs_kv, plan, X_HEAD_DIM, tm=MEM_LEN)


def kernel(x, mem, ffn1_norm, ffn1_w_gu, ffn1_w_down, mix_norm, ev_w_in, ev_q_gain, ev_k_gain, ev_sinks, ev_w_out, od_w_in, od_q_gain, od_k_gain, od_w_out, xa_norm, xa_mem_norm, xa_w_q, xa_w_kv, xa_q_gain, xa_k_gain, xa_w_o, ffn2_norm, ffn2_w_gu, ffn2_w_down):
    x = x.reshape(N_TOK, D_MODEL)
    mem2d = mem.reshape(BATCH * MEM_LEN, D_MODEL)
    w_gu, w_down = _cast_now((ffn1_w_gu, 0)), _cast_now((ffn1_w_down, 0))
    for layer in range(DEPTH):
        j = layer // 2
        even = layer % 2 == 0
        w_in3, w_mix3 = (ev_w_in, ev_w_out) if even else (od_w_in, od_w_out)
        jobs = [(w_in3, j), (w_mix3, j), (xa_w_q, layer), (xa_w_kv, layer), (xa_w_o, layer),
                (ffn2_w_gu, layer), (ffn2_w_down, layer)]
        x, (w_in, w_mix, w_q, w_kv, w_o, w_gu, w_down) = _ffn(x, ffn1_norm[layer], w_gu, w_down, jobs)
        kv = _memory_kv(mem2d, xa_mem_norm[layer], w_kv, xa_k_gain[layer])
        cs_q = jnp.tile(xa_q_gain[layer], X_HEADS) * (X_HEAD_DIM ** -0.5 * LOG2_E)
        if even:
            p = _even_projection(x, mix_norm[layer], w_in, ev_q_gain[j], ev_k_gain[j])
            x = _even_tail(x, p, _alibi_log2(A_Q_HEADS), ev_sinks[j].astype(F32) * LOG2_E, w_mix,
                           xa_norm[layer], w_q, cs_q, kv, w_o)
        else:
            heads = _odd_mixer_heads(x, mix_norm[layer], w_in, od_q_gain[j], od_k_gain[j])
            x = _mix_xattn(x, heads, w_mix, xa_norm[layer], w_q, cs_q, kv, w_o)
        jobs = [(ffn1_w_gu, layer + 1), (ffn1_w_down, layer + 1)] if layer + 1 < DEPTH else []
        x, next_ffn1 = _ffn(x, ffn2_norm[layer], w_gu, w_down, jobs)
        if next_ffn1:
            w_gu, w_down = next_ffn1
    return x.reshape(BATCH, SEQ, D_MODEL)
```

```python
import functools

import numpy as np
import jax
import jax.numpy as jnp
from jax import lax
from jax.experimental import pallas as pl
from jax.experimental.pallas import tpu as pltpu

D_MODEL = 1024
BATCH = 4
SEQ = 4096
N_TOK = BATCH * SEQ
DEPTH = 2
HEAD_DIM = 64
BLOCK = 128
A_Q_HEADS = 8
A_KV_HEADS = 2
A_WINDOW = 128
B_HEADS = 8
C_HEADS = 16
C_PATTERNS = ((128, 1), (512, 4), (2048, 16))
MEM_LEN = 256
X_HEADS = 4
X_HEAD_DIM = D_MODEL // X_HEADS
D_FF = 2816
RMS_EPS = 1e-6

LANES = 128
MXU_N = 256
VMEM_LIMIT = 56 * 1024 * 1024

BF = jnp.bfloat16
F32 = jnp.float32
NT_DIMS = (((1,), (1,)), ((), ()))
LOG2_E = 1.4426950408889634


def _params(sem, vmem=None):
    return pltpu.CompilerParams(dimension_semantics=sem, vmem_limit_bytes=vmem)


def _resident(shape):
    nd = len(shape)
    return pl.BlockSpec(shape, lambda *_: (0,) * nd, pipeline_mode=pl.Buffered(1))


BF16_SUBLANES = 16


def _cast_specs(job, steps):
    w3, layer = job
    _, r, c = w3.shape
    rb = next(rb for rb in range(BF16_SUBLANES, r + 1, BF16_SUBLANES) if r % rb == 0 and r // rb <= steps)
    last = r // rb - 1
    return (pl.BlockSpec((None, rb, c), lambda i: (layer, jnp.minimum(i, last), 0)),
            pl.BlockSpec((rb, c), lambda i: (jnp.minimum(i, last), 0)),
            jax.ShapeDtypeStruct((r, c), BF))


def _run_cast_jobs(in_refs, out_refs):
    for src, dst in zip(in_refs, out_refs):
        dst[...] = src[...].astype(BF)


def _cast_kernel(w_ref, o_ref):
    _run_cast_jobs([w_ref], [o_ref])


def _cast_now(job, *, rows=128):
    steps = job[0].shape[1] // rows
    in_spec, out_spec, out_shape = _cast_specs(job, steps)
    return pl.pallas_call(
        _cast_kernel, grid=(steps,), in_specs=[in_spec], out_specs=out_spec, out_shape=out_shape,
        compiler_params=_params(("arbitrary",)),
        name="cast",
    )(job[0])


def _rms(xv, g):
    ms = jnp.mean(xv * xv, axis=-1, keepdims=True)
    return xv * lax.rsqrt(ms + RMS_EPS) * g


FFN_SPLIT = (D_FF // MXU_N + 1) // 2 * MXU_N
FFN_CHUNKS = ((0, FFN_SPLIT), (FFN_SPLIT, D_FF))


def _ffn_kernel(*refs, n_jobs):
    x_ref, g_ref, wgu_ref, wd_ref = refs[:4]
    o_ref = refs[4 + n_jobs]
    xv = x_ref[...]
    h = _rms(xv, g_ref[...]).astype(BF)
    acc = jnp.zeros_like(xv)
    for c0, c1 in FFN_CHUNKS:
        gate = jnp.dot(h, wgu_ref[:, c0:c1], preferred_element_type=F32)
        up = jnp.dot(h, wgu_ref[:, D_FF + c0:D_FF + c1], preferred_element_type=F32)
        act = (gate * jax.nn.sigmoid(gate) * up).astype(BF)
        acc = acc + jnp.dot(act, wd_ref[c0:c1, :], preferred_element_type=F32)
    o_ref[...] = xv + 0.5 * acc
    _run_cast_jobs(refs[4:4 + n_jobs], refs[5 + n_jobs:])


def _ffn(x, g, w_gu, w_down, cast_jobs=(), *, tm=512):
    steps = N_TOK // tm
    specs = [_cast_specs(job, steps) for job in cast_jobs]
    out = pl.pallas_call(
        functools.partial(_ffn_kernel, n_jobs=len(cast_jobs)),
        grid=(steps,),
        in_specs=[pl.BlockSpec((tm, D_MODEL), lambda i: (i, 0)),
                  _resident((1, D_MODEL)),
                  _resident(w_gu.shape),
                  _resident(w_down.shape)] + [s[0] for s in specs],
        out_specs=[pl.BlockSpec((tm, D_MODEL), lambda i: (i, 0))] + [s[1] for s in specs],
        out_shape=[jax.ShapeDtypeStruct((N_TOK, D_MODEL), F32)] + [s[2] for s in specs],
        compiler_params=_params(("arbitrary",), VMEM_LIMIT),
        name="ffn",
    )(x, g.reshape(1, D_MODEL), w_gu, w_down, *[job[0] for job in cast_jobs])
    return out[0], out[1:]


def _proj_kernel(x_ref, g_ref, w_ref, cs_ref, *rest, plan, gs, side):
    if side is None:
        (o_ref,) = rest
    else:
        sx_ref, sg_ref, sw_ref, scs_ref, o_ref, so_ref = rest
        pl.when(pl.program_id(0) == 0)(
            functools.partial(_project, sx_ref, sg_ref, sw_ref, scs_ref, so_ref, plan=side[0], gs=side[1]))
    _project(x_ref, g_ref, w_ref, cs_ref, o_ref, plan=plan, gs=gs)


def _project(x_ref, g_ref, w_ref, cs_ref, o_ref, *, plan, gs):
    assert gs in (HEAD_DIM, MXU_N)
    h = _rms(x_ref[...], g_ref[...]).astype(BF)
    n_chunks = len(plan) // 2
    lo = lax.broadcasted_iota(jnp.int32, (x_ref.shape[0], LANES), 1) < HEAD_DIM

    def main(j):
        return jnp.dot(h, w_ref[:, MXU_N * j:MXU_N * (j + 1)], preferred_element_type=F32)

    acc_next = main(0)
    out = 0
    for j in range(n_chunks):
        cols = slice(MXU_N * j, MXU_N * (j + 1))
        acc = acc_next
        if j + 1 < n_chunks:
            acc_next = main(j + 1)
        y = acc * cs_ref[:, cols]
        halves = plan[2 * j:2 * j + 2]
        if gs == MXU_N and any(normed for normed, _ in halves):
            inv_chunk = lax.rsqrt(jnp.mean(acc * acc, axis=1, keepdims=True) + RMS_EPS)
        for half, (normed, dup) in enumerate(halves):
            lanes = slice(LANES * half, LANES * (half + 1))
            yh = y[:, lanes]
            if normed and gs == MXU_N:
                yh = yh * inv_chunk
            elif normed:
                sq = acc[:, lanes] * acc[:, lanes]
                s_lo = jnp.sum(jnp.where(lo, sq, 0.0), axis=1, keepdims=True)
                s_hi = jnp.sum(jnp.where(lo, 0.0, sq), axis=1, keepdims=True)
                yh = yh * lax.rsqrt(jnp.where(lo, s_lo, s_hi) * (1.0 / gs) + RMS_EPS)
            if dup:
                swapped = pltpu.roll(yh, HEAD_DIM, axis=1)
                o_ref[out] = jnp.where(lo, yh, swapped).astype(BF)
                o_ref[out + 1] = jnp.where(lo, swapped, yh).astype(BF)
                out += 2
            else:
                o_ref[out] = yh.astype(BF)
                out += 1


def _proj(x, g, w, colscale, plan, gs, *, side=None, tm=1024):
    rows = x.shape[0]
    wout = w.shape[1]
    assert wout == LANES * len(plan) and len(plan) % 2 == 0
    n_blocks = lambda pln: sum(2 if dup else 1 for _, dup in pln)
    args = [x, g.reshape(1, D_MODEL), w, colscale.reshape(1, wout).astype(F32)]
    in_specs = [pl.BlockSpec((tm, D_MODEL), lambda i: (i, 0)), _resident((1, D_MODEL)), _resident(w.shape),
                _resident((1, wout))]
    out_specs = [pl.BlockSpec((n_blocks(plan), tm, LANES), lambda i: (0, i, 0))]
    out_shape = [jax.ShapeDtypeStruct((n_blocks(plan), rows, LANES), BF)]
    if side is not None:
        x2, g2, w2, cs2, plan2, gs2 = side
        assert w2.shape[1] == LANES * len(plan2)
        args += [x2, g2.reshape(1, D_MODEL), w2, cs2.reshape(1, -1).astype(F32)]
        in_specs += [_resident(x2.shape), _resident((1, D_MODEL)), _resident(w2.shape), _resident((1, w2.shape[1]))]
        out_specs.append(pl.BlockSpec((n_blocks(plan2), x2.shape[0], LANES), lambda i: (0, 0, 0)))
        out_shape.append(jax.ShapeDtypeStruct((n_blocks(plan2), x2.shape[0], LANES), BF))
    out = pl.pallas_call(
        functools.partial(_proj_kernel, plan=tuple(plan), gs=gs,
                          side=None if side is None else (tuple(side[4]), side[5])),
        grid=(rows // tm,),
        in_specs=in_specs, out_specs=out_specs, out_shape=out_shape,
        compiler_params=_params(("arbitrary",), VMEM_LIMIT),
        name="proj",
    )(*args)
    return out[0] if side is None else tuple(out)


def _swa_block(q_blocks, kp, kc, vp, vc, has_prev, slopes_ref, sinks_ref, *, kv_div, max_dist):
    row = lax.broadcasted_iota(jnp.int32, (BLOCK, 2 * BLOCK), 0)
    col = lax.broadcasted_iota(jnp.int32, (BLOCK, 2 * BLOCK), 1)
    dist = row + BLOCK - col
    valid = (dist >= 0) & (dist <= max_dist)
    if has_prev is not True:
        valid = valid & ((col >= BLOCK) | has_prev)
    negmask = jnp.where(valid, 0.0, -jnp.inf)
    distf = dist.astype(F32)
    lo = lax.broadcasted_iota(jnp.int32, (BLOCK, LANES), 1) < HEAD_DIM

    n_groups = len(q_blocks) // kv_div
    heads_per_group = 2 * kv_div
    scores = []
    for g in range(n_groups):
        parts = []
        for p in range(g * kv_div, (g + 1) * kv_div):
            q2 = q_blocks[p].astype(F32)
            parts += [jnp.where(lo, q2, 0.0), jnp.where(lo, 0.0, q2)]
        q_stack = jnp.concatenate(parts, axis=0).astype(BF)
        scores.append(jnp.concatenate(
            [lax.dot_general(q_stack, kp[g], NT_DIMS, preferred_element_type=F32),
             lax.dot_general(q_stack, kc[g], NT_DIMS, preferred_element_type=F32)], axis=1))
    soft = []
    for g in range(n_groups):
        res = []
        for j in range(heads_per_group):
            h = g * heads_per_group + j
            s = scores[g][j * BLOCK:(j + 1) * BLOCK] - slopes_ref[h] * distf + negmask
            m = jnp.maximum(jnp.max(s, axis=1, keepdims=True), sinks_ref[h])
            pe = jnp.exp2(s - m)
            res.append((pe.astype(BF), jnp.sum(pe, axis=1, keepdims=True) + jnp.exp2(sinks_ref[h] - m)))
        soft.append(res)
    outs = []
    for g in range(n_groups):
        pb = jnp.concatenate([r[0] for r in soft[g]], axis=0)
        pv = (jnp.dot(pb[:, :BLOCK], vp[g], preferred_element_type=F32)
              + jnp.dot(pb[:, BLOCK:], vc[g], preferred_element_type=F32))
        for jp in range(kv_div):
            o0 = pv[(2 * jp) * BLOCK:(2 * jp + 1) * BLOCK] / soft[g][2 * jp][1]
            o1 = pv[(2 * jp + 1) * BLOCK:(2 * jp + 2) * BLOCK] / soft[g][2 * jp + 1][1]
            outs.append(jnp.where(lo, o0, o1))
    return outs


DIL_ORDER = tuple(sorted(C_PATTERNS, key=lambda wd: -wd[1]))
DIL_UNROLL = 16
DIL_AHEAD = 2
DIL_BASE = 4
DIL_Q = SEQ // DIL_BASE
DIL_CONVERT_ROWS = DIL_BASE * BLOCK


def _dilated_kernel(slopes_ref, q_ref, k_ref, v_ref, o_ref, qn_s, tq_s, tk_s, tv_s, q0_s, q1_s, k_s, v_s,
                    acc_r, m_r, l_r, acc_n, m_n, l_n):
    assert all(d == 1 or d % DIL_BASE == 0 for _, d in DIL_ORDER) and DIL_ORDER[-1][1] == 1
    p = pl.program_id(1)
    lo = lax.broadcasted_iota(jnp.int32, (BLOCK, LANES), 1) < HEAD_DIM

    def convert(c, carry):
        rows = pl.ds(pl.multiple_of(c * DIL_CONVERT_ROWS, DIL_CONVERT_ROWS), DIL_CONVERT_ROWS)
        q_nat = q_ref[0, rows, :].astype(F32)
        lo_c = lax.broadcasted_iota(jnp.int32, (DIL_CONVERT_ROWS, LANES), 1) < HEAD_DIM
        qn_s[0, rows, :] = jnp.where(lo_c, q_nat, 0.0).astype(BF)
        qn_s[1, rows, :] = jnp.where(lo_c, 0.0, q_nat).astype(BF)
        tq_s[...] = q_nat
        tk_s[...] = k_ref[0, rows, :].astype(F32)
        tv_s[...] = v_ref[0, rows, :].astype(F32)
        for rho in range(DIL_BASE):
            src = pl.ds(rho, BLOCK, stride=DIL_BASE)
            dst = pl.ds(pl.multiple_of(rho * DIL_Q + c * BLOCK, BLOCK), BLOCK)
            q = tq_s[src, :]
            q0_s[dst, :] = jnp.where(lo, q, 0.0)
            q1_s[dst, :] = jnp.where(lo, 0.0, q)
            k_s[dst, :] = tk_s[src, :]
            v_s[dst, :] = tv_s[src, :]
        return carry

    lax.fori_loop(0, SEQ // DIL_CONVERT_ROWS, convert, 0)

    row = lax.broadcasted_iota(jnp.int32, (BLOCK, 2 * BLOCK), 0)
    col = lax.broadcasted_iota(jnp.int32, (BLOCK, 2 * BLOCK), 1)
    dist = row + BLOCK - col
    distf = dist.astype(F32)
    no_prev = jnp.where(col < BLOCK, -jnp.inf, 0.0)

    def bcast2(a0, a1):
        return jnp.where(lo, jnp.broadcast_to(a0, (BLOCK, LANES)), jnp.broadcast_to(a1, (BLOCK, LANES)))

    for pi, (window, d) in enumerate(DIL_ORDER):
        first, last = pi == 0, pi == len(DIL_ORDER) - 1
        natural = d == 1
        nb = SEQ // d // BLOCK
        band = (dist >= 0) & (dist <= window // d)
        bias = [jnp.where(band, (-float(d) * slopes_ref[2 * p + hh]) * distf, -jnp.inf) for hh in range(2)]
        acc_s, m_s, l_s = (acc_n, m_n, l_n) if natural else (acc_r, m_r, l_r)
        to_natural = not natural and not last and DIL_ORDER[pi + 1][1] == 1
        assert (not to_natural or d == DIL_BASE) and (not natural or first or DIL_ORDER[pi - 1][1] == DIL_BASE)
        acc_o, m_o, l_o = (acc_n, m_n, l_n) if to_natural else (acc_s, m_s, l_s)

        def rows_of(r, n, n_blocks=1, d=d, natural=natural):
            size = n_blocks * BLOCK
            if natural:
                return pl.ds(pl.multiple_of(BLOCK * n, BLOCK), size)
            inner = d // DIL_BASE
            start = (r % DIL_BASE) * DIL_Q + r // DIL_BASE + inner * BLOCK * n
            return pl.ds(start, size, stride=inner) if inner > 1 else pl.ds(pl.multiple_of(start, BLOCK), size)

        def step(it, carry, nb=nb, bias=bias, first=first, last=last, natural=natural, rows_of=rows_of,
                 acc_s=acc_s, m_s=m_s, l_s=l_s, acc_o=acc_o, m_o=m_o, l_o=l_o, to_natural=to_natural):
            assert DIL_UNROLL % nb == 0 or nb % DIL_UNROLL == 0
            load_k = (lambda rr: k_ref[0, rr, :]) if natural else (lambda rr: k_s[rr, :].astype(BF))
            load_v = (lambda rr: v_ref[0, rr, :]) if natural else (lambda rr: v_s[rr, :].astype(BF))

            def scores(u):
                t = it * DIL_UNROLL + u
                r, n = t // nb, t % nb
                prev = (u % nb != 0) if nb <= DIL_UNROLL else (True if u else None)
                rows = rows_of(r, n)
                out_rows = pl.ds(DIL_BASE * BLOCK * n + r, BLOCK, stride=DIL_BASE) if to_natural else rows
                if prev is True:
                    kv_rows = [rows_of(r, n - 1, 2)]
                elif prev is None:
                    kv_rows = [rows_of(r, jnp.maximum(n - 1, 0)), rows]
                else:
                    kv_rows = [rows]
                if natural:
                    qh = jnp.concatenate([qn_s[0, rows, :], qn_s[1, rows, :]], axis=0)
                else:
                    qh = jnp.concatenate([q0_s[rows, :], q1_s[rows, :]], axis=0).astype(BF)
                s = jnp.concatenate([lax.dot_general(qh, load_k(rr), NT_DIMS, preferred_element_type=F32)
                                     for rr in kv_rows], axis=1)
                return n, prev, rows, out_rows, kv_rows, (s[:BLOCK], s[BLOCK:])

            def softmax_pv(blk):
                n, prev, rows, out_rows, kv_rows, s = blk
                ms, ls, pes = [], [], []
                for hh in range(2):
                    sh = s[hh] + (bias[hh][:, BLOCK:] if prev is False else bias[hh])
                    if prev is None:
                        sh = sh + jnp.where(n == 0, no_prev, 0.0)
                    m = jnp.max(sh, axis=1, keepdims=True)
                    pe = jnp.exp2(sh - m)
                    ms.append(m)
                    ls.append(jnp.sum(pe, axis=1, keepdims=True))
                    pes.append(pe.astype(BF))
                pb = jnp.concatenate(pes, axis=0)
                v = jnp.concatenate([load_v(rr) for rr in kv_rows], axis=0) if len(kv_rows) > 1 else load_v(kv_rows[0])
                pv = jnp.dot(pb, v, preferred_element_type=F32)
                return rows, out_rows, ms, ls, (pv[:BLOCK], pv[BLOCK:])

            def merge(rows, out_rows, ms, ls, pv):
                m2 = bcast2(ms[0], ms[1])
                l2 = bcast2(ls[0], ls[1])
                acc2 = jnp.where(lo, pv[0], pv[1])
                if not first:
                    m_old = m_s[rows, :]
                    m_new = jnp.maximum(m_old, m2)
                    a_old, a_new = jnp.exp2(m_old - m_new), jnp.exp2(m2 - m_new)
                    l2 = a_old * l_s[rows, :] + a_new * l2
                    acc2 = a_old * acc_s[rows, :] + a_new * acc2
                    m2 = m_new
                if last:
                    o_ref[0, rows, :] = (acc2 / l2).astype(BF)
                else:
                    m_o[out_rows, :] = m2
                    l_o[out_rows, :] = l2
                    acc_o[out_rows, :] = acc2

            pending = {u: scores(u) for u in range(DIL_AHEAD)}
            done = None
            for u in range(DIL_UNROLL):
                if u + DIL_AHEAD < DIL_UNROLL:
                    pending[u + DIL_AHEAD] = scores(u + DIL_AHEAD)
                cur = softmax_pv(pending.pop(u))
                if done is not None:
                    merge(*done)
                done = cur
            merge(*done)
            return carry

        lax.fori_loop(0, SEQ // BLOCK // DIL_UNROLL, step, 0)


def _dilated(qkv, slopes):
    n_pairs = C_HEADS // 2
    seq_f32 = pltpu.VMEM((SEQ, LANES), F32)
    chunk_f32 = pltpu.VMEM((DIL_CONVERT_ROWS, LANES), F32)
    return pl.pallas_call(
        _dilated_kernel,
        grid=(BATCH, n_pairs),
        in_specs=[pl.BlockSpec(memory_space=pltpu.SMEM),
                  pl.BlockSpec((1, SEQ, LANES), lambda b, p: (p, b, 0)),
                  pl.BlockSpec((1, SEQ, LANES), lambda b, p: (n_pairs + p, b, 0)),
                  pl.BlockSpec((1, SEQ, LANES), lambda b, p: (2 * n_pairs + p, b, 0))],
        out_specs=pl.BlockSpec((1, SEQ, LANES), lambda b, p: (p, b, 0)),
        out_shape=jax.ShapeDtypeStruct((n_pairs, N_TOK, LANES), BF),
        scratch_shapes=[pltpu.VMEM((2, SEQ, LANES), BF)] + [chunk_f32] * 3 + [seq_f32] * 10,
        compiler_params=_params(("parallel", "parallel"), VMEM_LIMIT),
        name="dilated",
    )(slopes, qkv, qkv, qkv)


SB_QB = MXU_N
SB_FIRST_TILES = 2
SB_DEAD_LOG2 = -150.0


def _sb_matrix():
    idx = np.arange(SB_QB)
    return jnp.asarray(-(idx[:, None] > idx[None, :]).astype(np.float32), dtype=BF)


def _sb_unit(q2, load_k, load_v, iq, uo):
    lo = lax.broadcasted_iota(jnp.int32, (SB_QB, LANES), 1) < HEAD_DIM
    q_stack = jnp.concatenate([jnp.where(lo, q2, 0.0), jnp.where(lo, 0.0, q2)], axis=0).astype(BF)
    rel1 = (lax.broadcasted_iota(jnp.int32, (SB_QB, SB_QB), 1)
            - lax.broadcasted_iota(jnp.int32, (SB_QB, SB_QB), 0))
    rel = jnp.concatenate([rel1, rel1], axis=0)

    def scores(first, n_tiles):
        return lax.dot_general(q_stack, load_k(first, n_tiles), NT_DIMS,
                               preferred_element_type=F32)

    def walk(first, n_tiles, carry, masked):
        c, o = carry
        z = scores(first, n_tiles)
        order = list(reversed(range(n_tiles)))
        ws, es, stricts, totals = {}, {}, {}, {}
        for t in order:
            zt = z[:, t * SB_QB:(t + 1) * SB_QB]
            sp = jnp.maximum(zt, 0.0) + jnp.log2(1.0 + jnp.exp2(-jnp.abs(zt)))
            es[t] = zt - sp
            if masked:
                stricts[t] = rel < (iq - first - t) * SB_QB
                sp = jnp.where(stricts[t], sp, 0.0)
            ws[t] = jnp.dot(sp.astype(BF), uo, preferred_element_type=F32)
            totals[t] = jnp.sum(sp, axis=1, keepdims=True)
        parts = {}
        for t in order:
            a = jnp.exp2(es[t] + jnp.concatenate([c] * (SB_QB // LANES), axis=1) + ws[t])
            if masked:
                a = jnp.where(stricts[t], a, 0.0)
            parts[t] = a.astype(BF)
            c = c - totals[t]
        pv = jnp.dot(jnp.concatenate([parts[t] for t in range(n_tiles)], axis=1), load_v(first, n_tiles),
                     preferred_element_type=F32)
        return c, o + jnp.where(lo, pv[:SB_QB], pv[SB_QB:])

    def alive(c):
        return jnp.max(c) > SB_DEAD_LOG2

    def body(state):
        g = state[0]
        c, o = walk(g, 1, state[2:], False)
        return g - 1, alive(c), c, o

    first = jnp.maximum(iq - 1, 0)
    zeros = (jnp.zeros((2 * SB_QB, LANES), F32), jnp.zeros((SB_QB, LANES), F32))
    c, o = walk(first, SB_FIRST_TILES, zeros, True)
    more = (first > 0) & alive(c)

    def finish():
        state = lax.while_loop(lambda st: (st[0] >= 0) & st[1], body, (first - 1, more, c, o))
        return state[3]

    return more, o, finish


def _mix_xattn_kernel(*refs):
    x_ref = refs[0]
    wm_ref, g_ref, wq_ref, cs_ref, kv_ref, wo_ref, o_ref = refs[-7:]
    mixed = jnp.concatenate([r[c] for r in refs[1:-7] for c in range(r.shape[0])], axis=1)
    xv = x_ref[...] + jnp.dot(mixed, wm_ref[...], preferred_element_type=F32)
    h = _rms(xv, g_ref[...]).astype(BF)
    heads = range(X_HEADS)
    cols = [slice(X_HEAD_DIM * hd, X_HEAD_DIM * (hd + 1)) for hd in heads]
    acc = [jnp.dot(h, wq_ref[:, cols[hd]], preferred_element_type=F32) for hd in heads]
    ms = [jnp.mean(acc[hd] * acc[hd], axis=1, keepdims=True) for hd in heads]
    q = [(acc[hd] * cs_ref[:, cols[hd]] * lax.rsqrt(ms[hd] + RMS_EPS)).astype(BF) for hd in heads]
    s = [lax.dot_general(q[hd], jnp.concatenate([kv_ref[2 * hd], kv_ref[2 * hd + 1]], axis=1), NT_DIMS,
                         preferred_element_type=F32) for hd in heads]
    pe, l = [], []
    for hd in heads:
        e = jnp.exp2(s[hd] - jnp.max(s[hd], axis=1, keepdims=True))
        l.append(jnp.sum(e, axis=1, keepdims=True))
        pe.append(e.astype(BF))
    v0 = 2 * X_HEADS
    pv = [jnp.dot(pe[hd], jnp.concatenate([kv_ref[v0 + 2 * hd], kv_ref[v0 + 2 * hd + 1]], axis=1),
                  preferred_element_type=F32) for hd in heads]
    o = jnp.concatenate([(pv[hd] / l[hd]).astype(BF) for hd in heads], axis=1)
    o_ref[...] = xv + jnp.dot(o, wo_ref[...], preferred_element_type=F32)


def _mix_xattn(x, mixer_heads, w_mix, g, w_q, q_colscale, kv, w_o, *, tm=1024):
    tiles_per_batch = SEQ // tm
    in_specs = [pl.BlockSpec((tm, D_MODEL), lambda i: (i, 0))]
    in_specs += [pl.BlockSpec((mh.shape[0], tm, LANES), lambda i: (0, i, 0)) for mh in mixer_heads]
    in_specs += [_resident(w_mix.shape),
                 _resident((1, D_MODEL)),
                 _resident(w_q.shape),
                 _resident((1, D_MODEL)),
                 pl.BlockSpec((4 * X_HEADS, MEM_LEN, LANES), lambda i: (0, i // tiles_per_batch, 0)),
                 _resident(w_o.shape)]
    return pl.pallas_call(
        _mix_xattn_kernel, grid=(N_TOK // tm,),
        in_specs=in_specs,
        out_specs=pl.BlockSpec((tm, D_MODEL), lambda i: (i, 0)),
        out_shape=jax.ShapeDtypeStruct((N_TOK, D_MODEL), F32),
        compiler_params=_params(("parallel",), VMEM_LIMIT),
        name="mix_xattn",
    )(x, *mixer_heads, w_mix, g.reshape(1, D_MODEL), w_q,
      q_colscale.reshape(1, D_MODEL).astype(F32), kv, w_o)


EVEN_TILE = 512


def _even_tail_kernel(slopes_ref, sinks_ref, x_ref, swq_ref, swkp_ref, swkc_ref, swvp_ref, swvc_ref,
                      sbq_ref, sbk_ref, sbv_ref, uo_ref, wm_ref, g_ref, wq_ref, cs_ref, kv_ref, wo_ref,
                      o_ref, heads_s, *, kv_div, max_dist):
    i = pl.program_id(0)
    tiles_per_seq = SEQ // EVEN_TILE
    t_in_seq = jnp.minimum(i, N_TOK // EVEN_TILE - 1) % tiles_per_seq
    wr, rd = i % 2, (i + 1) % 2
    n_pairs_a, n_pairs_b = swq_ref.shape[0], sbq_ref.shape[0]

    @pl.when(i == 0)
    def _():
        heads_s[1] = jnp.zeros(heads_s.shape[1:], BF)

    def mix_steps():
        heads = range(X_HEADS)
        cols = [slice(X_HEAD_DIM * hd, X_HEAD_DIM * (hd + 1)) for hd in heads]
        st = {"xv": [None] * X_HEADS, "q": [None] * X_HEADS, "pe": [None] * X_HEADS, "l": [None] * X_HEADS,
              "o": [None] * X_HEADS}

        def project(c):
            if c == 0:
                st["mixed"] = jnp.concatenate([heads_s[rd, k] for k in range(n_pairs_a + n_pairs_b)], axis=1)
            st["xv"][c] = x_ref[:, cols[c]] + jnp.dot(st["mixed"], wm_ref[:, cols[c]], preferred_element_type=F32)

        def q_proj(hd):
            if hd == 0:
                xv = jnp.concatenate(st["xv"], axis=1)
                st["h"] = _rms(xv, g_ref[...]).astype(BF)
            acc = jnp.dot(st["h"], wq_ref[:, cols[hd]], preferred_element_type=F32)
            ms = jnp.mean(acc * acc, axis=1, keepdims=True)
            st["q"][hd] = (acc * cs_ref[:, cols[hd]] * lax.rsqrt(ms + RMS_EPS)).astype(BF)

        def scores(hd):
            kh = jnp.concatenate([kv_ref[2 * hd], kv_ref[2 * hd + 1]], axis=1)
            s = lax.dot_general(st["q"][hd], kh, NT_DIMS, preferred_element_type=F32)
            e = jnp.exp2(s - jnp.max(s, axis=1, keepdims=True))
            st["l"][hd] = jnp.sum(e, axis=1, keepdims=True)
            st["pe"][hd] = e.astype(BF)

        def values(hd):
            v0 = 2 * X_HEADS
            vh = jnp.concatenate([kv_ref[v0 + 2 * hd], kv_ref[v0 + 2 * hd + 1]], axis=1)
            st["o"][hd] = (jnp.dot(st["pe"][hd], vh, preferred_element_type=F32) / st["l"][hd]).astype(BF)

        def out_proj(c):
            if c == 0:
                st["oc"] = jnp.concatenate(st["o"], axis=1)
            o_ref[:, cols[c]] = st["xv"][c] + jnp.dot(st["oc"], wo_ref[:, cols[c]], preferred_element_type=F32)

        return [functools.partial(f, k) for f in (project, q_proj, scores, values, out_proj) for k in heads]

    def attend_and_mix():
        for step in mix_steps():
            step()

        uo = uo_ref[...]
        blocks_per_tile = EVEN_TILE // SB_QB

        def sb_first_pass(p, j):
            rows = slice(SB_QB * j, SB_QB * (j + 1))
            load = lambda ref: (lambda first, n: ref[p, pl.ds(pl.multiple_of(first * SB_QB, SB_QB), n * SB_QB), :])
            return _sb_unit(sbq_ref[p, rows, :].astype(F32), load(sbk_ref), load(sbv_ref),
                            blocks_per_tile * t_in_seq + j, uo)

        def store_sb(p, j, o):
            heads_s[wr, n_pairs_a + p, SB_QB * j:SB_QB * (j + 1), :] = o.astype(BF)

        units = [(p, j) for p in range(n_pairs_b) for j in range(blocks_per_tile)]
        passes = [sb_first_pass(p, j) for p, j in units]
        for (p, j), (_, o, _) in zip(units, passes):
            store_sb(p, j, o)

        n_kv = swkc_ref.shape[0]
        for m in range(EVEN_TILE // BLOCK):
            rows = slice(BLOCK * m, BLOCK * (m + 1))
            if m == 0:
                kp, vp, has_prev = [swkp_ref[g] for g in range(n_kv)], [swvp_ref[g] for g in range(n_kv)], t_in_seq > 0
            else:
                prows = slice(BLOCK * (m - 1), BLOCK * m)
                kp, vp, has_prev = ([swkc_ref[g, prows, :] for g in range(n_kv)],
                                    [swvc_ref[g, prows, :] for g in range(n_kv)], True)
            outs = _swa_block([swq_ref[p, rows, :] for p in range(n_pairs_a)], kp,
                              [swkc_ref[g, rows, :] for g in range(n_kv)], vp, [swvc_ref[g, rows, :] for g in range(n_kv)],
                              has_prev, slopes_ref, sinks_ref, kv_div=kv_div, max_dist=max_dist)
            for p in range(n_pairs_a):
                heads_s[wr, p, rows, :] = outs[p].astype(BF)

        @pl.when(functools.reduce(jnp.logical_or, [more for more, _, _ in passes]))
        def _():
            for (p, j), (_, _, finish) in zip(units, passes):
                store_sb(p, j, finish())

    n_tiles = N_TOK // EVEN_TILE
    pl.when(i < n_tiles)(attend_and_mix)

    @pl.when(i == n_tiles)
    def _():
        for step in mix_steps():
            step()


def _even_tail(x, p, slopes, sinks, w_mix, g, w_q, q_colscale, kv, w_o):
    n_pairs_a, n_kv, n_pairs_b = A_Q_HEADS // 2, A_KV_HEADS, B_HEADS // 2
    assert p.shape[0] == n_pairs_a + 2 * n_kv + 3 * n_pairs_b and n_pairs_a == n_pairs_b == 2 * n_kv
    n_tiles = N_TOK // EVEN_TILE
    tiles_per_seq = SEQ // EVEN_TILE
    blocks_per_tile = EVEN_TILE // BLOCK
    att = lambda i: jnp.minimum(i, n_tiles - 1)
    mix = lambda i: jnp.maximum(i - 1, 0)
    tile_rows = lambda size, idx: pl.BlockSpec((size, EVEN_TILE, LANES), lambda i: (idx, att(i), 0))
    prev_block = lambda idx: pl.BlockSpec(
        (n_kv, BLOCK, LANES), lambda i: (idx, jnp.maximum(blocks_per_tile * att(i) - 1, 0), 0))
    whole_seq = lambda idx: pl.BlockSpec((n_pairs_b, SEQ, LANES), lambda i: (idx, att(i) // tiles_per_seq, 0),
                                         pipeline_mode=pl.Buffered(1))
    smem = pl.BlockSpec(memory_space=pltpu.SMEM)
    return pl.pallas_call(
        functools.partial(_even_tail_kernel, kv_div=n_pairs_a // n_kv, max_dist=A_WINDOW - 1),
        grid=(n_tiles + 1,),
        in_specs=[smem, smem,
                  pl.BlockSpec((EVEN_TILE, D_MODEL), lambda i: (mix(i), 0)),
                  tile_rows(n_pairs_a, 0), prev_block(2), tile_rows(n_kv, 2), prev_block(3), tile_rows(n_kv, 3),
                  tile_rows(n_pairs_b, 2), whole_seq(3), whole_seq(4),
                  _resident((SB_QB, SB_QB)),
                  _resident(w_mix.shape), _resident((1, D_MODEL)), _resident(w_q.shape), _resident((1, D_MODEL)),
                  pl.BlockSpec((4 * X_HEADS, MEM_LEN, LANES), lambda i: (0, mix(i) // tiles_per_seq, 0)),
                  _resident(w_o.shape)],
        out_specs=pl.BlockSpec((EVEN_TILE, D_MODEL), lambda i: (mix(i), 0)),
        out_shape=jax.ShapeDtypeStruct((N_TOK, D_MODEL), F32),
        scratch_shapes=[pltpu.VMEM((2, n_pairs_a + n_pairs_b, EVEN_TILE, LANES), BF)],
        compiler_params=_params(("arbitrary",), VMEM_LIMIT),
        name="even_tail",
    )(slopes, sinks, x, p, p, p, p, p, p, p, p, _sb_matrix(), w_mix, g.reshape(1, D_MODEL), w_q,
      q_colscale.reshape(1, D_MODEL).astype(F32), kv, w_o)


def _alibi_log2(n_heads):
    return jnp.asarray(LOG2_E * 2.0 ** (-8.0 * np.arange(1, n_heads + 1) / n_heads), dtype=F32)


def _even_projection(x, norm_g, w_in, q_gain, k_gain, side):
    hd = HEAD_DIM
    a_q, a_kv, b_w = A_Q_HEADS * hd, A_KV_HEADS * hd, B_HEADS * hd
    scale = hd ** -0.5 * LOG2_E
    ones = lambda n: jnp.ones((n,), F32)
    cs = jnp.concatenate([jnp.tile(q_gain, A_Q_HEADS) * scale, jnp.tile(k_gain, A_KV_HEADS), ones(a_kv),
                          ones(b_w) * scale, ones(2 * b_w)])
    plan = ([(True, False)] * (a_q // LANES) + [(True, True)] * (a_kv // LANES) + [(False, True)] * (a_kv // LANES)
            + [(False, False)] * (3 * b_w // LANES))
    return _proj(x, norm_g, w_in, cs, plan, hd, side=side)


def _odd_mixer_heads(x, norm_g, w_in, q_gain, k_gain, side):
    hd = HEAD_DIM
    cs = jnp.concatenate([jnp.tile(q_gain, C_HEADS) * (hd ** -0.5 * LOG2_E), jnp.tile(k_gain, C_HEADS),
                          jnp.ones((C_HEADS * hd,), F32)])
    head_blocks = C_HEADS * hd // LANES
    plan = [(True, False)] * (2 * head_blocks) + [(False, False)] * head_blocks
    p, side_out = _proj(x, norm_g, w_in, cs, plan, hd, side=side)
    return [_dilated(p, _alibi_log2(C_HEADS))], side_out


def _memory_kv_job(mem2d, mem_g, w_kv, k_gain):
    cs_kv = jnp.concatenate([jnp.tile(k_gain, X_HEADS), jnp.ones((D_MODEL,), F32)])
    plan = [(True, False)] * (D_MODEL // LANES) + [(False, False)] * (D_MODEL // LANES)
    return mem2d, mem_g, w_kv, cs_kv, plan, X_HEAD_DIM


def kernel(x, mem, ffn1_norm, ffn1_w_gu, ffn1_w_down, mix_norm, ev_w_in, ev_q_gain, ev_k_gain, ev_sinks, ev_w_out, od_w_in, od_q_gain, od_k_gain, od_w_out, xa_norm, xa_mem_norm, xa_w_q, xa_w_kv, xa_q_gain, xa_k_gain, xa_w_o, ffn2_norm, ffn2_w_gu, ffn2_w_down):
    x = x.reshape(N_TOK, D_MODEL)
    mem2d = mem.reshape(BATCH * MEM_LEN, D_MODEL)
    w_gu, w_down = _cast_now((ffn1_w_gu, 0)), _cast_now((ffn1_w_down, 0))
    for layer in range(DEPTH):
        j = layer // 2
        even = layer % 2 == 0
        w_in3, w_mix3 = (ev_w_in, ev_w_out) if even else (od_w_in, od_w_out)
        jobs = [(w_in3, j), (w_mix3, j), (xa_w_q, layer), (xa_w_kv, layer), (xa_w_o, layer),
                (ffn2_w_gu, layer), (ffn2_w_down, layer)]
        x, (w_in, w_mix, w_q, w_kv, w_o, w_gu, w_down) = _ffn(x, ffn1_norm[layer], w_gu, w_down, jobs)
        kv_job = _memory_kv_job(mem2d, xa_mem_norm[layer], w_kv, xa_k_gain[layer])
        cs_q = jnp.tile(xa_q_gain[layer], X_HEADS) * (X_HEAD_DIM ** -0.5 * LOG2_E)
        if even:
            p, kv = _even_projection(x, mix_norm[layer], w_in, ev_q_gain[j], ev_k_gain[j], kv_job)
            x = _even_tail(x, p, _alibi_log2(A_Q_HEADS), ev_sinks[j].astype(F32) * LOG2_E, w_mix,
                           xa_norm[layer], w_q, cs_q, kv, w_o)
        else:
            heads, kv = _odd_mixer_heads(x, mix_norm[layer], w_in, od_q_gain[j], od_k_gain[j], kv_job)
            x = _mix_xattn(x, heads, w_mix, xa_norm[layer], w_q, cs_q, kv, w_o)
        jobs = [(ffn1_w_gu, layer + 1), (ffn1_w_down, layer + 1)] if layer + 1 < DEPTH else []
        x, next_ffn1 = _ffn(x, ffn2_norm[layer], w_gu, w_down, jobs)
        if next_ffn1:
            w_gu, w_down = next_ffn1
    return x.reshape(BATCH, SEQ, D_MODEL)
```

```python
import functools

import numpy as np
import jax
import jax.numpy as jnp
from jax import lax
from jax.experimental import pallas as pl
from jax.experimental.pallas import tpu as pltpu

D_MODEL = 1024
BATCH = 4
SEQ = 4096
N_TOK = BATCH * SEQ
DEPTH = 2
HEAD_DIM = 64
BLOCK = 128
A_Q_HEADS = 8
A_KV_HEADS = 2
A_WINDOW = 128
B_HEADS = 8
C_HEADS = 16
C_PATTERNS = ((128, 1), (512, 4), (2048, 16))
MEM_LEN = 256
X_HEADS = 4
X_HEAD_DIM = D_MODEL // X_HEADS
D_FF = 2816
RMS_EPS = 1e-6

LANES = 128
MXU_N = 256
VMEM_LIMIT = 56 * 1024 * 1024

BF = jnp.bfloat16
F32 = jnp.float32
NT_DIMS = (((1,), (1,)), ((), ()))
LOG2_E = 1.4426950408889634


def _params(sem, vmem=None):
    return pltpu.CompilerParams(dimension_semantics=sem, vmem_limit_bytes=vmem)


def _resident(shape):
    nd = len(shape)
    return pl.BlockSpec(shape, lambda *_: (0,) * nd, pipeline_mode=pl.Buffered(1))


BF16_SUBLANES = 16


def _cast_specs(job, steps):
    w3, layer = job
    _, r, c = w3.shape
    rb = next(rb for rb in range(BF16_SUBLANES, r + 1, BF16_SUBLANES) if r % rb == 0 and r // rb <= steps)
    last = r // rb - 1
    return (pl.BlockSpec((None, rb, c), lambda i: (layer, jnp.minimum(i, last), 0)),
            pl.BlockSpec((rb, c), lambda i: (jnp.minimum(i, last), 0)),
            jax.ShapeDtypeStruct((r, c), BF))


def _run_cast_jobs(in_refs, out_refs):
    for src, dst in zip(in_refs, out_refs):
        dst[...] = src[...].astype(BF)


def _cast_kernel(*refs):
    _run_cast_jobs(refs[:len(refs) // 2], refs[len(refs) // 2:])


def _cast_now(jobs, *, steps=11):
    specs = [_cast_specs(job, steps) for job in jobs]
    return pl.pallas_call(
        _cast_kernel, grid=(steps,), in_specs=[s[0] for s in specs], out_specs=[s[1] for s in specs],
        out_shape=[s[2] for s in specs],
        compiler_params=_params(("arbitrary",), VMEM_LIMIT),
        name="cast",
    )(*[job[0] for job in jobs])


def _rms(xv, g):
    ms = jnp.mean(xv * xv, axis=-1, keepdims=True)
    return xv * lax.rsqrt(ms + RMS_EPS) * g


FFN_SPLIT = (D_FF // MXU_N + 1) // 2 * MXU_N
FFN_CHUNKS = ((0, FFN_SPLIT), (FFN_SPLIT, D_FF))


def _ffn_kernel(*refs, n_jobs):
    x_ref, g_ref, wgu_ref, wd_ref = refs[:4]
    o_ref = refs[4 + n_jobs]
    xv = x_ref[...]
    h = _rms(xv, g_ref[...]).astype(BF)
    acc = jnp.zeros_like(xv)
    for c0, c1 in FFN_CHUNKS:
        gate = jnp.dot(h, wgu_ref[:, c0:c1], preferred_element_type=F32)
        up = jnp.dot(h, wgu_ref[:, D_FF + c0:D_FF + c1], preferred_element_type=F32)
        act = (gate * jax.nn.sigmoid(gate) * up).astype(BF)
        acc = acc + jnp.dot(act, wd_ref[c0:c1, :], preferred_element_type=F32)
    o_ref[...] = xv + 0.5 * acc
    _run_cast_jobs(refs[4:4 + n_jobs], refs[5 + n_jobs:])


def _ffn(x, g, w_gu, w_down, cast_jobs=(), *, tm=512):
    steps = N_TOK // tm
    specs = [_cast_specs(job, steps) for job in cast_jobs]
    out = pl.pallas_call(
        functools.partial(_ffn_kernel, n_jobs=len(cast_jobs)),
        grid=(steps,),
        in_specs=[pl.BlockSpec((tm, D_MODEL), lambda i: (i, 0)),
                  _resident((1, D_MODEL)),
                  _resident(w_gu.shape),
                  _resident(w_down.shape)] + [s[0] for s in specs],
        out_specs=[pl.BlockSpec((tm, D_MODEL), lambda i: (i, 0))] + [s[1] for s in specs],
        out_shape=[jax.ShapeDtypeStruct((N_TOK, D_MODEL), F32)] + [s[2] for s in specs],
        compiler_params=_params(("arbitrary",), VMEM_LIMIT),
        name="ffn",
    )(x, g.reshape(1, D_MODEL), w_gu, w_down, *[job[0] for job in cast_jobs])
    return out[0], out[1:]


def _proj_kernel(x_ref, g_ref, w_ref, cs_ref, *rest, plan, gs, side):
    if side is None:
        (o_ref,) = rest
    else:
        sx_ref, sg_ref, sw_ref, scs_ref, o_ref, so_ref = rest
        pl.when(pl.program_id(0) == 0)(
            functools.partial(_project, sx_ref, sg_ref, sw_ref, scs_ref, so_ref, plan=side[0], gs=side[1]))
    _project(x_ref, g_ref, w_ref, cs_ref, o_ref, plan=plan, gs=gs)


def _project(x_ref, g_ref, w_ref, cs_ref, o_ref, *, plan, gs):
    assert gs in (HEAD_DIM, MXU_N)
    h = _rms(x_ref[...], g_ref[...]).astype(BF)
    n_chunks = len(plan) // 2
    lo = lax.broadcasted_iota(jnp.int32, (x_ref.shape[0], LANES), 1) < HEAD_DIM

    def main(j):
        return jnp.dot(h, w_ref[:, MXU_N * j:MXU_N * (j + 1)], preferred_element_type=F32)

    acc_next = main(0)
    out = 0
    for j in range(n_chunks):
        cols = slice(MXU_N * j, MXU_N * (j + 1))
        acc = acc_next
        if j + 1 < n_chunks:
            acc_next = main(j + 1)
        y = acc * cs_ref[:, cols]
        halves = plan[2 * j:2 * j + 2]
        if gs == MXU_N and any(normed for normed, _ in halves):
            inv_chunk = lax.rsqrt(jnp.mean(acc * acc, axis=1, keepdims=True) + RMS_EPS)
        for half, (normed, dup) in enumerate(halves):
            lanes = slice(LANES * half, LANES * (half + 1))
            yh = y[:, lanes]
            if normed and gs == MXU_N:
                yh = yh * inv_chunk
            elif normed:
                sq = acc[:, lanes] * acc[:, lanes]
                s_lo = jnp.sum(jnp.where(lo, sq, 0.0), axis=1, keepdims=True)
                s_hi = jnp.sum(jnp.where(lo, 0.0, sq), axis=1, keepdims=True)
                yh = yh * lax.rsqrt(jnp.where(lo, s_lo, s_hi) * (1.0 / gs) + RMS_EPS)
            if dup:
                swapped = pltpu.roll(yh, HEAD_DIM, axis=1)
                o_ref[out] = jnp.where(lo, yh, swapped).astype(BF)
                o_ref[out + 1] = jnp.where(lo, swapped, yh).astype(BF)
                out += 2
            else:
                o_ref[out] = yh.astype(BF)
                out += 1


def _proj(x, g, w, colscale, plan, gs, *, side=None, tm=1024):
    rows = x.shape[0]
    wout = w.shape[1]
    assert wout == LANES * len(plan) and len(plan) % 2 == 0
    n_blocks = lambda pln: sum(2 if dup else 1 for _, dup in pln)
    args = [x, g.reshape(1, D_MODEL), w, colscale.reshape(1, wout).astype(F32)]
    in_specs = [pl.BlockSpec((tm, D_MODEL), lambda i: (i, 0)), _resident((1, D_MODEL)), _resident(w.shape),
                _resident((1, wout))]
    out_specs = [pl.BlockSpec((n_blocks(plan), tm, LANES), lambda i: (0, i, 0))]
    out_shape = [jax.ShapeDtypeStruct((n_blocks(plan), rows, LANES), BF)]
    if side is not None:
        x2, g2, w2, cs2, plan2, gs2 = side
        assert w2.shape[1] == LANES * len(plan2)
        args += [x2, g2.reshape(1, D_MODEL), w2, cs2.reshape(1, -1).astype(F32)]
        in_specs += [_resident(x2.shape), _resident((1, D_MODEL)), _resident(w2.shape), _resident((1, w2.shape[1]))]
        out_specs.append(pl.BlockSpec((n_blocks(plan2), x2.shape[0], LANES), lambda i: (0, 0, 0)))
        out_shape.append(jax.ShapeDtypeStruct((n_blocks(plan2), x2.shape[0], LANES), BF))
    out = pl.pallas_call(
        functools.partial(_proj_kernel, plan=tuple(plan), gs=gs,
                          side=None if side is None else (tuple(side[4]), side[5])),
        grid=(rows // tm,),
        in_specs=in_specs, out_specs=out_specs, out_shape=out_shape,
        compiler_params=_params(("arbitrary",), VMEM_LIMIT),
        name="proj",
    )(*args)
    return out[0] if side is None else tuple(out)


def _swa_block(q_blocks, kp, kc, vp, vc, has_prev, slopes_ref, sinks_ref, *, kv_div, max_dist):
    row = lax.broadcasted_iota(jnp.int32, (BLOCK, 2 * BLOCK), 0)
    col = lax.broadcasted_iota(jnp.int32, (BLOCK, 2 * BLOCK), 1)
    dist = row + BLOCK - col
    valid = (dist >= 0) & (dist <= max_dist)
    if has_prev is not True:
        valid = valid & ((col >= BLOCK) | has_prev)
    negmask = jnp.where(valid, 0.0, -jnp.inf)
    distf = dist.astype(F32)
    lo = lax.broadcasted_iota(jnp.int32, (BLOCK, LANES), 1) < HEAD_DIM

    n_groups = len(q_blocks) // kv_div
    heads_per_group = 2 * kv_div
    scores = []
    for g in range(n_groups):
        parts = []
        for p in range(g * kv_div, (g + 1) * kv_div):
            q2 = q_blocks[p].astype(F32)
            parts += [jnp.where(lo, q2, 0.0), jnp.where(lo, 0.0, q2)]
        q_stack = jnp.concatenate(parts, axis=0).astype(BF)
        scores.append(jnp.concatenate(
            [lax.dot_general(q_stack, kp[g], NT_DIMS, preferred_element_type=F32),
             lax.dot_general(q_stack, kc[g], NT_DIMS, preferred_element_type=F32)], axis=1))
    soft = []
    for g in range(n_groups):
        res = []
        for j in range(heads_per_group):
            h = g * heads_per_group + j
            s = scores[g][j * BLOCK:(j + 1) * BLOCK] - slopes_ref[h] * distf + negmask
            m = jnp.maximum(jnp.max(s, axis=1, keepdims=True), sinks_ref[h])
            pe = jnp.exp2(s - m)
            res.append((pe.astype(BF), jnp.sum(pe, axis=1, keepdims=True) + jnp.exp2(sinks_ref[h] - m)))
        soft.append(res)
    outs = []
    for g in range(n_groups):
        pb = jnp.concatenate([r[0] for r in soft[g]], axis=0)
        pv = (jnp.dot(pb[:, :BLOCK], vp[g], preferred_element_type=F32)
              + jnp.dot(pb[:, BLOCK:], vc[g], preferred_element_type=F32))
        for jp in range(kv_div):
            o0 = pv[(2 * jp) * BLOCK:(2 * jp + 1) * BLOCK] / soft[g][2 * jp][1]
            o1 = pv[(2 * jp + 1) * BLOCK:(2 * jp + 2) * BLOCK] / soft[g][2 * jp + 1][1]
            outs.append(jnp.where(lo, o0, o1))
    return outs


DIL_ORDER = tuple(sorted(C_PATTERNS, key=lambda wd: -wd[1]))
DIL_UNROLL = 16
DIL_AHEAD = 2
DIL_BASE = 4
DIL_Q = SEQ // DIL_BASE
DIL_CONVERT_ROWS = DIL_BASE * BLOCK


def _dilated_kernel(slopes_ref, q_ref, k_ref, v_ref, o_ref, qn_s, tq_s, tk_s, tv_s, q0_s, q1_s, k_s, v_s,
                    acc_r, m_r, l_r, acc_n, m_n, l_n):
    assert all(d == 1 or d % DIL_BASE == 0 for _, d in DIL_ORDER) and DIL_ORDER[-1][1] == 1
    p = pl.program_id(1)
    lo = lax.broadcasted_iota(jnp.int32, (BLOCK, LANES), 1) < HEAD_DIM

    def convert(c, carry):
        rows = pl.ds(pl.multiple_of(c * DIL_CONVERT_ROWS, DIL_CONVERT_ROWS), DIL_CONVERT_ROWS)
        q_nat = q_ref[0, rows, :].astype(F32)
        lo_c = lax.broadcasted_iota(jnp.int32, (DIL_CONVERT_ROWS, LANES), 1) < HEAD_DIM
        qn_s[0, rows, :] = jnp.where(lo_c, q_nat, 0.0).astype(BF)
        qn_s[1, rows, :] = jnp.where(lo_c, 0.0, q_nat).astype(BF)
        tq_s[...] = q_nat
        tk_s[...] = k_ref[0, rows, :].astype(F32)
        tv_s[...] = v_ref[0, rows, :].astype(F32)
        for rho in range(DIL_BASE):
            src = pl.ds(rho, BLOCK, stride=DIL_BASE)
            dst = pl.ds(pl.multiple_of(rho * DIL_Q + c * BLOCK, BLOCK), BLOCK)
            q = tq_s[src, :]
            q0_s[dst, :] = jnp.where(lo, q, 0.0)
            q1_s[dst, :] = jnp.where(lo, 0.0, q)
            k_s[dst, :] = tk_s[src, :]
            v_s[dst, :] = tv_s[src, :]
        return carry

    lax.fori_loop(0, SEQ // DIL_CONVERT_ROWS, convert, 0)

    row = lax.broadcasted_iota(jnp.int32, (BLOCK, 2 * BLOCK), 0)
    col = lax.broadcasted_iota(jnp.int32, (BLOCK, 2 * BLOCK), 1)
    dist = row + BLOCK - col
    distf = dist.astype(F32)
    no_prev = jnp.where(col < BLOCK, -jnp.inf, 0.0)

    def bcast2(a0, a1):
        return jnp.where(lo, jnp.broadcast_to(a0, (BLOCK, LANES)), jnp.broadcast_to(a1, (BLOCK, LANES)))

    for pi, (window, d) in enumerate(DIL_ORDER):
        first, last = pi == 0, pi == len(DIL_ORDER) - 1
        natural = d == 1
        nb = SEQ // d // BLOCK
        band = (dist >= 0) & (dist <= window // d)
        bias = [jnp.where(band, (-float(d) * slopes_ref[2 * p + hh]) * distf, -jnp.inf) for hh in range(2)]
        acc_s, m_s, l_s = (acc_n, m_n, l_n) if natural else (acc_r, m_r, l_r)
        to_natural = not natural and not last and DIL_ORDER[pi + 1][1] == 1
        assert (not to_natural or d == DIL_BASE) and (not natural or first or DIL_ORDER[pi - 1][1] == DIL_BASE)
        acc_o, m_o, l_o = (acc_n, m_n, l_n) if to_natural else (acc_s, m_s, l_s)

        def rows_of(r, n, n_blocks=1, d=d, natural=natural):
            size = n_blocks * BLOCK
            if natural:
                return pl.ds(pl.multiple_of(BLOCK * n, BLOCK), size)
            inner = d // DIL_BASE
            start = (r % DIL_BASE) * DIL_Q + r // DIL_BASE + inner * BLOCK * n
            return pl.ds(start, size, stride=inner) if inner > 1 else pl.ds(pl.multiple_of(start, BLOCK), size)

        def step(it, carry, nb=nb, bias=bias, first=first, last=last, natural=natural, rows_of=rows_of,
                 acc_s=acc_s, m_s=m_s, l_s=l_s, acc_o=acc_o, m_o=m_o, l_o=l_o, to_natural=to_natural):
            assert DIL_UNROLL % nb == 0 or nb % DIL_UNROLL == 0
            load_k = (lambda rr: k_ref[0, rr, :]) if natural else (lambda rr: k_s[rr, :].astype(BF))
            load_v = (lambda rr: v_ref[0, rr, :]) if natural else (lambda rr: v_s[rr, :].astype(BF))

            def scores(u):
                t = it * DIL_UNROLL + u
                r, n = t // nb, t % nb
                prev = (u % nb != 0) if nb <= DIL_UNROLL else (True if u else None)
                rows = rows_of(r, n)
                out_rows = pl.ds(DIL_BASE * BLOCK * n + r, BLOCK, stride=DIL_BASE) if to_natural else rows
                if prev is True:
                    kv_rows = [rows_of(r, n - 1, 2)]
                elif prev is None:
                    kv_rows = [rows_of(r, jnp.maximum(n - 1, 0)), rows]
                else:
                    kv_rows = [rows]
                if natural:
                    qh = jnp.concatenate([qn_s[0, rows, :], qn_s[1, rows, :]], axis=0)
                else:
                    qh = jnp.concatenate([q0_s[rows, :], q1_s[rows, :]], axis=0).astype(BF)
                s = jnp.concatenate([lax.dot_general(qh, load_k(rr), NT_DIMS, preferred_element_type=F32)
                                     for rr in kv_rows], axis=1)
                return n, prev, rows, out_rows, kv_rows, (s[:BLOCK], s[BLOCK:])

            def softmax_pv(blk):
                n, prev, rows, out_rows, kv_rows, s = blk
                ms, ls, pes = [], [], []
                for hh in range(2):
                    sh = s[hh] + (bias[hh][:, BLOCK:] if prev is False else bias[hh])
                    if prev is None:
                        sh = sh + jnp.where(n == 0, no_prev, 0.0)
                    m = jnp.max(sh, axis=1, keepdims=True)
                    pe = jnp.exp2(sh - m)
                    ms.append(m)
                    ls.append(jnp.sum(pe, axis=1, keepdims=True))
                    pes.append(pe.astype(BF))
                pb = jnp.concatenate(pes, axis=0)
                v = jnp.concatenate([load_v(rr) for rr in kv_rows], axis=0) if len(kv_rows) > 1 else load_v(kv_rows[0])
                pv = jnp.dot(pb, v, preferred_element_type=F32)
                return rows, out_rows, ms, ls, (pv[:BLOCK], pv[BLOCK:])

            def merge(rows, out_rows, ms, ls, pv):
                m2 = bcast2(ms[0], ms[1])
                l2 = bcast2(ls[0], ls[1])
                acc2 = jnp.where(lo, pv[0], pv[1])
                if not first:
                    m_old = m_s[rows, :]
                    m_new = jnp.maximum(m_old, m2)
                    a_old, a_new = jnp.exp2(m_old - m_new), jnp.exp2(m2 - m_new)
                    l2 = a_old * l_s[rows, :] + a_new * l2
                    acc2 = a_old * acc_s[rows, :] + a_new * acc2
                    m2 = m_new
                if last:
                    o_ref[0, rows, :] = (acc2 / l2).astype(BF)
                else:
                    m_o[out_rows, :] = m2
                    l_o[out_rows, :] = l2
                    acc_o[out_rows, :] = acc2

            pending = {u: scores(u) for u in range(DIL_AHEAD)}
            done = None
            for u in range(DIL_UNROLL):
                if u + DIL_AHEAD < DIL_UNROLL:
                    pending[u + DIL_AHEAD] = scores(u + DIL_AHEAD)
                cur = softmax_pv(pending.pop(u))
                if done is not None:
                    merge(*done)
                done = cur
            merge(*done)
            return carry

        lax.fori_loop(0, SEQ // BLOCK // DIL_UNROLL, step, 0)


def _dilated(qkv, slopes):
    n_pairs = C_HEADS // 2
    seq_f32 = pltpu.VMEM((SEQ, LANES), F32)
    chunk_f32 = pltpu.VMEM((DIL_CONVERT_ROWS, LANES), F32)
    return pl.pallas_call(
        _dilated_kernel,
        grid=(BATCH, n_pairs),
        in_specs=[pl.BlockSpec(memory_space=pltpu.SMEM),
                  pl.BlockSpec((1, SEQ, LANES), lambda b, p: (p, b, 0)),
                  pl.BlockSpec((1, SEQ, LANES), lambda b, p: (n_pairs + p, b, 0)),
                  pl.BlockSpec((1, SEQ, LANES), lambda b, p: (2 * n_pairs + p, b, 0))],
        out_specs=pl.BlockSpec((1, SEQ, LANES), lambda b, p: (p, b, 0)),
        out_shape=jax.ShapeDtypeStruct((n_pairs, N_TOK, LANES), BF),
        scratch_shapes=[pltpu.VMEM((2, SEQ, LANES), BF)] + [chunk_f32] * 3 + [seq_f32] * 10,
        compiler_params=_params(("parallel", "parallel"), VMEM_LIMIT),
        name="dilated",
    )(slopes, qkv, qkv, qkv)


SB_QB = MXU_N
SB_FIRST_TILES = 2
SB_DEAD_LOG2 = -150.0


def _sb_matrix():
    idx = np.arange(SB_QB)
    return jnp.asarray(-(idx[:, None] > idx[None, :]).astype(np.float32), dtype=BF)


def _sb_unit(q2, load_k, load_v, iq, uo):
    lo = lax.broadcasted_iota(jnp.int32, (SB_QB, LANES), 1) < HEAD_DIM
    q_stack = jnp.concatenate([jnp.where(lo, q2, 0.0), jnp.where(lo, 0.0, q2)], axis=0).astype(BF)
    rel1 = (lax.broadcasted_iota(jnp.int32, (SB_QB, SB_QB), 1)
            - lax.broadcasted_iota(jnp.int32, (SB_QB, SB_QB), 0))
    rel = jnp.concatenate([rel1, rel1], axis=0)

    def scores(first, n_tiles):
        return lax.dot_general(q_stack, load_k(first, n_tiles), NT_DIMS,
                               preferred_element_type=F32)

    def walk(first, n_tiles, carry, masked):
        c, o = carry
        z = scores(first, n_tiles)
        order = list(reversed(range(n_tiles)))
        ws, es, stricts, totals = {}, {}, {}, {}
        for t in order:
            zt = z[:, t * SB_QB:(t + 1) * SB_QB]
            sp = jnp.maximum(zt, 0.0) + jnp.log2(1.0 + jnp.exp2(-jnp.abs(zt)))
            es[t] = zt - sp
            if masked:
                stricts[t] = rel < (iq - first - t) * SB_QB
                sp = jnp.where(stricts[t], sp, 0.0)
            ws[t] = jnp.dot(sp.astype(BF), uo, preferred_element_type=F32)
            totals[t] = jnp.sum(sp, axis=1, keepdims=True)
        parts = {}
        for t in order:
            a = jnp.exp2(es[t] + jnp.concatenate([c] * (SB_QB // LANES), axis=1) + ws[t])
            if masked:
                a = jnp.where(stricts[t], a, 0.0)
            parts[t] = a.astype(BF)
            c = c - totals[t]
        pv = jnp.dot(jnp.concatenate([parts[t] for t in range(n_tiles)], axis=1), load_v(first, n_tiles),
                     preferred_element_type=F32)
        return c, o + jnp.where(lo, pv[:SB_QB], pv[SB_QB:])

    def alive(c):
        return jnp.max(c) > SB_DEAD_LOG2

    def body(state):
        g = state[0]
        c, o = walk(g, 1, state[2:], False)
        return g - 1, alive(c), c, o

    first = jnp.maximum(iq - 1, 0)
    zeros = (jnp.zeros((2 * SB_QB, LANES), F32), jnp.zeros((SB_QB, LANES), F32))
    c, o = walk(first, SB_FIRST_TILES, zeros, True)
    more = (first > 0) & alive(c)

    def finish():
        state = lax.while_loop(lambda st: (st[0] >= 0) & st[1], body, (first - 1, more, c, o))
        return state[3]

    return more, o, finish


def _mix_xattn_kernel(*refs):
    x_ref = refs[0]
    wm_ref, g_ref, wq_ref, cs_ref, kv_ref, wo_ref, o_ref = refs[-7:]
    mixed = jnp.concatenate([r[c] for r in refs[1:-7] for c in range(r.shape[0])], axis=1)
    xv = x_ref[...] + jnp.dot(mixed, wm_ref[...], preferred_element_type=F32)
    h = _rms(xv, g_ref[...]).astype(BF)
    heads = range(X_HEADS)
    cols = [slice(X_HEAD_DIM * hd, X_HEAD_DIM * (hd + 1)) for hd in heads]
    acc = [jnp.dot(h, wq_ref[:, cols[hd]], preferred_element_type=F32) for hd in heads]
    ms = [jnp.mean(acc[hd] * acc[hd], axis=1, keepdims=True) for hd in heads]
    q = [(acc[hd] * cs_ref[:, cols[hd]] * lax.rsqrt(ms[hd] + RMS_EPS)).astype(BF) for hd in heads]
    s = [lax.dot_general(q[hd], jnp.concatenate([kv_ref[2 * hd], kv_ref[2 * hd + 1]], axis=1), NT_DIMS,
                         preferred_element_type=F32) for hd in heads]
    pe, l = [], []
    for hd in heads:
        e = jnp.exp2(s[hd] - jnp.max(s[hd], axis=1, keepdims=True))
        l.append(jnp.sum(e, axis=1, keepdims=True))
        pe.append(e.astype(BF))
    v0 = 2 * X_HEADS
    pv = [jnp.dot(pe[hd], jnp.concatenate([kv_ref[v0 + 2 * hd], kv_ref[v0 + 2 * hd + 1]], axis=1),
                  preferred_element_type=F32) for hd in heads]
    o = jnp.concatenate([(pv[hd] / l[hd]).astype(BF) for hd in heads], axis=1)
    o_ref[...] = xv + jnp.dot(o, wo_ref[...], preferred_element_type=F32)


def _mix_xattn(x, mixer_heads, w_mix, g, w_q, q_colscale, kv, w_o, *, tm=1024):
    tiles_per_batch = SEQ // tm
    in_specs = [pl.BlockSpec((tm, D_MODEL), lambda i: (i, 0))]
    in_specs += [pl.BlockSpec((mh.shape[0], tm, LANES), lambda i: (0, i, 0)) for mh in mixer_heads]
    in_specs += [_resident(w_mix.shape),
                 _resident((1, D_MODEL)),
                 _resident(w_q.shape),
                 _resident((1, D_MODEL)),
                 pl.BlockSpec((4 * X_HEADS, MEM_LEN, LANES), lambda i: (0, i // tiles_per_batch, 0)),
                 _resident(w_o.shape)]
    return pl.pallas_call(
        _mix_xattn_kernel, grid=(N_TOK // tm,),
        in_specs=in_specs,
        out_specs=pl.BlockSpec((tm, D_MODEL), lambda i: (i, 0)),
        out_shape=jax.ShapeDtypeStruct((N_TOK, D_MODEL), F32),
        compiler_params=_params(("parallel",), VMEM_LIMIT),
        name="mix_xattn",
    )(x, *mixer_heads, w_mix, g.reshape(1, D_MODEL), w_q,
      q_colscale.reshape(1, D_MODEL).astype(F32), kv, w_o)


EVEN_TILE = 512


def _even_tail_kernel(slopes_ref, sinks_ref, x_ref, swq_ref, swkp_ref, swkc_ref, swvp_ref, swvc_ref,
                      sbq_ref, sbk_ref, sbv_ref, uo_ref, wm_ref, g_ref, wq_ref, cs_ref, kv_ref, wo_ref,
                      o_ref, heads_s, *, kv_div, max_dist):
    i = pl.program_id(0)
    tiles_per_seq = SEQ // EVEN_TILE
    t_in_seq = jnp.minimum(i, N_TOK // EVEN_TILE - 1) % tiles_per_seq
    wr, rd = i % 2, (i + 1) % 2
    n_pairs_a, n_pairs_b = swq_ref.shape[0], sbq_ref.shape[0]

    @pl.when(i == 0)
    def _():
        heads_s[1] = jnp.zeros(heads_s.shape[1:], BF)

    def mix_steps():
        heads = range(X_HEADS)
        cols = [slice(X_HEAD_DIM * hd, X_HEAD_DIM * (hd + 1)) for hd in heads]
        st = {"xv": [None] * X_HEADS, "q": [None] * X_HEADS, "pe": [None] * X_HEADS, "l": [None] * X_HEADS,
              "o": [None] * X_HEADS}

        def project(c):
            if c == 0:
                st["mixed"] = jnp.concatenate([heads_s[rd, k] for k in range(n_pairs_a + n_pairs_b)], axis=1)
            st["xv"][c] = x_ref[:, cols[c]] + jnp.dot(st["mixed"], wm_ref[:, cols[c]], preferred_element_type=F32)

        def q_proj(hd):
            if hd == 0:
                xv = jnp.concatenate(st["xv"], axis=1)
                st["h"] = _rms(xv, g_ref[...]).astype(BF)
            acc = jnp.dot(st["h"], wq_ref[:, cols[hd]], preferred_element_type=F32)
            ms = jnp.mean(acc * acc, axis=1, keepdims=True)
            st["q"][hd] = (acc * cs_ref[:, cols[hd]] * lax.rsqrt(ms + RMS_EPS)).astype(BF)

        def scores(hd):
            kh = jnp.concatenate([kv_ref[2 * hd], kv_ref[2 * hd + 1]], axis=1)
            s = lax.dot_general(st["q"][hd], kh, NT_DIMS, preferred_element_type=F32)
            e = jnp.exp2(s - jnp.max(s, axis=1, keepdims=True))
            st["l"][hd] = jnp.sum(e, axis=1, keepdims=True)
            st["pe"][hd] = e.astype(BF)

        def values(hd):
            v0 = 2 * X_HEADS
            vh = jnp.concatenate([kv_ref[v0 + 2 * hd], kv_ref[v0 + 2 * hd + 1]], axis=1)
            st["o"][hd] = (jnp.dot(st["pe"][hd], vh, preferred_element_type=F32) / st["l"][hd]).astype(BF)

        def out_proj(c):
            if c == 0:
                st["oc"] = jnp.concatenate(st["o"], axis=1)
            o_ref[:, cols[c]] = st["xv"][c] + jnp.dot(st["oc"], wo_ref[:, cols[c]], preferred_element_type=F32)

        return [functools.partial(f, k) for f in (project, q_proj, scores, values, out_proj) for k in heads]

    def attend_and_mix():
        for step in mix_steps():
            step()

        uo = uo_ref[...]
        blocks_per_tile = EVEN_TILE // SB_QB

        def sb_first_pass(p, j):
            rows = slice(SB_QB * j, SB_QB * (j + 1))
            load = lambda ref: (lambda first, n: ref[p, pl.ds(pl.multiple_of(first * SB_QB, SB_QB), n * SB_QB), :])
            return _sb_unit(sbq_ref[p, rows, :].astype(F32), load(sbk_ref), load(sbv_ref),
                            blocks_per_tile * t_in_seq + j, uo)

        def store_sb(p, j, o):
            heads_s[wr, n_pairs_a + p, SB_QB * j:SB_QB * (j + 1), :] = o.astype(BF)

        units = [(p, j) for p in range(n_pairs_b) for j in range(blocks_per_tile)]
        passes = [sb_first_pass(p, j) for p, j in units]
        for (p, j), (_, o, _) in zip(units, passes):
            store_sb(p, j, o)

        n_kv = swkc_ref.shape[0]
        for m in range(EVEN_TILE // BLOCK):
            rows = slice(BLOCK * m, BLOCK * (m + 1))
            if m == 0:
                kp, vp, has_prev = [swkp_ref[g] for g in range(n_kv)], [swvp_ref[g] for g in range(n_kv)], t_in_seq > 0
            else:
                prows = slice(BLOCK * (m - 1), BLOCK * m)
                kp, vp, has_prev = ([swkc_ref[g, prows, :] for g in range(n_kv)],
                                    [swvc_ref[g, prows, :] for g in range(n_kv)], True)
            outs = _swa_block([swq_ref[p, rows, :] for p in range(n_pairs_a)], kp,
                              [swkc_ref[g, rows, :] for g in range(n_kv)], vp, [swvc_ref[g, rows, :] for g in range(n_kv)],
                              has_prev, slopes_ref, sinks_ref, kv_div=kv_div, max_dist=max_dist)
            for p in range(n_pairs_a):
                heads_s[wr, p, rows, :] = outs[p].astype(BF)

        @pl.when(functools.reduce(jnp.logical_or, [more for more, _, _ in passes]))
        def _():
            for (p, j), (_, _, finish) in zip(units, passes):
                store_sb(p, j, finish())

    n_tiles = N_TOK // EVEN_TILE
    pl.when(i < n_tiles)(attend_and_mix)

    @pl.when(i == n_tiles)
    def _():
        for step in mix_steps():
            step()


def _even_tail(x, p, slopes, sinks, w_mix, g, w_q, q_colscale, kv, w_o):
    n_pairs_a, n_kv, n_pairs_b = A_Q_HEADS // 2, A_KV_HEADS, B_HEADS // 2
    assert p.shape[0] == n_pairs_a + 2 * n_kv + 3 * n_pairs_b and n_pairs_a == n_pairs_b == 2 * n_kv
    n_tiles = N_TOK // EVEN_TILE
    tiles_per_seq = SEQ // EVEN_TILE
    blocks_per_tile = EVEN_TILE // BLOCK
    att = lambda i: jnp.minimum(i, n_tiles - 1)
    mix = lambda i: jnp.maximum(i - 1, 0)
    tile_rows = lambda size, idx: pl.BlockSpec((size, EVEN_TILE, LANES), lambda i: (idx, att(i), 0))
    prev_block = lambda idx: pl.BlockSpec(
        (n_kv, BLOCK, LANES), lambda i: (idx, jnp.maximum(blocks_per_tile * att(i) - 1, 0), 0))
    whole_seq = lambda idx: pl.BlockSpec((n_pairs_b, SEQ, LANES), lambda i: (idx, att(i) // tiles_per_seq, 0),
                                         pipeline_mode=pl.Buffered(1))
    smem = pl.BlockSpec(memory_space=pltpu.SMEM)
    return pl.pallas_call(
        functools.partial(_even_tail_kernel, kv_div=n_pairs_a // n_kv, max_dist=A_WINDOW - 1),
        grid=(n_tiles + 1,),
        in_specs=[smem, smem,
                  pl.BlockSpec((EVEN_TILE, D_MODEL), lambda i: (mix(i), 0)),
                  tile_rows(n_pairs_a, 0), prev_block(2), tile_rows(n_kv, 2), prev_block(3), tile_rows(n_kv, 3),
                  tile_rows(n_pairs_b, 2), whole_seq(3), whole_seq(4),
                  _resident((SB_QB, SB_QB)),
                  _resident(w_mix.shape), _resident((1, D_MODEL)), _resident(w_q.shape), _resident((1, D_MODEL)),
                  pl.BlockSpec((4 * X_HEADS, MEM_LEN, LANES), lambda i: (0, mix(i) // tiles_per_seq, 0)),
                  _resident(w_o.shape)],
        out_specs=pl.BlockSpec((EVEN_TILE, D_MODEL), lambda i: (mix(i), 0)),
        out_shape=jax.ShapeDtypeStruct((N_TOK, D_MODEL), F32),
        scratch_shapes=[pltpu.VMEM((2, n_pairs_a + n_pairs_b, EVEN_TILE, LANES), BF)],
        compiler_params=_params(("arbitrary",), VMEM_LIMIT),
        name="even_tail",
    )(slopes, sinks, x, p, p, p, p, p, p, p, p, _sb_matrix(), w_mix, g.reshape(1, D_MODEL), w_q,
      q_colscale.reshape(1, D_MODEL).astype(F32), kv, w_o)


def _alibi_log2(n_heads):
    return jnp.asarray(LOG2_E * 2.0 ** (-8.0 * np.arange(1, n_heads + 1) / n_heads), dtype=F32)


def _even_projection(x, norm_g, w_in, q_gain, k_gain, side):
    hd = HEAD_DIM
    a_q, a_kv, b_w = A_Q_HEADS * hd, A_KV_HEADS * hd, B_HEADS * hd
    scale = hd ** -0.5 * LOG2_E
    ones = lambda n: jnp.ones((n,), F32)
    cs = jnp.concatenate([jnp.tile(q_gain, A_Q_HEADS) * scale, jnp.tile(k_gain, A_KV_HEADS), ones(a_kv),
                          ones(b_w) * scale, ones(2 * b_w)])
    plan = ([(True, False)] * (a_q // LANES) + [(True, True)] * (a_kv // LANES) + [(False, True)] * (a_kv // LANES)
            + [(False, False)] * (3 * b_w // LANES))
    return _proj(x, norm_g, w_in, cs, plan, hd, side=side)


def _odd_mixer_heads(x, norm_g, w_in, q_gain, k_gain, side):
    hd = HEAD_DIM
    cs = jnp.concatenate([jnp.tile(q_gain, C_HEADS) * (hd ** -0.5 * LOG2_E), jnp.tile(k_gain, C_HEADS),
                          jnp.ones((C_HEADS * hd,), F32)])
    head_blocks = C_HEADS * hd // LANES
    plan = [(True, False)] * (2 * head_blocks) + [(False, False)] * head_blocks
    p, side_out = _proj(x, norm_g, w_in, cs, plan, hd, side=side)
    return [_dilated(p, _alibi_log2(C_HEADS))], side_out


def _memory_kv_job(mem2d, mem_g, w_kv, k_gain):
    cs_kv = jnp.concatenate([jnp.tile(k_gain, X_HEADS), jnp.ones((D_MODEL,), F32)])
    plan = [(True, False)] * (D_MODEL // LANES) + [(False, False)] * (D_MODEL // LANES)
    return mem2d, mem_g, w_kv, cs_kv, plan, X_HEAD_DIM


def kernel(x, mem, ffn1_norm, ffn1_w_gu, ffn1_w_down, mix_norm, ev_w_in, ev_q_gain, ev_k_gain, ev_sinks, ev_w_out, od_w_in, od_q_gain, od_k_gain, od_w_out, xa_norm, xa_mem_norm, xa_w_q, xa_w_kv, xa_q_gain, xa_k_gain, xa_w_o, ffn2_norm, ffn2_w_gu, ffn2_w_down):
    x = x.reshape(N_TOK, D_MODEL)
    mem2d = mem.reshape(BATCH * MEM_LEN, D_MODEL)
    w_gu, w_down = _cast_now([(ffn1_w_gu, 0), (ffn1_w_down, 0)])
    for layer in range(DEPTH):
        j = layer // 2
        even = layer % 2 == 0
        w_in3, w_mix3 = (ev_w_in, ev_w_out) if even else (od_w_in, od_w_out)
        jobs = [(w_in3, j), (w_mix3, j), (xa_w_q, layer), (xa_w_kv, layer), (xa_w_o, layer),
                (ffn2_w_gu, layer), (ffn2_w_down, layer)]
        x, (w_in, w_mix, w_q, w_kv, w_o, w_gu, w_down) = _ffn(x, ffn1_norm[layer], w_gu, w_down, jobs)
        kv_job = _memory_kv_job(mem2d, xa_mem_norm[layer], w_kv, xa_k_gain[layer])
        cs_q = jnp.tile(xa_q_gain[layer], X_HEADS) * (X_HEAD_DIM ** -0.5 * LOG2_E)
        if even:
            p, kv = _even_projection(x, mix_norm[layer], w_in, ev_q_gain[j], ev_k_gain[j], kv_job)
            x = _even_tail(x, p, _alibi_log2(A_Q_HEADS), ev_sinks[j].astype(F32) * LOG2_E, w_mix,
                           xa_norm[layer], w_q, cs_q, kv, w_o)
        else:
            heads, kv = _odd_mixer_heads(x, mix_norm[layer], w_in, od_q_gain[j], od_k_gain[j], kv_job)
            x = _mix_xattn(x, heads, w_mix, xa_norm[layer], w_q, cs_q, kv, w_o)
        jobs = [(ffn1_w_gu, layer + 1), (ffn1_w_down, layer + 1)] if layer + 1 < DEPTH else []
        x, next_ffn1 = _ffn(x, ffn2_norm[layer], w_gu, w_down, jobs)
        if next_ffn1:
            w_gu, w_down = next_ffn1
    return x.reshape(BATCH, SEQ, D_MODEL)
```

```python
import functools

import numpy as np
import jax
import jax.numpy as jnp
from jax import lax
from jax.experimental import pallas as pl
from jax.experimental.pallas import tpu as pltpu

D_MODEL = 1024
BATCH = 4
SEQ = 4096
N_TOK = BATCH * SEQ
DEPTH = 2
HEAD_DIM = 64
BLOCK = 128
A_Q_HEADS = 8
A_KV_HEADS = 2
A_WINDOW = 128
B_HEADS = 8
C_HEADS = 16
C_PATTERNS = ((128, 1), (512, 4), (2048, 16))
MEM_LEN = 256
X_HEADS = 4
X_HEAD_DIM = D_MODEL // X_HEADS
D_FF = 2816
RMS_EPS = 1e-6

LANES = 128
MXU_N = 256
VMEM_LIMIT = 56 * 1024 * 1024

BF = jnp.bfloat16
F32 = jnp.float32
NT_DIMS = (((1,), (1,)), ((), ()))
LOG2_E = 1.4426950408889634


def _params(sem, vmem=None):
    return pltpu.CompilerParams(dimension_semantics=sem, vmem_limit_bytes=vmem)


def _resident(shape):
    nd = len(shape)
    return pl.BlockSpec(shape, lambda *_: (0,) * nd, pipeline_mode=pl.Buffered(1))


BF16_SUBLANES = 16


def _cast_specs(job, steps):
    w3, layer = job
    _, r, c = w3.shape
    rb = next(rb for rb in range(BF16_SUBLANES, r + 1, BF16_SUBLANES) if r % rb == 0 and r // rb <= steps)
    last = r // rb - 1
    return (pl.BlockSpec((None, rb, c), lambda i: (layer, jnp.minimum(i, last), 0)),
            pl.BlockSpec((rb, c), lambda i: (jnp.minimum(i, last), 0)),
            jax.ShapeDtypeStruct((r, c), BF))


def _run_cast_jobs(in_refs, out_refs):
    for src, dst in zip(in_refs, out_refs):
        dst[...] = src[...].astype(BF)


def _cast_kernel(*refs):
    _run_cast_jobs(refs[:len(refs) // 2], refs[len(refs) // 2:])


def _cast_now(jobs, *, steps=11):
    specs = [_cast_specs(job, steps) for job in jobs]
    return pl.pallas_call(
        _cast_kernel, grid=(steps,), in_specs=[s[0] for s in specs], out_specs=[s[1] for s in specs],
        out_shape=[s[2] for s in specs],
        compiler_params=_params(("arbitrary",), VMEM_LIMIT),
        name="cast",
    )(*[job[0] for job in jobs])


def _rms(xv, g):
    ms = jnp.mean(xv * xv, axis=-1, keepdims=True)
    return xv * lax.rsqrt(ms + RMS_EPS) * g


FFN_SPLIT = (D_FF // MXU_N + 1) // 2 * MXU_N
FFN_CHUNKS = ((0, FFN_SPLIT), (FFN_SPLIT, D_FF))


def _ffn_kernel(*refs, n_jobs):
    x_ref, g_ref, wgu_ref, wd_ref = refs[:4]
    o_ref = refs[4 + n_jobs]
    xv = x_ref[...]
    h = _rms(xv, g_ref[...]).astype(BF)
    acc = jnp.zeros_like(xv)
    for c0, c1 in FFN_CHUNKS:
        gate = jnp.dot(h, wgu_ref[:, c0:c1], preferred_element_type=F32)
        up = jnp.dot(h, wgu_ref[:, D_FF + c0:D_FF + c1], preferred_element_type=F32)
        act = (gate * jax.nn.sigmoid(gate) * up).astype(BF)
        acc = acc + jnp.dot(act, wd_ref[c0:c1, :], preferred_element_type=F32)
    o_ref[...] = xv + 0.5 * acc
    _run_cast_jobs(refs[4:4 + n_jobs], refs[5 + n_jobs:])


def _ffn(x, g, w_gu, w_down, cast_jobs=(), *, tm=512):
    steps = N_TOK // tm
    specs = [_cast_specs(job, steps) for job in cast_jobs]
    out = pl.pallas_call(
        functools.partial(_ffn_kernel, n_jobs=len(cast_jobs)),
        grid=(steps,),
        in_specs=[pl.BlockSpec((tm, D_MODEL), lambda i: (i, 0)),
                  _resident((1, D_MODEL)),
                  _resident(w_gu.shape),
                  _resident(w_down.shape)] + [s[0] for s in specs],
        out_specs=[pl.BlockSpec((tm, D_MODEL), lambda i: (i, 0))] + [s[1] for s in specs],
        out_shape=[jax.ShapeDtypeStruct((N_TOK, D_MODEL), F32)] + [s[2] for s in specs],
        compiler_params=_params(("arbitrary",), VMEM_LIMIT),
        name="ffn",
    )(x, g.reshape(1, D_MODEL), w_gu, w_down, *[job[0] for job in cast_jobs])
    return out[0], out[1:]


def _proj_kernel(x_ref, g_ref, w_ref, cs_ref, *rest, plan, gs, side):
    if side is None:
        (o_ref,) = rest
    else:
        sx_ref, sg_ref, sw_ref, scs_ref, o_ref, so_ref = rest
        pl.when(pl.program_id(0) == 0)(
            functools.partial(_project, sx_ref, sg_ref, sw_ref, scs_ref, so_ref, plan=side[0], gs=side[1]))
    _project(x_ref, g_ref, w_ref, cs_ref, o_ref, plan=plan, gs=gs)


def _project(x_ref, g_ref, w_ref, cs_ref, o_ref, *, plan, gs):
    assert gs in (HEAD_DIM, MXU_N)
    h = _rms(x_ref[...], g_ref[...]).astype(BF)
    n_chunks = len(plan) // 2
    lo = lax.broadcasted_iota(jnp.int32, (x_ref.shape[0], LANES), 1) < HEAD_DIM

    def main(j):
        return jnp.dot(h, w_ref[:, MXU_N * j:MXU_N * (j + 1)], preferred_element_type=F32)

    acc_next = main(0)
    out = 0
    for j in range(n_chunks):
        cols = slice(MXU_N * j, MXU_N * (j + 1))
        acc = acc_next
        if j + 1 < n_chunks:
            acc_next = main(j + 1)
        y = acc * cs_ref[:, cols]
        halves = plan[2 * j:2 * j + 2]
        if gs == MXU_N and any(normed for normed, _ in halves):
            inv_chunk = lax.rsqrt(jnp.mean(acc * acc, axis=1, keepdims=True) + RMS_EPS)
        for half, (normed, dup) in enumerate(halves):
            lanes = slice(LANES * half, LANES * (half + 1))
            yh = y[:, lanes]
            if normed and gs == MXU_N:
                yh = yh * inv_chunk
            elif normed:
                sq = acc[:, lanes] * acc[:, lanes]
                s_lo = jnp.sum(jnp.where(lo, sq, 0.0), axis=1, keepdims=True)
                s_hi = jnp.sum(jnp.where(lo, 0.0, sq), axis=1, keepdims=True)
                yh = yh * lax.rsqrt(jnp.where(lo, s_lo, s_hi) * (1.0 / gs) + RMS_EPS)
            if dup:
                swapped = pltpu.roll(yh, HEAD_DIM, axis=1)
                o_ref[out] = jnp.where(lo, yh, swapped).astype(BF)
                o_ref[out + 1] = jnp.where(lo, swapped, yh).astype(BF)
                out += 2
            else:
                o_ref[out] = yh.astype(BF)
                out += 1


def _proj(x, g, w, colscale, plan, gs, *, side=None, tm=1024):
    rows = x.shape[0]
    wout = w.shape[1]
    assert wout == LANES * len(plan) and len(plan) % 2 == 0
    n_blocks = lambda pln: sum(2 if dup else 1 for _, dup in pln)
    args = [x, g.reshape(1, D_MODEL), w, colscale.reshape(1, wout).astype(F32)]
    in_specs = [pl.BlockSpec((tm, D_MODEL), lambda i: (i, 0)), _resident((1, D_MODEL)), _resident(w.shape),
                _resident((1, wout))]
    out_specs = [pl.BlockSpec((n_blocks(plan), tm, LANES), lambda i: (0, i, 0))]
    out_shape = [jax.ShapeDtypeStruct((n_blocks(plan), rows, LANES), BF)]
    if side is not None:
        x2, g2, w2, cs2, plan2, gs2 = side
        assert w2.shape[1] == LANES * len(plan2)
        args += [x2, g2.reshape(1, D_MODEL), w2, cs2.reshape(1, -1).astype(F32)]
        in_specs += [_resident(x2.shape), _resident((1, D_MODEL)), _resident(w2.shape), _resident((1, w2.shape[1]))]
        out_specs.append(pl.BlockSpec((n_blocks(plan2), x2.shape[0], LANES), lambda i: (0, 0, 0)))
        out_shape.append(jax.ShapeDtypeStruct((n_blocks(plan2), x2.shape[0], LANES), BF))
    out = pl.pallas_call(
        functools.partial(_proj_kernel, plan=tuple(plan), gs=gs,
                          side=None if side is None else (tuple(side[4]), side[5])),
        grid=(rows // tm,),
        in_specs=in_specs, out_specs=out_specs, out_shape=out_shape,
        compiler_params=_params(("arbitrary",), VMEM_LIMIT),
        name="proj",
    )(*args)
    return out[0] if side is None else tuple(out)


def _swa_block(q_blocks, kp, kc, vp, vc, has_prev, slopes_ref, sinks_ref, *, kv_div, max_dist):
    row = lax.broadcasted_iota(jnp.int32, (BLOCK, 2 * BLOCK), 0)
    col = lax.broadcasted_iota(jnp.int32, (BLOCK, 2 * BLOCK), 1)
    dist = row + BLOCK - col
    valid = (dist >= 0) & (dist <= max_dist)
    if has_prev is not True:
        valid = valid & ((col >= BLOCK) | has_prev)
    negmask = jnp.where(valid, 0.0, -jnp.inf)
    distf = dist.astype(F32)
    lo = lax.broadcasted_iota(jnp.int32, (BLOCK, LANES), 1) < HEAD_DIM

    n_groups = len(q_blocks) // kv_div
    heads_per_group = 2 * kv_div
    scores = []
    for g in range(n_groups):
        parts = []
        for p in range(g * kv_div, (g + 1) * kv_div):
            q2 = q_blocks[p].astype(F32)
            parts += [jnp.where(lo, q2, 0.0), jnp.where(lo, 0.0, q2)]
        q_stack = jnp.concatenate(parts, axis=0).astype(BF)
        scores.append(jnp.concatenate(
            [lax.dot_general(q_stack, kp[g], NT_DIMS, preferred_element_type=F32),
             lax.dot_general(q_stack, kc[g], NT_DIMS, preferred_element_type=F32)], axis=1))
    soft = []
    for g in range(n_groups):
        res = []
        for j in range(heads_per_group):
            h = g * heads_per_group + j
            s = scores[g][j * BLOCK:(j + 1) * BLOCK] - slopes_ref[h] * distf + negmask
            m = jnp.maximum(jnp.max(s, axis=1, keepdims=True), sinks_ref[h])
            pe = jnp.exp2(s - m)
            res.append((pe.astype(BF), jnp.sum(pe, axis=1, keepdims=True) + jnp.exp2(sinks_ref[h] - m)))
        soft.append(res)
    outs = []
    for g in range(n_groups):
        pb = jnp.concatenate([r[0] for r in soft[g]], axis=0)
        pv = (jnp.dot(pb[:, :BLOCK], vp[g], preferred_element_type=F32)
              + jnp.dot(pb[:, BLOCK:], vc[g], preferred_element_type=F32))
        for jp in range(kv_div):
            o0 = pv[(2 * jp) * BLOCK:(2 * jp + 1) * BLOCK] / soft[g][2 * jp][1]
            o1 = pv[(2 * jp + 1) * BLOCK:(2 * jp + 2) * BLOCK] / soft[g][2 * jp + 1][1]
            outs.append(jnp.where(lo, o0, o1))
    return outs


DIL_ORDER = tuple(sorted(C_PATTERNS, key=lambda wd: -wd[1]))
DIL_UNROLL = 16
DIL_AHEAD = 2
DIL_BASE = 4
DIL_Q = SEQ // DIL_BASE
DIL_CONVERT_ROWS = DIL_BASE * BLOCK


def _dilated_kernel(slopes_ref, q_ref, k_ref, v_ref, o_ref, qn_s, tq_s, tk_s, tv_s, q0_s, q1_s, k_s, v_s,
                    acc_r, m_r, l_r, acc_n, m_n, l_n):
    assert all(d == 1 or d % DIL_BASE == 0 for _, d in DIL_ORDER) and DIL_ORDER[-1][1] == 1
    p = pl.program_id(1)
    lo = lax.broadcasted_iota(jnp.int32, (BLOCK, LANES), 1) < HEAD_DIM

    def convert(c, carry):
        rows = pl.ds(pl.multiple_of(c * DIL_CONVERT_ROWS, DIL_CONVERT_ROWS), DIL_CONVERT_ROWS)
        q_nat = q_ref[0, rows, :].astype(F32)
        lo_c = lax.broadcasted_iota(jnp.int32, (DIL_CONVERT_ROWS, LANES), 1) < HEAD_DIM
        qn_s[0, rows, :] = jnp.where(lo_c, q_nat, 0.0).astype(BF)
        qn_s[1, rows, :] = jnp.where(lo_c, 0.0, q_nat).astype(BF)
        tq_s[...] = q_nat
        tk_s[...] = k_ref[0, rows, :].astype(F32)
        tv_s[...] = v_ref[0, rows, :].astype(F32)
        for rho in range(DIL_BASE):
            src = pl.ds(rho, BLOCK, stride=DIL_BASE)
            dst = pl.ds(pl.multiple_of(rho * DIL_Q + c * BLOCK, BLOCK), BLOCK)
            q = tq_s[src, :]
            q0_s[dst, :] = jnp.where(lo, q, 0.0)
            q1_s[dst, :] = jnp.where(lo, 0.0, q)
            k_s[dst, :] = tk_s[src, :]
            v_s[dst, :] = tv_s[src, :]
        return carry

    lax.fori_loop(0, SEQ // DIL_CONVERT_ROWS, convert, 0)

    row = lax.broadcasted_iota(jnp.int32, (BLOCK, 2 * BLOCK), 0)
    col = lax.broadcasted_iota(jnp.int32, (BLOCK, 2 * BLOCK), 1)
    dist = row + BLOCK - col
    distf = dist.astype(F32)
    no_prev = jnp.where(col < BLOCK, -jnp.inf, 0.0)

    def bcast2(a0, a1):
        return jnp.where(lo, jnp.broadcast_to(a0, (BLOCK, LANES)), jnp.broadcast_to(a1, (BLOCK, LANES)))

    for pi, (window, d) in enumerate(DIL_ORDER):
        first, last = pi == 0, pi == len(DIL_ORDER) - 1
        natural = d == 1
        nb = SEQ // d // BLOCK
        band = (dist >= 0) & (dist <= window // d)
        bias = [jnp.where(band, (-float(d) * slopes_ref[2 * p + hh]) * distf, -jnp.inf) for hh in range(2)]
        acc_s, m_s, l_s = (acc_n, m_n, l_n) if natural else (acc_r, m_r, l_r)
        to_natural = not natural and not last and DIL_ORDER[pi + 1][1] == 1
        assert (not to_natural or d == DIL_BASE) and (not natural or first or DIL_ORDER[pi - 1][1] == DIL_BASE)
        acc_o, m_o, l_o = (acc_n, m_n, l_n) if to_natural else (acc_s, m_s, l_s)

        def rows_of(r, n, n_blocks=1, d=d, natural=natural):
            size = n_blocks * BLOCK
            if natural:
                return pl.ds(pl.multiple_of(BLOCK * n, BLOCK), size)
            inner = d // DIL_BASE
            start = (r % DIL_BASE) * DIL_Q + r // DIL_BASE + inner * BLOCK * n
            return pl.ds(start, size, stride=inner) if inner > 1 else pl.ds(pl.multiple_of(start, BLOCK), size)

        def step(it, carry, nb=nb, bias=bias, first=first, last=last, natural=natural, rows_of=rows_of,
                 acc_s=acc_s, m_s=m_s, l_s=l_s, acc_o=acc_o, m_o=m_o, l_o=l_o, to_natural=to_natural):
            assert DIL_UNROLL % nb == 0 or nb % DIL_UNROLL == 0
            load_k = (lambda rr: k_ref[0, rr, :]) if natural else (lambda rr: k_s[rr, :].astype(BF))
            load_v = (lambda rr: v_ref[0, rr, :]) if natural else (lambda rr: v_s[rr, :].astype(BF))

            def scores(u):
                t = it * DIL_UNROLL + u
                r, n = t // nb, t % nb
                prev = (u % nb != 0) if nb <= DIL_UNROLL else (True if u else None)
                rows = rows_of(r, n)
                out_rows = pl.ds(DIL_BASE * BLOCK * n + r, BLOCK, stride=DIL_BASE) if to_natural else rows
                if prev is True:
                    kv_rows = [rows_of(r, n - 1, 2)]
                elif prev is None:
                    kv_rows = [rows_of(r, jnp.maximum(n - 1, 0)), rows]
                else:
                    kv_rows = [rows]
                if natural:
                    qh = jnp.concatenate([qn_s[0, rows, :], qn_s[1, rows, :]], axis=0)
                else:
                    qh = jnp.concatenate([q0_s[rows, :], q1_s[rows, :]], axis=0).astype(BF)
                s = jnp.concatenate([lax.dot_general(qh, load_k(rr), NT_DIMS, preferred_element_type=F32)
                                     for rr in kv_rows], axis=1)
                return n, prev, rows, out_rows, kv_rows, (s[:BLOCK], s[BLOCK:])

            def softmax_pv(blk):
                n, prev, rows, out_rows, kv_rows, s = blk
                ms, ls, pes = [], [], []
                for hh in range(2):
                    sh = s[hh] + (bias[hh][:, BLOCK:] if prev is False else bias[hh])
                    if prev is None:
                        sh = sh + jnp.where(n == 0, no_prev, 0.0)
                    m = jnp.max(sh, axis=1, keepdims=True)
                    pe = jnp.exp2(sh - m)
                    ms.append(m)
                    ls.append(jnp.sum(pe, axis=1, keepdims=True))
                    pes.append(pe.astype(BF))
                pb = jnp.concatenate(pes, axis=0)
                v = jnp.concatenate([load_v(rr) for rr in kv_rows], axis=0) if len(kv_rows) > 1 else load_v(kv_rows[0])
                pv = jnp.dot(pb, v, preferred_element_type=F32)
                return rows, out_rows, ms, ls, (pv[:BLOCK], pv[BLOCK:])

            def merge(rows, out_rows, ms, ls, pv):
                m2 = bcast2(ms[0], ms[1])
                l2 = bcast2(ls[0], ls[1])
                acc2 = jnp.where(lo, pv[0], pv[1])
                if not first:
                    m_old = m_s[rows, :]
                    m_new = jnp.maximum(m_old, m2)
                    a_old, a_new = jnp.exp2(m_old - m_new), jnp.exp2(m2 - m_new)
                    l2 = a_old * l_s[rows, :] + a_new * l2
                    acc2 = a_old * acc_s[rows, :] + a_new * acc2
                    m2 = m_new
                if last:
                    o_ref[0, rows, :] = (acc2 / l2).astype(BF)
                else:
                    m_o[out_rows, :] = m2
                    l_o[out_rows, :] = l2
                    acc_o[out_rows, :] = acc2

            pending = {u: scores(u) for u in range(DIL_AHEAD)}
            done = None
            for u in range(DIL_UNROLL):
                if u + DIL_AHEAD < DIL_UNROLL:
                    pending[u + DIL_AHEAD] = scores(u + DIL_AHEAD)
                cur = softmax_pv(pending.pop(u))
                if done is not None:
                    merge(*done)
                done = cur
            merge(*done)
            return carry

        lax.fori_loop(0, SEQ // BLOCK // DIL_UNROLL, step, 0)


def _dilated(qkv, slopes):
    n_pairs = C_HEADS // 2
    seq_f32 = pltpu.VMEM((SEQ, LANES), F32)
    chunk_f32 = pltpu.VMEM((DIL_CONVERT_ROWS, LANES), F32)
    return pl.pallas_call(
        _dilated_kernel,
        grid=(BATCH, n_pairs),
        in_specs=[pl.BlockSpec(memory_space=pltpu.SMEM),
                  pl.BlockSpec((1, SEQ, LANES), lambda b, p: (p, b, 0)),
                  pl.BlockSpec((1, SEQ, LANES), lambda b, p: (n_pairs + p, b, 0)),
                  pl.BlockSpec((1, SEQ, LANES), lambda b, p: (2 * n_pairs + p, b, 0))],
        out_specs=pl.BlockSpec((1, SEQ, LANES), lambda b, p: (p, b, 0)),
        out_shape=jax.ShapeDtypeStruct((n_pairs, N_TOK, LANES), BF),
        scratch_shapes=[pltpu.VMEM((2, SEQ, LANES), BF)] + [chunk_f32] * 3 + [seq_f32] * 10,
        compiler_params=_params(("parallel", "parallel"), VMEM_LIMIT),
        name="dilated",
    )(slopes, qkv, qkv, qkv)


SB_QB = MXU_N
SB_FIRST_TILES = 2
SB_DEAD_LOG2 = -150.0


def _sb_matrix():
    idx = np.arange(SB_QB)
    return jnp.asarray(-(idx[:, None] > idx[None, :]).astype(np.float32), dtype=BF)


def _sb_unit(q2, load_k, load_v, iq, uo):
    lo = lax.broadcasted_iota(jnp.int32, (SB_QB, LANES), 1) < HEAD_DIM
    q_stack = jnp.concatenate([jnp.where(lo, q2, 0.0), jnp.where(lo, 0.0, q2)], axis=0).astype(BF)
    def walk(tiles, carry, weights=None):
        c, o = carry
        k = jnp.concatenate([load_k(t, 1) for t in tiles], axis=0) if len(tiles) > 1 else load_k(tiles[0], 1)
        v = jnp.concatenate([load_v(t, 1) for t in tiles], axis=0) if len(tiles) > 1 else load_v(tiles[0], 1)
        z = lax.dot_general(q_stack, k, NT_DIMS, preferred_element_type=F32)
        order = list(reversed(range(len(tiles))))
        ws, es, totals = {}, {}, {}
        for t in order:
            zt = z[:, t * SB_QB:(t + 1) * SB_QB]
            sp = jnp.maximum(zt, 0.0) + jnp.log2(1.0 + jnp.exp2(-jnp.abs(zt)))
            es[t] = zt - sp
            if weights is not None:
                sp = sp * weights[t]
            ws[t] = jnp.dot(sp.astype(BF), uo, preferred_element_type=F32)
            totals[t] = jnp.sum(sp, axis=1, keepdims=True)
        parts = {}
        for t in order:
            a = jnp.exp2(es[t] + jnp.concatenate([c] * (SB_QB // LANES), axis=1) + ws[t])
            if weights is not None:
                a = a * weights[t]
            parts[t] = a.astype(BF)
            c = c - totals[t]
        pv = jnp.dot(jnp.concatenate([parts[t] for t in range(len(tiles))], axis=1), v,
                     preferred_element_type=F32)
        return c, o + jnp.where(lo, pv[:SB_QB], pv[SB_QB:])

    def alive(c):
        return jnp.max(c) > SB_DEAD_LOG2

    def body(state):
        g = state[0]
        c, o = walk([g], state[2:])
        return g - 1, alive(c), c, o

    assert SB_FIRST_TILES == 2
    first = jnp.maximum(iq - 1, 0)
    causal = -uo.astype(F32)
    weights = [jnp.where(iq > 0, 1.0, 0.0).astype(F32), jnp.concatenate([causal, causal], axis=0)]
    zeros = (jnp.zeros((2 * SB_QB, LANES), F32), jnp.zeros((SB_QB, LANES), F32))
    c, o = walk([first, iq], zeros, weights)
    more = (first > 0) & alive(c)

    def finish():
        state = lax.while_loop(lambda st: (st[0] >= 0) & st[1], body, (first - 1, more, c, o))
        return state[3]

    return more, o, finish


def _mix_xattn_kernel(*refs):
    x_ref = refs[0]
    wm_ref, g_ref, wq_ref, cs_ref, kv_ref, wo_ref, o_ref = refs[-7:]
    mixed = jnp.concatenate([r[c] for r in refs[1:-7] for c in range(r.shape[0])], axis=1)
    xv = x_ref[...] + jnp.dot(mixed, wm_ref[...], preferred_element_type=F32)
    h = _rms(xv, g_ref[...]).astype(BF)
    heads = range(X_HEADS)
    cols = [slice(X_HEAD_DIM * hd, X_HEAD_DIM * (hd + 1)) for hd in heads]
    acc = [jnp.dot(h, wq_ref[:, cols[hd]], preferred_element_type=F32) for hd in heads]
    ms = [jnp.mean(acc[hd] * acc[hd], axis=1, keepdims=True) for hd in heads]
    q = [(acc[hd] * cs_ref[:, cols[hd]] * lax.rsqrt(ms[hd] + RMS_EPS)).astype(BF) for hd in heads]
    s = [lax.dot_general(q[hd], jnp.concatenate([kv_ref[2 * hd], kv_ref[2 * hd + 1]], axis=1), NT_DIMS,
                         preferred_element_type=F32) for hd in heads]
    pe, l = [], []
    for hd in heads:
        e = jnp.exp2(s[hd] - jnp.max(s[hd], axis=1, keepdims=True))
        l.append(jnp.sum(e, axis=1, keepdims=True))
        pe.append(e.astype(BF))
    v0 = 2 * X_HEADS
    pv = [jnp.dot(pe[hd], jnp.concatenate([kv_ref[v0 + 2 * hd], kv_ref[v0 + 2 * hd + 1]], axis=1),
                  preferred_element_type=F32) for hd in heads]
    o = jnp.concatenate([(pv[hd] / l[hd]).astype(BF) for hd in heads], axis=1)
    o_ref[...] = xv + jnp.dot(o, wo_ref[...], preferred_element_type=F32)


def _mix_xattn(x, mixer_heads, w_mix, g, w_q, q_colscale, kv, w_o, *, tm=1024):
    tiles_per_batch = SEQ // tm
    in_specs = [pl.BlockSpec((tm, D_MODEL), lambda i: (i, 0))]
    in_specs += [pl.BlockSpec((mh.shape[0], tm, LANES), lambda i: (0, i, 0)) for mh in mixer_heads]
    in_specs += [_resident(w_mix.shape),
                 _resident((1, D_MODEL)),
                 _resident(w_q.shape),
                 _resident((1, D_MODEL)),
                 pl.BlockSpec((4 * X_HEADS, MEM_LEN, LANES), lambda i: (0, i // tiles_per_batch, 0)),
                 _resident(w_o.shape)]
    return pl.pallas_call(
        _mix_xattn_kernel, grid=(N_TOK // tm,),
        in_specs=in_specs,
        out_specs=pl.BlockSpec((tm, D_MODEL), lambda i: (i, 0)),
        out_shape=jax.ShapeDtypeStruct((N_TOK, D_MODEL), F32),
        compiler_params=_params(("parallel",), VMEM_LIMIT),
        name="mix_xattn",
    )(x, *mixer_heads, w_mix, g.reshape(1, D_MODEL), w_q,
      q_colscale.reshape(1, D_MODEL).astype(F32), kv, w_o)


EVEN_TILE = 512


def _even_tail_kernel(slopes_ref, sinks_ref, x_ref, swq_ref, swkp_ref, swkc_ref, swvp_ref, swvc_ref,
                      sbq_ref, sbk_ref, sbv_ref, uo_ref, wm_ref, g_ref, wq_ref, cs_ref, kv_ref, wo_ref,
                      o_ref, heads_s, *, kv_div, max_dist):
    i = pl.program_id(0)
    tiles_per_seq = SEQ // EVEN_TILE
    t_in_seq = jnp.minimum(i, N_TOK // EVEN_TILE - 1) % tiles_per_seq
    wr, rd = i % 2, (i + 1) % 2
    n_pairs_a, n_pairs_b = swq_ref.shape[0], sbq_ref.shape[0]

    @pl.when(i == 0)
    def _():
        heads_s[1] = jnp.zeros(heads_s.shape[1:], BF)

    def mix_steps():
        heads = range(X_HEADS)
        cols = [slice(X_HEAD_DIM * hd, X_HEAD_DIM * (hd + 1)) for hd in heads]
        st = {"xv": [None] * X_HEADS, "q": [None] * X_HEADS, "pe": [None] * X_HEADS, "l": [None] * X_HEADS,
              "o": [None] * X_HEADS}

        def project(c):
            if c == 0:
                st["mixed"] = jnp.concatenate([heads_s[rd, k] for k in range(n_pairs_a + n_pairs_b)], axis=1)
            st["xv"][c] = x_ref[:, cols[c]] + jnp.dot(st["mixed"], wm_ref[:, cols[c]], preferred_element_type=F32)

        def q_proj(hd):
            if hd == 0:
                xv = jnp.concatenate(st["xv"], axis=1)
                st["h"] = _rms(xv, g_ref[...]).astype(BF)
            acc = jnp.dot(st["h"], wq_ref[:, cols[hd]], preferred_element_type=F32)
            ms = jnp.mean(acc * acc, axis=1, keepdims=True)
            st["q"][hd] = (acc * cs_ref[:, cols[hd]] * lax.rsqrt(ms + RMS_EPS)).astype(BF)

        def scores(hd):
            kh = jnp.concatenate([kv_ref[2 * hd], kv_ref[2 * hd + 1]], axis=1)
            s = lax.dot_general(st["q"][hd], kh, NT_DIMS, preferred_element_type=F32)
            e = jnp.exp2(s - jnp.max(s, axis=1, keepdims=True))
            st["l"][hd] = jnp.sum(e, axis=1, keepdims=True)
            st["pe"][hd] = e.astype(BF)

        def values(hd):
            v0 = 2 * X_HEADS
            vh = jnp.concatenate([kv_ref[v0 + 2 * hd], kv_ref[v0 + 2 * hd + 1]], axis=1)
            st["o"][hd] = (jnp.dot(st["pe"][hd], vh, preferred_element_type=F32) / st["l"][hd]).astype(BF)

        def out_proj(c):
            if c == 0:
                st["oc"] = jnp.concatenate(st["o"], axis=1)
            o_ref[:, cols[c]] = st["xv"][c] + jnp.dot(st["oc"], wo_ref[:, cols[c]], preferred_element_type=F32)

        return [functools.partial(f, k) for f in (project, q_proj, scores, values, out_proj) for k in heads]

    def attend_and_mix():
        for step in mix_steps():
            step()

        uo = uo_ref[...]
        blocks_per_tile = EVEN_TILE // SB_QB

        def sb_first_pass(p, j):
            rows = slice(SB_QB * j, SB_QB * (j + 1))
            load = lambda ref: (lambda first, n: ref[p, pl.ds(pl.multiple_of(first * SB_QB, SB_QB), n * SB_QB), :])
            return _sb_unit(sbq_ref[p, rows, :].astype(F32), load(sbk_ref), load(sbv_ref),
                            blocks_per_tile * t_in_seq + j, uo)

        def store_sb(p, j, o):
            heads_s[wr, n_pairs_a + p, SB_QB * j:SB_QB * (j + 1), :] = o.astype(BF)

        units = [(p, j) for p in range(n_pairs_b) for j in range(blocks_per_tile)]
        passes = [sb_first_pass(p, j) for p, j in units]
        for (p, j), (_, o, _) in zip(units, passes):
            store_sb(p, j, o)

        n_kv = swkc_ref.shape[0]
        for m in range(EVEN_TILE // BLOCK):
            rows = slice(BLOCK * m, BLOCK * (m + 1))
            if m == 0:
                kp, vp, has_prev = [swkp_ref[g] for g in range(n_kv)], [swvp_ref[g] for g in range(n_kv)], t_in_seq > 0
            else:
                prows = slice(BLOCK * (m - 1), BLOCK * m)
                kp, vp, has_prev = ([swkc_ref[g, prows, :] for g in range(n_kv)],
                                    [swvc_ref[g, prows, :] for g in range(n_kv)], True)
            outs = _swa_block([swq_ref[p, rows, :] for p in range(n_pairs_a)], kp,
                              [swkc_ref[g, rows, :] for g in range(n_kv)], vp, [swvc_ref[g, rows, :] for g in range(n_kv)],
                              has_prev, slopes_ref, sinks_ref, kv_div=kv_div, max_dist=max_dist)
            for p in range(n_pairs_a):
                heads_s[wr, p, rows, :] = outs[p].astype(BF)

        @pl.when(functools.reduce(jnp.logical_or, [more for more, _, _ in passes]))
        def _():
            for (p, j), (_, _, finish) in zip(units, passes):
                store_sb(p, j, finish())

    n_tiles = N_TOK // EVEN_TILE
    pl.when(i < n_tiles)(attend_and_mix)

    @pl.when(i == n_tiles)
    def _():
        for step in mix_steps():
            step()


def _even_tail(x, p, slopes, sinks, w_mix, g, w_q, q_colscale, kv, w_o):
    n_pairs_a, n_kv, n_pairs_b = A_Q_HEADS // 2, A_KV_HEADS, B_HEADS // 2
    assert p.shape[0] == n_pairs_a + 2 * n_kv + 3 * n_pairs_b and n_pairs_a == n_pairs_b == 2 * n_kv
    n_tiles = N_TOK // EVEN_TILE
    tiles_per_seq = SEQ // EVEN_TILE
    blocks_per_tile = EVEN_TILE // BLOCK
    att = lambda i: jnp.minimum(i, n_tiles - 1)
    mix = lambda i: jnp.maximum(i - 1, 0)
    tile_rows = lambda size, idx: pl.BlockSpec((size, EVEN_TILE, LANES), lambda i: (idx, att(i), 0))
    prev_block = lambda idx: pl.BlockSpec(
        (n_kv, BLOCK, LANES), lambda i: (idx, jnp.maximum(blocks_per_tile * att(i) - 1, 0), 0))
    whole_seq = lambda idx: pl.BlockSpec((n_pairs_b, SEQ, LANES), lambda i: (idx, att(i) // tiles_per_seq, 0),
                                         pipeline_mode=pl.Buffered(1))
    smem = pl.BlockSpec(memory_space=pltpu.SMEM)
    return pl.pallas_call(
        functools.partial(_even_tail_kernel, kv_div=n_pairs_a // n_kv, max_dist=A_WINDOW - 1),
        grid=(n_tiles + 1,),
        in_specs=[smem, smem,
                  pl.BlockSpec((EVEN_TILE, D_MODEL), lambda i: (mix(i), 0)),
                  tile_rows(n_pairs_a, 0), prev_block(2), tile_rows(n_kv, 2), prev_block(3), tile_rows(n_kv, 3),
                  tile_rows(n_pairs_b, 2), whole_seq(3), whole_seq(4),
                  _resident((SB_QB, SB_QB)),
                  _resident(w_mix.shape), _resident((1, D_MODEL)), _resident(w_q.shape), _resident((1, D_MODEL)),
                  pl.BlockSpec((4 * X_HEADS, MEM_LEN, LANES), lambda i: (0, mix(i) // tiles_per_seq, 0)),
                  _resident(w_o.shape)],
        out_specs=pl.BlockSpec((EVEN_TILE, D_MODEL), lambda i: (mix(i), 0)),
        out_shape=jax.ShapeDtypeStruct((N_TOK, D_MODEL), F32),
        scratch_shapes=[pltpu.VMEM((2, n_pairs_a + n_pairs_b, EVEN_TILE, LANES), BF)],
        compiler_params=_params(("arbitrary",), VMEM_LIMIT),
        name="even_tail",
    )(slopes, sinks, x, p, p, p, p, p, p, p, p, _sb_matrix(), w_mix, g.reshape(1, D_MODEL), w_q,
      q_colscale.reshape(1, D_MODEL).astype(F32), kv, w_o)


def _alibi_log2(n_heads):
    return jnp.asarray(LOG2_E * 2.0 ** (-8.0 * np.arange(1, n_heads + 1) / n_heads), dtype=F32)


def _even_projection(x, norm_g, w_in, q_gain, k_gain, side):
    hd = HEAD_DIM
    a_q, a_kv, b_w = A_Q_HEADS * hd, A_KV_HEADS * hd, B_HEADS * hd
    scale = hd ** -0.5 * LOG2_E
    ones = lambda n: jnp.ones((n,), F32)
    cs = jnp.concatenate([jnp.tile(q_gain, A_Q_HEADS) * scale, jnp.tile(k_gain, A_KV_HEADS), ones(a_kv),
                          ones(b_w) * scale, ones(2 * b_w)])
    plan = ([(True, False)] * (a_q // LANES) + [(True, True)] * (a_kv // LANES) + [(False, True)] * (a_kv // LANES)
            + [(False, False)] * (3 * b_w // LANES))
    return _proj(x, norm_g, w_in, cs, plan, hd, side=side)


def _odd_mixer_heads(x, norm_g, w_in, q_gain, k_gain, side):
    hd = HEAD_DIM
    cs = jnp.concatenate([jnp.tile(q_gain, C_HEADS) * (hd ** -0.5 * LOG2_E), jnp.tile(k_gain, C_HEADS),
                          jnp.ones((C_HEADS * hd,), F32)])
    head_blocks = C_HEADS * hd // LANES
    plan = [(True, False)] * (2 * head_blocks) + [(False, False)] * head_blocks
    p, side_out = _proj(x, norm_g, w_in, cs, plan, hd, side=side)
    return [_dilated(p, _alibi_log2(C_HEADS))], side_out


def _memory_kv_job(mem2d, mem_g, w_kv, k_gain):
    cs_kv = jnp.concatenate([jnp.tile(k_gain, X_HEADS), jnp.ones((D_MODEL,), F32)])
    plan = [(True, False)] * (D_MODEL // LANES) + [(False, False)] * (D_MODEL // LANES)
    return mem2d, mem_g, w_kv, cs_kv, plan, X_HEAD_DIM


def kernel(x, mem, ffn1_norm, ffn1_w_gu, ffn1_w_down, mix_norm, ev_w_in, ev_q_gain, ev_k_gain, ev_sinks, ev_w_out, od_w_in, od_q_gain, od_k_gain, od_w_out, xa_norm, xa_mem_norm, xa_w_q, xa_w_kv, xa_q_gain, xa_k_gain, xa_w_o, ffn2_norm, ffn2_w_gu, ffn2_w_down):
    x = x.reshape(N_TOK, D_MODEL)
    mem2d = mem.reshape(BATCH * MEM_LEN, D_MODEL)
    w_gu, w_down = _cast_now([(ffn1_w_gu, 0), (ffn1_w_down, 0)])
    for layer in range(DEPTH):
        j = layer // 2
        even = layer % 2 == 0
        w_in3, w_mix3 = (ev_w_in, ev_w_out) if even else (od_w_in, od_w_out)
        jobs = [(w_in3, j), (w_mix3, j), (xa_w_q, layer), (xa_w_kv, layer), (xa_w_o, layer),
                (ffn2_w_gu, layer), (ffn2_w_down, layer)]
        x, (w_in, w_mix, w_q, w_kv, w_o, w_gu, w_down) = _ffn(x, ffn1_norm[layer], w_gu, w_down, jobs)
        kv_job = _memory_kv_job(mem2d, xa_mem_norm[layer], w_kv, xa_k_gain[layer])
        cs_q = jnp.tile(xa_q_gain[layer], X_HEADS) * (X_HEAD_DIM ** -0.5 * LOG2_E)
        if even:
            p, kv = _even_projection(x, mix_norm[layer], w_in, ev_q_gain[j], ev_k_gain[j], kv_job)
            x = _even_tail(x, p, _alibi_log2(A_Q_HEADS), ev_sinks[j].astype(F32) * LOG2_E, w_mix,
                           xa_norm[layer], w_q, cs_q, kv, w_o)
        else:
            heads, kv = _odd_mixer_heads(x, mix_norm[layer], w_in, od_q_gain[j], od_k_gain[j], kv_job)
            x = _mix_xattn(x, heads, w_mix, xa_norm[layer], w_q, cs_q, kv, w_o)
        jobs = [(ffn1_w_gu, layer + 1), (ffn1_w_down, layer + 1)] if layer + 1 < DEPTH else []
        x, next_ffn1 = _ffn(x, ffn2_norm[layer], w_gu, w_down, jobs)
        if next_ffn1:
            w_gu, w_down = next_ffn1
    return x.reshape(BATCH, SEQ, D_MODEL)
```

```python
import functools

import numpy as np
import jax
import jax.numpy as jnp
from jax import lax
from jax.experimental import pallas as pl
from jax.experimental.pallas import tpu as pltpu

D_MODEL = 1024
BATCH = 4
SEQ = 4096
N_TOK = BATCH * SEQ
DEPTH = 2
HEAD_DIM = 64
BLOCK = 128
A_Q_HEADS = 8
A_KV_HEADS = 2
A_WINDOW = 128
B_HEADS = 8
C_HEADS = 16
C_PATTERNS = ((128, 1), (512, 4), (2048, 16))
MEM_LEN = 256
X_HEADS = 4
X_HEAD_DIM = D_MODEL // X_HEADS
D_FF = 2816
RMS_EPS = 1e-6

LANES = 128
MXU_N = 256
VMEM_LIMIT = 56 * 1024 * 1024

BF = jnp.bfloat16
F32 = jnp.float32
NT_DIMS = (((1,), (1,)), ((), ()))
LOG2_E = 1.4426950408889634


def _params(sem, vmem=None):
    return pltpu.CompilerParams(dimension_semantics=sem, vmem_limit_bytes=vmem)


def _resident(shape):
    nd = len(shape)
    return pl.BlockSpec(shape, lambda *_: (0,) * nd, pipeline_mode=pl.Buffered(1))


BF16_SUBLANES = 16


def _cast_specs(job, steps):
    w3, layer = job
    _, r, c = w3.shape
    rb = next(rb for rb in range(BF16_SUBLANES, r + 1, BF16_SUBLANES) if r % rb == 0 and r // rb <= steps)
    last = r // rb - 1
    return (pl.BlockSpec((None, rb, c), lambda i: (layer, jnp.minimum(i, last), 0)),
            pl.BlockSpec((rb, c), lambda i: (jnp.minimum(i, last), 0)),
            jax.ShapeDtypeStruct((r, c), BF))


def _run_cast_jobs(in_refs, out_refs):
    for src, dst in zip(in_refs, out_refs):
        dst[...] = src[...].astype(BF)


def _cast_kernel(*refs):
    _run_cast_jobs(refs[:len(refs) // 2], refs[len(refs) // 2:])


def _cast_now(jobs, *, steps=11):
    specs = [_cast_specs(job, steps) for job in jobs]
    return pl.pallas_call(
        _cast_kernel, grid=(steps,), in_specs=[s[0] for s in specs], out_specs=[s[1] for s in specs],
        out_shape=[s[2] for s in specs],
        compiler_params=_params(("arbitrary",), VMEM_LIMIT),
        name="cast",
    )(*[job[0] for job in jobs])


def _rms(xv, g):
    ms = jnp.mean(xv * xv, axis=-1, keepdims=True)
    return xv * lax.rsqrt(ms + RMS_EPS) * g


FFN_SPLIT = (D_FF // MXU_N + 1) // 2 * MXU_N
FFN_CHUNKS = ((0, FFN_SPLIT), (FFN_SPLIT, D_FF))


def _ffn_kernel(*refs, n_jobs):
    x_ref, g_ref, wgu_ref, wd_ref = refs[:4]
    o_ref = refs[4 + n_jobs]
    h = _rms(x_ref[...], g_ref[...]).astype(BF)
    acc = jnp.zeros(x_ref.shape, F32)
    for c0, c1 in FFN_CHUNKS:
        gate = jnp.dot(h, wgu_ref[:, c0:c1], preferred_element_type=F32)
        up = jnp.dot(h, wgu_ref[:, D_FF + c0:D_FF + c1], preferred_element_type=F32)
        act = (gate * jax.nn.sigmoid(gate) * up).astype(BF)
        acc = acc + jnp.dot(act, wd_ref[c0:c1, :], preferred_element_type=F32)
    o_ref[...] = x_ref[...] + 0.5 * acc
    _run_cast_jobs(refs[4:4 + n_jobs], refs[5 + n_jobs:])


def _ffn(x, g, w_gu, w_down, cast_jobs=(), *, tm=512):
    steps = N_TOK // tm
    specs = [_cast_specs(job, steps) for job in cast_jobs]
    out = pl.pallas_call(
        functools.partial(_ffn_kernel, n_jobs=len(cast_jobs)),
        grid=(steps,),
        in_specs=[pl.BlockSpec((tm, D_MODEL), lambda i: (i, 0)),
                  _resident((1, D_MODEL)),
                  _resident(w_gu.shape),
                  _resident(w_down.shape)] + [s[0] for s in specs],
        out_specs=[pl.BlockSpec((tm, D_MODEL), lambda i: (i, 0))] + [s[1] for s in specs],
        out_shape=[jax.ShapeDtypeStruct((N_TOK, D_MODEL), F32)] + [s[2] for s in specs],
        compiler_params=_params(("arbitrary",), VMEM_LIMIT),
        name="ffn",
    )(x, g.reshape(1, D_MODEL), w_gu, w_down, *[job[0] for job in cast_jobs])
    return out[0], out[1:]


def _proj_kernel(x_ref, g_ref, w_ref, cs_ref, *rest, plan, gs, side):
    if side is None:
        (o_ref,) = rest
    else:
        sx_ref, sg_ref, sw_ref, scs_ref, o_ref, so_ref = rest
        pl.when(pl.program_id(0) == 0)(
            functools.partial(_project, sx_ref, sg_ref, sw_ref, scs_ref, so_ref, plan=side[0], gs=side[1]))
    _project(x_ref, g_ref, w_ref, cs_ref, o_ref, plan=plan, gs=gs)


def _project(x_ref, g_ref, w_ref, cs_ref, o_ref, *, plan, gs):
    assert gs in (HEAD_DIM, MXU_N)
    h = _rms(x_ref[...], g_ref[...]).astype(BF)
    n_chunks = len(plan) // 2
    lo = lax.broadcasted_iota(jnp.int32, (x_ref.shape[0], LANES), 1) < HEAD_DIM

    def main(j):
        return jnp.dot(h, w_ref[:, MXU_N * j:MXU_N * (j + 1)], preferred_element_type=F32)

    acc_next = main(0)
    out = 0
    for j in range(n_chunks):
        cols = slice(MXU_N * j, MXU_N * (j + 1))
        acc = acc_next
        if j + 1 < n_chunks:
            acc_next = main(j + 1)
        y = acc * cs_ref[:, cols]
        halves = plan[2 * j:2 * j + 2]
        if gs == MXU_N and any(normed for normed, _ in halves):
            inv_chunk = lax.rsqrt(jnp.mean(acc * acc, axis=1, keepdims=True) + RMS_EPS)
        for half, (normed, dup) in enumerate(halves):
            lanes = slice(LANES * half, LANES * (half + 1))
            yh = y[:, lanes]
            if normed and gs == MXU_N:
                yh = yh * inv_chunk
            elif normed:
                sq = acc[:, lanes] * acc[:, lanes]
                s_lo = jnp.sum(jnp.where(lo, sq, 0.0), axis=1, keepdims=True)
                s_hi = jnp.sum(jnp.where(lo, 0.0, sq), axis=1, keepdims=True)
                yh = yh * lax.rsqrt(jnp.where(lo, s_lo, s_hi) * (1.0 / gs) + RMS_EPS)
            if dup:
                swapped = pltpu.roll(yh, HEAD_DIM, axis=1)
                o_ref[out] = jnp.where(lo, yh, swapped).astype(BF)
                o_ref[out + 1] = jnp.where(lo, swapped, yh).astype(BF)
                out += 2
            else:
                o_ref[out] = yh.astype(BF)
                out += 1


def _proj(x, g, w, colscale, plan, gs, *, side=None, tm=1024):
    rows = x.shape[0]
    wout = w.shape[1]
    assert wout == LANES * len(plan) and len(plan) % 2 == 0
    n_blocks = lambda pln: sum(2 if dup else 1 for _, dup in pln)
    args = [x, g.reshape(1, D_MODEL), w, colscale.reshape(1, wout).astype(F32)]
    in_specs = [pl.BlockSpec((tm, D_MODEL), lambda i: (i, 0)), _resident((1, D_MODEL)), _resident(w.shape),
                _resident((1, wout))]
    out_specs = [pl.BlockSpec((n_blocks(plan), tm, LANES), lambda i: (0, i, 0))]
    out_shape = [jax.ShapeDtypeStruct((n_blocks(plan), rows, LANES), BF)]
    if side is not None:
        x2, g2, w2, cs2, plan2, gs2 = side
        assert w2.shape[1] == LANES * len(plan2)
        args += [x2, g2.reshape(1, D_MODEL), w2, cs2.reshape(1, -1).astype(F32)]
        in_specs += [_resident(x2.shape), _resident((1, D_MODEL)), _resident(w2.shape), _resident((1, w2.shape[1]))]
        out_specs.append(pl.BlockSpec((n_blocks(plan2), x2.shape[0], LANES), lambda i: (0, 0, 0)))
        out_shape.append(jax.ShapeDtypeStruct((n_blocks(plan2), x2.shape[0], LANES), BF))
    out = pl.pallas_call(
        functools.partial(_proj_kernel, plan=tuple(plan), gs=gs,
                          side=None if side is None else (tuple(side[4]), side[5])),
        grid=(rows // tm,),
        in_specs=in_specs, out_specs=out_specs, out_shape=out_shape,
        compiler_params=_params(("arbitrary",), VMEM_LIMIT),
        name="proj",
    )(*args)
    return out[0] if side is None else tuple(out)


def _swa_block(q_blocks, kp, kc, vp, vc, has_prev, slopes_ref, sinks_ref, *, kv_div, max_dist):
    row = lax.broadcasted_iota(jnp.int32, (BLOCK, 2 * BLOCK), 0)
    col = lax.broadcasted_iota(jnp.int32, (BLOCK, 2 * BLOCK), 1)
    dist = row + BLOCK - col
    valid = (dist >= 0) & (dist <= max_dist)
    if has_prev is not True:
        valid = valid & ((col >= BLOCK) | has_prev)
    negmask = jnp.where(valid, 0.0, -jnp.inf)
    distf = dist.astype(F32)
    lo = lax.broadcasted_iota(jnp.int32, (BLOCK, LANES), 1) < HEAD_DIM

    n_groups = len(q_blocks) // kv_div
    heads_per_group = 2 * kv_div
    scores = []
    for g in range(n_groups):
        parts = []
        for p in range(g * kv_div, (g + 1) * kv_div):
            q2 = q_blocks[p].astype(F32)
            parts += [jnp.where(lo, q2, 0.0), jnp.where(lo, 0.0, q2)]
        q_stack = jnp.concatenate(parts, axis=0).astype(BF)
        scores.append(jnp.concatenate(
            [lax.dot_general(q_stack, kp[g], NT_DIMS, preferred_element_type=F32),
             lax.dot_general(q_stack, kc[g], NT_DIMS, preferred_element_type=F32)], axis=1))
    soft = []
    for g in range(n_groups):
        res = []
        for j in range(heads_per_group):
            h = g * heads_per_group + j
            s = scores[g][j * BLOCK:(j + 1) * BLOCK] - slopes_ref[h] * distf + negmask
            m = jnp.maximum(jnp.max(s, axis=1, keepdims=True), sinks_ref[h])
            pe = jnp.exp2(s - m)
            res.append((pe.astype(BF), jnp.sum(pe, axis=1, keepdims=True) + jnp.exp2(sinks_ref[h] - m)))
        soft.append(res)
    outs = []
    for g in range(n_groups):
        pb = jnp.concatenate([r[0] for r in soft[g]], axis=0)
        pv = (jnp.dot(pb[:, :BLOCK], vp[g], preferred_element_type=F32)
              + jnp.dot(pb[:, BLOCK:], vc[g], preferred_element_type=F32))
        for jp in range(kv_div):
            o0 = pv[(2 * jp) * BLOCK:(2 * jp + 1) * BLOCK] / soft[g][2 * jp][1]
            o1 = pv[(2 * jp + 1) * BLOCK:(2 * jp + 2) * BLOCK] / soft[g][2 * jp + 1][1]
            outs.append(jnp.where(lo, o0, o1))
    return outs


DIL_ORDER = tuple(sorted(C_PATTERNS, key=lambda wd: -wd[1]))
DIL_UNROLL = 16
DIL_AHEAD = 2
DIL_BASE = 4
DIL_Q = SEQ // DIL_BASE
DIL_CONVERT_ROWS = DIL_BASE * BLOCK


def _dilated_kernel(slopes_ref, q_ref, k_ref, v_ref, o_ref, qn_s, tq_s, tk_s, tv_s, q0_s, q1_s, k_s, v_s,
                    acc_r, m_r, l_r, acc_n, m_n, l_n):
    assert all(d == 1 or d % DIL_BASE == 0 for _, d in DIL_ORDER) and DIL_ORDER[-1][1] == 1
    p = pl.program_id(1)
    lo = lax.broadcasted_iota(jnp.int32, (BLOCK, LANES), 1) < HEAD_DIM

    def convert(c, carry):
        rows = pl.ds(pl.multiple_of(c * DIL_CONVERT_ROWS, DIL_CONVERT_ROWS), DIL_CONVERT_ROWS)
        q_nat = q_ref[0, rows, :].astype(F32)
        lo_c = lax.broadcasted_iota(jnp.int32, (DIL_CONVERT_ROWS, LANES), 1) < HEAD_DIM
        qn_s[0, rows, :] = jnp.where(lo_c, q_nat, 0.0).astype(BF)
        qn_s[1, rows, :] = jnp.where(lo_c, 0.0, q_nat).astype(BF)
        tq_s[...] = q_nat
        tk_s[...] = k_ref[0, rows, :].astype(F32)
        tv_s[...] = v_ref[0, rows, :].astype(F32)
        for rho in range(DIL_BASE):
            src = pl.ds(rho, BLOCK, stride=DIL_BASE)
            dst = pl.ds(pl.multiple_of(rho * DIL_Q + c * BLOCK, BLOCK), BLOCK)
            q = tq_s[src, :]
            q0_s[dst, :] = jnp.where(lo, q, 0.0)
            q1_s[dst, :] = jnp.where(lo, 0.0, q)
            k_s[dst, :] = tk_s[src, :]
            v_s[dst, :] = tv_s[src, :]
        return carry

    lax.fori_loop(0, SEQ // DIL_CONVERT_ROWS, convert, 0)

    row = lax.broadcasted_iota(jnp.int32, (BLOCK, 2 * BLOCK), 0)
    col = lax.broadcasted_iota(jnp.int32, (BLOCK, 2 * BLOCK), 1)
    dist = row + BLOCK - col
    distf = dist.astype(F32)
    no_prev = jnp.where(col < BLOCK, -jnp.inf, 0.0)

    def bcast2(a0, a1):
        return jnp.where(lo, jnp.broadcast_to(a0, (BLOCK, LANES)), jnp.broadcast_to(a1, (BLOCK, LANES)))

    for pi, (window, d) in enumerate(DIL_ORDER):
        first, last = pi == 0, pi == len(DIL_ORDER) - 1
        natural = d == 1
        nb = SEQ // d // BLOCK
        band = (dist >= 0) & (dist <= window // d)
        bias = [jnp.where(band, (-float(d) * slopes_ref[2 * p + hh]) * distf, -jnp.inf) for hh in range(2)]
        acc_s, m_s, l_s = (acc_n, m_n, l_n) if natural else (acc_r, m_r, l_r)
        to_natural = not natural and not last and DIL_ORDER[pi + 1][1] == 1
        assert (not to_natural or d == DIL_BASE) and (not natural or first or DIL_ORDER[pi - 1][1] == DIL_BASE)
        acc_o, m_o, l_o = (acc_n, m_n, l_n) if to_natural else (acc_s, m_s, l_s)

        def rows_of(r, n, n_blocks=1, d=d, natural=natural):
            size = n_blocks * BLOCK
            if natural:
                return pl.ds(pl.multiple_of(BLOCK * n, BLOCK), size)
            inner = d // DIL_BASE
            start = (r % DIL_BASE) * DIL_Q + r // DIL_BASE + inner * BLOCK * n
            return pl.ds(start, size, stride=inner) if inner > 1 else pl.ds(pl.multiple_of(start, BLOCK), size)

        def step(it, carry, nb=nb, bias=bias, first=first, last=last, natural=natural, rows_of=rows_of,
                 acc_s=acc_s, m_s=m_s, l_s=l_s, acc_o=acc_o, m_o=m_o, l_o=l_o, to_natural=to_natural):
            assert DIL_UNROLL % nb == 0 or nb % DIL_UNROLL == 0
            load_k = (lambda rr: k_ref[0, rr, :]) if natural else (lambda rr: k_s[rr, :].astype(BF))
            load_v = (lambda rr: v_ref[0, rr, :]) if natural else (lambda rr: v_s[rr, :].astype(BF))

            def scores(u):
                t = it * DIL_UNROLL + u
                r, n = t // nb, t % nb
                prev = (u % nb != 0) if nb <= DIL_UNROLL else (True if u else None)
                rows = rows_of(r, n)
                out_rows = pl.ds(DIL_BASE * BLOCK * n + r, BLOCK, stride=DIL_BASE) if to_natural else rows
                if prev is True:
                    kv_rows = [rows_of(r, n - 1, 2)]
                elif prev is None:
                    kv_rows = [rows_of(r, jnp.maximum(n - 1, 0)), rows]
                else:
                    kv_rows = [rows]
                if natural:
                    qh = jnp.concatenate([qn_s[0, rows, :], qn_s[1, rows, :]], axis=0)
                else:
                    qh = jnp.concatenate([q0_s[rows, :], q1_s[rows, :]], axis=0).astype(BF)
                s = jnp.concatenate([lax.dot_general(qh, load_k(rr), NT_DIMS, preferred_element_type=F32)
                                     for rr in kv_rows], axis=1)
                return n, prev, rows, out_rows, kv_rows, (s[:BLOCK], s[BLOCK:])

            def softmax_pv(blk):
                n, prev, rows, out_rows, kv_rows, s = blk
                ms, ls, pes = [], [], []
                for hh in range(2):
                    sh = s[hh] + (bias[hh][:, BLOCK:] if prev is False else bias[hh])
                    if prev is None:
                        sh = sh + jnp.where(n == 0, no_prev, 0.0)
                    m = jnp.max(sh, axis=1, keepdims=True)
                    pe = jnp.exp2(sh - m)
                    ms.append(m)
                    ls.append(jnp.sum(pe, axis=1, keepdims=True))
                    pes.append(pe.astype(BF))
                pb = jnp.concatenate(pes, axis=0)
                v = jnp.concatenate([load_v(rr) for rr in kv_rows], axis=0) if len(kv_rows) > 1 else load_v(kv_rows[0])
                pv = jnp.dot(pb, v, preferred_element_type=F32)
                return rows, out_rows, ms, ls, (pv[:BLOCK], pv[BLOCK:])

            def merge(rows, out_rows, ms, ls, pv):
                m2 = bcast2(ms[0], ms[1])
                l2 = bcast2(ls[0], ls[1])
                acc2 = jnp.where(lo, pv[0], pv[1])
                if not first:
                    m_old = m_s[rows, :]
                    m_new = jnp.maximum(m_old, m2)
                    a_old, a_new = jnp.exp2(m_old - m_new), jnp.exp2(m2 - m_new)
                    l2 = a_old * l_s[rows, :] + a_new * l2
                    acc2 = a_old * acc_s[rows, :] + a_new * acc2
                    m2 = m_new
                if last:
                    o_ref[0, rows, :] = (acc2 / l2).astype(BF)
                else:
                    m_o[out_rows, :] = m2
                    l_o[out_rows, :] = l2
                    acc_o[out_rows, :] = acc2

            pending = {u: scores(u) for u in range(DIL_AHEAD)}
            done = None
            for u in range(DIL_UNROLL):
                if u + DIL_AHEAD < DIL_UNROLL:
                    pending[u + DIL_AHEAD] = scores(u + DIL_AHEAD)
                cur = softmax_pv(pending.pop(u))
                if done is not None:
                    merge(*done)
                done = cur
            merge(*done)
            return carry

        lax.fori_loop(0, SEQ // BLOCK // DIL_UNROLL, step, 0)


def _dilated(qkv, slopes):
    n_pairs = C_HEADS // 2
    seq_f32 = pltpu.VMEM((SEQ, LANES), F32)
    chunk_f32 = pltpu.VMEM((DIL_CONVERT_ROWS, LANES), F32)
    return pl.pallas_call(
        _dilated_kernel,
        grid=(BATCH, n_pairs),
        in_specs=[pl.BlockSpec(memory_space=pltpu.SMEM),
                  pl.BlockSpec((1, SEQ, LANES), lambda b, p: (p, b, 0)),
                  pl.BlockSpec((1, SEQ, LANES), lambda b, p: (n_pairs + p, b, 0)),
                  pl.BlockSpec((1, SEQ, LANES), lambda b, p: (2 * n_pairs + p, b, 0))],
        out_specs=pl.BlockSpec((1, SEQ, LANES), lambda b, p: (p, b, 0)),
        out_shape=jax.ShapeDtypeStruct((n_pairs, N_TOK, LANES), BF),
        scratch_shapes=[pltpu.VMEM((2, SEQ, LANES), BF)] + [chunk_f32] * 3 + [seq_f32] * 10,
        compiler_params=_params(("parallel", "parallel"), VMEM_LIMIT),
        name="dilated",
    )(slopes, qkv, qkv, qkv)


SB_QB = MXU_N
SB_FIRST_TILES = 2
SB_DEAD_LOG2 = -150.0


def _sb_matrix():
    idx = np.arange(SB_QB)
    return jnp.asarray(-(idx[:, None] > idx[None, :]).astype(np.float32), dtype=BF)


def _sb_unit(q2, load_k, load_v, iq, uo):
    lo = lax.broadcasted_iota(jnp.int32, (SB_QB, LANES), 1) < HEAD_DIM
    q_stack = jnp.concatenate([jnp.where(lo, q2, 0.0), jnp.where(lo, 0.0, q2)], axis=0).astype(BF)
    def walk(tiles, carry, weights=None):
        c, o = carry
        k = jnp.concatenate([load_k(t, 1) for t in tiles], axis=0) if len(tiles) > 1 else load_k(tiles[0], 1)
        v = jnp.concatenate([load_v(t, 1) for t in tiles], axis=0) if len(tiles) > 1 else load_v(tiles[0], 1)
        z = lax.dot_general(q_stack, k, NT_DIMS, preferred_element_type=F32)
        order = list(reversed(range(len(tiles))))
        ws, es, totals = {}, {}, {}
        for t in order:
            zt = z[:, t * SB_QB:(t + 1) * SB_QB]
            sp = jnp.maximum(zt, 0.0) + jnp.log2(1.0 + jnp.exp2(-jnp.abs(zt)))
            es[t] = zt - sp
            if weights is not None:
                sp = sp * weights[t]
            ws[t] = jnp.dot(sp.astype(BF), uo, preferred_element_type=F32)
            totals[t] = jnp.sum(sp, axis=1, keepdims=True)
        parts = {}
        for t in order:
            a = jnp.exp2(es[t] + jnp.concatenate([c] * (SB_QB // LANES), axis=1) + ws[t])
            if weights is not None:
                a = a * weights[t]
            parts[t] = a.astype(BF)
            c = c - totals[t]
        pv = jnp.dot(jnp.concatenate([parts[t] for t in range(len(tiles))], axis=1), v,
                     preferred_element_type=F32)
        return c, o + jnp.where(lo, pv[:SB_QB], pv[SB_QB:])

    def alive(c):
        return jnp.max(c) > SB_DEAD_LOG2

    def body(state):
        g = state[0]
        c, o = walk([g], state[2:])
        return g - 1, alive(c), c, o

    assert SB_FIRST_TILES == 2
    first = jnp.maximum(iq - 1, 0)
    causal = -uo.astype(F32)
    weights = [jnp.where(iq > 0, 1.0, 0.0).astype(F32), jnp.concatenate([causal, causal], axis=0)]
    zeros = (jnp.zeros((2 * SB_QB, LANES), F32), jnp.zeros((SB_QB, LANES), F32))
    c, o = walk([first, iq], zeros, weights)
    more = (first > 0) & alive(c)

    def finish():
        state = lax.while_loop(lambda st: (st[0] >= 0) & st[1], body, (first - 1, more, c, o))
        return state[3]

    return more, o, finish


def _mix_xattn_kernel(*refs):
    x_ref = refs[0]
    wm_ref, g_ref, wq_ref, cs_ref, kv_ref, wo_ref, o_ref = refs[-7:]
    mixed = jnp.concatenate([r[c] for r in refs[1:-7] for c in range(r.shape[0])], axis=1)
    xv = x_ref[...] + jnp.dot(mixed, wm_ref[...], preferred_element_type=F32)
    h = _rms(xv, g_ref[...]).astype(BF)
    heads = range(X_HEADS)
    cols = [slice(X_HEAD_DIM * hd, X_HEAD_DIM * (hd + 1)) for hd in heads]
    acc = [jnp.dot(h, wq_ref[:, cols[hd]], preferred_element_type=F32) for hd in heads]
    ms = [jnp.mean(acc[hd] * acc[hd], axis=1, keepdims=True) for hd in heads]
    q = [(acc[hd] * cs_ref[:, cols[hd]] * lax.rsqrt(ms[hd] + RMS_EPS)).astype(BF) for hd in heads]
    s = [lax.dot_general(q[hd], jnp.concatenate([kv_ref[2 * hd], kv_ref[2 * hd + 1]], axis=1), NT_DIMS,
                         preferred_element_type=F32) for hd in heads]
    pe, l = [], []
    for hd in heads:
        e = jnp.exp2(s[hd] - jnp.max(s[hd], axis=1, keepdims=True))
        l.append(jnp.sum(e, axis=1, keepdims=True))
        pe.append(e.astype(BF))
    v0 = 2 * X_HEADS
    pv = [jnp.dot(pe[hd], jnp.concatenate([kv_ref[v0 + 2 * hd], kv_ref[v0 + 2 * hd + 1]], axis=1),
                  preferred_element_type=F32) for hd in heads]
    o = jnp.concatenate([(pv[hd] / l[hd]).astype(BF) for hd in heads], axis=1)
    o_ref[...] = xv + jnp.dot(o, wo_ref[...], preferred_element_type=F32)


def _mix_xattn(x, mixer_heads, w_mix, g, w_q, q_colscale, kv, w_o, *, tm=1024):
    tiles_per_batch = SEQ // tm
    in_specs = [pl.BlockSpec((tm, D_MODEL), lambda i: (i, 0))]
    in_specs += [pl.BlockSpec((mh.shape[0], tm, LANES), lambda i: (0, i, 0)) for mh in mixer_heads]
    in_specs += [_resident(w_mix.shape),
                 _resident((1, D_MODEL)),
                 _resident(w_q.shape),
                 _resident((1, D_MODEL)),
                 pl.BlockSpec((4 * X_HEADS, MEM_LEN, LANES), lambda i: (0, i // tiles_per_batch, 0)),
                 _resident(w_o.shape)]
    return pl.pallas_call(
        _mix_xattn_kernel, grid=(N_TOK // tm,),
        in_specs=in_specs,
        out_specs=pl.BlockSpec((tm, D_MODEL), lambda i: (i, 0)),
        out_shape=jax.ShapeDtypeStruct((N_TOK, D_MODEL), F32),
        compiler_params=_params(("parallel",), VMEM_LIMIT),
        name="mix_xattn",
    )(x, *mixer_heads, w_mix, g.reshape(1, D_MODEL), w_q,
      q_colscale.reshape(1, D_MODEL).astype(F32), kv, w_o)


EVEN_TILE = 512


def _even_tail_kernel(slopes_ref, sinks_ref, x_ref, swq_ref, swkp_ref, swkc_ref, swvp_ref, swvc_ref,
                      sbq_ref, sbk_ref, sbv_ref, uo_ref, wm_ref, g_ref, wq_ref, cs_ref, kv_ref, wo_ref,
                      o_ref, heads_s, *, kv_div, max_dist):
    i = pl.program_id(0)
    tiles_per_seq = SEQ // EVEN_TILE
    t_in_seq = jnp.minimum(i, N_TOK // EVEN_TILE - 1) % tiles_per_seq
    wr, rd = i % 2, (i + 1) % 2
    n_pairs_a, n_pairs_b = swq_ref.shape[0], sbq_ref.shape[0]

    @pl.when(i == 0)
    def _():
        heads_s[1] = jnp.zeros(heads_s.shape[1:], BF)

    def mix_steps():
        heads = range(X_HEADS)
        cols = [slice(X_HEAD_DIM * hd, X_HEAD_DIM * (hd + 1)) for hd in heads]
        st = {"xv": [None] * X_HEADS, "q": [None] * X_HEADS, "pe": [None] * X_HEADS, "l": [None] * X_HEADS,
              "o": [None] * X_HEADS}

        def project(c):
            if c == 0:
                st["mixed"] = jnp.concatenate([heads_s[rd, k] for k in range(n_pairs_a + n_pairs_b)], axis=1)
            st["xv"][c] = x_ref[:, cols[c]] + jnp.dot(st["mixed"], wm_ref[:, cols[c]], preferred_element_type=F32)

        def q_proj(hd):
            if hd == 0:
                xv = jnp.concatenate(st["xv"], axis=1)
                st["h"] = _rms(xv, g_ref[...]).astype(BF)
            acc = jnp.dot(st["h"], wq_ref[:, cols[hd]], preferred_element_type=F32)
            ms = jnp.mean(acc * acc, axis=1, keepdims=True)
            st["q"][hd] = (acc * cs_ref[:, cols[hd]] * lax.rsqrt(ms + RMS_EPS)).astype(BF)

        def scores(hd):
            kh = jnp.concatenate([kv_ref[2 * hd], kv_ref[2 * hd + 1]], axis=1)
            s = lax.dot_general(st["q"][hd], kh, NT_DIMS, preferred_element_type=F32)
            e = jnp.exp2(s - jnp.max(s, axis=1, keepdims=True))
            st["l"][hd] = jnp.sum(e, axis=1, keepdims=True)
            st["pe"][hd] = e.astype(BF)

        def values(hd):
            v0 = 2 * X_HEADS
            vh = jnp.concatenate([kv_ref[v0 + 2 * hd], kv_ref[v0 + 2 * hd + 1]], axis=1)
            st["o"][hd] = (jnp.dot(st["pe"][hd], vh, preferred_element_type=F32) / st["l"][hd]).astype(BF)

        def out_proj(c):
            if c == 0:
                st["oc"] = jnp.concatenate(st["o"], axis=1)
            o_ref[:, cols[c]] = st["xv"][c] + jnp.dot(st["oc"], wo_ref[:, cols[c]], preferred_element_type=F32)

        return [functools.partial(f, k) for f in (project, q_proj, scores, values, out_proj) for k in heads]

    def attend_and_mix():
        for step in mix_steps():
            step()

        uo = uo_ref[...]
        blocks_per_tile = EVEN_TILE // SB_QB

        def sb_first_pass(p, j):
            rows = slice(SB_QB * j, SB_QB * (j + 1))
            load = lambda ref: (lambda first, n: ref[p, pl.ds(pl.multiple_of(first * SB_QB, SB_QB), n * SB_QB), :])
            return _sb_unit(sbq_ref[p, rows, :].astype(F32), load(sbk_ref), load(sbv_ref),
                            blocks_per_tile * t_in_seq + j, uo)

        def store_sb(p, j, o):
            heads_s[wr, n_pairs_a + p, SB_QB * j:SB_QB * (j + 1), :] = o.astype(BF)

        units = [(p, j) for p in range(n_pairs_b) for j in range(blocks_per_tile)]
        passes = [sb_first_pass(p, j) for p, j in units]
        for (p, j), (_, o, _) in zip(units, passes):
            store_sb(p, j, o)

        n_kv = swkc_ref.shape[0]
        for m in range(EVEN_TILE // BLOCK):
            rows = slice(BLOCK * m, BLOCK * (m + 1))
            if m == 0:
                kp, vp, has_prev = [swkp_ref[g] for g in range(n_kv)], [swvp_ref[g] for g in range(n_kv)], t_in_seq > 0
            else:
                prows = slice(BLOCK * (m - 1), BLOCK * m)
                kp, vp, has_prev = ([swkc_ref[g, prows, :] for g in range(n_kv)],
                                    [swvc_ref[g, prows, :] for g in range(n_kv)], True)
            outs = _swa_block([swq_ref[p, rows, :] for p in range(n_pairs_a)], kp,
                              [swkc_ref[g, rows, :] for g in range(n_kv)], vp, [swvc_ref[g, rows, :] for g in range(n_kv)],
                              has_prev, slopes_ref, sinks_ref, kv_div=kv_div, max_dist=max_dist)
            for p in range(n_pairs_a):
                heads_s[wr, p, rows, :] = outs[p].astype(BF)

        @pl.when(functools.reduce(jnp.logical_or, [more for more, _, _ in passes]))
        def _():
            for (p, j), (_, _, finish) in zip(units, passes):
                store_sb(p, j, finish())

    n_tiles = N_TOK // EVEN_TILE
    pl.when(i < n_tiles)(attend_and_mix)

    @pl.when(i == n_tiles)
    def _():
        for step in mix_steps():
            step()


def _even_tail(x, p, slopes, sinks, w_mix, g, w_q, q_colscale, kv, w_o):
    n_pairs_a, n_kv, n_pairs_b = A_Q_HEADS // 2, A_KV_HEADS, B_HEADS // 2
    assert p.shape[0] == n_pairs_a + 2 * n_kv + 3 * n_pairs_b and n_pairs_a == n_pairs_b == 2 * n_kv
    n_tiles = N_TOK // EVEN_TILE
    tiles_per_seq = SEQ // EVEN_TILE
    blocks_per_tile = EVEN_TILE // BLOCK
    att = lambda i: jnp.minimum(i, n_tiles - 1)
    mix = lambda i: jnp.maximum(i - 1, 0)
    tile_rows = lambda size, idx: pl.BlockSpec((size, EVEN_TILE, LANES), lambda i: (idx, att(i), 0))
    prev_block = lambda idx: pl.BlockSpec(
        (n_kv, BLOCK, LANES), lambda i: (idx, jnp.maximum(blocks_per_tile * att(i) - 1, 0), 0))
    whole_seq = lambda idx: pl.BlockSpec((n_pairs_b, SEQ, LANES), lambda i: (idx, att(i) // tiles_per_seq, 0),
                                         pipeline_mode=pl.Buffered(1))
    smem = pl.BlockSpec(memory_space=pltpu.SMEM)
    return pl.pallas_call(
        functools.partial(_even_tail_kernel, kv_div=n_pairs_a // n_kv, max_dist=A_WINDOW - 1),
        grid=(n_tiles + 1,),
        in_specs=[smem, smem,
                  pl.BlockSpec((EVEN_TILE, D_MODEL), lambda i: (mix(i), 0)),
                  tile_rows(n_pairs_a, 0), prev_block(2), tile_rows(n_kv, 2), prev_block(3), tile_rows(n_kv, 3),
                  tile_rows(n_pairs_b, 2), whole_seq(3), whole_seq(4),
                  _resident((SB_QB, SB_QB)),
                  _resident(w_mix.shape), _resident((1, D_MODEL)), _resident(w_q.shape), _resident((1, D_MODEL)),
                  pl.BlockSpec((4 * X_HEADS, MEM_LEN, LANES), lambda i: (0, mix(i) // tiles_per_seq, 0)),
                  _resident(w_o.shape)],
        out_specs=pl.BlockSpec((EVEN_TILE, D_MODEL), lambda i: (mix(i), 0)),
        out_shape=jax.ShapeDtypeStruct((N_TOK, D_MODEL), F32),
        scratch_shapes=[pltpu.VMEM((2, n_pairs_a + n_pairs_b, EVEN_TILE, LANES), BF)],
        compiler_params=_params(("arbitrary",), VMEM_LIMIT),
        name="even_tail",
    )(slopes, sinks, x, p, p, p, p, p, p, p, p, _sb_matrix(), w_mix, g.reshape(1, D_MODEL), w_q,
      q_colscale.reshape(1, D_MODEL).astype(F32), kv, w_o)


def _alibi_log2(n_heads):
    return jnp.asarray(LOG2_E * 2.0 ** (-8.0 * np.arange(1, n_heads + 1) / n_heads), dtype=F32)


def _even_projection(x, norm_g, w_in, q_gain, k_gain, side):
    hd = HEAD_DIM
    a_q, a_kv, b_w = A_Q_HEADS * hd, A_KV_HEADS * hd, B_HEADS * hd
    scale = hd ** -0.5 * LOG2_E
    ones = lambda n: jnp.ones((n,), F32)
    cs = jnp.concatenate([jnp.tile(q_gain, A_Q_HEADS) * scale, jnp.tile(k_gain, A_KV_HEADS), ones(a_kv),
                          ones(b_w) * scale, ones(2 * b_w)])
    plan = ([(True, False)] * (a_q // LANES) + [(True, True)] * (a_kv // LANES) + [(False, True)] * (a_kv // LANES)
            + [(False, False)] * (3 * b_w // LANES))
    return _proj(x, norm_g, w_in, cs, plan, hd, side=side)


def _odd_mixer_heads(x, norm_g, w_in, q_gain, k_gain, side):
    hd = HEAD_DIM
    cs = jnp.concatenate([jnp.tile(q_gain, C_HEADS) * (hd ** -0.5 * LOG2_E), jnp.tile(k_gain, C_HEADS),
                          jnp.ones((C_HEADS * hd,), F32)])
    head_blocks = C_HEADS * hd // LANES
    plan = [(True, False)] * (2 * head_blocks) + [(False, False)] * head_blocks
    p, side_out = _proj(x, norm_g, w_in, cs, plan, hd, side=side)
    return [_dilated(p, _alibi_log2(C_HEADS))], side_out


def _memory_kv_job(mem2d, mem_g, w_kv, k_gain):
    cs_kv = jnp.concatenate([jnp.tile(k_gain, X_HEADS), jnp.ones((D_MODEL,), F32)])
    plan = [(True, False)] * (D_MODEL // LANES) + [(False, False)] * (D_MODEL // LANES)
    return mem2d, mem_g, w_kv, cs_kv, plan, X_HEAD_DIM


def kernel(x, mem, ffn1_norm, ffn1_w_gu, ffn1_w_down, mix_norm, ev_w_in, ev_q_gain, ev_k_gain, ev_sinks, ev_w_out, od_w_in, od_q_gain, od_k_gain, od_w_out, xa_norm, xa_mem_norm, xa_w_q, xa_w_kv, xa_q_gain, xa_k_gain, xa_w_o, ffn2_norm, ffn2_w_gu, ffn2_w_down):
    x = x.reshape(N_TOK, D_MODEL)
    mem2d = mem.reshape(BATCH * MEM_LEN, D_MODEL)
    w_gu, w_down = _cast_now([(ffn1_w_gu, 0), (ffn1_w_down, 0)])
    for layer in range(DEPTH):
        j = layer // 2
        even = layer % 2 == 0
        w_in3, w_mix3 = (ev_w_in, ev_w_out) if even else (od_w_in, od_w_out)
        jobs = [(w_in3, j), (w_mix3, j), (xa_w_q, layer), (xa_w_kv, layer), (xa_w_o, layer),
                (ffn2_w_gu, layer), (ffn2_w_down, layer)]
        x, (w_in, w_mix, w_q, w_kv, w_o, w_gu, w_down) = _ffn(x, ffn1_norm[layer], w_gu, w_down, jobs)
        kv_job = _memory_kv_job(mem2d, xa_mem_norm[layer], w_kv, xa_k_gain[layer])
        cs_q = jnp.tile(xa_q_gain[layer], X_HEADS) * (X_HEAD_DIM ** -0.5 * LOG2_E)
        if even:
            p, kv = _even_projection(x, mix_norm[layer], w_in, ev_q_gain[j], ev_k_gain[j], kv_job)
            x = _even_tail(x, p, _alibi_log2(A_Q_HEADS), ev_sinks[j].astype(F32) * LOG2_E, w_mix,
                           xa_norm[layer], w_q, cs_q, kv, w_o)
        else:
            heads, kv = _odd_mixer_heads(x, mix_norm[layer], w_in, od_q_gain[j], od_k_gain[j], kv_job)
            x = _mix_xattn(x, heads, w_mix, xa_norm[layer], w_q, cs_q, kv, w_o)
        jobs = [(ffn1_w_gu, layer + 1), (ffn1_w_down, layer + 1)] if layer + 1 < DEPTH else []
        x, next_ffn1 = _ffn(x, ffn2_norm[layer], w_gu, w_down, jobs)
        if next_ffn1:
            w_gu, w_down = next_ffn1
    return x.reshape(BATCH, SEQ, D_MODEL)
```
